```python
import functools
import jax, jax.numpy as jnp
from jax import lax
import numpy as np

D_MODEL = 1024
BATCH = 32
SEQ = 2048
DEPTH = 2
DEC_BATCH = 32
DEC_SEQ = 16
PAST_LEN = 4096

CHUNK = 64
N_MIXERS = 2
N_HGRN_LAYERS = (DEPTH + 1) // 2
N_DSA_LAYERS = DEPTH // 2
EPS = 1e-6
HG_HEADS = 8
HG_KDIM = 128
HG_VDIM = D_MODEL // HG_HEADS
HG_FDIM = HG_HEADS * HG_KDIM
HG_IN = 2 * HG_FDIM + 2 * D_MODEL
REC_BLOCK = 32
N_HEADS = 16
HEAD_DIM = D_MODEL // N_HEADS
N_KV_HEADS = 4
N_GROUPS = N_HEADS // N_KV_HEADS
IDX_HEADS = 8
IDX_DIM = 64
TOPK_MAX = 256
Q_BLOCK = 64
ROPE_THETA = 500000.0
DSA_IN = N_HEADS * HEAD_DIM + 2 * N_KV_HEADS * HEAD_DIM + IDX_HEADS * IDX_DIM + IDX_DIM + IDX_HEADS
D_FF = 2816
CONV_W = 3

kernel_name = 'hybrid_hgrn2_dsa_stream_step'

F32 = jnp.float32


def rms_norm(x, g):
    xf = x.astype(F32)
    y = xf * lax.rsqrt(jnp.mean(xf * xf, axis=-1, keepdims=True) + EPS)
    return (y * g.astype(F32)).astype(x.dtype)


def layer_norm(x, w, b):
    xf = x.astype(F32)
    mu = jnp.mean(xf, axis=-1, keepdims=True)
    var = jnp.mean(jnp.square(xf - mu), axis=-1, keepdims=True)
    return ((xf - mu) * lax.rsqrt(var + EPS) * w.astype(F32) + b.astype(F32)).astype(x.dtype)


def partial_rope(x, pos):
    rot = x.shape[-1] // 4
    half = rot // 2
    inv_freq = ROPE_THETA ** (-jnp.arange(half, dtype=F32) / half)
    ang = pos.astype(F32)[:, None] * inv_freq[None, :]
    cos = jnp.cos(ang)[:, None, :]
    sin = jnp.sin(ang)[:, None, :]
    xf = x.astype(F32)
    x1, x2, rest = xf[..., :half], xf[..., half:rot], xf[..., rot:]
    out = jnp.concatenate([x1 * cos - x2 * sin, x2 * cos + x1 * sin, rest], axis=-1)
    return out.astype(x.dtype)


def hgrn_recurrence(S0, q, k, v, logf, block):
    B, T, H, K = q.shape
    V = v.shape[-1]
    n = T // block

    def to_blocks(a):
        return a.reshape(B, n, block, H, a.shape[-1]).transpose(1, 0, 3, 2, 4)

    tri = jnp.tril(jnp.ones((block, block), dtype=bool))

    def step(S, xs):
        qb, kb, vb, gb = xs
        G = jnp.cumsum(gb, axis=2)
        diff = G[:, :, :, None, :] - G[:, :, None, :, :]
        decay = jnp.exp(jnp.where(tri[:, :, None], diff, -jnp.inf))
        A = jnp.einsum('bhtk,bhsk,bhtsk->bhts', qb, kb, decay)
        o = jnp.einsum('bhts,bhsv->bhtv', A, vb) + jnp.einsum('bhtk,bhkv->bhtv', qb * jnp.exp(G), S)
        G_last = G[:, :, -1:, :]
        S_new = jnp.exp(G_last[:, :, 0, :])[..., None] * S + jnp.einsum(
            'bhsk,bhsv->bhkv', kb * jnp.exp(G_last - G), vb)
        return S_new, o

    S, o = lax.scan(step, S0, (to_blocks(q), to_blocks(k), to_blocks(v), to_blocks(logf)))
    o = o.transpose(1, 0, 3, 2, 4).reshape(B, T, H, V)
    return o, S


def hgrn_mixer(h, S0, w_in, lb, norm_g, w_out):
    B, T, _ = h.shape
    proj = h @ w_in
    q = proj[..., :HG_FDIM].astype(F32)
    f = proj[..., HG_FDIM:2 * HG_FDIM].astype(F32)
    i = proj[..., 2 * HG_FDIM:2 * HG_FDIM + D_MODEL].astype(F32)
    g = proj[..., 2 * HG_FDIM + D_MODEL:].astype(F32)
    fg = lb + (1.0 - lb) * jax.nn.sigmoid(f)
    q = jax.nn.silu(q).reshape(B, T, HG_HEADS, HG_KDIM)
    k = (1.0 - fg).reshape(B, T, HG_HEADS, HG_KDIM)
    logf = jnp.log(fg).reshape(B, T, HG_HEADS, HG_KDIM)
    v = i.reshape(B, T, HG_HEADS, HG_VDIM)
    block = REC_BLOCK if T % REC_BLOCK == 0 else T
    o, S = hgrn_recurrence(S0.astype(F32), q, k, v, logf, block)
    o = rms_norm(o, norm_g) * jax.nn.silu(g).reshape(B, T, HG_HEADS, HG_VDIM)
    out = o.reshape(B, T, D_MODEL).astype(h.dtype) @ w_out
    return out, S.astype(h.dtype)


def dsa_attention(q, k_all, v_all, qi, ki_all, w_idx, q_pos, n_sel):
    B, T = q.shape[:2]
    L = k_all.shape[1]
    blk = Q_BLOCK if T % Q_BLOCK == 0 else T
    n = T // blk
    key_chunk = jnp.arange(L) // CHUNK
    scale = HEAD_DIM ** -0.5

    def blocks(a):
        return a.reshape((B, n, blk) + a.shape[2:]).swapaxes(0, 1)

    def one_block(xs):
        qb, qib, wb, pb = xs
        admissible = key_chunk[None, :] <= (pb // CHUNK)[:, None]
        dots = jnp.einsum('bthd,bsd->bths', qib, ki_all, preferred_element_type=F32)
        score = jnp.einsum('bth,bths->bts', wb.astype(F32), jax.nn.relu(dots))
        score = jnp.where(admissible[None], score, -jnp.inf)
        sel_val, sel_idx = lax.top_k(score, n_sel)
        valid = sel_val > -jnp.inf
        kg = jax.vmap(lambda kb, ib: kb[ib])(k_all, sel_idx)
        vg = jax.vmap(lambda vb, ib: vb[ib])(v_all, sel_idx)
        qg = qb.reshape(B, blk, N_KV_HEADS, N_GROUPS, HEAD_DIM)
        logits = jnp.einsum('btkgd,btskd->btkgs', qg, kg, preferred_element_type=F32) * scale
        logits = jnp.where(valid[:, :, None, None, :], logits, -jnp.inf)
        p = jax.nn.softmax(logits, axis=-1).astype(vg.dtype)
        o = jnp.einsum('btkgs,btskd->btkgd', p, vg)
        return o.reshape(B, blk, N_HEADS * HEAD_DIM)

    out = lax.map(one_block, (blocks(q), blocks(qi), blocks(w_idx), q_pos.reshape(n, blk)))
    return out.swapaxes(0, 1).reshape(B, T, N_HEADS * HEAD_DIM)


def dsa_mixer(h, k_cache, v_cache, ki_cache, w_in, kln_w, kln_b, w_out):
    B, T, _ = h.shape
    past = k_cache.shape[1]
    pos = past + jnp.arange(T)
    o1 = N_HEADS * HEAD_DIM
    o2 = o1 + N_KV_HEADS * HEAD_DIM
    o3 = o2 + N_KV_HEADS * HEAD_DIM
    o4 = o3 + IDX_HEADS * IDX_DIM
    o5 = o4 + IDX_DIM
    proj = h @ w_in
    q = partial_rope(proj[..., :o1].reshape(B, T, N_HEADS, HEAD_DIM), pos)
    k = partial_rope(proj[..., o1:o2].reshape(B, T, N_KV_HEADS, HEAD_DIM), pos)
    v = proj[..., o2:o3].reshape(B, T, N_KV_HEADS, HEAD_DIM)
    qi = partial_rope(proj[..., o3:o4].reshape(B, T, IDX_HEADS, IDX_DIM), pos)
    ki = layer_norm(proj[..., o4:o5], kln_w, kln_b)
    ki = partial_rope(ki[:, :, None, :], pos)[:, :, 0, :]
    wi = proj[..., o5:] * (IDX_HEADS * IDX_DIM) ** -0.5
    k_all = jnp.concatenate([k_cache.astype(k.dtype), k], axis=1)
    v_all = jnp.concatenate([v_cache.astype(v.dtype), v], axis=1)
    ki_all = jnp.concatenate([ki_cache.astype(ki.dtype), ki], axis=1)
    n_sel = min(TOPK_MAX, (past + T) // 4)
    o = dsa_attention(q, k_all, v_all, qi, ki_all, wi, pos, n_sel)
    return o @ w_out, k, v, ki


def conv_ffn(h, state, w_in, conv_w, conv_b, w_down):
    T = h.shape[1]
    gu = h @ w_in
    a, u = gu[..., :D_FF], gu[..., D_FF:]
    a_ext = jnp.concatenate([state.astype(a.dtype), a], axis=1)
    c = conv_b
    for j in range(CONV_W):
        c = c + a_ext[:, j:j + T] * conv_w[j]
    y = (jax.nn.silu(c) * u) @ w_down
    return y, a_ext[:, -(CONV_W - 1):]


def trunk(x, hg_state, conv_state, k_cache, v_cache, ki_cache,
          norm_mix, norm_ffn, norm_final, hg_w_in, hg_lb, hg_norm, hg_w_out,
          ds_w_in, ds_kln_w, ds_kln_b, ds_w_out, ffn_w_in, ffn_conv_w, ffn_conv_b, ffn_w_down):
    lower_bounds = jnp.cumsum(jax.nn.softmax(hg_lb.astype(F32), axis=0), axis=0)
    new_hg, new_conv, new_k, new_v, new_ki = [], [], [], [], []
    for i in range(DEPTH):
        j = i // N_MIXERS
        h = rms_norm(x, norm_mix[i])
        if i % N_MIXERS == 0:
            out, S = hgrn_mixer(h, hg_state[j], hg_w_in[j], lower_bounds[j], hg_norm[j], hg_w_out[j])
            new_hg.append(S)
        else:
            out, k, v, ki = dsa_mixer(h, k_cache[j], v_cache[j], ki_cache[j],
                                      ds_w_in[j], ds_kln_w[j], ds_kln_b[j], ds_w_out[j])
            new_k.append(k)
            new_v.append(v)
            new_ki.append(ki)
        x = x + out
        h = rms_norm(x, norm_ffn[i])
        out, cs = conv_ffn(h, conv_state[i], ffn_w_in[i], ffn_conv_w[i], ffn_conv_b[i], ffn_w_down[i])
        new_conv.append(cs)
        x = x + out
    y = rms_norm(x, norm_final)
    return (y, jnp.stack(new_hg), jnp.stack(new_conv),
            jnp.stack(new_k), jnp.stack(new_v), jnp.stack(new_ki))


def setup_inputs(seed: int = 0) -> dict:
    key = jax.random.key(seed)
    ks = jax.random.split(key, 22)

    def nrm(k, shape, scale):
        return jax.random.normal(k, shape, F32) * scale

    return {
        'x_prompt': nrm(ks[0], (BATCH, SEQ, D_MODEL), 1.0),
        'x_sample': nrm(ks[1], (DEC_BATCH, DEC_SEQ, D_MODEL), 1.0),
        'cache_k': nrm(ks[2], (N_DSA_LAYERS, DEC_BATCH, PAST_LEN, N_KV_HEADS, HEAD_DIM), 1.0),
        'cache_v': nrm(ks[3], (N_DSA_LAYERS, DEC_BATCH, PAST_LEN, N_KV_HEADS, HEAD_DIM), 1.0),
        'cache_kidx': nrm(ks[4], (N_DSA_LAYERS, DEC_BATCH, PAST_LEN, IDX_DIM), 1.0),
        'state_hgrn': nrm(ks[5], (N_HGRN_LAYERS, DEC_BATCH, HG_HEADS, HG_KDIM, HG_VDIM), 0.5),
        'state_conv': nrm(ks[6], (DEPTH, DEC_BATCH, CONV_W - 1, D_FF), 1.0),
        'norm_mix': 1.0 + nrm(ks[7], (DEPTH, D_MODEL), 0.02),
        'norm_ffn': 1.0 + nrm(ks[8], (DEPTH, D_MODEL), 0.02),
        'norm_final': 1.0 + nrm(ks[9], (D_MODEL,), 0.02),
        'hg_w_in': nrm(ks[10], (N_HGRN_LAYERS, D_MODEL, HG_IN), D_MODEL ** -0.5),
        'hg_lb': nrm(ks[11], (N_HGRN_LAYERS + 1, HG_FDIM), 0.1),
        'hg_norm': 1.0 + nrm(ks[12], (N_HGRN_LAYERS, HG_VDIM), 0.02),
        'hg_w_out': nrm(ks[13], (N_HGRN_LAYERS, D_MODEL, D_MODEL), D_MODEL ** -0.5),
        'ds_w_in': nrm(ks[14], (N_DSA_LAYERS, D_MODEL, DSA_IN), D_MODEL ** -0.5),
        'ds_kln_w': 1.0 + nrm(ks[15], (N_DSA_LAYERS, IDX_DIM), 0.02),
        'ds_kln_b': nrm(ks[16], (N_DSA_LAYERS, IDX_DIM), 0.02),
        'ds_w_out': nrm(ks[17], (N_DSA_LAYERS, N_HEADS * HEAD_DIM, D_MODEL), (N_HEADS * HEAD_DIM) ** -0.5),
        'ffn_w_in': nrm(ks[18], (DEPTH, D_MODEL, 2 * D_FF), D_MODEL ** -0.5),
        'ffn_conv_w': nrm(ks[19], (DEPTH, CONV_W, D_FF), CONV_W ** -0.5),
        'ffn_conv_b': nrm(ks[20], (DEPTH, D_FF), 0.02),
        'ffn_w_down': nrm(ks[21], (DEPTH, D_FF, D_MODEL), D_FF ** -0.5),
    }


def reference(x_prompt, x_sample, cache_k, cache_v, cache_kidx, state_hgrn, state_conv,
              norm_mix, norm_ffn, norm_final, hg_w_in, hg_lb, hg_norm, hg_w_out,
              ds_w_in, ds_kln_w, ds_kln_b, ds_w_out, ffn_w_in, ffn_conv_w, ffn_conv_b, ffn_w_down):
    run = functools.partial(
        trunk, norm_mix=norm_mix, norm_ffn=norm_ffn, norm_final=norm_final,
        hg_w_in=hg_w_in, hg_lb=hg_lb, hg_norm=hg_norm, hg_w_out=hg_w_out,
        ds_w_in=ds_w_in, ds_kln_w=ds_kln_w, ds_kln_b=ds_kln_b, ds_w_out=ds_w_out,
        ffn_w_in=ffn_w_in, ffn_conv_w=ffn_conv_w, ffn_conv_b=ffn_conv_b, ffn_w_down=ffn_w_down)
    b = x_prompt.shape[0]
    dt = x_prompt.dtype
    hg0 = jnp.zeros((N_HGRN_LAYERS, b, HG_HEADS, HG_KDIM, HG_VDIM), dt)
    conv0 = jnp.zeros((DEPTH, b, CONV_W - 1, D_FF), dt)
    k0 = jnp.zeros((N_DSA_LAYERS, b, 0, N_KV_HEADS, HEAD_DIM), dt)
    v0 = jnp.zeros((N_DSA_LAYERS, b, 0, N_KV_HEADS, HEAD_DIM), dt)
    ki0 = jnp.zeros((N_DSA_LAYERS, b, 0, IDX_DIM), dt)
    y_prompt, hg_p, conv_p, k_p, v_p, ki_p = run(x_prompt, hg0, conv0, k0, v0, ki0)
    y_sample, hg_s, conv_s, k_s, v_s, ki_s = run(x_sample, state_hgrn, state_conv, cache_k, cache_v, cache_kidx)
    return (y_prompt, y_sample, k_p, v_p, ki_p, hg_p, conv_p, k_s, v_s, ki_s, hg_s, conv_s)
```

```python
import functools

import jax
import jax.numpy as jnp
from jax import lax
from jax.experimental import pallas as pl
from jax.experimental.pallas import tpu as pltpu

F32 = jnp.float32
BF16 = jnp.bfloat16
I32 = jnp.int32

CHUNK = 64
EPS = 1e-6
HG_HEADS = 8
HG_KDIM = 128
N_HEADS = 16
HEAD_DIM = 64
N_KV_HEADS = 4
IDX_HEADS = 8
IDX_DIM = 64
TOPK_MAX = 256
ROPE_THETA = 500000.0
CONV_W = 3

LANES = 128
VMEM_LIMIT = 56 * 1024 * 1024
EXP_CLAMP = 80.0
KEY_NEG_INF = -2139095041
INT_MIN = -2147483648


def _cparams(n_axes):
    return pltpu.CompilerParams(dimension_semantics=("arbitrary",) * n_axes,
                                vmem_limit_bytes=VMEM_LIMIT)


def _const_spec(shape):
    nd = len(shape)
    return pl.BlockSpec(shape, lambda *_: (0,) * nd, pipeline_mode=pl.Buffered(1))


def _rms(x, g):
    return x * lax.rsqrt(jnp.mean(x * x, axis=-1, keepdims=True) + EPS) * g


def _silu(x):
    return x * jax.nn.sigmoid(x)


def _dot(a, b):
    return jnp.dot(a, b, preferred_element_type=F32)


def _dot_nt(a, b):
    return lax.dot_general(a, b, (((1,), (1,)), ((), ())), preferred_element_type=F32)


def _dot_tn(a, b):
    return lax.dot_general(a, b, (((0,), (0,)), ((), ())), preferred_element_type=F32)


def _hgrn_kernel(x_ref, s0_ref, g_ref, win_ref, lb_ref, hn_ref, wout_ref, o_ref, s_ref,
                 q_s, k_s, g_s, v_s, gate_s, on_s, st_s, *, bb, tt, blk, sub):
    j = pl.program_id(1)
    rows = bb * tt
    d = x_ref.shape[-1]
    fdim = HG_HEADS * HG_KDIM
    n_sub = blk // sub

    @pl.when(j == 0)
    def _():
        for bi in range(bb):
            for hh in range(HG_HEADS):
                st_s[bi * HG_HEADS + hh] = s0_ref[bi, hh].T

    x = x_ref[...].reshape(rows, d)
    h = _rms(x, g_ref[...]).astype(BF16)
    q_s[...] = _silu(_dot(h, win_ref[:, 0:fdim]))
    lb = lb_ref[...]
    fg = lb + (1.0 - lb) * jax.nn.sigmoid(_dot(h, win_ref[:, fdim:2 * fdim]))
    k_s[...] = 1.0 - fg
    g_s[...] = jnp.log(fg)
    v_s[...] = _dot(h, win_ref[:, 2 * fdim:2 * fdim + d])
    gate_s[...] = _silu(_dot(h, win_ref[:, 2 * fdim + d:]))

    r_i = lax.broadcasted_iota(I32, (blk, blk), 0)
    c_i = lax.broadcasted_iota(I32, (blk, blk), 1)
    causal = c_i <= r_i
    tril = jnp.where(causal, 1.0, 0.0).astype(BF16)
    row_id = lax.broadcasted_iota(I32, (blk, 1), 0)
    hn = hn_ref[...]
    blocks_per_stream = tt // blk

    def block_body(idx, carry):
        r0 = pl.multiple_of(idx * blk, blk)
        bi = idx // blocks_per_stream
        lg = g_s[pl.ds(r0, blk), :]
        lg_hi = lg.astype(BF16)
        rem = lg - lg_hi.astype(F32)
        lg_mid = rem.astype(BF16)
        lg_lo = (rem - lg_mid.astype(F32)).astype(BF16)
        g_all = _dot(tril, lg_hi) + _dot(tril, lg_mid) + _dot(tril, lg_lo)
        for hh in range(HG_HEADS):
            sl = slice(hh * HG_KDIM, (hh + 1) * HG_KDIM)
            gc = g_all[:, sl]
            q = q_s[pl.ds(r0, blk), sl]
            kk = k_s[pl.ds(r0, blk), sl]
            v = v_s[pl.ds(r0, blk), sl]
            v16 = v.astype(BF16)
            refs = [jnp.zeros((1, HG_KDIM), F32)] + [gc[s * sub - 1:s * sub, :] for s in range(1, n_sub)]
            g_ref_rows = jnp.concatenate([jnp.broadcast_to(r, (sub, HG_KDIM)) for r in refs], axis=0)
            qg = q * jnp.exp(gc - g_ref_rows)
            q_parts, k_parts = [], []
            for s in range(n_sub):
                in_sub = (row_id >= s * sub) & (row_id < (s + 1) * sub)
                q_parts.append(jnp.where(in_sub, qg, 0.0))
                kdec = kk * jnp.exp(jnp.minimum(refs[s] - gc, EXP_CLAMP))
                k_parts.append(jnp.where(row_id < (s + 1) * sub, kdec, 0.0))
            qcat = jnp.concatenate(q_parts, axis=1).astype(BF16)
            kcat = jnp.concatenate(k_parts, axis=1).astype(BF16)
            a = jnp.where(causal, _dot_nt(qcat, kcat), 0.0)
            st = st_s[bi * HG_HEADS + hh]
            o = _dot(a.astype(BF16), v16) + _dot_nt((q * jnp.exp(gc)).astype(BF16), st.astype(BF16))
            g_last = gc[blk - 1:blk, :]
            kd = (kk * jnp.exp(g_last - gc)).astype(BF16)
            st_s[bi * HG_HEADS + hh] = st * jnp.exp(g_last) + _dot_tn(v16, kd)
            on = _rms(o, hn) * gate_s[pl.ds(r0, blk), sl]
            on_s[pl.ds(r0, blk), sl] = on.astype(BF16)
        return carry

    lax.fori_loop(0, rows // blk, block_body, 0)

    out = _dot(on_s[...], wout_ref[...]) + x
    o_ref[...] = out.reshape(bb, tt, d)

    @pl.when(j == pl.num_programs(1) - 1)
    def _():
        for bi in range(bb):
            for hh in range(HG_HEADS):
                s_ref[bi, hh] = st_s[bi * HG_HEADS + hh].T


def _hgrn_mixer(x, s0, norm_g, w_in, lb, hn, w_out, *, bb, tt, blk, sub):
    b, t, d = x.shape
    fdim = HG_HEADS * HG_KDIM
    rows = bb * tt
    kern = functools.partial(_hgrn_kernel, bb=bb, tt=tt, blk=blk, sub=sub)
    return pl.pallas_call(
        kern,
        grid=(b // bb, t // tt),
        in_specs=[
            pl.BlockSpec((bb, tt, d), lambda i, j: (i, j, 0)),
            pl.BlockSpec((bb, HG_HEADS, HG_KDIM, d // HG_HEADS), lambda i, j: (i, 0, 0, 0)),
            _const_spec((1, d)),
            _const_spec(w_in.shape),
            _const_spec((1, fdim)),
            _const_spec((1, d // HG_HEADS)),
            _const_spec(w_out.shape),
        ],
        out_specs=[
            pl.BlockSpec((bb, tt, d), lambda i, j: (i, j, 0)),
            pl.BlockSpec((bb, HG_HEADS, HG_KDIM, d // HG_HEADS), lambda i, j: (i, 0, 0, 0)),
        ],
        out_shape=[jax.ShapeDtypeStruct(x.shape, F32), jax.ShapeDtypeStruct(s0.shape, F32)],
        scratch_shapes=[
            pltpu.VMEM((rows, fdim), F32),
            pltpu.VMEM((rows, fdim), F32),
            pltpu.VMEM((rows, fdim), F32),
            pltpu.VMEM((rows, d), F32),
            pltpu.VMEM((rows, d), F32),
            pltpu.VMEM((rows, d), BF16),
            pltpu.VMEM((bb * HG_HEADS, d // HG_HEADS, HG_KDIM), F32),
        ],
        compiler_params=_cparams(2),
        name="hgrn_mixer",
    )(x, s0, norm_g.reshape(1, d), w_in, lb.reshape(1, fdim), hn.reshape(1, -1), w_out)


CONV_HEAD = 8


def _ffn_kernel(x_ref, cs_ref, g_ref, win_ref, cw_ref, cb_ref, wdn_ref, fg_ref, o_ref, ns_ref,
                a_s, *, bb, tt, final_norm):
    j = pl.program_id(1)
    rows = bb * tt
    d = x_ref.shape[-1]
    dff = cw_ref.shape[-1]
    hist = CONV_W - 1

    @pl.when(j == 0)
    def _():
        a_s[:, CONV_HEAD - hist:CONV_HEAD, :] = cs_ref[...]

    x = x_ref[...].reshape(rows, d)
    h = _rms(x, g_ref[...]).astype(BF16)
    a = _dot(h, win_ref[:, 0:dff])
    u = _dot(h, win_ref[:, dff:])
    a_s[:, CONV_HEAD:CONV_HEAD + tt, :] = a.reshape(bb, tt, dff)
    c = jnp.broadcast_to(cb_ref[...].reshape(1, 1, dff), (bb, tt, dff))
    for w in range(CONV_W):
        lo = CONV_HEAD - hist + w
        c = c + a_s[:, lo:lo + tt, :] * cw_ref[w:w + 1, :].reshape(1, 1, dff)
    new_state = a_s[:, CONV_HEAD + tt - hist:CONV_HEAD + tt, :]
    a_s[:, CONV_HEAD - hist:CONV_HEAD, :] = new_state
    ns_ref[...] = new_state
    act = (_silu(c).reshape(rows, dff) * u).astype(BF16)
    y = _dot(act, wdn_ref[...]) + x
    if final_norm:
        y = _rms(y, fg_ref[...])
    o_ref[...] = y.reshape(bb, tt, d)


def _conv_ffn(x, conv_state, norm_g, w_in, conv_w, conv_b, w_down, final_g, *, bb, tt, final_norm):
    b, t, d = x.shape
    dff = conv_w.shape[-1]
    kern = functools.partial(_ffn_kernel, bb=bb, tt=tt, final_norm=final_norm)
    return pl.pallas_call(
        kern,
        grid=(b // bb, t // tt),
        in_specs=[
            pl.BlockSpec((bb, tt, d), lambda i, j: (i, j, 0)),
            pl.BlockSpec((bb, CONV_W - 1, dff), lambda i, j: (i, 0, 0)),
            _const_spec((1, d)),
            _const_spec(w_in.shape),
            _const_spec(conv_w.shape),
            _const_spec((1, dff)),
            _const_spec(w_down.shape),
            _const_spec((1, d)),
        ],
        out_specs=[
            pl.BlockSpec((bb, tt, d), lambda i, j: (i, j, 0)),
            pl.BlockSpec((bb, CONV_W - 1, dff), lambda i, j: (i, 0, 0)),
        ],
        out_shape=[jax.ShapeDtypeStruct(x.shape, F32), jax.ShapeDtypeStruct(conv_state.shape, F32)],
        scratch_shapes=[pltpu.VMEM((bb, CONV_HEAD + tt, dff), F32)],
        compiler_params=_cparams(2),
        name="conv_ffn",
    )(x, conv_state, norm_g.reshape(1, d), w_in, conv_w, conv_b.reshape(1, dff), w_down,
      final_g.reshape(1, d))


def _rope_tables(pos):
    rot = HEAD_DIM // 4
    half = rot // 2
    inv_freq = ROPE_THETA ** (-jnp.arange(half, dtype=F32) / half)
    ang = pos.astype(F32)[:, None] * inv_freq[None, :]
    cos, sin = jnp.cos(ang), jnp.sin(ang)
    t = pos.shape[0]
    pad = HEAD_DIM - rot
    one = jnp.ones((t, pad), F32)
    zero_h = jnp.zeros((t, half), F32)
    zero_p = jnp.zeros((t, pad), F32)
    c64 = jnp.concatenate([cos, cos, one], axis=1)
    s1_64 = jnp.concatenate([zero_h, sin, zero_p], axis=1)
    s2_64 = jnp.concatenate([-sin, zero_h, zero_p], axis=1)
    ident = jnp.ones((t, HEAD_DIM), F32)
    zero64 = jnp.zeros((t, HEAD_DIM), F32)
    return jnp.stack([
        jnp.concatenate([c64, c64], axis=1), jnp.concatenate([s1_64, s1_64], axis=1),
        jnp.concatenate([s2_64, s2_64], axis=1),
        jnp.concatenate([c64, ident], axis=1), jnp.concatenate([s1_64, zero64], axis=1),
        jnp.concatenate([s2_64, zero64], axis=1)])


def _dsa_proj_kernel(x_ref, g_ref, w_ref, tab_ref, lnw_ref, lnb_ref,
                     q_ref, k_ref, v_ref, qi_ref, kw_ref, *, bb, tt):
    rows = bb * tt
    d = x_ref.shape[-1]
    half = HEAD_DIM // 8
    o1 = N_HEADS * HEAD_DIM
    o2 = o1 + N_KV_HEADS * HEAD_DIM
    o3 = o2 + N_KV_HEADS * HEAD_DIM
    o4 = o3 + IDX_HEADS * IDX_DIM

    def tab(i):
        t = tab_ref[i]
        return jnp.broadcast_to(t[None], (bb, tt, LANES)).reshape(rows, LANES)

    def rope(y, base):
        cos, s1, s2 = tab(base), tab(base + 1), tab(base + 2)
        tiles = []
        for m in range(y.shape[1] // LANES):
            yt = y[:, m * LANES:(m + 1) * LANES]
            tiles.append(yt * cos + pltpu.roll(yt, half, 1) * s1 + pltpu.roll(yt, LANES - half, 1) * s2)
        return tiles[0] if len(tiles) == 1 else jnp.concatenate(tiles, axis=1)

    x = x_ref[...].reshape(rows, d)
    h = _rms(x, g_ref[...]).astype(BF16)
    q = rope(_dot(h, w_ref[:, 0:o1]), 0) * (HEAD_DIM ** -0.5)
    q_ref[...] = q.astype(BF16).reshape(bb, tt, o1)
    k_ref[...] = rope(_dot(h, w_ref[:, o1:o2]), 0).reshape(bb, tt, o2 - o1)
    v_ref[...] = _dot(h, w_ref[:, o2:o3]).reshape(bb, tt, o3 - o2)
    qi_ref[...] = rope(_dot(h, w_ref[:, o3:o4]), 0).astype(BF16).reshape(bb, tt, o4 - o3)
    t = _dot(h, w_ref[:, o4:o4 + LANES])
    lane = lax.broadcasted_iota(I32, (1, LANES), 1)
    is_ki = lane < IDX_DIM
    mu = jnp.sum(jnp.where(is_ki, t, 0.0), axis=-1, keepdims=True) / IDX_DIM
    cen = jnp.where(is_ki, t - mu, 0.0)
    var = jnp.sum(cen * cen, axis=-1, keepdims=True) / IDX_DIM
    ki = rope(cen * lax.rsqrt(var + EPS) * lnw_ref[...] + lnb_ref[...], 3)
    wi = t * ((IDX_HEADS * IDX_DIM) ** -0.5)
    kw = jnp.where(is_ki, ki, jnp.where(lane < IDX_DIM + IDX_HEADS, wi, 0.0))
    kw_ref[...] = kw.reshape(bb, tt, LANES)


def _dsa_proj(x, norm_g, w_pad, tables, ln_w, ln_b, *, bb, tt):
    b, t, d = x.shape
    n_q = N_HEADS * HEAD_DIM
    n_kv = N_KV_HEADS * HEAD_DIM
    n_qi = IDX_HEADS * IDX_DIM
    pad = jnp.zeros((LANES - IDX_DIM,), F32)
    lnw = jnp.concatenate([ln_w, pad]).reshape(1, LANES)
    lnb = jnp.concatenate([ln_b, pad]).reshape(1, LANES)
    kern = functools.partial(_dsa_proj_kernel, bb=bb, tt=tt)

    def row_spec(n):
        return pl.BlockSpec((bb, tt, n), lambda i, j: (i, j, 0))

    return pl.pallas_call(
        kern,
        grid=(b // bb, t // tt),
        in_specs=[
            row_spec(d),
            _const_spec((1, d)),
            _const_spec(w_pad.shape),
            pl.BlockSpec((6, tt, LANES), lambda i, j: (0, j, 0)),
            _const_spec((1, LANES)),
            _const_spec((1, LANES)),
        ],
        out_specs=[row_spec(n_q), row_spec(n_kv), row_spec(n_kv), row_spec(n_qi), row_spec(LANES)],
        out_shape=[
            jax.ShapeDtypeStruct((b, t, n_q), BF16),
            jax.ShapeDtypeStruct((b, t, n_kv), F32),
            jax.ShapeDtypeStruct((b, t, n_kv), F32),
            jax.ShapeDtypeStruct((b, t, n_qi), BF16),
            jax.ShapeDtypeStruct((b, t, LANES), F32),
        ],
        compiler_params=_cparams(2),
        name="dsa_proj",
    )(x, norm_g.reshape(1, d), w_pad, tables, lnw, lnb)


def _order_key(score):
    score = jnp.where(score == 0.0, 0.0, score)
    bits = lax.bitcast_convert_type(score, I32)
    return bits ^ ((bits >> 31) & 0x7FFFFFFF)


def _fill_padded(dst_e, dst_o, src, r0, nrows):
    lane = lax.broadcasted_iota(I32, (1, LANES), 1)
    lo = lane < HEAD_DIM
    for m in range(src.shape[1] // LANES):
        tile = src[:, m * LANES:(m + 1) * LANES]
        rolled = pltpu.roll(tile, HEAD_DIM, 1)
        a_lo = jnp.where(lo, tile, 0.0).astype(BF16)
        b_hi = jnp.where(lo, 0.0, tile).astype(BF16)
        b_lo = jnp.where(lo, rolled, 0.0).astype(BF16)
        a_hi = jnp.where(lo, 0.0, rolled).astype(BF16)
        c0, c1 = 2 * m * LANES, (2 * m + 1) * LANES
        dst_e[pl.ds(r0, nrows), c0:c0 + LANES] = a_lo
        dst_o[pl.ds(r0, nrows), c0:c0 + LANES] = a_hi
        dst_e[pl.ds(r0, nrows), c1:c1 + LANES] = b_lo
        dst_o[pl.ds(r0, nrows), c1:c1 + LANES] = b_hi


def _attn_core(q_ref, qi_ref, wq_ref, x_ref, wout_ref, o_ref, segs, *, tq, q_pos0, n_sel):
    qrow = lax.broadcasted_iota(I32, (tq, 1), 0)
    q_chunk = (q_pos0 + qrow) // CHUNK
    wq = wq_ref[...].reshape(tq, LANES)
    wi = [wq[:, IDX_DIM + hh:IDX_DIM + hh + 1] for hh in range(IDX_HEADS)]
    qi = qi_ref[...].reshape(tq, IDX_HEADS * IDX_DIM)

    for sg in segs:
        for c0 in range(0, sg["len"], sg["chunk"]):
            cw = min(sg["chunk"], sg["len"] - c0)
            sc = jnp.zeros((tq, cw), F32)
            for m in range(IDX_HEADS // 2):
                qt = qi[:, m * LANES:(m + 1) * LANES]
                de = _dot_nt(qt, sg["kie"][c0:c0 + cw, 0:LANES])
                do = _dot_nt(qt, sg["kio"][c0:c0 + cw, 0:LANES])
                sc = sc + wi[2 * m] * jnp.maximum(de, 0.0) + wi[2 * m + 1] * jnp.maximum(do, 0.0)
            kpos = sg["pos0"] + c0 + lax.broadcasted_iota(I32, (1, cw), 1)
            adm = (kpos // CHUNK) <= q_chunk
            sg["key_s"][:, c0:c0 + cw] = jnp.where(adm, _order_key(sc), KEY_NEG_INF)

    def count(pred):
        tot = jnp.zeros((tq, 1), I32)
        for sg in segs:
            keys = sg["key_s"][:, 0:sg["len"]]
            tot = tot + jnp.sum(pred(keys).astype(I32), axis=-1, keepdims=True)
        return tot

    thr = jnp.where(count(lambda k: k >= 0) >= n_sel, 0, INT_MIN).astype(I32)

    def bit_step(i, thr):
        cand = thr | lax.shift_left(jnp.int32(1), 30 - i)
        return jnp.where(count(lambda k: k >= cand) >= n_sel, cand, thr)

    thr = lax.fori_loop(0, 31, bit_step, thr)
    n_gt = count(lambda k: k > thr)
    n_ge = count(lambda k: k >= thr)
    ties_wanted = n_sel - n_gt

    for sg in segs:
        keys = sg["key_s"][:, 0:sg["len"]]
        sel = (keys >= thr) & (keys > KEY_NEG_INF)
        sg["bias_s"][:, 0:sg["len"]] = jnp.where(sel, 0.0, -jnp.inf)

    has_excess = jnp.max(jnp.where(thr > KEY_NEG_INF, n_ge - n_sel, 0)) > 0

    @pl.when(has_excess)
    def _():
        seen = jnp.zeros((tq, 1), F32)
        wanted = ties_wanted.astype(F32)
        for sg in segs:
            for c0 in range(0, sg["len"], LANES):
                cw = min(LANES, sg["len"] - c0)
                keys = sg["key_s"][:, c0:c0 + cw]
                tie = keys == thr
                r_i = lax.broadcasted_iota(I32, (cw, cw), 0)
                c_i = lax.broadcasted_iota(I32, (cw, cw), 1)
                upper = jnp.where(r_i <= c_i, 1.0, 0.0).astype(BF16)
                tie_f = jnp.where(tie, 1.0, 0.0)
                rank = seen + _dot(tie_f.astype(BF16), upper)
                sel = ((keys > thr) | (tie & (rank <= wanted))) & (keys > KEY_NEG_INF)
                sg["bias_s"][:, c0:c0 + cw] = jnp.where(sel, 0.0, -jnp.inf)
                seen = seen + jnp.sum(tie_f, axis=-1, keepdims=True)

    q = q_ref[...].reshape(tq, N_HEADS * HEAD_DIM)
    tiles = []
    for m in range(N_HEADS // 2):
        g = (2 * m) // (N_HEADS // N_KV_HEADS)
        gs = slice(g * LANES, (g + 1) * LANES)
        qt = q[:, m * LANES:(m + 1) * LANES]
        acc = jnp.zeros((tq, LANES), F32)
        for kname, vname in (("ke", "ve"), ("ko", "vo")):
            logits = [_dot_nt(qt, sg[kname][0:sg["len"], gs]) + sg["bias_s"][:, 0:sg["len"]] for sg in segs]
            mx = functools.reduce(jnp.maximum, [jnp.max(l, axis=-1, keepdims=True) for l in logits])
            ps = [jnp.exp(l - mx) for l in logits]
            den = functools.reduce(jnp.add, [jnp.sum(p, axis=-1, keepdims=True) for p in ps])
            pv = functools.reduce(jnp.add, [_dot(p.astype(BF16), sg[vname][0:sg["len"], gs])
                                            for p, sg in zip(ps, segs)])
            acc = acc + pv / den
        tiles.append(acc.astype(BF16))
    o = jnp.concatenate(tiles, axis=1)
    d = x_ref.shape[-1]
    o_ref[...] = (_dot(o, wout_ref[...]) + x_ref[...].reshape(tq, d)).reshape(o_ref.shape)


def _attn_prompt_kernel(q_ref, qi_ref, wq_ref, k_ref, v_ref, kw_ref, x_ref, wout_ref, o_ref,
                        ke, ko, ve, vo, kie, kio, key_s, bias_s, *, tq, t_len, kchunk, n_sel):
    j = pl.program_id(1)

    @pl.when(j == 0)
    def _():
        for r0 in range(0, t_len, kchunk):
            nr = min(kchunk, t_len - r0)
            _fill_padded(ke, ko, k_ref[0, r0:r0 + nr, :], r0, nr)
            _fill_padded(ve, vo, v_ref[0, r0:r0 + nr, :], r0, nr)
            _fill_padded(kie, kio, kw_ref[0, r0:r0 + nr, :], r0, nr)

    last_q = (j + 1) * tq - 1
    need = (last_q // CHUNK + 1) * CHUNK
    n_chunks_needed = (need + kchunk - 1) // kchunk
    for nc in range(1, (t_len + kchunk - 1) // kchunk + 1):
        lk = min(nc * kchunk, t_len)

        @pl.when(n_chunks_needed == nc)
        def _(lk=lk):
            seg = dict(ke=ke, ko=ko, ve=ve, vo=vo, kie=kie, kio=kio, key_s=key_s, bias_s=bias_s,
                       len=lk, pos0=0, chunk=kchunk)
            _attn_core(q_ref, qi_ref, wq_ref, x_ref, wout_ref, o_ref, [seg],
                       tq=tq, q_pos0=j * tq, n_sel=n_sel)


def _dsa_attn_prompt(q, qi, kw, k, v, x, w_out, *, tq, kchunk):
    b, t, d = x.shape
    n_sel = min(TOPK_MAX, t // 4)
    n_kv = N_KV_HEADS * HEAD_DIM
    kern = functools.partial(_attn_prompt_kernel, tq=tq, t_len=t, kchunk=kchunk, n_sel=n_sel)

    def tile_spec(n):
        return pl.BlockSpec((1, tq, n), lambda i, j: (i, j, 0))

    def full_spec(n):
        return pl.BlockSpec((1, t, n), lambda i, j: (i, 0, 0))

    return pl.pallas_call(
        kern,
        grid=(b, t // tq),
        in_specs=[tile_spec(q.shape[-1]), tile_spec(qi.shape[-1]), tile_spec(LANES),
                  full_spec(n_kv), full_spec(n_kv), full_spec(LANES),
                  tile_spec(d), _const_spec(w_out.shape)],
        out_specs=tile_spec(d),
        out_shape=jax.ShapeDtypeStruct(x.shape, F32),
        scratch_shapes=[
            pltpu.VMEM((t, 2 * n_kv), BF16), pltpu.VMEM((t, 2 * n_kv), BF16),
            pltpu.VMEM((t, 2 * n_kv), BF16), pltpu.VMEM((t, 2 * n_kv), BF16),
            pltpu.VMEM((t, 2 * LANES), BF16), pltpu.VMEM((t, 2 * LANES), BF16),
            pltpu.VMEM((tq, t), I32), pltpu.VMEM((tq, t), F32),
        ],
        compiler_params=_cparams(2),
        name="dsa_attn_prompt",
    )(q, qi, kw, k, v, kw, x, w_out)


def _attn_sample_kernel(q_ref, qi_ref, wq_ref, k_ref, v_ref, kw_ref, ck_ref, cv_ref, cki_ref,
                        x_ref, wout_ref, o_ref,
                        ke, ko, ve, vo, kie, kio, key_s, bias_s,
                        cke, cko, cve, cvo, ckie, ckio, ckey_s, cbias_s,
                        *, tq, past, kchunk, n_sel):
    _fill_padded(ke, ko, k_ref[0], 0, tq)
    _fill_padded(ve, vo, v_ref[0], 0, tq)
    _fill_padded(kie, kio, kw_ref[0], 0, tq)
    for r0 in range(0, past, kchunk):
        nr = min(kchunk, past - r0)
        _fill_padded(cke, cko, ck_ref[0, r0:r0 + nr, :], r0, nr)
        _fill_padded(cve, cvo, cv_ref[0, r0:r0 + nr, :], r0, nr)
        ki = cki_ref[0, r0:r0 + nr, :]
        zero = jnp.zeros_like(ki)
        ckie[r0:r0 + nr, :] = jnp.concatenate([ki, zero], axis=1).astype(BF16)
        ckio[r0:r0 + nr, :] = jnp.concatenate([zero, ki], axis=1).astype(BF16)
    cache = dict(ke=cke, ko=cko, ve=cve, vo=cvo, kie=ckie, kio=ckio, key_s=ckey_s, bias_s=cbias_s,
                 len=past, pos0=0, chunk=kchunk)
    new = dict(ke=ke, ko=ko, ve=ve, vo=vo, kie=kie, kio=kio, key_s=key_s, bias_s=bias_s,
               len=tq, pos0=past, chunk=tq)
    _attn_core(q_ref, qi_ref, wq_ref, x_ref, wout_ref, o_ref, [cache, new],
               tq=tq, q_pos0=past, n_sel=n_sel)


def _dsa_attn_sample(q, qi, kw, k, v, cache_k, cache_v, cache_ki, x, w_out, *, kchunk):
    b, t, d = x.shape
    past = cache_k.shape[1]
    n_sel = min(TOPK_MAX, (past + t) // 4)
    n_kv = N_KV_HEADS * HEAD_DIM
    kern = functools.partial(_attn_sample_kernel, tq=t, past=past, kchunk=kchunk, n_sel=n_sel)

    def spec(rows, n):
        return pl.BlockSpec((1, rows, n), lambda i: (i, 0, 0))

    return pl.pallas_call(
        kern,
        grid=(b,),
        in_specs=[spec(t, q.shape[-1]), spec(t, qi.shape[-1]), spec(t, LANES),
                  spec(t, n_kv), spec(t, n_kv), spec(t, LANES),
                  spec(past, n_kv), spec(past, n_kv), spec(past, IDX_DIM),
                  spec(t, d), _const_spec(w_out.shape)],
        out_specs=spec(t, d),
        out_shape=jax.ShapeDtypeStruct(x.shape, F32),
        scratch_shapes=[
            pltpu.VMEM((t, 2 * n_kv), BF16), pltpu.VMEM((t, 2 * n_kv), BF16),
            pltpu.VMEM((t, 2 * n_kv), BF16), pltpu.VMEM((t, 2 * n_kv), BF16),
            pltpu.VMEM((t, 2 * LANES), BF16), pltpu.VMEM((t, 2 * LANES), BF16),
            pltpu.VMEM((t, t), I32), pltpu.VMEM((t, t), F32),
            pltpu.VMEM((past, 2 * n_kv), BF16), pltpu.VMEM((past, 2 * n_kv), BF16),
            pltpu.VMEM((past, 2 * n_kv), BF16), pltpu.VMEM((past, 2 * n_kv), BF16),
            pltpu.VMEM((past, LANES), BF16), pltpu.VMEM((past, LANES), BF16),
            pltpu.VMEM((t, past), I32), pltpu.VMEM((t, past), F32),
        ],
        compiler_params=_cparams(1),
        name="dsa_attn_sample",
    )(q, qi, kw, k, v, kw, cache_k, cache_v, cache_ki, x, w_out)


def _tiling(b, t):
    if t >= 256:
        return dict(bb=1, tt=256, blk=64, sub=16)
    bb = max(1, min(b, 128 // t))
    return dict(bb=bb, tt=t, blk=t, sub=t)


def _trunk(x, hg_state, conv_state, cache, p):
    b, t, d = x.shape
    tl = _tiling(b, t)
    bb, tt = tl["bb"], tl["tt"]
    x, s_new = _hgrn_mixer(x, hg_state[0], p["norm_mix"][0], p["hg_w_in"][0], p["lower_bounds"][0],
                           p["hg_norm"][0], p["hg_w_out"][0], **tl)
    x, cs0 = _conv_ffn(x, conv_state[0], p["norm_ffn"][0], p["ffn_w_in"][0], p["ffn_conv_w"][0],
                       p["ffn_conv_b"][0], p["ffn_w_down"][0], p["norm_final"], bb=bb, tt=tt, final_norm=False)
    past = 0 if cache is None else cache[0].shape[1]
    tables = _rope_tables(past + jnp.arange(t))
    q, k, v, qi, kw = _dsa_proj(x, p["norm_mix"][1], p["ds_w_in"][0], tables, p["ds_kln_w"][0],
                                p["ds_kln_b"][0], bb=bb, tt=tt)
    if cache is None:
        x = _dsa_attn_prompt(q, qi, kw, k, v, x, p["ds_w_out"][0], tq=min(t, 256), kchunk=min(t, 512))
    else:
        ck, cv, cki = cache
        x = _dsa_attn_sample(q, qi, kw, k, v, ck.reshape(b, past, -1), cv.reshape(b, past, -1), cki,
                             x, p["ds_w_out"][0], kchunk=512)
    x, cs1 = _conv_ffn(x, conv_state[1], p["norm_ffn"][1], p["ffn_w_in"][1], p["ffn_conv_w"][1],
                       p["ffn_conv_b"][1], p["ffn_w_down"][1], p["norm_final"], bb=bb, tt=tt, final_norm=True)
    k = k.reshape(1, b, t, N_KV_HEADS, HEAD_DIM)
    v = v.reshape(1, b, t, N_KV_HEADS, HEAD_DIM)
    ki = kw[:, :, :IDX_DIM].reshape(1, b, t, IDX_DIM)
    return x, s_new[None], jnp.stack([cs0, cs1]), k, v, ki


def kernel(x_prompt, x_sample, cache_k, cache_v, cache_kidx, state_hgrn, state_conv, norm_mix, norm_ffn, norm_final, hg_w_in, hg_lb, hg_norm, hg_w_out, ds_w_in, ds_kln_w, ds_kln_b, ds_w_out, ffn_w_in, ffn_conv_w, ffn_conv_b, ffn_w_down):
    dsa_in = ds_w_in.shape[-1]
    dsa_pad = (-dsa_in) % LANES
    p = dict(
        norm_mix=norm_mix, norm_ffn=norm_ffn, norm_final=norm_final,
        hg_w_in=hg_w_in.astype(BF16), hg_norm=hg_norm, hg_w_out=hg_w_out.astype(BF16),
        lower_bounds=jnp.cumsum(jax.nn.softmax(hg_lb.astype(F32), axis=0), axis=0),
        ds_w_in=jnp.pad(ds_w_in, ((0, 0), (0, 0), (0, dsa_pad))).astype(BF16),
        ds_kln_w=ds_kln_w, ds_kln_b=ds_kln_b, ds_w_out=ds_w_out.astype(BF16),
        ffn_w_in=ffn_w_in.astype(BF16), ffn_conv_w=ffn_conv_w, ffn_conv_b=ffn_conv_b,
        ffn_w_down=ffn_w_down.astype(BF16),
    )
    b = x_prompt.shape[0]
    hg0 = jnp.zeros((state_hgrn.shape[0], b) + state_hgrn.shape[2:], F32)
    conv0 = jnp.zeros((state_conv.shape[0], b) + state_conv.shape[2:], F32)
    y_p, hg_p, conv_p, k_p, v_p, ki_p = _trunk(x_prompt, hg0, conv0, None, p)
    y_s, hg_s, conv_s, k_s, v_s, ki_s = _trunk(x_sample, state_hgrn, state_conv,
                                               (cache_k[0], cache_v[0], cache_kidx[0]), p)
    return (y_p, y_s, k_p, v_p, ki_p, hg_p, conv_p, k_s, v_s, ki_s, hg_s, conv_s)
```

```python
import functools

import jax
import jax.numpy as jnp
from jax import lax
from jax.experimental import pallas as pl
from jax.experimental.pallas import tpu as pltpu

F32 = jnp.float32
BF16 = jnp.bfloat16
I32 = jnp.int32

CHUNK = 64
EPS = 1e-6
HG_HEADS = 8
HG_KDIM = 128
N_HEADS = 16
HEAD_DIM = 64
N_KV_HEADS = 4
IDX_HEADS = 8
IDX_DIM = 64
TOPK_MAX = 256
ROPE_THETA = 500000.0
CONV_W = 3

LANES = 128
VMEM_LIMIT = 56 * 1024 * 1024
EXP_CLAMP = 80.0
LOG2_E = 1.4426950408889634
KEY_NEG_INF = -2139095041
INT_MIN = -2147483648


def _cparams(n_axes, flags=None):
    return pltpu.CompilerParams(dimension_semantics=("arbitrary",) * n_axes,
                                vmem_limit_bytes=VMEM_LIMIT, flags=flags)


def _const_spec(shape):
    nd = len(shape)
    return pl.BlockSpec(shape, lambda *_: (0,) * nd, pipeline_mode=pl.Buffered(1))


def _rms(x, g):
    return x * lax.rsqrt(jnp.mean(x * x, axis=-1, keepdims=True) + EPS) * g


def _silu(x):
    return x * jax.nn.sigmoid(x)


def _dot(a, b):
    return jnp.dot(a, b, preferred_element_type=F32)


def _dot_nt(a, b):
    return lax.dot_general(a, b, (((1,), (1,)), ((), ())), preferred_element_type=F32)


def _dot_tn(a, b):
    return lax.dot_general(a, b, (((0,), (0,)), ((), ())), preferred_element_type=F32)


def _hgrn_kernel(x_ref, s0_ref, g_ref, win_ref, lb_ref, hn_ref, wout_ref, o_ref, s_ref,
                 q_s, k_s, g_s, v_s, gate_s, on_s, st_s, *, bb, tt, blk, sub):
    j = pl.program_id(1)
    rows = bb * tt
    d = x_ref.shape[-1]
    fdim = HG_HEADS * HG_KDIM
    n_sub = blk // sub

    @pl.when(j == 0)
    def _():
        for bi in range(bb):
            for hh in range(HG_HEADS):
                st_s[bi * HG_HEADS + hh] = s0_ref[bi, hh].T

    x = x_ref[...].reshape(rows, d)
    h = _rms(x, g_ref[...]).astype(BF16)
    q_s[...] = _silu(_dot(h, win_ref[:, 0:fdim]))
    lb = lb_ref[...]
    fg = lb + (1.0 - lb) * jax.nn.sigmoid(_dot(h, win_ref[:, fdim:2 * fdim]))
    k_s[...] = 1.0 - fg
    g_s[...] = jnp.log(fg)
    v_s[...] = _dot(h, win_ref[:, 2 * fdim:2 * fdim + d])
    gate_s[...] = _silu(_dot(h, win_ref[:, 2 * fdim + d:]))

    r_i = lax.broadcasted_iota(I32, (blk, blk), 0)
    c_i = lax.broadcasted_iota(I32, (blk, blk), 1)
    causal = c_i <= r_i
    tril = jnp.where(causal, 1.0, 0.0).astype(BF16)
    row_id = lax.broadcasted_iota(I32, (blk, 1), 0)
    hn = hn_ref[...]
    blocks_per_stream = tt // blk

    def block_body(idx, carry):
        r0 = pl.multiple_of(idx * blk, blk)
        bi = idx // blocks_per_stream
        lg = g_s[pl.ds(r0, blk), :]
        lg_hi = lg.astype(BF16)
        rem = lg - lg_hi.astype(F32)
        lg_mid = rem.astype(BF16)
        lg_lo = (rem - lg_mid.astype(F32)).astype(BF16)
        g_all = _dot(tril, lg_hi) + _dot(tril, lg_mid) + _dot(tril, lg_lo)
        for hh in range(HG_HEADS):
            sl = slice(hh * HG_KDIM, (hh + 1) * HG_KDIM)
            gc = g_all[:, sl]
            q = q_s[pl.ds(r0, blk), sl]
            kk = k_s[pl.ds(r0, blk), sl]
            v = v_s[pl.ds(r0, blk), sl]
            v16 = v.astype(BF16)
            refs = [jnp.zeros((1, HG_KDIM), F32)] + [gc[s * sub - 1:s * sub, :] for s in range(1, n_sub)]
            g_ref_rows = jnp.concatenate([jnp.broadcast_to(r, (sub, HG_KDIM)) for r in refs], axis=0)
            qg = q * jnp.exp(gc - g_ref_rows)
            q_parts, k_parts = [], []
            for s in range(n_sub):
                in_sub = (row_id >= s * sub) & (row_id < (s + 1) * sub)
                q_parts.append(jnp.where(in_sub, qg, 0.0))
                kdec = kk * jnp.exp(jnp.minimum(refs[s] - gc, EXP_CLAMP))
                k_parts.append(jnp.where(row_id < (s + 1) * sub, kdec, 0.0))
            qcat = jnp.concatenate(q_parts, axis=1).astype(BF16)
            kcat = jnp.concatenate(k_parts, axis=1).astype(BF16)
            a = jnp.where(causal, _dot_nt(qcat, kcat), 0.0)
            st = st_s[bi * HG_HEADS + hh]
            o = _dot(a.astype(BF16), v16) + _dot_nt((q * jnp.exp(gc)).astype(BF16), st.astype(BF16))
            g_last = gc[blk - 1:blk, :]
            kd = (kk * jnp.exp(g_last - gc)).astype(BF16)
            st_s[bi * HG_HEADS + hh] = st * jnp.exp(g_last) + _dot_tn(v16, kd)
            on = _rms(o, hn) * gate_s[pl.ds(r0, blk), sl]
            on_s[pl.ds(r0, blk), sl] = on.astype(BF16)
        return carry

    lax.fori_loop(0, rows // blk, block_body, 0)

    out = _dot(on_s[...], wout_ref[...]) + x
    o_ref[...] = out.reshape(bb, tt, d)

    @pl.when(j == pl.num_programs(1) - 1)
    def _():
        for bi in range(bb):
            for hh in range(HG_HEADS):
                s_ref[bi, hh] = st_s[bi * HG_HEADS + hh].T


def _hgrn_mixer(x, s0, norm_g, w_in, lb, hn, w_out, *, bb, tt, blk, sub):
    b, t, d = x.shape
    fdim = HG_HEADS * HG_KDIM
    rows = bb * tt
    kern = functools.partial(_hgrn_kernel, bb=bb, tt=tt, blk=blk, sub=sub)
    return pl.pallas_call(
        kern,
        grid=(b // bb, t // tt),
        in_specs=[
            pl.BlockSpec((bb, tt, d), lambda i, j: (i, j, 0)),
            pl.BlockSpec((bb, HG_HEADS, HG_KDIM, d // HG_HEADS), lambda i, j: (i, 0, 0, 0)),
            _const_spec((1, d)),
            _const_spec(w_in.shape),
            _const_spec((1, fdim)),
            _const_spec((1, d // HG_HEADS)),
            _const_spec(w_out.shape),
        ],
        out_specs=[
            pl.BlockSpec((bb, tt, d), lambda i, j: (i, j, 0)),
            pl.BlockSpec((bb, HG_HEADS, HG_KDIM, d // HG_HEADS), lambda i, j: (i, 0, 0, 0)),
        ],
        out_shape=[jax.ShapeDtypeStruct(x.shape, F32), jax.ShapeDtypeStruct(s0.shape, F32)],
        scratch_shapes=[
            pltpu.VMEM((rows, fdim), F32),
            pltpu.VMEM((rows, fdim), F32),
            pltpu.VMEM((rows, fdim), F32),
            pltpu.VMEM((rows, d), F32),
            pltpu.VMEM((rows, d), F32),
            pltpu.VMEM((rows, d), BF16),
            pltpu.VMEM((bb * HG_HEADS, d // HG_HEADS, HG_KDIM), F32),
        ],
        compiler_params=_cparams(2),
        name="hgrn_mixer",
    )(x, s0, norm_g.reshape(1, d), w_in, lb.reshape(1, fdim), hn.reshape(1, -1), w_out)


CONV_HEAD = 8


def _ffn_kernel(x_ref, cs_ref, g_ref, win_ref, cw_ref, cb_ref, wdn_ref, fg_ref, o_ref, ns_ref,
                a_s, *, bb, tt, final_norm):
    j = pl.program_id(1)
    rows = bb * tt
    d = x_ref.shape[-1]
    dff = cw_ref.shape[-1]
    hist = CONV_W - 1

    @pl.when(j == 0)
    def _():
        a_s[:, CONV_HEAD - hist:CONV_HEAD, :] = cs_ref[...]

    x = x_ref[...].reshape(rows, d)
    h = _rms(x, g_ref[...]).astype(BF16)
    a = _dot(h, win_ref[:, 0:dff])
    u = _dot(h, win_ref[:, dff:])
    a_s[:, CONV_HEAD:CONV_HEAD + tt, :] = a.reshape(bb, tt, dff)
    c = jnp.broadcast_to(cb_ref[...].reshape(1, 1, dff), (bb, tt, dff))
    for w in range(CONV_W):
        lo = CONV_HEAD - hist + w
        c = c + a_s[:, lo:lo + tt, :] * cw_ref[w:w + 1, :].reshape(1, 1, dff)
    new_state = a_s[:, CONV_HEAD + tt - hist:CONV_HEAD + tt, :]
    a_s[:, CONV_HEAD - hist:CONV_HEAD, :] = new_state
    ns_ref[...] = new_state
    act = (_silu(c).reshape(rows, dff) * u).astype(BF16)
    y = _dot(act, wdn_ref[...]) + x
    if final_norm:
        y = _rms(y, fg_ref[...])
    o_ref[...] = y.reshape(bb, tt, d)


def _conv_ffn(x, conv_state, norm_g, w_in, conv_w, conv_b, w_down, final_g, *, bb, tt, final_norm):
    b, t, d = x.shape
    dff = conv_w.shape[-1]
    kern = functools.partial(_ffn_kernel, bb=bb, tt=tt, final_norm=final_norm)
    return pl.pallas_call(
        kern,
        grid=(b // bb, t // tt),
        in_specs=[
            pl.BlockSpec((bb, tt, d), lambda i, j: (i, j, 0)),
            pl.BlockSpec((bb, CONV_W - 1, dff), lambda i, j: (i, 0, 0)),
            _const_spec((1, d)),
            _const_spec(w_in.shape),
            _const_spec(conv_w.shape),
            _const_spec((1, dff)),
            _const_spec(w_down.shape),
            _const_spec((1, d)),
        ],
        out_specs=[
            pl.BlockSpec((bb, tt, d), lambda i, j: (i, j, 0)),
            pl.BlockSpec((bb, CONV_W - 1, dff), lambda i, j: (i, 0, 0)),
        ],
        out_shape=[jax.ShapeDtypeStruct(x.shape, F32), jax.ShapeDtypeStruct(conv_state.shape, F32)],
        scratch_shapes=[pltpu.VMEM((bb, CONV_HEAD + tt, dff), F32)],
        compiler_params=_cparams(2),
        name="conv_ffn",
    )(x, conv_state, norm_g.reshape(1, d), w_in, conv_w, conv_b.reshape(1, dff), w_down,
      final_g.reshape(1, d))


def _rope_tables(pos):
    rot = HEAD_DIM // 4
    half = rot // 2
    inv_freq = ROPE_THETA ** (-jnp.arange(half, dtype=F32) / half)
    ang = pos.astype(F32)[:, None] * inv_freq[None, :]
    cos, sin = jnp.cos(ang), jnp.sin(ang)
    t = pos.shape[0]
    pad = HEAD_DIM - rot
    one = jnp.ones((t, pad), F32)
    zero_h = jnp.zeros((t, half), F32)
    zero_p = jnp.zeros((t, pad), F32)
    c64 = jnp.concatenate([cos, cos, one], axis=1)
    s1_64 = jnp.concatenate([zero_h, sin, zero_p], axis=1)
    s2_64 = jnp.concatenate([-sin, zero_h, zero_p], axis=1)
    ident = jnp.ones((t, HEAD_DIM), F32)
    zero64 = jnp.zeros((t, HEAD_DIM), F32)
    return jnp.stack([
        jnp.concatenate([c64, c64], axis=1), jnp.concatenate([s1_64, s1_64], axis=1),
        jnp.concatenate([s2_64, s2_64], axis=1),
        jnp.concatenate([c64, ident], axis=1), jnp.concatenate([s1_64, zero64], axis=1),
        jnp.concatenate([s2_64, zero64], axis=1)])


def _dsa_proj_kernel(x_ref, g_ref, w_ref, tab_ref, lnw_ref, lnb_ref,
                     q_ref, k_ref, v_ref, qi_ref, kw_ref, *, bb, tt):
    rows = bb * tt
    d = x_ref.shape[-1]
    half = HEAD_DIM // 8
    o1 = N_HEADS * HEAD_DIM
    o2 = o1 + N_KV_HEADS * HEAD_DIM
    o3 = o2 + N_KV_HEADS * HEAD_DIM
    o4 = o3 + IDX_HEADS * IDX_DIM

    def tab(i):
        t = tab_ref[i]
        return jnp.broadcast_to(t[None], (bb, tt, LANES)).reshape(rows, LANES)

    def rope(y, base):
        cos, s1, s2 = tab(base), tab(base + 1), tab(base + 2)
        tiles = []
        for m in range(y.shape[1] // LANES):
            yt = y[:, m * LANES:(m + 1) * LANES]
            tiles.append(yt * cos + pltpu.roll(yt, half, 1) * s1 + pltpu.roll(yt, LANES - half, 1) * s2)
        return tiles[0] if len(tiles) == 1 else jnp.concatenate(tiles, axis=1)

    x = x_ref[...].reshape(rows, d)
    h = _rms(x, g_ref[...]).astype(BF16)
    q = rope(_dot(h, w_ref[:, 0:o1]), 0) * (HEAD_DIM ** -0.5 * LOG2_E)
    q_ref[...] = q.astype(BF16).reshape(bb, tt, o1)
    k_ref[...] = rope(_dot(h, w_ref[:, o1:o2]), 0).reshape(bb, tt, o2 - o1)
    v_ref[...] = _dot(h, w_ref[:, o2:o3]).reshape(bb, tt, o3 - o2)
    qi_ref[...] = rope(_dot(h, w_ref[:, o3:o4]), 0).astype(BF16).reshape(bb, tt, o4 - o3)
    t = _dot(h, w_ref[:, o4:o4 + LANES])
    lane = lax.broadcasted_iota(I32, (1, LANES), 1)
    is_ki = lane < IDX_DIM
    mu = jnp.sum(jnp.where(is_ki, t, 0.0), axis=-1, keepdims=True) / IDX_DIM
    cen = jnp.where(is_ki, t - mu, 0.0)
    var = jnp.sum(cen * cen, axis=-1, keepdims=True) / IDX_DIM
    ki = rope(cen * lax.rsqrt(var + EPS) * lnw_ref[...] + lnb_ref[...], 3)
    wi = t * ((IDX_HEADS * IDX_DIM) ** -0.5)
    kw = jnp.where(is_ki, ki, jnp.where(lane < IDX_DIM + IDX_HEADS, wi, 0.0))
    kw_ref[...] = kw.reshape(bb, tt, LANES)


def _dsa_proj(x, norm_g, w_pad, tables, ln_w, ln_b, *, bb, tt):
    b, t, d = x.shape
    n_q = N_HEADS * HEAD_DIM
    n_kv = N_KV_HEADS * HEAD_DIM
    n_qi = IDX_HEADS * IDX_DIM
    pad = jnp.zeros((LANES - IDX_DIM,), F32)
    lnw = jnp.concatenate([ln_w, pad]).reshape(1, LANES)
    lnb = jnp.concatenate([ln_b, pad]).reshape(1, LANES)
    kern = functools.partial(_dsa_proj_kernel, bb=bb, tt=tt)

    def row_spec(n):
        return pl.BlockSpec((bb, tt, n), lambda i, j: (i, j, 0))

    return pl.pallas_call(
        kern,
        grid=(b // bb, t // tt),
        in_specs=[
            row_spec(d),
            _const_spec((1, d)),
            _const_spec(w_pad.shape),
            pl.BlockSpec((6, tt, LANES), lambda i, j: (0, j, 0)),
            _const_spec((1, LANES)),
            _const_spec((1, LANES)),
        ],
        out_specs=[row_spec(n_q), row_spec(n_kv), row_spec(n_kv), row_spec(n_qi), row_spec(LANES)],
        out_shape=[
            jax.ShapeDtypeStruct((b, t, n_q), BF16),
            jax.ShapeDtypeStruct((b, t, n_kv), F32),
            jax.ShapeDtypeStruct((b, t, n_kv), F32),
            jax.ShapeDtypeStruct((b, t, n_qi), BF16),
            jax.ShapeDtypeStruct((b, t, LANES), F32),
        ],
        compiler_params=_cparams(2),
        name="dsa_proj",
    )(x, norm_g.reshape(1, d), w_pad, tables, lnw, lnb)


def _order_key(score):
    score = jnp.where(score == 0.0, 0.0, score)
    bits = lax.bitcast_convert_type(score, I32)
    return bits ^ ((bits >> 31) & 0x7FFFFFFF)


def _fill_padded(dst_e, dst_o, src, r0, nrows):
    lane = lax.broadcasted_iota(I32, (1, LANES), 1)
    lo = lane < HEAD_DIM
    for m in range(src.shape[1] // LANES):
        tile = src[:, m * LANES:(m + 1) * LANES]
        rolled = pltpu.roll(tile, HEAD_DIM, 1)
        a_lo = jnp.where(lo, tile, 0.0).astype(BF16)
        b_hi = jnp.where(lo, 0.0, tile).astype(BF16)
        b_lo = jnp.where(lo, rolled, 0.0).astype(BF16)
        a_hi = jnp.where(lo, 0.0, rolled).astype(BF16)
        c0, c1 = 2 * m * LANES, (2 * m + 1) * LANES
        dst_e[pl.ds(r0, nrows), c0:c0 + LANES] = a_lo
        dst_o[pl.ds(r0, nrows), c0:c0 + LANES] = a_hi
        dst_e[pl.ds(r0, nrows), c1:c1 + LANES] = b_lo
        dst_o[pl.ds(r0, nrows), c1:c1 + LANES] = b_hi


def _attn_core(q_ref, qi_ref, wq_ref, x_ref, wout_ref, o_ref, segs, *, tq, q_pos0, n_sel):
    qrow = lax.broadcasted_iota(I32, (tq, 1), 0)
    q_chunk = (q_pos0 + qrow) // CHUNK
    wq = wq_ref[...].reshape(tq, LANES)
    wi = [wq[:, IDX_DIM + hh:IDX_DIM + hh + 1] for hh in range(IDX_HEADS)]
    qi = qi_ref[...].reshape(tq, IDX_HEADS * IDX_DIM)

    for sg in segs:
        for c0 in range(0, sg["len"], sg["chunk"]):
            cw = min(sg["chunk"], sg["len"] - c0)
            sc = jnp.zeros((tq, cw), F32)
            for m in range(IDX_HEADS // 2):
                qt = qi[:, m * LANES:(m + 1) * LANES]
                de = _dot_nt(qt, sg["kie"][c0:c0 + cw, 0:LANES])
                do = _dot_nt(qt, sg["kio"][c0:c0 + cw, 0:LANES])
                sc = sc + wi[2 * m] * jnp.maximum(de, 0.0) + wi[2 * m + 1] * jnp.maximum(do, 0.0)
            kpos = sg["pos0"] + c0 + lax.broadcasted_iota(I32, (1, cw), 1)
            adm = (kpos // CHUNK) <= q_chunk
            sg["key_s"][:, c0:c0 + cw] = jnp.where(adm, _order_key(sc), KEY_NEG_INF)

    def count(pred):
        tot = jnp.zeros((tq, 1), I32)
        for sg in segs:
            keys = sg["key_s"][:, 0:sg["len"]]
            tot = tot + jnp.sum(pred(keys).astype(I32), axis=-1, keepdims=True)
        return tot

    thr = jnp.where(count(lambda k: k >= 0) >= n_sel, 0, INT_MIN).astype(I32)

    def bit_step(i, thr):
        cand = thr | lax.shift_left(jnp.int32(1), 30 - i)
        return jnp.where(count(lambda k: k >= cand) >= n_sel, cand, thr)

    thr = lax.fori_loop(0, 31, bit_step, thr)
    n_gt = count(lambda k: k > thr)
    n_ge = count(lambda k: k >= thr)
    ties_wanted = n_sel - n_gt

    for sg in segs:
        keys = sg["key_s"][:, 0:sg["len"]]
        sel = (keys >= thr) & (keys > KEY_NEG_INF)
        sg["bias_s"][:, 0:sg["len"]] = jnp.where(sel, 0.0, -jnp.inf)

    has_excess = jnp.max(jnp.where(thr > KEY_NEG_INF, n_ge - n_sel, 0)) > 0

    @pl.when(has_excess)
    def _():
        seen = jnp.zeros((tq, 1), F32)
        wanted = ties_wanted.astype(F32)
        for sg in segs:
            for c0 in range(0, sg["len"], LANES):
                cw = min(LANES, sg["len"] - c0)
                keys = sg["key_s"][:, c0:c0 + cw]
                tie = keys == thr
                r_i = lax.broadcasted_iota(I32, (cw, cw), 0)
                c_i = lax.broadcasted_iota(I32, (cw, cw), 1)
                upper = jnp.where(r_i <= c_i, 1.0, 0.0).astype(BF16)
                tie_f = jnp.where(tie, 1.0, 0.0)
                rank = seen + _dot(tie_f.astype(BF16), upper)
                sel = ((keys > thr) | (tie & (rank <= wanted))) & (keys > KEY_NEG_INF)
                sg["bias_s"][:, c0:c0 + cw] = jnp.where(sel, 0.0, -jnp.inf)
                seen = seen + jnp.sum(tie_f, axis=-1, keepdims=True)

    q = q_ref[...].reshape(tq, N_HEADS * HEAD_DIM)
    tiles = []
    for m in range(N_HEADS // 2):
        g = (2 * m) // (N_HEADS // N_KV_HEADS)
        gs = slice(g * LANES, (g + 1) * LANES)
        qt = q[:, m * LANES:(m + 1) * LANES]
        acc = jnp.zeros((tq, LANES), F32)
        for kname, vname in (("ke", "ve"), ("ko", "vo")):
            logits = [_dot_nt(qt, sg[kname][0:sg["len"], gs]) + sg["bias_s"][:, 0:sg["len"]] for sg in segs]
            mx = functools.reduce(jnp.maximum, [jnp.max(l, axis=-1, keepdims=True) for l in logits])
            ps = [jnp.exp2(l - mx) for l in logits]
            den = functools.reduce(jnp.add, [jnp.sum(p, axis=-1, keepdims=True) for p in ps])
            pv = functools.reduce(jnp.add, [_dot(p.astype(BF16), sg[vname][0:sg["len"], gs])
                                            for p, sg in zip(ps, segs)])
            acc = acc + pv / den
        tiles.append(acc.astype(BF16))
    o = jnp.concatenate(tiles, axis=1)
    d = x_ref.shape[-1]
    o_ref[...] = (_dot(o, wout_ref[...]) + x_ref[...].reshape(tq, d)).reshape(o_ref.shape)


def _attn_prompt_kernel(q_ref, qi_ref, wq_ref, k_ref, v_ref, kw_ref, x_ref, wout_ref, o_ref,
                        ke, ko, vt, kie, kio, qs, key_s, bias_s, ot_s, *, tq, t_len, kc, n_sel):
    j = pl.program_id(1)
    nct = t_len // kc
    sub_per_chunk = kc // LANES
    lane = lax.broadcasted_iota(I32, (1, LANES), 1)
    lo = lane < HEAD_DIM

    @pl.when(j == 0)
    def _():
        for c in range(nct):
            rows = slice(c * kc, (c + 1) * kc)
            k_c = k_ref[0, rows, :]
            for m in range(N_KV_HEADS // 2):
                tile = k_c[:, m * LANES:(m + 1) * LANES]
                rolled = pltpu.roll(tile, HEAD_DIM, 1)
                ke[(2 * m) * nct + c] = jnp.where(lo, tile, 0.0).astype(BF16)
                ko[(2 * m) * nct + c] = jnp.where(lo, 0.0, rolled).astype(BF16)
                ke[(2 * m + 1) * nct + c] = jnp.where(lo, rolled, 0.0).astype(BF16)
                ko[(2 * m + 1) * nct + c] = jnp.where(lo, 0.0, tile).astype(BF16)
            v_t = v_ref[0, rows, :].T
            for g in range(N_KV_HEADS):
                vt[g * nct + c] = v_t[g * HEAD_DIM:(g + 1) * HEAD_DIM, :].astype(BF16)
            kw_c = kw_ref[0, rows, :]
            kie[c] = jnp.where(lo, kw_c, 0.0).astype(BF16)
            kio[c] = jnp.where(lo, 0.0, pltpu.roll(kw_c, HEAD_DIM, 1)).astype(BF16)

    for m in range(N_HEADS // 2):
        qs[m] = q_ref[0, :, m * LANES:(m + 1) * LANES]

    n_chunks = ((j + 1) * tq + kc - 1) // kc
    w_t = wq_ref[0].T
    w_rows = [w_t[IDX_DIM + hh:IDX_DIM + hh + 1, :] for hh in range(IDX_HEADS)]
    qi = qi_ref[0]
    q_chunk = (j * tq + lax.broadcasted_iota(I32, (1, tq), 1)) // CHUNK

    def score_body(c, carry):
        kie_c, kio_c = kie[c], kio[c]
        sc = jnp.zeros((kc, tq), F32)
        for m in range(IDX_HEADS // 2):
            qt = qi[:, m * LANES:(m + 1) * LANES]
            sc = (sc + w_rows[2 * m] * jnp.maximum(_dot_nt(kie_c, qt), 0.0)
                  + w_rows[2 * m + 1] * jnp.maximum(_dot_nt(kio_c, qt), 0.0))
        k_chunk = (c * kc + lax.broadcasted_iota(I32, (kc, 1), 0)) // CHUNK
        key_s[c] = jnp.where(k_chunk <= q_chunk, _order_key(sc), KEY_NEG_INF)
        return carry

    lax.fori_loop(0, n_chunks, score_body, 0)

    def count(pred):
        def body(c, acc):
            hit = pred(key_s[c]).astype(I32)
            return acc + jnp.sum(hit.reshape(kc // 8, 8, tq), axis=0)
        acc = lax.fori_loop(0, n_chunks, body, jnp.zeros((8, tq), I32))
        return jnp.sum(acc, axis=0, keepdims=True)

    thr = jnp.where(count(lambda k: k >= 0) >= n_sel, 0, INT_MIN).astype(I32)

    def bit_step(i, thr):
        cand = thr | lax.shift_left(jnp.int32(1), 30 - i)
        return jnp.where(count(lambda k: k >= cand) >= n_sel, cand, thr)

    thr = lax.fori_loop(0, 31, bit_step, thr)
    n_gt = count(lambda k: k > thr)
    n_ge = count(lambda k: k >= thr)

    def bias_body(c, carry):
        keys = key_s[c]
        bias_s[c] = jnp.where((keys >= thr) & (keys > KEY_NEG_INF), 0.0, -jnp.inf)
        return carry

    lax.fori_loop(0, n_chunks, bias_body, 0)

    has_excess = jnp.max(jnp.where(thr > KEY_NEG_INF, n_ge - n_sel, 0)) > 0

    @pl.when(has_excess)
    def _():
        wanted = (n_sel - n_gt).astype(F32)
        r_i = lax.broadcasted_iota(I32, (LANES, LANES), 0)
        c_i = lax.broadcasted_iota(I32, (LANES, LANES), 1)
        lower = jnp.where(c_i <= r_i, 1.0, 0.0).astype(BF16)

        def tie_body(u, seen):
            c = u // sub_per_chunk
            r0 = pl.multiple_of((u % sub_per_chunk) * LANES, LANES)
            keys = key_s[c, pl.ds(r0, LANES), :]
            tie = keys == thr
            tie_f = jnp.where(tie, 1.0, 0.0)
            rank = seen + _dot(lower, tie_f.astype(BF16))
            sel = ((keys > thr) | (tie & (rank <= wanted))) & (keys > KEY_NEG_INF)
            bias_s[c, pl.ds(r0, LANES), :] = jnp.where(sel, 0.0, -jnp.inf)
            return seen + jnp.sum(tie_f, axis=0, keepdims=True)

        lax.fori_loop(0, n_chunks * sub_per_chunk, tie_body, jnp.zeros((1, tq), F32))

    n_grp = N_HEADS // N_KV_HEADS

    def group_body(g, carry):
        q_tiles = [qs[g * (n_grp // 2) + i] for i in range(n_grp // 2)]

        def chunk_body(c, st):
            bias = bias_s[c]
            v_c = vt[g * nct + c]
            k_e, k_o = ke[g * nct + c], ko[g * nct + c]
            logits = [_dot_nt(k_x, qt) for qt in q_tiles for k_x in (k_e, k_o)]
            tails = [lax.bitcast_convert_type(l[kc - 8:kc, :], I32) for l in logits[1:]]
            anchor = functools.reduce(jnp.bitwise_or, tails) & lax.shift_right_arithmetic(c, 31)
            anchor = lax.bitcast_convert_type(anchor, F32)[0:1, :]
            out = []
            for h in range(n_grp):
                m_run, l_run, acc = st[3 * h:3 * h + 3]
                if h == 0:
                    m_run = m_run + anchor
                logit = logits[h] + bias
                m_new = jnp.maximum(m_run, jnp.max(logit, axis=0, keepdims=True))
                m_safe = jnp.where(m_new > -jnp.inf, m_new, 0.0)
                p = jnp.exp2(logit - m_safe)
                alpha = jnp.exp2(m_run - m_safe)
                out += [m_new, alpha * l_run + jnp.sum(p, axis=0, keepdims=True),
                        alpha * acc + _dot(v_c, p.astype(BF16))]
            return tuple(out)

        init = (jnp.full((1, tq), -jnp.inf, F32), jnp.zeros((1, tq), F32), jnp.zeros((HEAD_DIM, tq), F32))
        st = lax.fori_loop(0, n_chunks, chunk_body, init * n_grp)
        for h in range(n_grp):
            r0 = pl.multiple_of(g * (n_grp * HEAD_DIM) + h * HEAD_DIM, HEAD_DIM)
            ot_s[pl.ds(r0, HEAD_DIM), :] = st[3 * h + 2] / st[3 * h + 1]
        return carry

    lax.fori_loop(0, N_KV_HEADS, group_body, 0)

    o = jnp.concatenate([ot_s[m * LANES:(m + 1) * LANES, :].T for m in range(N_HEADS // 2)], axis=1)
    o_ref[0] = _dot(o.astype(BF16), wout_ref[...]) + x_ref[0]


def _dsa_attn_prompt(q, qi, kw, k, v, x, w_out, *, tq, kc):
    b, t, d = x.shape
    n_sel = min(TOPK_MAX, t // 4)
    n_kv = N_KV_HEADS * HEAD_DIM
    nct = t // kc
    kern = functools.partial(_attn_prompt_kernel, tq=tq, t_len=t, kc=kc, n_sel=n_sel)

    def tile_spec(n):
        return pl.BlockSpec((1, tq, n), lambda i, j: (i, j, 0))

    def full_spec(n):
        return pl.BlockSpec((1, t, n), lambda i, j: (i, 0, 0))

    return pl.pallas_call(
        kern,
        grid=(b, t // tq),
        in_specs=[tile_spec(q.shape[-1]), tile_spec(qi.shape[-1]), tile_spec(LANES),
                  full_spec(n_kv), full_spec(n_kv), full_spec(LANES),
                  tile_spec(d), _const_spec(w_out.shape)],
        out_specs=tile_spec(d),
        out_shape=jax.ShapeDtypeStruct(x.shape, F32),
        scratch_shapes=[
            pltpu.VMEM((N_KV_HEADS * nct, kc, LANES), BF16), pltpu.VMEM((N_KV_HEADS * nct, kc, LANES), BF16),
            pltpu.VMEM((N_KV_HEADS * nct, HEAD_DIM, kc), BF16),
            pltpu.VMEM((nct, kc, LANES), BF16), pltpu.VMEM((nct, kc, LANES), BF16),
            pltpu.VMEM((N_HEADS // 2, tq, LANES), BF16),
            pltpu.VMEM((nct, kc, tq), I32), pltpu.VMEM((nct, kc, tq), F32),
            pltpu.VMEM((N_HEADS * HEAD_DIM, tq), F32),
        ],
        compiler_params=_cparams(2),
        name="dsa_attn_prompt",
    )(q, qi, kw, k, v, kw, x, w_out)


def _attn_sample_kernel(q_ref, qi_ref, wq_ref, k_ref, v_ref, kw_ref, ck_ref, cv_ref, cki_ref,
                        x_ref, wout_ref, o_ref,
                        ke, ko, ve, vo, kie, kio, key_s, bias_s,
                        cke, cko, cve, cvo, ckie, ckio, ckey_s, cbias_s,
                        *, tq, past, kchunk, n_sel):
    _fill_padded(ke, ko, k_ref[0], 0, tq)
    _fill_padded(ve, vo, v_ref[0], 0, tq)
    _fill_padded(kie, kio, kw_ref[0], 0, tq)
    for r0 in range(0, past, kchunk):
        nr = min(kchunk, past - r0)
        _fill_padded(cke, cko, ck_ref[0, r0:r0 + nr, :], r0, nr)
        _fill_padded(cve, cvo, cv_ref[0, r0:r0 + nr, :], r0, nr)
        ki = cki_ref[0, r0:r0 + nr, :]
        zero = jnp.zeros_like(ki)
        ckie[r0:r0 + nr, :] = jnp.concatenate([ki, zero], axis=1).astype(BF16)
        ckio[r0:r0 + nr, :] = jnp.concatenate([zero, ki], axis=1).astype(BF16)
    cache = dict(ke=cke, ko=cko, ve=cve, vo=cvo, kie=ckie, kio=ckio, key_s=ckey_s, bias_s=cbias_s,
                 len=past, pos0=0, chunk=kchunk)
    new = dict(ke=ke, ko=ko, ve=ve, vo=vo, kie=kie, kio=kio, key_s=key_s, bias_s=bias_s,
               len=tq, pos0=past, chunk=tq)
    _attn_core(q_ref, qi_ref, wq_ref, x_ref, wout_ref, o_ref, [cache, new],
               tq=tq, q_pos0=past, n_sel=n_sel)


def _dsa_attn_sample(q, qi, kw, k, v, cache_k, cache_v, cache_ki, x, w_out, *, kchunk):
    b, t, d = x.shape
    past = cache_k.shape[1]
    n_sel = min(TOPK_MAX, (past + t) // 4)
    n_kv = N_KV_HEADS * HEAD_DIM
    kern = functools.partial(_attn_sample_kernel, tq=t, past=past, kchunk=kchunk, n_sel=n_sel)

    def spec(rows, n):
        return pl.BlockSpec((1, rows, n), lambda i: (i, 0, 0))

    return pl.pallas_call(
        kern,
        grid=(b,),
        in_specs=[spec(t, q.shape[-1]), spec(t, qi.shape[-1]), spec(t, LANES),
                  spec(t, n_kv), spec(t, n_kv), spec(t, LANES),
                  spec(past, n_kv), spec(past, n_kv), spec(past, IDX_DIM),
                  spec(t, d), _const_spec(w_out.shape)],
        out_specs=spec(t, d),
        out_shape=jax.ShapeDtypeStruct(x.shape, F32),
        scratch_shapes=[
            pltpu.VMEM((t, 2 * n_kv), BF16), pltpu.VMEM((t, 2 * n_kv), BF16),
            pltpu.VMEM((t, 2 * n_kv), BF16), pltpu.VMEM((t, 2 * n_kv), BF16),
            pltpu.VMEM((t, 2 * LANES), BF16), pltpu.VMEM((t, 2 * LANES), BF16),
            pltpu.VMEM((t, t), I32), pltpu.VMEM((t, t), F32),
            pltpu.VMEM((past, 2 * n_kv), BF16), pltpu.VMEM((past, 2 * n_kv), BF16),
            pltpu.VMEM((past, 2 * n_kv), BF16), pltpu.VMEM((past, 2 * n_kv), BF16),
            pltpu.VMEM((past, LANES), BF16), pltpu.VMEM((past, LANES), BF16),
            pltpu.VMEM((t, past), I32), pltpu.VMEM((t, past), F32),
        ],
        compiler_params=_cparams(1),
        name="dsa_attn_sample",
    )(q, qi, kw, k, v, kw, cache_k, cache_v, cache_ki, x, w_out)


def _tiling(b, t):
    if t >= 256:
        return dict(bb=1, tt=256, blk=64, sub=16)
    bb = max(1, min(b, 128 // t))
    return dict(bb=bb, tt=t, blk=t, sub=t)


def _trunk(x, hg_state, conv_state, cache, p):
    b, t, d = x.shape
    tl = _tiling(b, t)
    bb, tt = tl["bb"], tl["tt"]
    x, s_new = _hgrn_mixer(x, hg_state[0], p["norm_mix"][0], p["hg_w_in"][0], p["lower_bounds"][0],
                           p["hg_norm"][0], p["hg_w_out"][0], **tl)
    x, cs0 = _conv_ffn(x, conv_state[0], p["norm_ffn"][0], p["ffn_w_in"][0], p["ffn_conv_w"][0],
                       p["ffn_conv_b"][0], p["ffn_w_down"][0], p["norm_final"], bb=bb, tt=tt, final_norm=False)
    past = 0 if cache is None else cache[0].shape[1]
    tables = _rope_tables(past + jnp.arange(t))
    q, k, v, qi, kw = _dsa_proj(x, p["norm_mix"][1], p["ds_w_in"][0], tables, p["ds_kln_w"][0],
                                p["ds_kln_b"][0], bb=bb, tt=tt)
    if cache is None:
        x = _dsa_attn_prompt(q, qi, kw, k, v, x, p["ds_w_out"][0], tq=min(t, 256), kc=min(t, 512))
    else:
        ck, cv, cki = cache
        x = _dsa_attn_sample(q, qi, kw, k, v, ck.reshape(b, past, -1), cv.reshape(b, past, -1), cki,
                             x, p["ds_w_out"][0], kchunk=512)
    x, cs1 = _conv_ffn(x, conv_state[1], p["norm_ffn"][1], p["ffn_w_in"][1], p["ffn_conv_w"][1],
                       p["ffn_conv_b"][1], p["ffn_w_down"][1], p["norm_final"], bb=bb, tt=tt, final_norm=True)
    k = k.reshape(1, b, t, N_KV_HEADS, HEAD_DIM)
    v = v.reshape(1, b, t, N_KV_HEADS, HEAD_DIM)
    ki = kw[:, :, :IDX_DIM].reshape(1, b, t, IDX_DIM)
    return x, s_new[None], jnp.stack([cs0, cs1]), k, v, ki


def kernel(x_prompt, x_sample, cache_k, cache_v, cache_kidx, state_hgrn, state_conv, norm_mix, norm_ffn, norm_final, hg_w_in, hg_lb, hg_norm, hg_w_out, ds_w_in, ds_kln_w, ds_kln_b, ds_w_out, ffn_w_in, ffn_conv_w, ffn_conv_b, ffn_w_down):
    dsa_in = ds_w_in.shape[-1]
    dsa_pad = (-dsa_in) % LANES
    p = dict(
        norm_mix=norm_mix, norm_ffn=norm_ffn, norm_final=norm_final,
        hg_w_in=hg_w_in.astype(BF16), hg_norm=hg_norm, hg_w_out=hg_w_out.astype(BF16),
        lower_bounds=jnp.cumsum(jax.nn.softmax(hg_lb.astype(F32), axis=0), axis=0),
        ds_w_in=jnp.pad(ds_w_in, ((0, 0), (0, 0), (0, dsa_pad))).astype(BF16),
        ds_kln_w=ds_kln_w, ds_kln_b=ds_kln_b, ds_w_out=ds_w_out.astype(BF16),
        ffn_w_in=ffn_w_in.astype(BF16), ffn_conv_w=ffn_conv_w, ffn_conv_b=ffn_conv_b,
        ffn_w_down=ffn_w_down.astype(BF16),
    )
    b = x_prompt.shape[0]
    hg0 = jnp.zeros((state_hgrn.shape[0], b) + state_hgrn.shape[2:], F32)
    conv0 = jnp.zeros((state_conv.shape[0], b) + state_conv.shape[2:], F32)
    y_p, hg_p, conv_p, k_p, v_p, ki_p = _trunk(x_prompt, hg0, conv0, None, p)
    y_s, hg_s, conv_s, k_s, v_s, ki_s = _trunk(x_sample, state_hgrn, state_conv,
                                               (cache_k[0], cache_v[0], cache_kidx[0]), p)
    return (y_p, y_s, k_p, v_p, ki_p, hg_p, conv_p, k_s, v_s, ki_s, hg_s, conv_s)
```

```python
import functools

import jax
import jax.numpy as jnp
from jax import lax
from jax.experimental import pallas as pl
from jax.experimental.pallas import tpu as pltpu

F32 = jnp.float32
BF16 = jnp.bfloat16
I32 = jnp.int32

CHUNK = 64
EPS = 1e-6
HG_HEADS = 8
HG_KDIM = 128
N_HEADS = 16
HEAD_DIM = 64
N_KV_HEADS = 4
IDX_HEADS = 8
IDX_DIM = 64
TOPK_MAX = 256
ROPE_THETA = 500000.0
CONV_W = 3

LANES = 128
VMEM_LIMIT = 56 * 1024 * 1024
EXP_CLAMP = 80.0
LOG2_E = 1.4426950408889634
KEY_NEG_INF = -2139095041
INT_MIN = -2147483648


def _cparams(n_axes, flags=None):
    return pltpu.CompilerParams(dimension_semantics=("arbitrary",) * n_axes,
                                vmem_limit_bytes=VMEM_LIMIT, flags=flags)


def _const_spec(shape):
    nd = len(shape)
    return pl.BlockSpec(shape, lambda *_: (0,) * nd, pipeline_mode=pl.Buffered(1))


def _rms(x, g):
    return x * lax.rsqrt(jnp.mean(x * x, axis=-1, keepdims=True) + EPS) * g


def _silu(x):
    return x * jax.nn.sigmoid(x)


def _dot(a, b):
    return jnp.dot(a, b, preferred_element_type=F32)


def _dot_nt(a, b):
    return lax.dot_general(a, b, (((1,), (1,)), ((), ())), preferred_element_type=F32)


def _dot_tn(a, b):
    return lax.dot_general(a, b, (((0,), (0,)), ((), ())), preferred_element_type=F32)


def _hgrn_kernel(x_ref, s0_ref, g_ref, win_ref, lb_ref, hn_ref, wout_ref, o_ref, s_ref,
                 q_s, k_s, g_s, v_s, gate_s, on_s, st_s, *, bb, tt, blk, sub):
    j = pl.program_id(1)
    rows = bb * tt
    d = x_ref.shape[-1]
    fdim = HG_HEADS * HG_KDIM
    n_sub = blk // sub

    @pl.when(j == 0)
    def _():
        for bi in range(bb):
            for hh in range(HG_HEADS):
                st_s[bi * HG_HEADS + hh] = s0_ref[bi, hh].T

    x = x_ref[...].reshape(rows, d)
    h = _rms(x, g_ref[...]).astype(BF16)
    q_s[...] = _silu(_dot(h, win_ref[:, 0:fdim]))
    lb = lb_ref[...]
    fg = lb + (1.0 - lb) * jax.nn.sigmoid(_dot(h, win_ref[:, fdim:2 * fdim]))
    k_s[...] = 1.0 - fg
    g_s[...] = jnp.log(fg)
    v_s[...] = _dot(h, win_ref[:, 2 * fdim:2 * fdim + d])
    gate_s[...] = _silu(_dot(h, win_ref[:, 2 * fdim + d:]))

    r_i = lax.broadcasted_iota(I32, (blk, blk), 0)
    c_i = lax.broadcasted_iota(I32, (blk, blk), 1)
    causal = c_i <= r_i
    tril = jnp.where(causal, 1.0, 0.0).astype(BF16)
    hn = hn_ref[...]
    blocks_per_stream = tt // blk

    def block_body(idx, carry):
        r0 = pl.multiple_of(idx * blk, blk)
        bi = idx // blocks_per_stream
        lg = g_s[pl.ds(r0, blk), :]
        lg_hi = lg.astype(BF16)
        rem = lg - lg_hi.astype(F32)
        lg_mid = rem.astype(BF16)
        lg_lo = (rem - lg_mid.astype(F32)).astype(BF16)
        g_all = _dot(tril, lg_hi) + _dot(tril, lg_mid) + _dot(tril, lg_lo)
        run_zero = lax.shift_right_arithmetic(idx, 31)

        def anchor(x):
            tail = lax.bitcast_convert_type(x[x.shape[0] - 8:, :], I32) & run_zero
            return lax.bitcast_convert_type(tail, F32)[0:1, :]

        heads = [slice(hh * HG_KDIM, (hh + 1) * HG_KDIM) for hh in range(HG_HEADS)]
        qcat, kcat, qfull, kdec_last, v16, decay = [], [], [], [], [], []
        for sl in heads:
            gc = g_all[:, sl]
            q = q_s[pl.ds(r0, blk), sl]
            kk = k_s[pl.ds(r0, blk), sl]
            refs = [jnp.zeros((1, HG_KDIM), F32)] + [gc[s * sub - 1:s * sub, :] for s in range(1, n_sub)]
            q_parts, k_parts = [], []
            for s in range(n_sub):
                lo_r, hi_r = s * sub, (s + 1) * sub
                qg = q[lo_r:hi_r, :] * jnp.exp(gc[lo_r:hi_r, :] - refs[s])
                pieces = [qg]
                if lo_r:
                    pieces.insert(0, jnp.zeros((lo_r, HG_KDIM), F32))
                if blk - hi_r:
                    pieces.append(jnp.zeros((blk - hi_r, HG_KDIM), F32))
                q_parts.append(pieces[0] if len(pieces) == 1 else jnp.concatenate(pieces, axis=0))
                kd_s = kk[0:hi_r, :] * jnp.exp(jnp.minimum(refs[s] - gc[0:hi_r, :], EXP_CLAMP))
                k_parts.append(kd_s if hi_r == blk else
                               jnp.concatenate([kd_s, jnp.zeros((blk - hi_r, HG_KDIM), F32)], axis=0))
            qcat.append(jnp.concatenate(q_parts, axis=1).astype(BF16))
            kcat.append(jnp.concatenate(k_parts, axis=1).astype(BF16))
            qfull.append((q * jnp.exp(gc)).astype(BF16))
            g_last = gc[blk - 1:blk, :]
            kdec_last.append((kk * jnp.exp(g_last - gc)).astype(BF16))
            decay.append(jnp.exp(g_last))
            v16.append(v_s[pl.ds(r0, blk), sl].astype(BF16))
        a_list = [_dot_nt(qcat[hh], kcat[hh]) for hh in range(HG_HEADS)]
        s_inc = [_dot_tn(v16[hh], kdec_last[hh]) for hh in range(HG_HEADS)]
        a_list[0] = a_list[0] + anchor(a_list[-1])
        o_list = []
        for hh in range(HG_HEADS):
            a = jnp.where(causal, a_list[hh], 0.0).astype(BF16)
            st = st_s[bi * HG_HEADS + hh]
            o_list.append(_dot(a, v16[hh]) + _dot_nt(qfull[hh], st.astype(BF16)))
            st_s[bi * HG_HEADS + hh] = st * decay[hh] + s_inc[hh]
        o_list[0] = o_list[0] + anchor(o_list[-1])
        for hh, sl in enumerate(heads):
            on = _rms(o_list[hh], hn) * gate_s[pl.ds(r0, blk), sl]
            on_s[pl.ds(r0, blk), sl] = on.astype(BF16)
        return carry

    lax.fori_loop(0, rows // blk, block_body, 0, unroll=2)

    out = _dot(on_s[...], wout_ref[...]) + x
    o_ref[...] = out.reshape(bb, tt, d)

    @pl.when(j == pl.num_programs(1) - 1)
    def _():
        for bi in range(bb):
            for hh in range(HG_HEADS):
                s_ref[bi, hh] = st_s[bi * HG_HEADS + hh].T


def _hgrn_mixer(x, s0, norm_g, w_in, lb, hn, w_out, *, bb, tt, blk, sub):
    b, t, d = x.shape
    fdim = HG_HEADS * HG_KDIM
    rows = bb * tt
    kern = functools.partial(_hgrn_kernel, bb=bb, tt=tt, blk=blk, sub=sub)
    return pl.pallas_call(
        kern,
        grid=(b // bb, t // tt),
        in_specs=[
            pl.BlockSpec((bb, tt, d), lambda i, j: (i, j, 0)),
            pl.BlockSpec((bb, HG_HEADS, HG_KDIM, d // HG_HEADS), lambda i, j: (i, 0, 0, 0)),
            _const_spec((1, d)),
            _const_spec(w_in.shape),
            _const_spec((1, fdim)),
            _const_spec((1, d // HG_HEADS)),
            _const_spec(w_out.shape),
        ],
        out_specs=[
            pl.BlockSpec((bb, tt, d), lambda i, j: (i, j, 0)),
            pl.BlockSpec((bb, HG_HEADS, HG_KDIM, d // HG_HEADS), lambda i, j: (i, 0, 0, 0)),
        ],
        out_shape=[jax.ShapeDtypeStruct(x.shape, F32), jax.ShapeDtypeStruct(s0.shape, F32)],
        scratch_shapes=[
            pltpu.VMEM((rows, fdim), F32),
            pltpu.VMEM((rows, fdim), F32),
            pltpu.VMEM((rows, fdim), F32),
            pltpu.VMEM((rows, d), F32),
            pltpu.VMEM((rows, d), F32),
            pltpu.VMEM((rows, d), BF16),
            pltpu.VMEM((bb * HG_HEADS, d // HG_HEADS, HG_KDIM), F32),
        ],
        compiler_params=_cparams(2),
        name="hgrn_mixer",
    )(x, s0, norm_g.reshape(1, d), w_in, lb.reshape(1, fdim), hn.reshape(1, -1), w_out)


CONV_HEAD = 8


def _ffn_kernel(x_ref, cs_ref, g_ref, win_ref, cw_ref, cb_ref, wdn_ref, fg_ref, o_ref, ns_ref,
                a_s, *, bb, tt, final_norm):
    j = pl.program_id(1)
    rows = bb * tt
    d = x_ref.shape[-1]
    dff = cw_ref.shape[-1]
    hist = CONV_W - 1

    @pl.when(j == 0)
    def _():
        a_s[:, CONV_HEAD - hist:CONV_HEAD, :] = cs_ref[...]

    x = x_ref[...].reshape(rows, d)
    h = _rms(x, g_ref[...]).astype(BF16)
    a = _dot(h, win_ref[:, 0:dff])
    u = _dot(h, win_ref[:, dff:])
    a_s[:, CONV_HEAD:CONV_HEAD + tt, :] = a.reshape(bb, tt, dff)
    c = jnp.broadcast_to(cb_ref[...].reshape(1, 1, dff), (bb, tt, dff))
    for w in range(CONV_W):
        lo = CONV_HEAD - hist + w
        c = c + a_s[:, lo:lo + tt, :] * cw_ref[w:w + 1, :].reshape(1, 1, dff)
    new_state = a_s[:, CONV_HEAD + tt - hist:CONV_HEAD + tt, :]
    a_s[:, CONV_HEAD - hist:CONV_HEAD, :] = new_state
    ns_ref[...] = new_state
    act = (_silu(c).reshape(rows, dff) * u).astype(BF16)
    y = _dot(act, wdn_ref[...]) + x
    if final_norm:
        y = _rms(y, fg_ref[...])
    o_ref[...] = y.reshape(bb, tt, d)


def _conv_ffn(x, conv_state, norm_g, w_in, conv_w, conv_b, w_down, final_g, *, bb, tt, final_norm):
    b, t, d = x.shape
    dff = conv_w.shape[-1]
    kern = functools.partial(_ffn_kernel, bb=bb, tt=tt, final_norm=final_norm)
    return pl.pallas_call(
        kern,
        grid=(b // bb, t // tt),
        in_specs=[
            pl.BlockSpec((bb, tt, d), lambda i, j: (i, j, 0)),
            pl.BlockSpec((bb, CONV_W - 1, dff), lambda i, j: (i, 0, 0)),
            _const_spec((1, d)),
            _const_spec(w_in.shape),
            _const_spec(conv_w.shape),
            _const_spec((1, dff)),
            _const_spec(w_down.shape),
            _const_spec((1, d)),
        ],
        out_specs=[
            pl.BlockSpec((bb, tt, d), lambda i, j: (i, j, 0)),
            pl.BlockSpec((bb, CONV_W - 1, dff), lambda i, j: (i, 0, 0)),
        ],
        out_shape=[jax.ShapeDtypeStruct(x.shape, F32), jax.ShapeDtypeStruct(conv_state.shape, F32)],
        scratch_shapes=[pltpu.VMEM((bb, CONV_HEAD + tt, dff), F32)],
        compiler_params=_cparams(2),
        name="conv_ffn",
    )(x, conv_state, norm_g.reshape(1, d), w_in, conv_w, conv_b.reshape(1, dff), w_down,
      final_g.reshape(1, d))


def _rope_tables(pos):
    rot = HEAD_DIM // 4
    half = rot // 2
    inv_freq = ROPE_THETA ** (-jnp.arange(half, dtype=F32) / half)
    ang = pos.astype(F32)[:, None] * inv_freq[None, :]
    cos, sin = jnp.cos(ang), jnp.sin(ang)
    t = pos.shape[0]
    pad = HEAD_DIM - rot
    one = jnp.ones((t, pad), F32)
    zero_h = jnp.zeros((t, half), F32)
    zero_p = jnp.zeros((t, pad), F32)
    c64 = jnp.concatenate([cos, cos, one], axis=1)
    s1_64 = jnp.concatenate([zero_h, sin, zero_p], axis=1)
    s2_64 = jnp.concatenate([-sin, zero_h, zero_p], axis=1)
    ident = jnp.ones((t, HEAD_DIM), F32)
    zero64 = jnp.zeros((t, HEAD_DIM), F32)
    return jnp.stack([
        jnp.concatenate([c64, c64], axis=1), jnp.concatenate([s1_64, s1_64], axis=1),
        jnp.concatenate([s2_64, s2_64], axis=1),
        jnp.concatenate([c64, ident], axis=1), jnp.concatenate([s1_64, zero64], axis=1),
        jnp.concatenate([s2_64, zero64], axis=1)])


def _dsa_proj_kernel(x_ref, g_ref, w_ref, tab_ref, lnw_ref, lnb_ref,
                     q_ref, k_ref, v_ref, qi_ref, kw_ref, *, bb, tt):
    rows = bb * tt
    d = x_ref.shape[-1]
    half = HEAD_DIM // 8
    o1 = N_HEADS * HEAD_DIM
    o2 = o1 + N_KV_HEADS * HEAD_DIM
    o3 = o2 + N_KV_HEADS * HEAD_DIM
    o4 = o3 + IDX_HEADS * IDX_DIM

    def tab(i):
        t = tab_ref[i]
        return jnp.broadcast_to(t[None], (bb, tt, LANES)).reshape(rows, LANES)

    def rope(y, base):
        cos, s1, s2 = tab(base), tab(base + 1), tab(base + 2)
        tiles = []
        for m in range(y.shape[1] // LANES):
            yt = y[:, m * LANES:(m + 1) * LANES]
            tiles.append(yt * cos + pltpu.roll(yt, half, 1) * s1 + pltpu.roll(yt, LANES - half, 1) * s2)
        return tiles[0] if len(tiles) == 1 else jnp.concatenate(tiles, axis=1)

    x = x_ref[...].reshape(rows, d)
    h = _rms(x, g_ref[...]).astype(BF16)
    q = rope(_dot(h, w_ref[:, 0:o1]), 0) * (HEAD_DIM ** -0.5 * LOG2_E)
    q_ref[...] = q.astype(BF16).reshape(bb, tt, o1)
    k_ref[...] = rope(_dot(h, w_ref[:, o1:o2]), 0).reshape(bb, tt, o2 - o1)
    v_ref[...] = _dot(h, w_ref[:, o2:o3]).reshape(bb, tt, o3 - o2)
    qi_ref[...] = rope(_dot(h, w_ref[:, o3:o4]), 0).astype(BF16).reshape(bb, tt, o4 - o3)
    t = _dot(h, w_ref[:, o4:o4 + LANES])
    lane = lax.broadcasted_iota(I32, (1, LANES), 1)
    is_ki = lane < IDX_DIM
    mu = jnp.sum(jnp.where(is_ki, t, 0.0), axis=-1, keepdims=True) / IDX_DIM
    cen = jnp.where(is_ki, t - mu, 0.0)
    var = jnp.sum(cen * cen, axis=-1, keepdims=True) / IDX_DIM
    ki = rope(cen * lax.rsqrt(var + EPS) * lnw_ref[...] + lnb_ref[...], 3)
    wi = t * ((IDX_HEADS * IDX_DIM) ** -0.5)
    kw = jnp.where(is_ki, ki, jnp.where(lane < IDX_DIM + IDX_HEADS, wi, 0.0))
    kw_ref[...] = kw.reshape(bb, tt, LANES)


def _dsa_proj(x, norm_g, w_pad, tables, ln_w, ln_b, *, bb, tt):
    b, t, d = x.shape
    n_q = N_HEADS * HEAD_DIM
    n_kv = N_KV_HEADS * HEAD_DIM
    n_qi = IDX_HEADS * IDX_DIM
    pad = jnp.zeros((LANES - IDX_DIM,), F32)
    lnw = jnp.concatenate([ln_w, pad]).reshape(1, LANES)
    lnb = jnp.concatenate([ln_b, pad]).reshape(1, LANES)
    kern = functools.partial(_dsa_proj_kernel, bb=bb, tt=tt)

    def row_spec(n):
        return pl.BlockSpec((bb, tt, n), lambda i, j: (i, j, 0))

    return pl.pallas_call(
        kern,
        grid=(b // bb, t // tt),
        in_specs=[
            row_spec(d),
            _const_spec((1, d)),
            _const_spec(w_pad.shape),
            pl.BlockSpec((6, tt, LANES), lambda i, j: (0, j, 0)),
            _const_spec((1, LANES)),
            _const_spec((1, LANES)),
        ],
        out_specs=[row_spec(n_q), row_spec(n_kv), row_spec(n_kv), row_spec(n_qi), row_spec(LANES)],
        out_shape=[
            jax.ShapeDtypeStruct((b, t, n_q), BF16),
            jax.ShapeDtypeStruct((b, t, n_kv), F32),
            jax.ShapeDtypeStruct((b, t, n_kv), F32),
            jax.ShapeDtypeStruct((b, t, n_qi), BF16),
            jax.ShapeDtypeStruct((b, t, LANES), F32),
        ],
        compiler_params=_cparams(2),
        name="dsa_proj",
    )(x, norm_g.reshape(1, d), w_pad, tables, lnw, lnb)


def _order_key(score):
    score = jnp.where(score == 0.0, 0.0, score)
    bits = lax.bitcast_convert_type(score, I32)
    return bits ^ ((bits >> 31) & 0x7FFFFFFF)


def _fill_padded(dst_e, dst_o, src, r0, nrows):
    lane = lax.broadcasted_iota(I32, (1, LANES), 1)
    lo = lane < HEAD_DIM
    for m in range(src.shape[1] // LANES):
        tile = src[:, m * LANES:(m + 1) * LANES]
        rolled = pltpu.roll(tile, HEAD_DIM, 1)
        a_lo = jnp.where(lo, tile, 0.0).astype(BF16)
        b_hi = jnp.where(lo, 0.0, tile).astype(BF16)
        b_lo = jnp.where(lo, rolled, 0.0).astype(BF16)
        a_hi = jnp.where(lo, 0.0, rolled).astype(BF16)
        c0, c1 = 2 * m * LANES, (2 * m + 1) * LANES
        dst_e[pl.ds(r0, nrows), c0:c0 + LANES] = a_lo
        dst_o[pl.ds(r0, nrows), c0:c0 + LANES] = a_hi
        dst_e[pl.ds(r0, nrows), c1:c1 + LANES] = b_lo
        dst_o[pl.ds(r0, nrows), c1:c1 + LANES] = b_hi


def _attn_core(q_ref, qi_ref, wq_ref, x_ref, wout_ref, o_ref, segs, *, tq, q_pos0, n_sel):
    qrow = lax.broadcasted_iota(I32, (tq, 1), 0)
    q_chunk = (q_pos0 + qrow) // CHUNK
    wq = wq_ref[...].reshape(tq, LANES)
    wi = [wq[:, IDX_DIM + hh:IDX_DIM + hh + 1] for hh in range(IDX_HEADS)]
    qi = qi_ref[...].reshape(tq, IDX_HEADS * IDX_DIM)

    for sg in segs:
        for c0 in range(0, sg["len"], sg["chunk"]):
            cw = min(sg["chunk"], sg["len"] - c0)
            sc = jnp.zeros((tq, cw), F32)
            for m in range(IDX_HEADS // 2):
                qt = qi[:, m * LANES:(m + 1) * LANES]
                de = _dot_nt(qt, sg["kie"][c0:c0 + cw, 0:LANES])
                do = _dot_nt(qt, sg["kio"][c0:c0 + cw, 0:LANES])
                sc = sc + wi[2 * m] * jnp.maximum(de, 0.0) + wi[2 * m + 1] * jnp.maximum(do, 0.0)
            kpos = sg["pos0"] + c0 + lax.broadcasted_iota(I32, (1, cw), 1)
            adm = (kpos // CHUNK) <= q_chunk
            sg["key_s"][:, c0:c0 + cw] = jnp.where(adm, _order_key(sc), KEY_NEG_INF)

    def count(pred):
        tot = jnp.zeros((tq, 1), I32)
        for sg in segs:
            keys = sg["key_s"][:, 0:sg["len"]]
            tot = tot + jnp.sum(pred(keys).astype(I32), axis=-1, keepdims=True)
        return tot

    thr = jnp.where(count(lambda k: k >= 0) >= n_sel, 0, INT_MIN).astype(I32)

    def bit_step(i, thr):
        cand = thr | lax.shift_left(jnp.int32(1), 30 - i)
        return jnp.where(count(lambda k: k >= cand) >= n_sel, cand, thr)

    thr = lax.fori_loop(0, 31, bit_step, thr)
    n_gt = count(lambda k: k > thr)
    n_ge = count(lambda k: k >= thr)
    ties_wanted = n_sel - n_gt

    for sg in segs:
        keys = sg["key_s"][:, 0:sg["len"]]
        sel = (keys >= thr) & (keys > KEY_NEG_INF)
        sg["bias_s"][:, 0:sg["len"]] = jnp.where(sel, 0.0, -jnp.inf)

    has_excess = jnp.max(jnp.where(thr > KEY_NEG_INF, n_ge - n_sel, 0)) > 0

    @pl.when(has_excess)
    def _():
        seen = jnp.zeros((tq, 1), F32)
        wanted = ties_wanted.astype(F32)
        for sg in segs:
            for c0 in range(0, sg["len"], LANES):
                cw = min(LANES, sg["len"] - c0)
                keys = sg["key_s"][:, c0:c0 + cw]
                tie = keys == thr
                r_i = lax.broadcasted_iota(I32, (cw, cw), 0)
                c_i = lax.broadcasted_iota(I32, (cw, cw), 1)
                upper = jnp.where(r_i <= c_i, 1.0, 0.0).astype(BF16)
                tie_f = jnp.where(tie, 1.0, 0.0)
                rank = seen + _dot(tie_f.astype(BF16), upper)
                sel = ((keys > thr) | (tie & (rank <= wanted))) & (keys > KEY_NEG_INF)
                sg["bias_s"][:, c0:c0 + cw] = jnp.where(sel, 0.0, -jnp.inf)
                seen = seen + jnp.sum(tie_f, axis=-1, keepdims=True)

    q = q_ref[...].reshape(tq, N_HEADS * HEAD_DIM)
    tiles = []
    for m in range(N_HEADS // 2):
        g = (2 * m) // (N_HEADS // N_KV_HEADS)
        gs = slice(g * LANES, (g + 1) * LANES)
        qt = q[:, m * LANES:(m + 1) * LANES]
        acc = jnp.zeros((tq, LANES), F32)
        for kname, vname in (("ke", "ve"), ("ko", "vo")):
            logits = [_dot_nt(qt, sg[kname][0:sg["len"], gs]) + sg["bias_s"][:, 0:sg["len"]] for sg in segs]
            mx = functools.reduce(jnp.maximum, [jnp.max(l, axis=-1, keepdims=True) for l in logits])
            ps = [jnp.exp2(l - mx) for l in logits]
            den = functools.reduce(jnp.add, [jnp.sum(p, axis=-1, keepdims=True) for p in ps])
            pv = functools.reduce(jnp.add, [_dot(p.astype(BF16), sg[vname][0:sg["len"], gs])
                                            for p, sg in zip(ps, segs)])
            acc = acc + pv / den
        tiles.append(acc.astype(BF16))
    o = jnp.concatenate(tiles, axis=1)
    d = x_ref.shape[-1]
    o_ref[...] = (_dot(o, wout_ref[...]) + x_ref[...].reshape(tq, d)).reshape(o_ref.shape)


def _attn_prompt_kernel(q_ref, qi_ref, wq_ref, k_ref, v_ref, kw_ref, x_ref, wout_ref, o_ref,
                        ke, ko, vt, kie, kio, qs, key_s, bias_s, ot_s, *, tq, t_len, kc, n_sel):
    j = pl.program_id(1)
    nct = t_len // kc
    sub_per_chunk = kc // LANES
    lane = lax.broadcasted_iota(I32, (1, LANES), 1)
    lo = lane < HEAD_DIM

    @pl.when(j == 0)
    def _():
        for c in range(nct):
            rows = slice(c * kc, (c + 1) * kc)
            k_c = k_ref[0, rows, :]
            for m in range(N_KV_HEADS // 2):
                tile = k_c[:, m * LANES:(m + 1) * LANES]
                rolled = pltpu.roll(tile, HEAD_DIM, 1)
                ke[(2 * m) * nct + c] = jnp.where(lo, tile, 0.0).astype(BF16)
                ko[(2 * m) * nct + c] = jnp.where(lo, 0.0, rolled).astype(BF16)
                ke[(2 * m + 1) * nct + c] = jnp.where(lo, rolled, 0.0).astype(BF16)
                ko[(2 * m + 1) * nct + c] = jnp.where(lo, 0.0, tile).astype(BF16)
            v_t = v_ref[0, rows, :].T
            for g in range(N_KV_HEADS):
                vt[g * nct + c] = v_t[g * HEAD_DIM:(g + 1) * HEAD_DIM, :].astype(BF16)
            kw_c = kw_ref[0, rows, :]
            kie[c] = jnp.where(lo, kw_c, 0.0).astype(BF16)
            kio[c] = jnp.where(lo, 0.0, pltpu.roll(kw_c, HEAD_DIM, 1)).astype(BF16)

    for m in range(N_HEADS // 2):
        qs[m] = q_ref[0, :, m * LANES:(m + 1) * LANES]

    n_chunks = ((j + 1) * tq + kc - 1) // kc
    w_t = wq_ref[0].T
    w_rows = [w_t[IDX_DIM + hh:IDX_DIM + hh + 1, :] for hh in range(IDX_HEADS)]
    qi = qi_ref[0]
    q_chunk = (j * tq + lax.broadcasted_iota(I32, (1, tq), 1)) // CHUNK

    def score_body(c, carry):
        kie_c, kio_c = kie[c], kio[c]
        sc = jnp.zeros((kc, tq), F32)
        for m in range(IDX_HEADS // 2):
            qt = qi[:, m * LANES:(m + 1) * LANES]
            sc = (sc + w_rows[2 * m] * jnp.maximum(_dot_nt(kie_c, qt), 0.0)
                  + w_rows[2 * m + 1] * jnp.maximum(_dot_nt(kio_c, qt), 0.0))
        k_chunk = (c * kc + lax.broadcasted_iota(I32, (kc, 1), 0)) // CHUNK
        key_s[c] = jnp.where(k_chunk <= q_chunk, _order_key(sc), KEY_NEG_INF)
        return carry

    lax.fori_loop(0, n_chunks, score_body, 0)

    def count(pred):
        def body(c, acc):
            hit = pred(key_s[c]).astype(I32)
            return acc + jnp.sum(hit.reshape(kc // 8, 8, tq), axis=0)
        acc = lax.fori_loop(0, n_chunks, body, jnp.zeros((8, tq), I32))
        return jnp.sum(acc, axis=0, keepdims=True)

    thr = jnp.where(count(lambda k: k >= 0) >= n_sel, 0, INT_MIN).astype(I32)

    def bit_step(i, thr):
        cand = thr | lax.shift_left(jnp.int32(1), 30 - i)
        return jnp.where(count(lambda k: k >= cand) >= n_sel, cand, thr)

    thr = lax.fori_loop(0, 31, bit_step, thr)
    n_gt = count(lambda k: k > thr)
    n_ge = count(lambda k: k >= thr)

    def bias_body(c, carry):
        keys = key_s[c]
        bias_s[c] = jnp.where((keys >= thr) & (keys > KEY_NEG_INF), 0.0, -jnp.inf)
        return carry

    lax.fori_loop(0, n_chunks, bias_body, 0)

    has_excess = jnp.max(jnp.where(thr > KEY_NEG_INF, n_ge - n_sel, 0)) > 0

    @pl.when(has_excess)
    def _():
        wanted = (n_sel - n_gt).astype(F32)
        r_i = lax.broadcasted_iota(I32, (LANES, LANES), 0)
        c_i = lax.broadcasted_iota(I32, (LANES, LANES), 1)
        lower = jnp.where(c_i <= r_i, 1.0, 0.0).astype(BF16)

        def tie_body(u, seen):
            c = u // sub_per_chunk
            r0 = pl.multiple_of((u % sub_per_chunk) * LANES, LANES)
            keys = key_s[c, pl.ds(r0, LANES), :]
            tie = keys == thr
            tie_f = jnp.where(tie, 1.0, 0.0)
            rank = seen + _dot(lower, tie_f.astype(BF16))
            sel = ((keys > thr) | (tie & (rank <= wanted))) & (keys > KEY_NEG_INF)
            bias_s[c, pl.ds(r0, LANES), :] = jnp.where(sel, 0.0, -jnp.inf)
            return seen + jnp.sum(tie_f, axis=0, keepdims=True)

        lax.fori_loop(0, n_chunks * sub_per_chunk, tie_body, jnp.zeros((1, tq), F32))

    n_grp = N_HEADS // N_KV_HEADS

    def group_body(g, carry):
        q_tiles = [qs[g * (n_grp // 2) + i] for i in range(n_grp // 2)]

        def chunk_body(c, st):
            bias = bias_s[c]
            v_c = vt[g * nct + c]
            k_e, k_o = ke[g * nct + c], ko[g * nct + c]
            logits = [_dot_nt(k_x, qt) for qt in q_tiles for k_x in (k_e, k_o)]
            tails = [lax.bitcast_convert_type(l[kc - 8:kc, :], I32) for l in logits[1:]]
            anchor = functools.reduce(jnp.bitwise_or, tails) & lax.shift_right_arithmetic(c, 31)
            anchor = lax.bitcast_convert_type(anchor, F32)[0:1, :]
            out = []
            for h in range(n_grp):
                m_run, l_run, acc = st[3 * h:3 * h + 3]
                if h == 0:
                    m_run = m_run + anchor
                logit = logits[h] + bias
                m_new = jnp.maximum(m_run, jnp.max(logit, axis=0, keepdims=True))
                m_safe = jnp.where(m_new > -jnp.inf, m_new, 0.0)
                p = jnp.exp2(logit - m_safe)
                alpha = jnp.exp2(m_run - m_safe)
                out += [m_new, alpha * l_run + jnp.sum(p, axis=0, keepdims=True),
                        alpha * acc + _dot(v_c, p.astype(BF16))]
            return tuple(out)

        init = (jnp.full((1, tq), -jnp.inf, F32), jnp.zeros((1, tq), F32), jnp.zeros((HEAD_DIM, tq), F32))
        st = lax.fori_loop(0, n_chunks, chunk_body, init * n_grp)
        for h in range(n_grp):
            r0 = pl.multiple_of(g * (n_grp * HEAD_DIM) + h * HEAD_DIM, HEAD_DIM)
            ot_s[pl.ds(r0, HEAD_DIM), :] = st[3 * h + 2] / st[3 * h + 1]
        return carry

    lax.fori_loop(0, N_KV_HEADS, group_body, 0)

    o = jnp.concatenate([ot_s[m * LANES:(m + 1) * LANES, :].T for m in range(N_HEADS // 2)], axis=1)
    o_ref[0] = _dot(o.astype(BF16), wout_ref[...]) + x_ref[0]


def _dsa_attn_prompt(q, qi, kw, k, v, x, w_out, *, tq, kc):
    b, t, d = x.shape
    n_sel = min(TOPK_MAX, t // 4)
    n_kv = N_KV_HEADS * HEAD_DIM
    nct = t // kc
    kern = functools.partial(_attn_prompt_kernel, tq=tq, t_len=t, kc=kc, n_sel=n_sel)

    def tile_spec(n):
        return pl.BlockSpec((1, tq, n), lambda i, j: (i, j, 0))

    def full_spec(n):
        return pl.BlockSpec((1, t, n), lambda i, j: (i, 0, 0))

    return pl.pallas_call(
        kern,
        grid=(b, t // tq),
        in_specs=[tile_spec(q.shape[-1]), tile_spec(qi.shape[-1]), tile_spec(LANES),
                  full_spec(n_kv), full_spec(n_kv), full_spec(LANES),
                  tile_spec(d), _const_spec(w_out.shape)],
        out_specs=tile_spec(d),
        out_shape=jax.ShapeDtypeStruct(x.shape, F32),
        scratch_shapes=[
            pltpu.VMEM((N_KV_HEADS * nct, kc, LANES), BF16), pltpu.VMEM((N_KV_HEADS * nct, kc, LANES), BF16),
            pltpu.VMEM((N_KV_HEADS * nct, HEAD_DIM, kc), BF16),
            pltpu.VMEM((nct, kc, LANES), BF16), pltpu.VMEM((nct, kc, LANES), BF16),
            pltpu.VMEM((N_HEADS // 2, tq, LANES), BF16),
            pltpu.VMEM((nct, kc, tq), I32), pltpu.VMEM((nct, kc, tq), F32),
            pltpu.VMEM((N_HEADS * HEAD_DIM, tq), F32),
        ],
        compiler_params=_cparams(2),
        name="dsa_attn_prompt",
    )(q, qi, kw, k, v, kw, x, w_out)


def _attn_sample_kernel(q_ref, qi_ref, wq_ref, k_ref, v_ref, kw_ref, ck_ref, cv_ref, cki_ref,
                        x_ref, wout_ref, o_ref,
                        ke, ko, ve, vo, kie, kio, key_s, bias_s,
                        cke, cko, cve, cvo, ckie, ckio, ckey_s, cbias_s,
                        *, tq, past, kchunk, n_sel):
    _fill_padded(ke, ko, k_ref[0], 0, tq)
    _fill_padded(ve, vo, v_ref[0], 0, tq)
    _fill_padded(kie, kio, kw_ref[0], 0, tq)
    for r0 in range(0, past, kchunk):
        nr = min(kchunk, past - r0)
        _fill_padded(cke, cko, ck_ref[0, r0:r0 + nr, :], r0, nr)
        _fill_padded(cve, cvo, cv_ref[0, r0:r0 + nr, :], r0, nr)
        ki = cki_ref[0, r0:r0 + nr, :]
        zero = jnp.zeros_like(ki)
        ckie[r0:r0 + nr, :] = jnp.concatenate([ki, zero], axis=1).astype(BF16)
        ckio[r0:r0 + nr, :] = jnp.concatenate([zero, ki], axis=1).astype(BF16)
    cache = dict(ke=cke, ko=cko, ve=cve, vo=cvo, kie=ckie, kio=ckio, key_s=ckey_s, bias_s=cbias_s,
                 len=past, pos0=0, chunk=kchunk)
    new = dict(ke=ke, ko=ko, ve=ve, vo=vo, kie=kie, kio=kio, key_s=key_s, bias_s=bias_s,
               len=tq, pos0=past, chunk=tq)
    _attn_core(q_ref, qi_ref, wq_ref, x_ref, wout_ref, o_ref, [cache, new],
               tq=tq, q_pos0=past, n_sel=n_sel)


def _dsa_attn_sample(q, qi, kw, k, v, cache_k, cache_v, cache_ki, x, w_out, *, kchunk):
    b, t, d = x.shape
    past = cache_k.shape[1]
    n_sel = min(TOPK_MAX, (past + t) // 4)
    n_kv = N_KV_HEADS * HEAD_DIM
    kern = functools.partial(_attn_sample_kernel, tq=t, past=past, kchunk=kchunk, n_sel=n_sel)

    def spec(rows, n):
        return pl.BlockSpec((1, rows, n), lambda i: (i, 0, 0))

    return pl.pallas_call(
        kern,
        grid=(b,),
        in_specs=[spec(t, q.shape[-1]), spec(t, qi.shape[-1]), spec(t, LANES),
                  spec(t, n_kv), spec(t, n_kv), spec(t, LANES),
                  spec(past, n_kv), spec(past, n_kv), spec(past, IDX_DIM),
                  spec(t, d), _const_spec(w_out.shape)],
        out_specs=spec(t, d),
        out_shape=jax.ShapeDtypeStruct(x.shape, F32),
        scratch_shapes=[
            pltpu.VMEM((t, 2 * n_kv), BF16), pltpu.VMEM((t, 2 * n_kv), BF16),
            pltpu.VMEM((t, 2 * n_kv), BF16), pltpu.VMEM((t, 2 * n_kv), BF16),
            pltpu.VMEM((t, 2 * LANES), BF16), pltpu.VMEM((t, 2 * LANES), BF16),
            pltpu.VMEM((t, t), I32), pltpu.VMEM((t, t), F32),
            pltpu.VMEM((past, 2 * n_kv), BF16), pltpu.VMEM((past, 2 * n_kv), BF16),
            pltpu.VMEM((past, 2 * n_kv), BF16), pltpu.VMEM((past, 2 * n_kv), BF16),
            pltpu.VMEM((past, LANES), BF16), pltpu.VMEM((past, LANES), BF16),
            pltpu.VMEM((t, past), I32), pltpu.VMEM((t, past), F32),
        ],
        compiler_params=_cparams(1),
        name="dsa_attn_sample",
    )(q, qi, kw, k, v, kw, cache_k, cache_v, cache_ki, x, w_out)


def _tiling(b, t):
    if t >= 256:
        return dict(bb=1, tt=256, blk=64, sub=16)
    bb = max(1, min(b, 128 // t))
    return dict(bb=bb, tt=t, blk=t, sub=t)


def _trunk(x, hg_state, conv_state, cache, p):
    b, t, d = x.shape
    tl = _tiling(b, t)
    bb, tt = tl["bb"], tl["tt"]
    x, s_new = _hgrn_mixer(x, hg_state[0], p["norm_mix"][0], p["hg_w_in"][0], p["lower_bounds"][0],
                           p["hg_norm"][0], p["hg_w_out"][0], **tl)
    x, cs0 = _conv_ffn(x, conv_state[0], p["norm_ffn"][0], p["ffn_w_in"][0], p["ffn_conv_w"][0],
                       p["ffn_conv_b"][0], p["ffn_w_down"][0], p["norm_final"], bb=bb, tt=tt, final_norm=False)
    past = 0 if cache is None else cache[0].shape[1]
    tables = _rope_tables(past + jnp.arange(t))
    q, k, v, qi, kw = _dsa_proj(x, p["norm_mix"][1], p["ds_w_in"][0], tables, p["ds_kln_w"][0],
                                p["ds_kln_b"][0], bb=bb, tt=tt)
    if cache is None:
        x = _dsa_attn_prompt(q, qi, kw, k, v, x, p["ds_w_out"][0], tq=min(t, 512), kc=min(t, 512))
    else:
        ck, cv, cki = cache
        x = _dsa_attn_sample(q, qi, kw, k, v, ck.reshape(b, past, -1), cv.reshape(b, past, -1), cki,
                             x, p["ds_w_out"][0], kchunk=512)
    x, cs1 = _conv_ffn(x, conv_state[1], p["norm_ffn"][1], p["ffn_w_in"][1], p["ffn_conv_w"][1],
                       p["ffn_conv_b"][1], p["ffn_w_down"][1], p["norm_final"], bb=bb, tt=tt, final_norm=True)
    k = k.reshape(1, b, t, N_KV_HEADS, HEAD_DIM)
    v = v.reshape(1, b, t, N_KV_HEADS, HEAD_DIM)
    ki = kw[:, :, :IDX_DIM].reshape(1, b, t, IDX_DIM)
    return x, s_new[None], jnp.stack([cs0, cs1]), k, v, ki


def kernel(x_prompt, x_sample, cache_k, cache_v, cache_kidx, state_hgrn, state_conv, norm_mix, norm_ffn, norm_final, hg_w_in, hg_lb, hg_norm, hg_w_out, ds_w_in, ds_kln_w, ds_kln_b, ds_w_out, ffn_w_in, ffn_conv_w, ffn_conv_b, ffn_w_down):
    dsa_in = ds_w_in.shape[-1]
    dsa_pad = (-dsa_in) % LANES
    p = dict(
        norm_mix=norm_mix, norm_ffn=norm_ffn, norm_final=norm_final,
        hg_w_in=hg_w_in.astype(BF16), hg_norm=hg_norm, hg_w_out=hg_w_out.astype(BF16),
        lower_bounds=jnp.cumsum(jax.nn.softmax(hg_lb.astype(F32), axis=0), axis=0),
        ds_w_in=jnp.pad(ds_w_in, ((0, 0), (0, 0), (0, dsa_pad))).astype(BF16),
        ds_kln_w=ds_kln_w, ds_kln_b=ds_kln_b, ds_w_out=ds_w_out.astype(BF16),
        ffn_w_in=ffn_w_in.astype(BF16), ffn_conv_w=ffn_conv_w, ffn_conv_b=ffn_conv_b,
        ffn_w_down=ffn_w_down.astype(BF16),
    )
    b = x_prompt.shape[0]
    hg0 = jnp.zeros((state_hgrn.shape[0], b) + state_hgrn.shape[2:], F32)
    conv0 = jnp.zeros((state_conv.shape[0], b) + state_conv.shape[2:], F32)
    y_p, hg_p, conv_p, k_p, v_p, ki_p = _trunk(x_prompt, hg0, conv0, None, p)
    y_s, hg_s, conv_s, k_s, v_s, ki_s = _trunk(x_sample, state_hgrn, state_conv,
                                               (cache_k[0], cache_v[0], cache_kidx[0]), p)
    return (y_p, y_s, k_p, v_p, ki_p, hg_p, conv_p, k_s, v_s, ki_s, hg_s, conv_s)
```

```python
import functools

import jax
import jax.numpy as jnp
from jax import lax
from jax.experimental import pallas as pl
from jax.experimental.pallas import tpu as pltpu

F32 = jnp.float32
BF16 = jnp.bfloat16
I32 = jnp.int32

CHUNK = 64
EPS = 1e-6
HG_HEADS = 8
HG_KDIM = 128
N_HEADS = 16
HEAD_DIM = 64
N_KV_HEADS = 4
IDX_HEADS = 8
IDX_DIM = 64
TOPK_MAX = 256
ROPE_THETA = 500000.0
CONV_W = 3

LANES = 128
VMEM_LIMIT = 56 * 1024 * 1024
EXP_CLAMP = 80.0
LOG2_E = 1.4426950408889634
ONES_ROWS = 16
KEY_NEG_INF = -2139095041
INT_MIN = -2147483648


def _cparams(n_axes, flags=None):
    return pltpu.CompilerParams(dimension_semantics=("arbitrary",) * n_axes,
                                vmem_limit_bytes=VMEM_LIMIT, flags=flags)


def _const_spec(shape):
    nd = len(shape)
    return pl.BlockSpec(shape, lambda *_: (0,) * nd, pipeline_mode=pl.Buffered(1))


def _rms(x, g):
    return x * lax.rsqrt(jnp.mean(x * x, axis=-1, keepdims=True) + EPS) * g


def _silu(x):
    return x * jax.nn.sigmoid(x)


def _dot(a, b):
    return jnp.dot(a, b, preferred_element_type=F32)


def _dot_nt(a, b):
    return lax.dot_general(a, b, (((1,), (1,)), ((), ())), preferred_element_type=F32)


def _dot_tn(a, b):
    return lax.dot_general(a, b, (((0,), (0,)), ((), ())), preferred_element_type=F32)


def _hgrn_kernel(x_ref, s0_ref, g_ref, win_ref, lb_ref, hn_ref, wout_ref, o_ref, s_ref,
                 q_s, k_s, g_s, v_s, gate_s, on_s, st_s, o_s, *, bb, tt, blk, sub):
    j = pl.program_id(1)
    rows = bb * tt
    d = x_ref.shape[-1]
    fdim = HG_HEADS * HG_KDIM
    n_sub = blk // sub

    @pl.when(j == 0)
    def _():
        for bi in range(bb):
            for hh in range(HG_HEADS):
                st_s[bi * HG_HEADS + hh] = s0_ref[bi, hh].T

    x = x_ref[...].reshape(rows, d)
    h = _rms(x, g_ref[...]).astype(BF16)
    q_s[...] = _silu(_dot(h, win_ref[:, 0:fdim]))
    lb = lb_ref[...]
    fg = lb + (1.0 - lb) * jax.nn.sigmoid(_dot(h, win_ref[:, fdim:2 * fdim]))
    k_s[...] = 1.0 - fg
    g_s[...] = jnp.log(fg)
    v_s[...] = _dot(h, win_ref[:, 2 * fdim:2 * fdim + d])
    gate_s[...] = _silu(_dot(h, win_ref[:, 2 * fdim + d:]))

    r_i = lax.broadcasted_iota(I32, (blk, blk), 0)
    c_i = lax.broadcasted_iota(I32, (blk, blk), 1)
    causal = c_i <= r_i
    tril = jnp.where(causal, 1.0, 0.0).astype(BF16)
    hn = hn_ref[...]
    blocks_per_stream = tt // blk

    def block_body(idx, carry):
        r0 = pl.multiple_of(idx * blk, blk)
        bi = idx // blocks_per_stream
        lg = g_s[pl.ds(r0, blk), :]
        lg_hi = lg.astype(BF16)
        rem = lg - lg_hi.astype(F32)
        lg_mid = rem.astype(BF16)
        lg_lo = (rem - lg_mid.astype(F32)).astype(BF16)
        g_all = _dot(tril, lg_hi) + _dot(tril, lg_mid) + _dot(tril, lg_lo)
        run_zero = lax.shift_right_arithmetic(idx, 31)

        def anchor(x):
            tail = lax.bitcast_convert_type(x[x.shape[0] - 8:, :], I32) & run_zero
            return lax.bitcast_convert_type(tail, F32)[0:1, :]

        heads = [slice(hh * HG_KDIM, (hh + 1) * HG_KDIM) for hh in range(HG_HEADS)]
        qcat, kcat, qfull, kdec_last, v16, decay = [], [], [], [], [], []
        for sl in heads:
            gc = g_all[:, sl]
            q = q_s[pl.ds(r0, blk), sl]
            kk = k_s[pl.ds(r0, blk), sl]
            refs = [jnp.zeros((1, HG_KDIM), F32)] + [gc[s * sub - 1:s * sub, :] for s in range(1, n_sub)]
            q_parts, k_parts = [], []
            for s in range(n_sub):
                lo_r, hi_r = s * sub, (s + 1) * sub
                qg = q[lo_r:hi_r, :] * jnp.exp(gc[lo_r:hi_r, :] - refs[s])
                pieces = [qg]
                if lo_r:
                    pieces.insert(0, jnp.zeros((lo_r, HG_KDIM), F32))
                if blk - hi_r:
                    pieces.append(jnp.zeros((blk - hi_r, HG_KDIM), F32))
                q_parts.append(pieces[0] if len(pieces) == 1 else jnp.concatenate(pieces, axis=0))
                kd_s = kk[0:hi_r, :] * jnp.exp(jnp.minimum(refs[s] - gc[0:hi_r, :], EXP_CLAMP))
                k_parts.append(kd_s if hi_r == blk else
                               jnp.concatenate([kd_s, jnp.zeros((blk - hi_r, HG_KDIM), F32)], axis=0))
            qcat.append(jnp.concatenate(q_parts, axis=1).astype(BF16))
            kcat.append(jnp.concatenate(k_parts, axis=1).astype(BF16))
            qfull.append((q * jnp.exp(gc)).astype(BF16))
            g_last = gc[blk - 1:blk, :]
            kdec_last.append((kk * jnp.exp(g_last - gc)).astype(BF16))
            decay.append(jnp.exp(g_last))
            v16.append(v_s[pl.ds(r0, blk), sl].astype(BF16))
        a_list = [_dot_nt(qcat[hh], kcat[hh]) for hh in range(HG_HEADS)]
        s_inc = [_dot_tn(v16[hh], kdec_last[hh]) for hh in range(HG_HEADS)]
        a_list[0] = a_list[0] + anchor(a_list[-1])
        o_list = []
        for hh in range(HG_HEADS):
            a = jnp.where(causal, a_list[hh], 0.0).astype(BF16)
            st = st_s[bi * HG_HEADS + hh]
            o_list.append(_dot(a, v16[hh]) + _dot_nt(qfull[hh], st.astype(BF16)))
            st_s[bi * HG_HEADS + hh] = st * decay[hh] + s_inc[hh]
        o_list[0] = o_list[0] + anchor(o_list[-1])
        for hh, sl in enumerate(heads):
            on = _rms(o_list[hh], hn) * gate_s[pl.ds(r0, blk), sl]
            on_s[pl.ds(r0, blk), sl] = on.astype(BF16)
        return carry

    sub_sums = jnp.sum(g_s[...].reshape(rows // sub, sub, fdim), axis=1)
    blocked_ok = jnp.min(sub_sums) >= -EXP_CLAMP

    @pl.when(blocked_ok)
    def _():
        lax.fori_loop(0, rows // blk, block_body, 0, unroll=2)

    @pl.when(jnp.logical_not(blocked_ok))
    def _():
        grp = 16
        o_s[...] = jnp.zeros(o_s.shape, F32)
        in_grp = lax.broadcasted_iota(I32, (grp, 1), 0)

        def frame_body(r, carry):
            r0 = pl.multiple_of((r // grp) * grp, grp)
            bi = r // tt
            this = in_grp == (r % grp)
            for hh in range(HG_HEADS):
                sl = slice(hh * HG_KDIM, (hh + 1) * HG_KDIM)

                def only(ref):
                    return jnp.where(this, ref[pl.ds(r0, grp), sl], 0.0)

                forget = jnp.exp(jnp.sum(only(g_s), axis=0, keepdims=True))
                st = (st_s[bi * HG_HEADS + hh] * forget
                      + _dot_tn(only(v_s).astype(BF16), only(k_s).astype(BF16)))
                st_s[bi * HG_HEADS + hh] = st
                o_s[pl.ds(r0, grp), sl] += _dot_nt(only(q_s).astype(BF16), st.astype(BF16))
            return carry

        lax.fori_loop(0, rows, frame_body, 0)
        for hh in range(HG_HEADS):
            sl = slice(hh * HG_KDIM, (hh + 1) * HG_KDIM)
            on_s[:, sl] = (_rms(o_s[:, sl], hn) * gate_s[:, sl]).astype(BF16)

    out = _dot(on_s[...], wout_ref[...]) + x
    o_ref[...] = out.reshape(bb, tt, d)

    @pl.when(j == pl.num_programs(1) - 1)
    def _():
        for bi in range(bb):
            for hh in range(HG_HEADS):
                s_ref[bi, hh] = st_s[bi * HG_HEADS + hh].T


def _hgrn_mixer(x, s0, norm_g, w_in, lb, hn, w_out, *, bb, tt, blk, sub):
    b, t, d = x.shape
    fdim = HG_HEADS * HG_KDIM
    rows = bb * tt
    kern = functools.partial(_hgrn_kernel, bb=bb, tt=tt, blk=blk, sub=sub)
    return pl.pallas_call(
        kern,
        grid=(b // bb, t // tt),
        in_specs=[
            pl.BlockSpec((bb, tt, d), lambda i, j: (i, j, 0)),
            pl.BlockSpec((bb, HG_HEADS, HG_KDIM, d // HG_HEADS), lambda i, j: (i, 0, 0, 0)),
            _const_spec((1, d)),
            _const_spec(w_in.shape),
            _const_spec((1, fdim)),
            _const_spec((1, d // HG_HEADS)),
            _const_spec(w_out.shape),
        ],
        out_specs=[
            pl.BlockSpec((bb, tt, d), lambda i, j: (i, j, 0)),
            pl.BlockSpec((bb, HG_HEADS, HG_KDIM, d // HG_HEADS), lambda i, j: (i, 0, 0, 0)),
        ],
        out_shape=[jax.ShapeDtypeStruct(x.shape, F32), jax.ShapeDtypeStruct(s0.shape, F32)],
        scratch_shapes=[
            pltpu.VMEM((rows, fdim), F32),
            pltpu.VMEM((rows, fdim), F32),
            pltpu.VMEM((rows, fdim), F32),
            pltpu.VMEM((rows, d), F32),
            pltpu.VMEM((rows, d), F32),
            pltpu.VMEM((rows, d), BF16),
            pltpu.VMEM((bb * HG_HEADS, d // HG_HEADS, HG_KDIM), F32),
            pltpu.VMEM((rows, d), F32),
        ],
        compiler_params=_cparams(2),
        name="hgrn_mixer",
    )(x, s0, norm_g.reshape(1, d), w_in, lb.reshape(1, fdim), hn.reshape(1, -1), w_out)


CONV_HEAD = 8


def _ffn_kernel(x_ref, cs_ref, g_ref, win_ref, cw_ref, cb_ref, wdn_ref, fg_ref, o_ref, ns_ref,
                a_s, *, bb, tt, final_norm):
    j = pl.program_id(1)
    rows = bb * tt
    d = x_ref.shape[-1]
    dff = cw_ref.shape[-1]
    hist = CONV_W - 1

    @pl.when(j == 0)
    def _():
        a_s[:, CONV_HEAD - hist:CONV_HEAD, :] = cs_ref[...]

    x = x_ref[...].reshape(rows, d)
    h = _rms(x, g_ref[...]).astype(BF16)
    a = _dot(h, win_ref[:, 0:dff])
    u = _dot(h, win_ref[:, dff:])
    a_s[:, CONV_HEAD:CONV_HEAD + tt, :] = a.reshape(bb, tt, dff)
    c = jnp.broadcast_to(cb_ref[...].reshape(1, 1, dff), (bb, tt, dff))
    for w in range(CONV_W):
        lo = CONV_HEAD - hist + w
        c = c + a_s[:, lo:lo + tt, :] * cw_ref[w:w + 1, :].reshape(1, 1, dff)
    new_state = a_s[:, CONV_HEAD + tt - hist:CONV_HEAD + tt, :]
    a_s[:, CONV_HEAD - hist:CONV_HEAD, :] = new_state
    ns_ref[...] = new_state
    act = (_silu(c).reshape(rows, dff) * u).astype(BF16)
    y = _dot(act, wdn_ref[...]) + x
    if final_norm:
        y = _rms(y, fg_ref[...])
    o_ref[...] = y.reshape(bb, tt, d)


def _conv_ffn(x, conv_state, norm_g, w_in, conv_w, conv_b, w_down, final_g, *, bb, tt, final_norm):
    b, t, d = x.shape
    dff = conv_w.shape[-1]
    kern = functools.partial(_ffn_kernel, bb=bb, tt=tt, final_norm=final_norm)
    return pl.pallas_call(
        kern,
        grid=(b // bb, t // tt),
        in_specs=[
            pl.BlockSpec((bb, tt, d), lambda i, j: (i, j, 0)),
            pl.BlockSpec((bb, CONV_W - 1, dff), lambda i, j: (i, 0, 0)),
            _const_spec((1, d)),
            _const_spec(w_in.shape),
            _const_spec(conv_w.shape),
            _const_spec((1, dff)),
            _const_spec(w_down.shape),
            _const_spec((1, d)),
        ],
        out_specs=[
            pl.BlockSpec((bb, tt, d), lambda i, j: (i, j, 0)),
            pl.BlockSpec((bb, CONV_W - 1, dff), lambda i, j: (i, 0, 0)),
        ],
        out_shape=[jax.ShapeDtypeStruct(x.shape, F32), jax.ShapeDtypeStruct(conv_state.shape, F32)],
        scratch_shapes=[pltpu.VMEM((bb, CONV_HEAD + tt, dff), F32)],
        compiler_params=_cparams(2),
        name="conv_ffn",
    )(x, conv_state, norm_g.reshape(1, d), w_in, conv_w, conv_b.reshape(1, dff), w_down,
      final_g.reshape(1, d))


def _rope_tables(pos):
    rot = HEAD_DIM // 4
    half = rot // 2
    inv_freq = ROPE_THETA ** (-jnp.arange(half, dtype=F32) / half)
    ang = pos.astype(F32)[:, None] * inv_freq[None, :]
    cos, sin = jnp.cos(ang), jnp.sin(ang)
    t = pos.shape[0]
    pad = HEAD_DIM - rot
    one = jnp.ones((t, pad), F32)
    zero_h = jnp.zeros((t, half), F32)
    zero_p = jnp.zeros((t, pad), F32)
    c64 = jnp.concatenate([cos, cos, one], axis=1)
    s1_64 = jnp.concatenate([zero_h, sin, zero_p], axis=1)
    s2_64 = jnp.concatenate([-sin, zero_h, zero_p], axis=1)
    ident = jnp.ones((t, HEAD_DIM), F32)
    zero64 = jnp.zeros((t, HEAD_DIM), F32)
    return jnp.stack([
        jnp.concatenate([c64, c64], axis=1), jnp.concatenate([s1_64, s1_64], axis=1),
        jnp.concatenate([s2_64, s2_64], axis=1),
        jnp.concatenate([c64, ident], axis=1), jnp.concatenate([s1_64, zero64], axis=1),
        jnp.concatenate([s2_64, zero64], axis=1)])


def _dsa_proj_kernel(x_ref, g_ref, w_ref, tab_ref, lnw_ref, lnb_ref,
                     q_ref, k_ref, v_ref, qi_ref, kw_ref, *, bb, tt):
    rows = bb * tt
    d = x_ref.shape[-1]
    half = HEAD_DIM // 8
    o1 = N_HEADS * HEAD_DIM
    o2 = o1 + N_KV_HEADS * HEAD_DIM
    o3 = o2 + N_KV_HEADS * HEAD_DIM
    o4 = o3 + IDX_HEADS * IDX_DIM

    def tab(i):
        t = tab_ref[i]
        return jnp.broadcast_to(t[None], (bb, tt, LANES)).reshape(rows, LANES)

    def rope(y, base):
        cos, s1, s2 = tab(base), tab(base + 1), tab(base + 2)
        tiles = []
        for m in range(y.shape[1] // LANES):
            yt = y[:, m * LANES:(m + 1) * LANES]
            tiles.append(yt * cos + pltpu.roll(yt, half, 1) * s1 + pltpu.roll(yt, LANES - half, 1) * s2)
        return tiles[0] if len(tiles) == 1 else jnp.concatenate(tiles, axis=1)

    x = x_ref[...].reshape(rows, d)
    h = _rms(x, g_ref[...]).astype(BF16)
    q = rope(_dot(h, w_ref[:, 0:o1]), 0) * (HEAD_DIM ** -0.5 * LOG2_E)
    q_ref[...] = q.astype(BF16).reshape(bb, tt, o1)
    k_ref[...] = rope(_dot(h, w_ref[:, o1:o2]), 0).reshape(bb, tt, o2 - o1)
    v_ref[...] = _dot(h, w_ref[:, o2:o3]).reshape(bb, tt, o3 - o2)
    qi_ref[...] = rope(_dot(h, w_ref[:, o3:o4]), 0).astype(BF16).reshape(bb, tt, o4 - o3)
    t = _dot(h, w_ref[:, o4:o4 + LANES])
    lane = lax.broadcasted_iota(I32, (1, LANES), 1)
    is_ki = lane < IDX_DIM
    mu = jnp.sum(jnp.where(is_ki, t, 0.0), axis=-1, keepdims=True) / IDX_DIM
    cen = jnp.where(is_ki, t - mu, 0.0)
    var = jnp.sum(cen * cen, axis=-1, keepdims=True) / IDX_DIM
    ki = rope(cen * lax.rsqrt(var + EPS) * lnw_ref[...] + lnb_ref[...], 3)
    wi = t * ((IDX_HEADS * IDX_DIM) ** -0.5)
    kw = jnp.where(is_ki, ki, jnp.where(lane < IDX_DIM + IDX_HEADS, wi, 0.0))
    kw_ref[...] = kw.reshape(bb, tt, LANES)


def _dsa_proj(x, norm_g, w_pad, tables, ln_w, ln_b, *, bb, tt):
    b, t, d = x.shape
    n_q = N_HEADS * HEAD_DIM
    n_kv = N_KV_HEADS * HEAD_DIM
    n_qi = IDX_HEADS * IDX_DIM
    pad = jnp.zeros((LANES - IDX_DIM,), F32)
    lnw = jnp.concatenate([ln_w, pad]).reshape(1, LANES)
    lnb = jnp.concatenate([ln_b, pad]).reshape(1, LANES)
    kern = functools.partial(_dsa_proj_kernel, bb=bb, tt=tt)

    def row_spec(n):
        return pl.BlockSpec((bb, tt, n), lambda i, j: (i, j, 0))

    return pl.pallas_call(
        kern,
        grid=(b // bb, t // tt),
        in_specs=[
            row_spec(d),
            _const_spec((1, d)),
            _const_spec(w_pad.shape),
            pl.BlockSpec((6, tt, LANES), lambda i, j: (0, j, 0)),
            _const_spec((1, LANES)),
            _const_spec((1, LANES)),
        ],
        out_specs=[row_spec(n_q), row_spec(n_kv), row_spec(n_kv), row_spec(n_qi), row_spec(LANES)],
        out_shape=[
            jax.ShapeDtypeStruct((b, t, n_q), BF16),
            jax.ShapeDtypeStruct((b, t, n_kv), F32),
            jax.ShapeDtypeStruct((b, t, n_kv), F32),
            jax.ShapeDtypeStruct((b, t, n_qi), BF16),
            jax.ShapeDtypeStruct((b, t, LANES), F32),
        ],
        compiler_params=_cparams(2),
        name="dsa_proj",
    )(x, norm_g.reshape(1, d), w_pad, tables, lnw, lnb)


def _order_key(score):
    score = jnp.where(score == 0.0, 0.0, score)
    bits = lax.bitcast_convert_type(score, I32)
    return bits ^ ((bits >> 31) & 0x7FFFFFFF)


def _fill_padded(dst_e, dst_o, src, r0, nrows):
    lane = lax.broadcasted_iota(I32, (1, LANES), 1)
    lo = lane < HEAD_DIM
    for m in range(src.shape[1] // LANES):
        tile = src[:, m * LANES:(m + 1) * LANES]
        rolled = pltpu.roll(tile, HEAD_DIM, 1)
        a_lo = jnp.where(lo, tile, 0.0).astype(BF16)
        b_hi = jnp.where(lo, 0.0, tile).astype(BF16)
        b_lo = jnp.where(lo, rolled, 0.0).astype(BF16)
        a_hi = jnp.where(lo, 0.0, rolled).astype(BF16)
        c0, c1 = 2 * m * LANES, (2 * m + 1) * LANES
        dst_e[pl.ds(r0, nrows), c0:c0 + LANES] = a_lo
        dst_o[pl.ds(r0, nrows), c0:c0 + LANES] = a_hi
        dst_e[pl.ds(r0, nrows), c1:c1 + LANES] = b_lo
        dst_o[pl.ds(r0, nrows), c1:c1 + LANES] = b_hi


def _attn_core(q_ref, qi_ref, wq_ref, x_ref, wout_ref, o_ref, segs, *, tq, q_pos0, n_sel):
    qrow = lax.broadcasted_iota(I32, (tq, 1), 0)
    q_chunk = (q_pos0 + qrow) // CHUNK
    wq = wq_ref[...].reshape(tq, LANES)
    wi = [wq[:, IDX_DIM + hh:IDX_DIM + hh + 1] for hh in range(IDX_HEADS)]
    qi = qi_ref[...].reshape(tq, IDX_HEADS * IDX_DIM)

    for sg in segs:
        for c0 in range(0, sg["len"], sg["chunk"]):
            cw = min(sg["chunk"], sg["len"] - c0)
            sc = jnp.zeros((tq, cw), F32)
            for m in range(IDX_HEADS // 2):
                qt = qi[:, m * LANES:(m + 1) * LANES]
                de = _dot_nt(qt, sg["kie"][c0:c0 + cw, 0:LANES])
                do = _dot_nt(qt, sg["kio"][c0:c0 + cw, 0:LANES])
                sc = sc + wi[2 * m] * jnp.maximum(de, 0.0) + wi[2 * m + 1] * jnp.maximum(do, 0.0)
            kpos = sg["pos0"] + c0 + lax.broadcasted_iota(I32, (1, cw), 1)
            adm = (kpos // CHUNK) <= q_chunk
            sg["key_s"][:, c0:c0 + cw] = jnp.where(adm, _order_key(sc), KEY_NEG_INF)

    def count(pred):
        tot = jnp.zeros((tq, 1), I32)
        for sg in segs:
            keys = sg["key_s"][:, 0:sg["len"]]
            tot = tot + jnp.sum(pred(keys).astype(I32), axis=-1, keepdims=True)
        return tot

    thr = jnp.where(count(lambda k: k >= 0) >= n_sel, 0, INT_MIN).astype(I32)

    def bit_step(i, thr):
        cand = thr | lax.shift_left(jnp.int32(1), 30 - i)
        return jnp.where(count(lambda k: k >= cand) >= n_sel, cand, thr)

    thr = lax.fori_loop(0, 31, bit_step, thr)
    n_gt = count(lambda k: k > thr)
    n_ge = count(lambda k: k >= thr)
    ties_wanted = n_sel - n_gt

    for sg in segs:
        keys = sg["key_s"][:, 0:sg["len"]]
        sel = (keys >= thr) & (keys > KEY_NEG_INF)
        sg["bias_s"][:, 0:sg["len"]] = jnp.where(sel, 0.0, -jnp.inf)

    has_excess = jnp.max(jnp.where(thr > KEY_NEG_INF, n_ge - n_sel, 0)) > 0

    @pl.when(has_excess)
    def _():
        seen = jnp.zeros((tq, 1), F32)
        wanted = ties_wanted.astype(F32)
        for sg in segs:
            for c0 in range(0, sg["len"], LANES):
                cw = min(LANES, sg["len"] - c0)
                keys = sg["key_s"][:, c0:c0 + cw]
                tie = keys == thr
                r_i = lax.broadcasted_iota(I32, (cw, cw), 0)
                c_i = lax.broadcasted_iota(I32, (cw, cw), 1)
                upper = jnp.where(r_i <= c_i, 1.0, 0.0).astype(BF16)
                tie_f = jnp.where(tie, 1.0, 0.0)
                rank = seen + _dot(tie_f.astype(BF16), upper)
                sel = ((keys > thr) | (tie & (rank <= wanted))) & (keys > KEY_NEG_INF)
                sg["bias_s"][:, c0:c0 + cw] = jnp.where(sel, 0.0, -jnp.inf)
                seen = seen + jnp.sum(tie_f, axis=-1, keepdims=True)

    q = q_ref[...].reshape(tq, N_HEADS * HEAD_DIM)
    tiles = []
    for m in range(N_HEADS // 2):
        g = (2 * m) // (N_HEADS // N_KV_HEADS)
        gs = slice(g * LANES, (g + 1) * LANES)
        qt = q[:, m * LANES:(m + 1) * LANES]
        acc = jnp.zeros((tq, LANES), F32)
        for kname, vname in (("ke", "ve"), ("ko", "vo")):
            logits = [_dot_nt(qt, sg[kname][0:sg["len"], gs]) + sg["bias_s"][:, 0:sg["len"]] for sg in segs]
            mx = functools.reduce(jnp.maximum, [jnp.max(l, axis=-1, keepdims=True) for l in logits])
            ps = [jnp.exp2(l - mx) for l in logits]
            den = functools.reduce(jnp.add, [jnp.sum(p, axis=-1, keepdims=True) for p in ps])
            pv = functools.reduce(jnp.add, [_dot(p.astype(BF16), sg[vname][0:sg["len"], gs])
                                            for p, sg in zip(ps, segs)])
            acc = acc + pv / den
        tiles.append(acc.astype(BF16))
    o = jnp.concatenate(tiles, axis=1)
    d = x_ref.shape[-1]
    o_ref[...] = (_dot(o, wout_ref[...]) + x_ref[...].reshape(tq, d)).reshape(o_ref.shape)


def _attn_prompt_kernel(q_ref, qi_ref, wq_ref, k_ref, v_ref, kw_ref, x_ref, wout_ref, o_ref,
                        ke, ko, vt, kie, kio, qs, key_s, bias_s, ot_s, *, tq, t_len, kc, n_sel):
    j = pl.program_id(1)
    nct = t_len // kc
    sub_per_chunk = kc // LANES
    lane = lax.broadcasted_iota(I32, (1, LANES), 1)
    lo = lane < HEAD_DIM

    @pl.when(j == 0)
    def _():
        for c in range(nct):
            rows = slice(c * kc, (c + 1) * kc)
            k_c = k_ref[0, rows, :]
            for m in range(N_KV_HEADS // 2):
                tile = k_c[:, m * LANES:(m + 1) * LANES]
                rolled = pltpu.roll(tile, HEAD_DIM, 1)
                ke[(2 * m) * nct + c] = jnp.where(lo, tile, 0.0).astype(BF16)
                ko[(2 * m) * nct + c] = jnp.where(lo, 0.0, rolled).astype(BF16)
                ke[(2 * m + 1) * nct + c] = jnp.where(lo, rolled, 0.0).astype(BF16)
                ko[(2 * m + 1) * nct + c] = jnp.where(lo, 0.0, tile).astype(BF16)
            v_t = v_ref[0, rows, :].T
            for g in range(N_KV_HEADS):
                vt[g * nct + c, 0:HEAD_DIM, :] = v_t[g * HEAD_DIM:(g + 1) * HEAD_DIM, :].astype(BF16)
                vt[g * nct + c, HEAD_DIM:HEAD_DIM + ONES_ROWS, :] = jnp.ones((ONES_ROWS, kc), BF16)
            kw_c = kw_ref[0, rows, :]
            kie[c] = jnp.where(lo, kw_c, 0.0).astype(BF16)
            kio[c] = jnp.where(lo, 0.0, pltpu.roll(kw_c, HEAD_DIM, 1)).astype(BF16)

    for m in range(N_HEADS // 2):
        qs[m] = q_ref[0, :, m * LANES:(m + 1) * LANES]

    n_chunks = ((j + 1) * tq + kc - 1) // kc
    w_t = wq_ref[0].T
    w_rows = [w_t[IDX_DIM + hh:IDX_DIM + hh + 1, :] for hh in range(IDX_HEADS)]
    qi = qi_ref[0]
    q_chunk = (j * tq + lax.broadcasted_iota(I32, (1, tq), 1)) // CHUNK

    def score_body(c, carry):
        kie_c, kio_c = kie[c], kio[c]
        sc = jnp.zeros((kc, tq), F32)
        for m in range(IDX_HEADS // 2):
            qt = qi[:, m * LANES:(m + 1) * LANES]
            sc = (sc + w_rows[2 * m] * jnp.maximum(_dot_nt(kie_c, qt), 0.0)
                  + w_rows[2 * m + 1] * jnp.maximum(_dot_nt(kio_c, qt), 0.0))
        k_chunk = (c * kc + lax.broadcasted_iota(I32, (kc, 1), 0)) // CHUNK
        key_s[c] = jnp.where(k_chunk <= q_chunk, _order_key(sc), KEY_NEG_INF)
        return carry

    lax.fori_loop(0, n_chunks, score_body, 0)

    def count(pred):
        def body(c, acc):
            hit = pred(key_s[c]).astype(I32)
            return acc + jnp.sum(hit.reshape(kc // 8, 8, tq), axis=0)
        acc = lax.fori_loop(0, n_chunks, body, jnp.zeros((8, tq), I32))
        return jnp.sum(acc, axis=0, keepdims=True)

    def bit_step(i, thr):
        cand = thr ^ lax.shift_left(jnp.int32(1), 31 - i)
        return jnp.where(count(lambda k: k >= cand) >= n_sel, cand, thr)

    thr = lax.fori_loop(0, 32, bit_step, jnp.full((1, tq), INT_MIN, I32))

    def bias_body(c, acc):
        keys = key_s[c]
        ge = keys >= thr
        bias_s[c] = jnp.where(ge & (keys > KEY_NEG_INF), 0.0, -jnp.inf)
        fold = lambda hit: jnp.sum(hit.astype(I32).reshape(kc // 8, 8, tq), axis=0)
        return acc[0] + fold(ge), acc[1] + fold(keys > thr)

    zero8 = jnp.zeros((8, tq), I32)
    n_ge, n_gt = [jnp.sum(a, axis=0, keepdims=True)
                  for a in lax.fori_loop(0, n_chunks, bias_body, (zero8, zero8))]

    has_excess = jnp.max(jnp.where(thr > KEY_NEG_INF, n_ge - n_sel, 0)) > 0

    @pl.when(has_excess)
    def _():
        wanted = (n_sel - n_gt).astype(F32)
        r_i = lax.broadcasted_iota(I32, (LANES, LANES), 0)
        c_i = lax.broadcasted_iota(I32, (LANES, LANES), 1)
        lower = jnp.where(c_i <= r_i, 1.0, 0.0).astype(BF16)

        def tie_body(u, seen):
            c = u // sub_per_chunk
            r0 = pl.multiple_of((u % sub_per_chunk) * LANES, LANES)
            keys = key_s[c, pl.ds(r0, LANES), :]
            tie = keys == thr
            tie_f = jnp.where(tie, 1.0, 0.0)
            rank = seen + _dot(lower, tie_f.astype(BF16))
            sel = ((keys > thr) | (tie & (rank <= wanted))) & (keys > KEY_NEG_INF)
            bias_s[c, pl.ds(r0, LANES), :] = jnp.where(sel, 0.0, -jnp.inf)
            return seen + jnp.sum(tie_f, axis=0, keepdims=True)

        lax.fori_loop(0, n_chunks * sub_per_chunk, tie_body, jnp.zeros((1, tq), F32))

    n_grp = N_HEADS // N_KV_HEADS

    def group_body(g, carry):
        q_tiles = [qs[g * (n_grp // 2) + i] for i in range(n_grp // 2)]

        def chunk_body(c, st):
            bias = bias_s[c]
            v_c = vt[g * nct + c]
            k_e, k_o = ke[g * nct + c], ko[g * nct + c]
            logits = [_dot_nt(k_x, qt) for qt in q_tiles for k_x in (k_e, k_o)]
            tails = [lax.bitcast_convert_type(l[kc - 8:kc, :], I32) for l in logits[1:]]
            anchor = functools.reduce(jnp.bitwise_or, tails) & lax.shift_right_arithmetic(c, 31)
            anchor = lax.bitcast_convert_type(anchor, F32)[0:1, :]
            out = []
            for h in range(n_grp):
                m_run, acc = st[2 * h:2 * h + 2]
                if h == 0:
                    m_run = m_run + anchor
                logit = logits[h] + bias
                m_new = jnp.maximum(m_run, jnp.max(logit, axis=0, keepdims=True))
                m_safe = jnp.where(m_new > -jnp.inf, m_new, 0.0)
                p = jnp.exp2(logit - m_safe)
                alpha = jnp.exp2(m_run - m_safe)
                out += [m_new, alpha * acc + _dot(v_c, p.astype(BF16))]
            return tuple(out)

        init = (jnp.full((1, tq), -jnp.inf, F32), jnp.zeros((HEAD_DIM + ONES_ROWS, tq), F32))
        st = lax.fori_loop(0, n_chunks, chunk_body, init * n_grp)
        for h in range(n_grp):
            r0 = pl.multiple_of(g * (n_grp * HEAD_DIM) + h * HEAD_DIM, HEAD_DIM)
            acc = st[2 * h + 1]
            ot_s[pl.ds(r0, HEAD_DIM), :] = acc[0:HEAD_DIM, :] / acc[HEAD_DIM:HEAD_DIM + 1, :]
        return carry

    lax.fori_loop(0, N_KV_HEADS, group_body, 0)

    o = jnp.concatenate([ot_s[m * LANES:(m + 1) * LANES, :].T for m in range(N_HEADS // 2)], axis=1)
    o_ref[0] = _dot(o.astype(BF16), wout_ref[...]) + x_ref[0]


def _dsa_attn_prompt(q, qi, kw, k, v, x, w_out, *, tq, kc):
    b, t, d = x.shape
    n_sel = min(TOPK_MAX, t // 4)
    n_kv = N_KV_HEADS * HEAD_DIM
    nct = t // kc
    kern = functools.partial(_attn_prompt_kernel, tq=tq, t_len=t, kc=kc, n_sel=n_sel)

    def tile_spec(n):
        return pl.BlockSpec((1, tq, n), lambda i, j: (i, j, 0))

    def full_spec(n):
        return pl.BlockSpec((1, t, n), lambda i, j: (i, 0, 0))

    return pl.pallas_call(
        kern,
        grid=(b, t // tq),
        in_specs=[tile_spec(q.shape[-1]), tile_spec(qi.shape[-1]), tile_spec(LANES),
                  full_spec(n_kv), full_spec(n_kv), full_spec(LANES),
                  tile_spec(d), _const_spec(w_out.shape)],
        out_specs=tile_spec(d),
        out_shape=jax.ShapeDtypeStruct(x.shape, F32),
        scratch_shapes=[
            pltpu.VMEM((N_KV_HEADS * nct, kc, LANES), BF16), pltpu.VMEM((N_KV_HEADS * nct, kc, LANES), BF16),
            pltpu.VMEM((N_KV_HEADS * nct, HEAD_DIM + ONES_ROWS, kc), BF16),
            pltpu.VMEM((nct, kc, LANES), BF16), pltpu.VMEM((nct, kc, LANES), BF16),
            pltpu.VMEM((N_HEADS // 2, tq, LANES), BF16),
            pltpu.VMEM((nct, kc, tq), I32), pltpu.VMEM((nct, kc, tq), F32),
            pltpu.VMEM((N_HEADS * HEAD_DIM, tq), F32),
        ],
        compiler_params=_cparams(2),
        name="dsa_attn_prompt",
    )(q, qi, kw, k, v, kw, x, w_out)


def _attn_sample_kernel(q_ref, qi_ref, wq_ref, k_ref, v_ref, kw_ref, ck_ref, cv_ref, cki_ref,
                        x_ref, wout_ref, o_ref,
                        ke, ko, ve, vo, kie, kio, key_s, bias_s,
                        cke, cko, cve, cvo, ckie, ckio, ckey_s, cbias_s,
                        *, tq, past, kchunk, n_sel):
    _fill_padded(ke, ko, k_ref[0], 0, tq)
    _fill_padded(ve, vo, v_ref[0], 0, tq)
    _fill_padded(kie, kio, kw_ref[0], 0, tq)
    for r0 in range(0, past, kchunk):
        nr = min(kchunk, past - r0)
        _fill_padded(cke, cko, ck_ref[0, r0:r0 + nr, :], r0, nr)
        _fill_padded(cve, cvo, cv_ref[0, r0:r0 + nr, :], r0, nr)
        ki = cki_ref[0, r0:r0 + nr, :]
        zero = jnp.zeros_like(ki)
        ckie[r0:r0 + nr, :] = jnp.concatenate([ki, zero], axis=1).astype(BF16)
        ckio[r0:r0 + nr, :] = jnp.concatenate([zero, ki], axis=1).astype(BF16)
    cache = dict(ke=cke, ko=cko, ve=cve, vo=cvo, kie=ckie, kio=ckio, key_s=ckey_s, bias_s=cbias_s,
                 len=past, pos0=0, chunk=kchunk)
    new = dict(ke=ke, ko=ko, ve=ve, vo=vo, kie=kie, kio=kio, key_s=key_s, bias_s=bias_s,
               len=tq, pos0=past, chunk=tq)
    _attn_core(q_ref, qi_ref, wq_ref, x_ref, wout_ref, o_ref, [cache, new],
               tq=tq, q_pos0=past, n_sel=n_sel)


def _dsa_attn_sample(q, qi, kw, k, v, cache_k, cache_v, cache_ki, x, w_out, *, kchunk):
    b, t, d = x.shape
    past = cache_k.shape[1]
    n_sel = min(TOPK_MAX, (past + t) // 4)
    n_kv = N_KV_HEADS * HEAD_DIM
    kern = functools.partial(_attn_sample_kernel, tq=t, past=past, kchunk=kchunk, n_sel=n_sel)

    def spec(rows, n):
        return pl.BlockSpec((1, rows, n), lambda i: (i, 0, 0))

    return pl.pallas_call(
        kern,
        grid=(b,),
        in_specs=[spec(t, q.shape[-1]), spec(t, qi.shape[-1]), spec(t, LANES),
                  spec(t, n_kv), spec(t, n_kv), spec(t, LANES),
                  spec(past, n_kv), spec(past, n_kv), spec(past, IDX_DIM),
                  spec(t, d), _const_spec(w_out.shape)],
        out_specs=spec(t, d),
        out_shape=jax.ShapeDtypeStruct(x.shape, F32),
        scratch_shapes=[
            pltpu.VMEM((t, 2 * n_kv), BF16), pltpu.VMEM((t, 2 * n_kv), BF16),
            pltpu.VMEM((t, 2 * n_kv), BF16), pltpu.VMEM((t, 2 * n_kv), BF16),
            pltpu.VMEM((t, 2 * LANES), BF16), pltpu.VMEM((t, 2 * LANES), BF16),
            pltpu.VMEM((t, t), I32), pltpu.VMEM((t, t), F32),
            pltpu.VMEM((past, 2 * n_kv), BF16), pltpu.VMEM((past, 2 * n_kv), BF16),
            pltpu.VMEM((past, 2 * n_kv), BF16), pltpu.VMEM((past, 2 * n_kv), BF16),
            pltpu.VMEM((past, LANES), BF16), pltpu.VMEM((past, LANES), BF16),
            pltpu.VMEM((t, past), I32), pltpu.VMEM((t, past), F32),
        ],
        compiler_params=_cparams(1),
        name="dsa_attn_sample",
    )(q, qi, kw, k, v, kw, cache_k, cache_v, cache_ki, x, w_out)


def _tiling(b, t):
    if t >= 256:
        return dict(bb=1, tt=256, blk=64, sub=16)
    bb = max(1, min(b, 128 // t))
    return dict(bb=bb, tt=t, blk=t, sub=t)


def _trunk(x, hg_state, conv_state, cache, p):
    b, t, d = x.shape
    tl = _tiling(b, t)
    bb, tt = tl["bb"], tl["tt"]
    x, s_new = _hgrn_mixer(x, hg_state[0], p["norm_mix"][0], p["hg_w_in"][0], p["lower_bounds"][0],
                           p["hg_norm"][0], p["hg_w_out"][0], **dict(tl, tt=512 if t % 512 == 0 else tt))
    x, cs0 = _conv_ffn(x, conv_state[0], p["norm_ffn"][0], p["ffn_w_in"][0], p["ffn_conv_w"][0],
                       p["ffn_conv_b"][0], p["ffn_w_down"][0], p["norm_final"], bb=bb, tt=tt, final_norm=False)
    past = 0 if cache is None else cache[0].shape[1]
    tables = _rope_tables(past + jnp.arange(t))
    q, k, v, qi, kw = _dsa_proj(x, p["norm_mix"][1], p["ds_w_in"][0], tables, p["ds_kln_w"][0],
                                p["ds_kln_b"][0], bb=bb, tt=tt)
    if cache is None:
        x = _dsa_attn_prompt(q, qi, kw, k, v, x, p["ds_w_out"][0], tq=min(t, 512), kc=min(t, 512))
    else:
        ck, cv, cki = cache
        x = _dsa_attn_sample(q, qi, kw, k, v, ck.reshape(b, past, -1), cv.reshape(b, past, -1), cki,
                             x, p["ds_w_out"][0], kchunk=512)
    x, cs1 = _conv_ffn(x, conv_state[1], p["norm_ffn"][1], p["ffn_w_in"][1], p["ffn_conv_w"][1],
                       p["ffn_conv_b"][1], p["ffn_w_down"][1], p["norm_final"], bb=bb, tt=tt, final_norm=True)
    k = k.reshape(1, b, t, N_KV_HEADS, HEAD_DIM)
    v = v.reshape(1, b, t, N_KV_HEADS, HEAD_DIM)
    ki = kw[:, :, :IDX_DIM].reshape(1, b, t, IDX_DIM)
    return x, s_new[None], jnp.stack([cs0, cs1]), k, v, ki


def kernel(x_prompt, x_sample, cache_k, cache_v, cache_kidx, state_hgrn, state_conv, norm_mix, norm_ffn, norm_final, hg_w_in, hg_lb, hg_norm, hg_w_out, ds_w_in, ds_kln_w, ds_kln_b, ds_w_out, ffn_w_in, ffn_conv_w, ffn_conv_b, ffn_w_down):
    dsa_in = ds_w_in.shape[-1]
    dsa_pad = (-dsa_in) % LANES
    p = dict(
        norm_mix=norm_mix, norm_ffn=norm_ffn, norm_final=norm_final,
        hg_w_in=hg_w_in.astype(BF16), hg_norm=hg_norm, hg_w_out=hg_w_out.astype(BF16),
        lower_bounds=jnp.cumsum(jax.nn.softmax(hg_lb.astype(F32), axis=0), axis=0),
        ds_w_in=jnp.pad(ds_w_in, ((0, 0), (0, 0), (0, dsa_pad))).astype(BF16),
        ds_kln_w=ds_kln_w, ds_kln_b=ds_kln_b, ds_w_out=ds_w_out.astype(BF16),
        ffn_w_in=ffn_w_in.astype(BF16), ffn_conv_w=ffn_conv_w, ffn_conv_b=ffn_conv_b,
        ffn_w_down=ffn_w_down.astype(BF16),
    )
    b = x_prompt.shape[0]
    hg0 = jnp.zeros((state_hgrn.shape[0], b) + state_hgrn.shape[2:], F32)
    conv0 = jnp.zeros((state_conv.shape[0], b) + state_conv.shape[2:], F32)
    y_p, hg_p, conv_p, k_p, v_p, ki_p = _trunk(x_prompt, hg0, conv0, None, p)
    y_s, hg_s, conv_s, k_s, v_s, ki_s = _trunk(x_sample, state_hgrn, state_conv,
                                               (cache_k[0], cache_v[0], cache_kidx[0]), p)
    return (y_p, y_s, k_p, v_p, ki_p, hg_p, conv_p, k_s, v_s, ki_s, hg_s, conv_s)
```

```python
import functools

import jax
import jax.numpy as jnp
from jax import lax
from jax.experimental import pallas as pl
from jax.experimental.pallas import tpu as pltpu

F32 = jnp.float32
BF16 = jnp.bfloat16
I32 = jnp.int32
I16 = jnp.int16

CHUNK = 64
EPS = 1e-6
HG_HEADS = 8
HG_KDIM = 128
N_HEADS = 16
HEAD_DIM = 64
N_KV_HEADS = 4
IDX_HEADS = 8
IDX_DIM = 64
TOPK_MAX = 256
ROPE_THETA = 500000.0
CONV_W = 3

LANES = 128
VMEM_LIMIT = 56 * 1024 * 1024
EXP_CLAMP = 80.0
LOG2_E = 1.4426950408889634
ONES_ROWS = 16
KEY_NEG_INF = -2139095041
INT_MIN = -2147483648


def _cparams(n_axes, flags=None):
    return pltpu.CompilerParams(dimension_semantics=("arbitrary",) * n_axes,
                                vmem_limit_bytes=VMEM_LIMIT, flags=flags)


def _const_spec(shape):
    nd = len(shape)
    return pl.BlockSpec(shape, lambda *_: (0,) * nd, pipeline_mode=pl.Buffered(1))


def _rms(x, g):
    return x * lax.rsqrt(jnp.mean(x * x, axis=-1, keepdims=True) + EPS) * g


def _silu(x):
    return x * jax.nn.sigmoid(x)


def _dot(a, b):
    return jnp.dot(a, b, preferred_element_type=F32)


def _dot_nt(a, b):
    return lax.dot_general(a, b, (((1,), (1,)), ((), ())), preferred_element_type=F32)


def _dot_tn(a, b):
    return lax.dot_general(a, b, (((0,), (0,)), ((), ())), preferred_element_type=F32)


def _hgrn_kernel(x_ref, s0_ref, g_ref, win_ref, lb_ref, hn_ref, wout_ref, o_ref, s_ref,
                 q_s, k_s, g_s, v_s, gate_s, on_s, st_s, o_s, *, bb, tt, blk, sub):
    j = pl.program_id(1)
    rows = bb * tt
    d = x_ref.shape[-1]
    fdim = HG_HEADS * HG_KDIM
    n_sub = blk // sub

    @pl.when(j == 0)
    def _():
        for bi in range(bb):
            for hh in range(HG_HEADS):
                st_s[bi * HG_HEADS + hh] = s0_ref[bi, hh].T

    x = x_ref[...].reshape(rows, d)
    h = _rms(x, g_ref[...]).astype(BF16)
    q_s[...] = _silu(_dot(h, win_ref[:, 0:fdim]))
    lb = lb_ref[...]
    fg = lb + (1.0 - lb) * jax.nn.sigmoid(_dot(h, win_ref[:, fdim:2 * fdim]))
    k_s[...] = 1.0 - fg
    g_s[...] = jnp.log(fg)
    v_s[...] = _dot(h, win_ref[:, 2 * fdim:2 * fdim + d])
    gate_s[...] = _silu(_dot(h, win_ref[:, 2 * fdim + d:]))

    r_i = lax.broadcasted_iota(I32, (blk, blk), 0)
    c_i = lax.broadcasted_iota(I32, (blk, blk), 1)
    causal = c_i <= r_i
    tril = jnp.where(causal, 1.0, 0.0).astype(BF16)
    hn = hn_ref[...]
    blocks_per_stream = tt // blk

    def block_body(idx, carry):
        r0 = pl.multiple_of(idx * blk, blk)
        bi = idx // blocks_per_stream
        lg = g_s[pl.ds(r0, blk), :]
        lg_hi = lg.astype(BF16)
        rem = lg - lg_hi.astype(F32)
        lg_mid = rem.astype(BF16)
        lg_lo = (rem - lg_mid.astype(F32)).astype(BF16)
        g_all = _dot(tril, lg_hi) + _dot(tril, lg_mid) + _dot(tril, lg_lo)
        run_zero = lax.shift_right_arithmetic(idx, 31)

        def anchor(x):
            tail = lax.bitcast_convert_type(x[x.shape[0] - 8:, :], I32) & run_zero
            return lax.bitcast_convert_type(tail, F32)[0:1, :]

        heads = [slice(hh * HG_KDIM, (hh + 1) * HG_KDIM) for hh in range(HG_HEADS)]
        qcat, kcat, qfull, kdec_last, v16, decay = [], [], [], [], [], []
        for sl in heads:
            gc = g_all[:, sl]
            q = q_s[pl.ds(r0, blk), sl]
            kk = k_s[pl.ds(r0, blk), sl]
            refs = [jnp.zeros((1, HG_KDIM), F32)] + [gc[s * sub - 1:s * sub, :] for s in range(1, n_sub)]
            q_parts, k_parts = [], []
            for s in range(n_sub):
                lo_r, hi_r = s * sub, (s + 1) * sub
                qg = q[lo_r:hi_r, :] * jnp.exp(gc[lo_r:hi_r, :] - refs[s])
                pieces = [qg]
                if lo_r:
                    pieces.insert(0, jnp.zeros((lo_r, HG_KDIM), F32))
                if blk - hi_r:
                    pieces.append(jnp.zeros((blk - hi_r, HG_KDIM), F32))
                q_parts.append(pieces[0] if len(pieces) == 1 else jnp.concatenate(pieces, axis=0))
                kd_s = kk[0:hi_r, :] * jnp.exp(jnp.minimum(refs[s] - gc[0:hi_r, :], EXP_CLAMP))
                k_parts.append(kd_s if hi_r == blk else
                               jnp.concatenate([kd_s, jnp.zeros((blk - hi_r, HG_KDIM), F32)], axis=0))
            qcat.append(jnp.concatenate(q_parts, axis=1).astype(BF16))
            kcat.append(jnp.concatenate(k_parts, axis=1).astype(BF16))
            qfull.append((q * jnp.exp(gc)).astype(BF16))
            g_last = gc[blk - 1:blk, :]
            kdec_last.append((kk * jnp.exp(g_last - gc)).astype(BF16))
            decay.append(jnp.exp(g_last))
            v16.append(v_s[pl.ds(r0, blk), sl].astype(BF16))
        a_list = [_dot_nt(qcat[hh], kcat[hh]) for hh in range(HG_HEADS)]
        s_inc = [_dot_tn(v16[hh], kdec_last[hh]) for hh in range(HG_HEADS)]
        a_list[0] = a_list[0] + anchor(a_list[-1])
        o_list = []
        for hh in range(HG_HEADS):
            a = jnp.where(causal, a_list[hh], 0.0).astype(BF16)
            st = st_s[bi * HG_HEADS + hh]
            o_list.append(_dot(a, v16[hh]) + _dot_nt(qfull[hh], st.astype(BF16)))
            st_s[bi * HG_HEADS + hh] = st * decay[hh] + s_inc[hh]
        o_list[0] = o_list[0] + anchor(o_list[-1])
        for hh, sl in enumerate(heads):
            on = _rms(o_list[hh], hn) * gate_s[pl.ds(r0, blk), sl]
            on_s[pl.ds(r0, blk), sl] = on.astype(BF16)
        return carry

    sub_sums = jnp.sum(g_s[...].reshape(rows // sub, sub, fdim), axis=1)
    blocked_ok = jnp.min(sub_sums) >= -EXP_CLAMP

    @pl.when(blocked_ok)
    def _():
        lax.fori_loop(0, rows // blk, block_body, 0, unroll=2)

    @pl.when(jnp.logical_not(blocked_ok))
    def _():
        grp = 16
        o_s[...] = jnp.zeros(o_s.shape, F32)
        in_grp = lax.broadcasted_iota(I32, (grp, 1), 0)

        def frame_body(r, carry):
            r0 = pl.multiple_of((r // grp) * grp, grp)
            bi = r // tt
            this = in_grp == (r % grp)
            for hh in range(HG_HEADS):
                sl = slice(hh * HG_KDIM, (hh + 1) * HG_KDIM)

                def only(ref):
                    return jnp.where(this, ref[pl.ds(r0, grp), sl], 0.0)

                forget = jnp.exp(jnp.sum(only(g_s), axis=0, keepdims=True))
                st = (st_s[bi * HG_HEADS + hh] * forget
                      + _dot_tn(only(v_s).astype(BF16), only(k_s).astype(BF16)))
                st_s[bi * HG_HEADS + hh] = st
                o_s[pl.ds(r0, grp), sl] += _dot_nt(only(q_s).astype(BF16), st.astype(BF16))
            return carry

        lax.fori_loop(0, rows, frame_body, 0)
        for hh in range(HG_HEADS):
            sl = slice(hh * HG_KDIM, (hh + 1) * HG_KDIM)
            on_s[:, sl] = (_rms(o_s[:, sl], hn) * gate_s[:, sl]).astype(BF16)

    out = _dot(on_s[...], wout_ref[...]) + x
    o_ref[...] = out.reshape(bb, tt, d)

    @pl.when(j == pl.num_programs(1) - 1)
    def _():
        for bi in range(bb):
            for hh in range(HG_HEADS):
                s_ref[bi, hh] = st_s[bi * HG_HEADS + hh].T


def _hgrn_mixer(x, s0, norm_g, w_in, lb, hn, w_out, *, bb, tt, blk, sub):
    b, t, d = x.shape
    fdim = HG_HEADS * HG_KDIM
    rows = bb * tt
    kern = functools.partial(_hgrn_kernel, bb=bb, tt=tt, blk=blk, sub=sub)
    return pl.pallas_call(
        kern,
        grid=(b // bb, t // tt),
        in_specs=[
            pl.BlockSpec((bb, tt, d), lambda i, j: (i, j, 0)),
            pl.BlockSpec((bb, HG_HEADS, HG_KDIM, d // HG_HEADS), lambda i, j: (i, 0, 0, 0)),
            _const_spec((1, d)),
            _const_spec(w_in.shape),
            _const_spec((1, fdim)),
            _const_spec((1, d // HG_HEADS)),
            _const_spec(w_out.shape),
        ],
        out_specs=[
            pl.BlockSpec((bb, tt, d), lambda i, j: (i, j, 0)),
            pl.BlockSpec((bb, HG_HEADS, HG_KDIM, d // HG_HEADS), lambda i, j: (i, 0, 0, 0)),
        ],
        out_shape=[jax.ShapeDtypeStruct(x.shape, F32), jax.ShapeDtypeStruct(s0.shape, F32)],
        scratch_shapes=[
            pltpu.VMEM((rows, fdim), F32),
            pltpu.VMEM((rows, fdim), F32),
            pltpu.VMEM((rows, fdim), F32),
            pltpu.VMEM((rows, d), F32),
            pltpu.VMEM((rows, d), F32),
            pltpu.VMEM((rows, d), BF16),
            pltpu.VMEM((bb * HG_HEADS, d // HG_HEADS, HG_KDIM), F32),
            pltpu.VMEM((rows, d), F32),
        ],
        compiler_params=_cparams(2),
        name="hgrn_mixer",
    )(x, s0, norm_g.reshape(1, d), w_in, lb.reshape(1, fdim), hn.reshape(1, -1), w_out)


CONV_HEAD = 8


def _ffn_kernel(x_ref, cs_ref, g_ref, win_ref, cw_ref, cb_ref, wdn_ref, fg_ref, o_ref, ns_ref,
                a_s, *, bb, tt, final_norm):
    j = pl.program_id(1)
    rows = bb * tt
    d = x_ref.shape[-1]
    dff = cw_ref.shape[-1]
    hist = CONV_W - 1

    @pl.when(j == 0)
    def _():
        a_s[:, CONV_HEAD - hist:CONV_HEAD, :] = cs_ref[...]

    x = x_ref[...].reshape(rows, d)
    h = _rms(x, g_ref[...]).astype(BF16)
    a = _dot(h, win_ref[:, 0:dff])
    u = _dot(h, win_ref[:, dff:])
    a_s[:, CONV_HEAD:CONV_HEAD + tt, :] = a.reshape(bb, tt, dff)
    c = jnp.broadcast_to(cb_ref[...].reshape(1, 1, dff), (bb, tt, dff))
    for w in range(CONV_W):
        lo = CONV_HEAD - hist + w
        c = c + a_s[:, lo:lo + tt, :] * cw_ref[w:w + 1, :].reshape(1, 1, dff)
    new_state = a_s[:, CONV_HEAD + tt - hist:CONV_HEAD + tt, :]
    a_s[:, CONV_HEAD - hist:CONV_HEAD, :] = new_state
    ns_ref[...] = new_state
    act = (_silu(c).reshape(rows, dff) * u).astype(BF16)
    y = _dot(act, wdn_ref[...]) + x
    if final_norm:
        y = _rms(y, fg_ref[...])
    o_ref[...] = y.reshape(bb, tt, d)


def _conv_ffn(x, conv_state, norm_g, w_in, conv_w, conv_b, w_down, final_g, *, bb, tt, final_norm):
    b, t, d = x.shape
    dff = conv_w.shape[-1]
    kern = functools.partial(_ffn_kernel, bb=bb, tt=tt, final_norm=final_norm)
    return pl.pallas_call(
        kern,
        grid=(b // bb, t // tt),
        in_specs=[
            pl.BlockSpec((bb, tt, d), lambda i, j: (i, j, 0)),
            pl.BlockSpec((bb, CONV_W - 1, dff), lambda i, j: (i, 0, 0)),
            _const_spec((1, d)),
            _const_spec(w_in.shape),
            _const_spec(conv_w.shape),
            _const_spec((1, dff)),
            _const_spec(w_down.shape),
            _const_spec((1, d)),
        ],
        out_specs=[
            pl.BlockSpec((bb, tt, d), lambda i, j: (i, j, 0)),
            pl.BlockSpec((bb, CONV_W - 1, dff), lambda i, j: (i, 0, 0)),
        ],
        out_shape=[jax.ShapeDtypeStruct(x.shape, F32), jax.ShapeDtypeStruct(conv_state.shape, F32)],
        scratch_shapes=[pltpu.VMEM((bb, CONV_HEAD + tt, dff), F32)],
        compiler_params=_cparams(2),
        name="conv_ffn",
    )(x, conv_state, norm_g.reshape(1, d), w_in, conv_w, conv_b.reshape(1, dff), w_down,
      final_g.reshape(1, d))


def _rope_tables(pos):
    rot = HEAD_DIM // 4
    half = rot // 2
    inv_freq = ROPE_THETA ** (-jnp.arange(half, dtype=F32) / half)
    ang = pos.astype(F32)[:, None] * inv_freq[None, :]
    cos, sin = jnp.cos(ang), jnp.sin(ang)
    t = pos.shape[0]
    pad = HEAD_DIM - rot
    one = jnp.ones((t, pad), F32)
    zero_h = jnp.zeros((t, half), F32)
    zero_p = jnp.zeros((t, pad), F32)
    c64 = jnp.concatenate([cos, cos, one], axis=1)
    s1_64 = jnp.concatenate([zero_h, sin, zero_p], axis=1)
    s2_64 = jnp.concatenate([-sin, zero_h, zero_p], axis=1)
    ident = jnp.ones((t, HEAD_DIM), F32)
    zero64 = jnp.zeros((t, HEAD_DIM), F32)
    return jnp.stack([
        jnp.concatenate([c64, c64], axis=1), jnp.concatenate([s1_64, s1_64], axis=1),
        jnp.concatenate([s2_64, s2_64], axis=1),
        jnp.concatenate([c64, ident], axis=1), jnp.concatenate([s1_64, zero64], axis=1),
        jnp.concatenate([s2_64, zero64], axis=1)])


def _dsa_proj_kernel(x_ref, g_ref, w_ref, tab_ref, lnw_ref, lnb_ref,
                     q_ref, k_ref, v_ref, qi_ref, kw_ref, *, bb, tt):
    rows = bb * tt
    d = x_ref.shape[-1]
    half = HEAD_DIM // 8
    o1 = N_HEADS * HEAD_DIM
    o2 = o1 + N_KV_HEADS * HEAD_DIM
    o3 = o2 + N_KV_HEADS * HEAD_DIM
    o4 = o3 + IDX_HEADS * IDX_DIM

    def tab(i):
        t = tab_ref[i]
        return jnp.broadcast_to(t[None], (bb, tt, LANES)).reshape(rows, LANES)

    def rope(y, base):
        cos, s1, s2 = tab(base), tab(base + 1), tab(base + 2)
        tiles = []
        for m in range(y.shape[1] // LANES):
            yt = y[:, m * LANES:(m + 1) * LANES]
            tiles.append(yt * cos + pltpu.roll(yt, half, 1) * s1 + pltpu.roll(yt, LANES - half, 1) * s2)
        return tiles[0] if len(tiles) == 1 else jnp.concatenate(tiles, axis=1)

    x = x_ref[...].reshape(rows, d)
    h = _rms(x, g_ref[...]).astype(BF16)
    q = rope(_dot(h, w_ref[:, 0:o1]), 0) * (HEAD_DIM ** -0.5 * LOG2_E)
    q_ref[...] = q.astype(BF16).reshape(bb, tt, o1)
    k_ref[...] = rope(_dot(h, w_ref[:, o1:o2]), 0).reshape(bb, tt, o2 - o1)
    v_ref[...] = _dot(h, w_ref[:, o2:o3]).reshape(bb, tt, o3 - o2)
    qi_ref[...] = rope(_dot(h, w_ref[:, o3:o4]), 0).astype(BF16).reshape(bb, tt, o4 - o3)
    t = _dot(h, w_ref[:, o4:o4 + LANES])
    lane = lax.broadcasted_iota(I32, (1, LANES), 1)
    is_ki = lane < IDX_DIM
    mu = jnp.sum(jnp.where(is_ki, t, 0.0), axis=-1, keepdims=True) / IDX_DIM
    cen = jnp.where(is_ki, t - mu, 0.0)
    var = jnp.sum(cen * cen, axis=-1, keepdims=True) / IDX_DIM
    ki = rope(cen * lax.rsqrt(var + EPS) * lnw_ref[...] + lnb_ref[...], 3)
    wi = t * ((IDX_HEADS * IDX_DIM) ** -0.5)
    kw = jnp.where(is_ki, ki, jnp.where(lane < IDX_DIM + IDX_HEADS, wi, 0.0))
    kw_ref[...] = kw.reshape(bb, tt, LANES)


def _dsa_proj(x, norm_g, w_pad, tables, ln_w, ln_b, *, bb, tt):
    b, t, d = x.shape
    n_q = N_HEADS * HEAD_DIM
    n_kv = N_KV_HEADS * HEAD_DIM
    n_qi = IDX_HEADS * IDX_DIM
    pad = jnp.zeros((LANES - IDX_DIM,), F32)
    lnw = jnp.concatenate([ln_w, pad]).reshape(1, LANES)
    lnb = jnp.concatenate([ln_b, pad]).reshape(1, LANES)
    kern = functools.partial(_dsa_proj_kernel, bb=bb, tt=tt)

    def row_spec(n):
        return pl.BlockSpec((bb, tt, n), lambda i, j: (i, j, 0))

    return pl.pallas_call(
        kern,
        grid=(b // bb, t // tt),
        in_specs=[
            row_spec(d),
            _const_spec((1, d)),
            _const_spec(w_pad.shape),
            pl.BlockSpec((6, tt, LANES), lambda i, j: (0, j, 0)),
            _const_spec((1, LANES)),
            _const_spec((1, LANES)),
        ],
        out_specs=[row_spec(n_q), row_spec(n_kv), row_spec(n_kv), row_spec(n_qi), row_spec(LANES)],
        out_shape=[
            jax.ShapeDtypeStruct((b, t, n_q), BF16),
            jax.ShapeDtypeStruct((b, t, n_kv), F32),
            jax.ShapeDtypeStruct((b, t, n_kv), F32),
            jax.ShapeDtypeStruct((b, t, n_qi), BF16),
            jax.ShapeDtypeStruct((b, t, LANES), F32),
        ],
        compiler_params=_cparams(2),
        name="dsa_proj",
    )(x, norm_g.reshape(1, d), w_pad, tables, lnw, lnb)


def _order_key(score):
    score = jnp.where(score == 0.0, 0.0, score)
    bits = lax.bitcast_convert_type(score, I32)
    return bits ^ ((bits >> 31) & 0x7FFFFFFF)


def _fill_padded(dst_e, dst_o, src, r0, nrows):
    lane = lax.broadcasted_iota(I32, (1, LANES), 1)
    lo = lane < HEAD_DIM
    for m in range(src.shape[1] // LANES):
        tile = src[:, m * LANES:(m + 1) * LANES]
        rolled = pltpu.roll(tile, HEAD_DIM, 1)
        a_lo = jnp.where(lo, tile, 0.0).astype(BF16)
        b_hi = jnp.where(lo, 0.0, tile).astype(BF16)
        b_lo = jnp.where(lo, rolled, 0.0).astype(BF16)
        a_hi = jnp.where(lo, 0.0, rolled).astype(BF16)
        c0, c1 = 2 * m * LANES, (2 * m + 1) * LANES
        dst_e[pl.ds(r0, nrows), c0:c0 + LANES] = a_lo
        dst_o[pl.ds(r0, nrows), c0:c0 + LANES] = a_hi
        dst_e[pl.ds(r0, nrows), c1:c1 + LANES] = b_lo
        dst_o[pl.ds(r0, nrows), c1:c1 + LANES] = b_hi


def _attn_core(q_ref, qi_ref, wq_ref, x_ref, wout_ref, o_ref, segs, *, tq, q_pos0, n_sel):
    qrow = lax.broadcasted_iota(I32, (tq, 1), 0)
    q_chunk = (q_pos0 + qrow) // CHUNK
    wq = wq_ref[...].reshape(tq, LANES)
    wi = [wq[:, IDX_DIM + hh:IDX_DIM + hh + 1] for hh in range(IDX_HEADS)]
    qi = qi_ref[...].reshape(tq, IDX_HEADS * IDX_DIM)

    for sg in segs:
        for c0 in range(0, sg["len"], sg["chunk"]):
            cw = min(sg["chunk"], sg["len"] - c0)
            sc = jnp.zeros((tq, cw), F32)
            for m in range(IDX_HEADS // 2):
                qt = qi[:, m * LANES:(m + 1) * LANES]
                de = _dot_nt(qt, sg["kie"][c0:c0 + cw, 0:LANES])
                do = _dot_nt(qt, sg["kio"][c0:c0 + cw, 0:LANES])
                sc = sc + wi[2 * m] * jnp.maximum(de, 0.0) + wi[2 * m + 1] * jnp.maximum(do, 0.0)
            kpos = sg["pos0"] + c0 + lax.broadcasted_iota(I32, (1, cw), 1)
            adm = (kpos // CHUNK) <= q_chunk
            sg["key_s"][:, c0:c0 + cw] = jnp.where(adm, _order_key(sc), KEY_NEG_INF)

    def count(pred):
        tot = jnp.zeros((tq, 1), I32)
        for sg in segs:
            keys = sg["key_s"][:, 0:sg["len"]]
            tot = tot + jnp.sum(pred(keys).astype(I32), axis=-1, keepdims=True)
        return tot

    thr = jnp.where(count(lambda k: k >= 0) >= n_sel, 0, INT_MIN).astype(I32)

    def bit_step(i, thr):
        cand = thr | lax.shift_left(jnp.int32(1), 30 - i)
        return jnp.where(count(lambda k: k >= cand) >= n_sel, cand, thr)

    thr = lax.fori_loop(0, 31, bit_step, thr)
    n_gt = count(lambda k: k > thr)
    n_ge = count(lambda k: k >= thr)
    ties_wanted = n_sel - n_gt

    for sg in segs:
        keys = sg["key_s"][:, 0:sg["len"]]
        sel = (keys >= thr) & (keys > KEY_NEG_INF)
        sg["bias_s"][:, 0:sg["len"]] = jnp.where(sel, 0.0, -jnp.inf)

    has_excess = jnp.max(jnp.where(thr > KEY_NEG_INF, n_ge - n_sel, 0)) > 0

    @pl.when(has_excess)
    def _():
        seen = jnp.zeros((tq, 1), F32)
        wanted = ties_wanted.astype(F32)
        for sg in segs:
            for c0 in range(0, sg["len"], LANES):
                cw = min(LANES, sg["len"] - c0)
                keys = sg["key_s"][:, c0:c0 + cw]
                tie = keys == thr
                r_i = lax.broadcasted_iota(I32, (cw, cw), 0)
                c_i = lax.broadcasted_iota(I32, (cw, cw), 1)
                upper = jnp.where(r_i <= c_i, 1.0, 0.0).astype(BF16)
                tie_f = jnp.where(tie, 1.0, 0.0)
                rank = seen + _dot(tie_f.astype(BF16), upper)
                sel = ((keys > thr) | (tie & (rank <= wanted))) & (keys > KEY_NEG_INF)
                sg["bias_s"][:, c0:c0 + cw] = jnp.where(sel, 0.0, -jnp.inf)
                seen = seen + jnp.sum(tie_f, axis=-1, keepdims=True)

    q = q_ref[...].reshape(tq, N_HEADS * HEAD_DIM)
    tiles = []
    for m in range(N_HEADS // 2):
        g = (2 * m) // (N_HEADS // N_KV_HEADS)
        gs = slice(g * LANES, (g + 1) * LANES)
        qt = q[:, m * LANES:(m + 1) * LANES]
        acc = jnp.zeros((tq, LANES), F32)
        for kname, vname in (("ke", "ve"), ("ko", "vo")):
            logits = [_dot_nt(qt, sg[kname][0:sg["len"], gs]) + sg["bias_s"][:, 0:sg["len"]] for sg in segs]
            mx = functools.reduce(jnp.maximum, [jnp.max(l, axis=-1, keepdims=True) for l in logits])
            ps = [jnp.exp2(l - mx) for l in logits]
            den = functools.reduce(jnp.add, [jnp.sum(p, axis=-1, keepdims=True) for p in ps])
            pv = functools.reduce(jnp.add, [_dot(p.astype(BF16), sg[vname][0:sg["len"], gs])
                                            for p, sg in zip(ps, segs)])
            acc = acc + pv / den
        tiles.append(acc.astype(BF16))
    o = jnp.concatenate(tiles, axis=1)
    d = x_ref.shape[-1]
    o_ref[...] = (_dot(o, wout_ref[...]) + x_ref[...].reshape(tq, d)).reshape(o_ref.shape)


def _attn_prompt_kernel(q_ref, qi_ref, wq_ref, k_ref, v_ref, kw_ref, x_ref, wout_ref, o_ref,
                        ke, ko, vt, kie, kio, qs, key_s, bias_s, ot_s, half_s, *, tq, t_len, kc, n_sel):
    j = pl.program_id(1)
    nct = t_len // kc
    sub_per_chunk = kc // LANES
    lane = lax.broadcasted_iota(I32, (1, LANES), 1)
    lo = lane < HEAD_DIM

    @pl.when(j == 0)
    def _():
        for c in range(nct):
            rows = slice(c * kc, (c + 1) * kc)
            k_c = k_ref[0, rows, :]
            for m in range(N_KV_HEADS // 2):
                tile = k_c[:, m * LANES:(m + 1) * LANES]
                rolled = pltpu.roll(tile, HEAD_DIM, 1)
                ke[(2 * m) * nct + c] = jnp.where(lo, tile, 0.0).astype(BF16)
                ko[(2 * m) * nct + c] = jnp.where(lo, 0.0, rolled).astype(BF16)
                ke[(2 * m + 1) * nct + c] = jnp.where(lo, rolled, 0.0).astype(BF16)
                ko[(2 * m + 1) * nct + c] = jnp.where(lo, 0.0, tile).astype(BF16)
            v_t = v_ref[0, rows, :].T
            for g in range(N_KV_HEADS):
                vt[g * nct + c, 0:HEAD_DIM, :] = v_t[g * HEAD_DIM:(g + 1) * HEAD_DIM, :].astype(BF16)
                vt[g * nct + c, HEAD_DIM:HEAD_DIM + ONES_ROWS, :] = jnp.ones((ONES_ROWS, kc), BF16)
            kw_c = kw_ref[0, rows, :]
            kie[c] = jnp.where(lo, kw_c, 0.0).astype(BF16)
            kio[c] = jnp.where(lo, 0.0, pltpu.roll(kw_c, HEAD_DIM, 1)).astype(BF16)

    for m in range(N_HEADS // 2):
        qs[m] = q_ref[0, :, m * LANES:(m + 1) * LANES]

    n_chunks = ((j + 1) * tq + kc - 1) // kc
    w_t = wq_ref[0].T
    w_rows = [w_t[IDX_DIM + hh:IDX_DIM + hh + 1, :] for hh in range(IDX_HEADS)]
    qi = qi_ref[0]
    q_chunk = (j * tq + lax.broadcasted_iota(I32, (1, tq), 1)) // CHUNK

    def score_body(c, carry):
        kie_c, kio_c = kie[c], kio[c]
        sc = jnp.zeros((kc, tq), F32)
        for m in range(IDX_HEADS // 2):
            qt = qi[:, m * LANES:(m + 1) * LANES]
            sc = (sc + w_rows[2 * m] * jnp.maximum(_dot_nt(kie_c, qt), 0.0)
                  + w_rows[2 * m + 1] * jnp.maximum(_dot_nt(kio_c, qt), 0.0))
        k_chunk = (c * kc + lax.broadcasted_iota(I32, (kc, 1), 0)) // CHUNK
        keys = jnp.where(k_chunk <= q_chunk, _order_key(sc), KEY_NEG_INF)
        key_s[c] = keys
        half_s[c] = lax.shift_right_arithmetic(keys, 16).astype(I16)
        return carry

    lax.fori_loop(0, n_chunks, score_body, 0)

    def search16(need):
        def count_ge(cand):
            def body(c, acc):
                hit = (half_s[c] >= cand).astype(I16)
                for r in range(0, kc, 16):
                    acc = acc + hit[r:r + 16, :]
                return acc
            acc = lax.fori_loop(0, n_chunks, body, jnp.zeros((16, tq), I16))
            return jnp.sum(acc.astype(I32), axis=0, keepdims=True)

        def bit_step(i, t16):
            cand = t16 ^ lax.shift_left(jnp.int32(1), 15 - i)
            cand = lax.shift_right_arithmetic(lax.shift_left(cand, 16), 16)
            return jnp.where(count_ge(cand.astype(I16)) >= need, cand, t16)

        return lax.fori_loop(0, 16, bit_step, jnp.full((1, tq), -32768, I32))

    thr_hi = search16(n_sel)

    def low_body(c, acc):
        keys = key_s[c]
        hi = lax.shift_right_arithmetic(keys, 16)
        low = (keys & 0xFFFF) - 32768
        half_s[c] = jnp.where(hi == thr_hi, low, -32768).astype(I16)
        return acc + jnp.sum((hi > thr_hi).astype(I32).reshape(kc // 8, 8, tq), axis=0)

    n_above = jnp.sum(lax.fori_loop(0, n_chunks, low_body, jnp.zeros((8, tq), I32)), axis=0, keepdims=True)
    thr_lo = search16(n_sel - n_above)
    thr = lax.shift_left(thr_hi, 16) | (thr_lo + 32768)

    def bias_body(c, acc):
        keys = key_s[c]
        ge = keys >= thr
        bias_s[c] = jnp.where(ge & (keys > KEY_NEG_INF), 0.0, -jnp.inf)
        fold = lambda hit: jnp.sum(hit.astype(I32).reshape(kc // 8, 8, tq), axis=0)
        return acc[0] + fold(ge), acc[1] + fold(keys > thr)

    zero8 = jnp.zeros((8, tq), I32)
    n_ge, n_gt = [jnp.sum(a, axis=0, keepdims=True)
                  for a in lax.fori_loop(0, n_chunks, bias_body, (zero8, zero8))]

    has_excess = jnp.max(jnp.where(thr > KEY_NEG_INF, n_ge - n_sel, 0)) > 0

    @pl.when(has_excess)
    def _():
        wanted = (n_sel - n_gt).astype(F32)
        r_i = lax.broadcasted_iota(I32, (LANES, LANES), 0)
        c_i = lax.broadcasted_iota(I32, (LANES, LANES), 1)
        lower = jnp.where(c_i <= r_i, 1.0, 0.0).astype(BF16)

        def tie_body(u, seen):
            c = u // sub_per_chunk
            r0 = pl.multiple_of((u % sub_per_chunk) * LANES, LANES)
            keys = key_s[c, pl.ds(r0, LANES), :]
            tie = keys == thr
            tie_f = jnp.where(tie, 1.0, 0.0)
            rank = seen + _dot(lower, tie_f.astype(BF16))
            sel = ((keys > thr) | (tie & (rank <= wanted))) & (keys > KEY_NEG_INF)
            bias_s[c, pl.ds(r0, LANES), :] = jnp.where(sel, 0.0, -jnp.inf)
            return seen + jnp.sum(tie_f, axis=0, keepdims=True)

        lax.fori_loop(0, n_chunks * sub_per_chunk, tie_body, jnp.zeros((1, tq), F32))

    n_grp = N_HEADS // N_KV_HEADS

    def group_body(g, carry):
        q_tiles = [qs[g * (n_grp // 2) + i] for i in range(n_grp // 2)]

        def chunk_body(c, st):
            bias = bias_s[c]
            v_c = vt[g * nct + c]
            k_e, k_o = ke[g * nct + c], ko[g * nct + c]
            logits = [_dot_nt(k_x, qt) for qt in q_tiles for k_x in (k_e, k_o)]
            tails = [lax.bitcast_convert_type(l[kc - 8:kc, :], I32) for l in logits[1:]]
            anchor = functools.reduce(jnp.bitwise_or, tails) & lax.shift_right_arithmetic(c, 31)
            anchor = lax.bitcast_convert_type(anchor, F32)[0:1, :]
            out = []
            for h in range(n_grp):
                m_run, acc = st[2 * h:2 * h + 2]
                if h == 0:
                    m_run = m_run + anchor
                logit = logits[h] + bias
                m_new = jnp.maximum(m_run, jnp.max(logit, axis=0, keepdims=True))
                m_safe = jnp.where(m_new > -jnp.inf, m_new, 0.0)
                p = jnp.exp2(logit - m_safe)
                alpha = jnp.exp2(m_run - m_safe)
                out += [m_new, alpha * acc + _dot(v_c, p.astype(BF16))]
            return tuple(out)

        init = (jnp.full((1, tq), -jnp.inf, F32), jnp.zeros((HEAD_DIM + ONES_ROWS, tq), F32))
        st = lax.fori_loop(0, n_chunks, chunk_body, init * n_grp)
        for h in range(n_grp):
            r0 = pl.multiple_of(g * (n_grp * HEAD_DIM) + h * HEAD_DIM, HEAD_DIM)
            acc = st[2 * h + 1]
            ot_s[pl.ds(r0, HEAD_DIM), :] = acc[0:HEAD_DIM, :] / acc[HEAD_DIM:HEAD_DIM + 1, :]
        return carry

    lax.fori_loop(0, N_KV_HEADS, group_body, 0)

    o = jnp.concatenate([ot_s[m * LANES:(m + 1) * LANES, :].T for m in range(N_HEADS // 2)], axis=1)
    o_ref[0] = _dot(o.astype(BF16), wout_ref[...]) + x_ref[0]


def _dsa_attn_prompt(q, qi, kw, k, v, x, w_out, *, tq, kc):
    b, t, d = x.shape
    n_sel = min(TOPK_MAX, t // 4)
    n_kv = N_KV_HEADS * HEAD_DIM
    nct = t // kc
    kern = functools.partial(_attn_prompt_kernel, tq=tq, t_len=t, kc=kc, n_sel=n_sel)

    def tile_spec(n):
        return pl.BlockSpec((1, tq, n), lambda i, j: (i, j, 0))

    def full_spec(n):
        return pl.BlockSpec((1, t, n), lambda i, j: (i, 0, 0))

    return pl.pallas_call(
        kern,
        grid=(b, t // tq),
        in_specs=[tile_spec(q.shape[-1]), tile_spec(qi.shape[-1]), tile_spec(LANES),
                  full_spec(n_kv), full_spec(n_kv), full_spec(LANES),
                  tile_spec(d), _const_spec(w_out.shape)],
        out_specs=tile_spec(d),
        out_shape=jax.ShapeDtypeStruct(x.shape, F32),
        scratch_shapes=[
            pltpu.VMEM((N_KV_HEADS * nct, kc, LANES), BF16), pltpu.VMEM((N_KV_HEADS * nct, kc, LANES), BF16),
            pltpu.VMEM((N_KV_HEADS * nct, HEAD_DIM + ONES_ROWS, kc), BF16),
            pltpu.VMEM((nct, kc, LANES), BF16), pltpu.VMEM((nct, kc, LANES), BF16),
            pltpu.VMEM((N_HEADS // 2, tq, LANES), BF16),
            pltpu.VMEM((nct, kc, tq), I32), pltpu.VMEM((nct, kc, tq), F32),
            pltpu.VMEM((N_HEADS * HEAD_DIM, tq), F32),
            pltpu.VMEM((nct, kc, tq), I16),
        ],
        compiler_params=_cparams(2),
        name="dsa_attn_prompt",
    )(q, qi, kw, k, v, kw, x, w_out)


def _attn_sample_kernel(q_ref, qi_ref, wq_ref, k_ref, v_ref, kw_ref, ck_ref, cv_ref, cki_ref,
                        x_ref, wout_ref, o_ref,
                        ke, ko, ve, vo, kie, kio, key_s, bias_s,
                        cke, cko, cve, cvo, ckie, ckio, ckey_s, cbias_s,
                        *, tq, past, kchunk, n_sel):
    _fill_padded(ke, ko, k_ref[0], 0, tq)
    _fill_padded(ve, vo, v_ref[0], 0, tq)
    _fill_padded(kie, kio, kw_ref[0], 0, tq)
    for r0 in range(0, past, kchunk):
        nr = min(kchunk, past - r0)
        _fill_padded(cke, cko, ck_ref[0, r0:r0 + nr, :], r0, nr)
        _fill_padded(cve, cvo, cv_ref[0, r0:r0 + nr, :], r0, nr)
        ki = cki_ref[0, r0:r0 + nr, :]
        zero = jnp.zeros_like(ki)
        ckie[r0:r0 + nr, :] = jnp.concatenate([ki, zero], axis=1).astype(BF16)
        ckio[r0:r0 + nr, :] = jnp.concatenate([zero, ki], axis=1).astype(BF16)
    cache = dict(ke=cke, ko=cko, ve=cve, vo=cvo, kie=ckie, kio=ckio, key_s=ckey_s, bias_s=cbias_s,
                 len=past, pos0=0, chunk=kchunk)
    new = dict(ke=ke, ko=ko, ve=ve, vo=vo, kie=kie, kio=kio, key_s=key_s, bias_s=bias_s,
               len=tq, pos0=past, chunk=tq)
    _attn_core(q_ref, qi_ref, wq_ref, x_ref, wout_ref, o_ref, [cache, new],
               tq=tq, q_pos0=past, n_sel=n_sel)


def _dsa_attn_sample(q, qi, kw, k, v, cache_k, cache_v, cache_ki, x, w_out, *, kchunk):
    b, t, d = x.shape
    past = cache_k.shape[1]
    n_sel = min(TOPK_MAX, (past + t) // 4)
    n_kv = N_KV_HEADS * HEAD_DIM
    kern = functools.partial(_attn_sample_kernel, tq=t, past=past, kchunk=kchunk, n_sel=n_sel)

    def spec(rows, n):
        return pl.BlockSpec((1, rows, n), lambda i: (i, 0, 0))

    return pl.pallas_call(
        kern,
        grid=(b,),
        in_specs=[spec(t, q.shape[-1]), spec(t, qi.shape[-1]), spec(t, LANES),
                  spec(t, n_kv), spec(t, n_kv), spec(t, LANES),
                  spec(past, n_kv), spec(past, n_kv), spec(past, IDX_DIM),
                  spec(t, d), _const_spec(w_out.shape)],
        out_specs=spec(t, d),
        out_shape=jax.ShapeDtypeStruct(x.shape, F32),
        scratch_shapes=[
            pltpu.VMEM((t, 2 * n_kv), BF16), pltpu.VMEM((t, 2 * n_kv), BF16),
            pltpu.VMEM((t, 2 * n_kv), BF16), pltpu.VMEM((t, 2 * n_kv), BF16),
            pltpu.VMEM((t, 2 * LANES), BF16), pltpu.VMEM((t, 2 * LANES), BF16),
            pltpu.VMEM((t, t), I32), pltpu.VMEM((t, t), F32),
            pltpu.VMEM((past, 2 * n_kv), BF16), pltpu.VMEM((past, 2 * n_kv), BF16),
            pltpu.VMEM((past, 2 * n_kv), BF16), pltpu.VMEM((past, 2 * n_kv), BF16),
            pltpu.VMEM((past, LANES), BF16), pltpu.VMEM((past, LANES), BF16),
            pltpu.VMEM((t, past), I32), pltpu.VMEM((t, past), F32),
        ],
        compiler_params=_cparams(1),
        name="dsa_attn_sample",
    )(q, qi, kw, k, v, kw, cache_k, cache_v, cache_ki, x, w_out)


def _tiling(b, t):
    if t >= 256:
        return dict(bb=1, tt=256, blk=128, sub=32)
    bb = max(1, min(b, 128 // t))
    return dict(bb=bb, tt=t, blk=t, sub=t)


def _trunk(x, hg_state, conv_state, cache, p):
    b, t, d = x.shape
    tl = _tiling(b, t)
    bb, tt = tl["bb"], tl["tt"]
    x, s_new = _hgrn_mixer(x, hg_state[0], p["norm_mix"][0], p["hg_w_in"][0], p["lower_bounds"][0],
                           p["hg_norm"][0], p["hg_w_out"][0], **dict(tl, tt=512 if t % 512 == 0 else tt))
    x, cs0 = _conv_ffn(x, conv_state[0], p["norm_ffn"][0], p["ffn_w_in"][0], p["ffn_conv_w"][0],
                       p["ffn_conv_b"][0], p["ffn_w_down"][0], p["norm_final"], bb=bb, tt=tt, final_norm=False)
    past = 0 if cache is None else cache[0].shape[1]
    tables = _rope_tables(past + jnp.arange(t))
    q, k, v, qi, kw = _dsa_proj(x, p["norm_mix"][1], p["ds_w_in"][0], tables, p["ds_kln_w"][0],
                                p["ds_kln_b"][0], bb=bb, tt=tt)
    if cache is None:
        x = _dsa_attn_prompt(q, qi, kw, k, v, x, p["ds_w_out"][0], tq=min(t, 512), kc=min(t, 512))
    else:
        ck, cv, cki = cache
        x = _dsa_attn_sample(q, qi, kw, k, v, ck.reshape(b, past, -1), cv.reshape(b, past, -1), cki,
                             x, p["ds_w_out"][0], kchunk=512)
    x, cs1 = _conv_ffn(x, conv_state[1], p["norm_ffn"][1], p["ffn_w_in"][1], p["ffn_conv_w"][1],
                       p["ffn_conv_b"][1], p["ffn_w_down"][1], p["norm_final"], bb=bb, tt=tt, final_norm=True)
    k = k.reshape(1, b, t, N_KV_HEADS, HEAD_DIM)
    v = v.reshape(1, b, t, N_KV_HEADS, HEAD_DIM)
    ki = kw[:, :, :IDX_DIM].reshape(1, b, t, IDX_DIM)
    return x, s_new[None], jnp.stack([cs0, cs1]), k, v, ki


def kernel(x_prompt, x_sample, cache_k, cache_v, cache_kidx, state_hgrn, state_conv, norm_mix, norm_ffn, norm_final, hg_w_in, hg_lb, hg_norm, hg_w_out, ds_w_in, ds_kln_w, ds_kln_b, ds_w_out, ffn_w_in, ffn_conv_w, ffn_conv_b, ffn_w_down):
    dsa_in = ds_w_in.shape[-1]
    dsa_pad = (-dsa_in) % LANES
    p = dict(
        norm_mix=norm_mix, norm_ffn=norm_ffn, norm_final=norm_final,
        hg_w_in=hg_w_in.astype(BF16), hg_norm=hg_norm, hg_w_out=hg_w_out.astype(BF16),
        lower_bounds=jnp.cumsum(jax.nn.softmax(hg_lb.astype(F32), axis=0), axis=0),
        ds_w_in=jnp.pad(ds_w_in, ((0, 0), (0, 0), (0, dsa_pad))).astype(BF16),
        ds_kln_w=ds_kln_w, ds_kln_b=ds_kln_b, ds_w_out=ds_w_out.astype(BF16),
        ffn_w_in=ffn_w_in.astype(BF16), ffn_conv_w=ffn_conv_w, ffn_conv_b=ffn_conv_b,
        ffn_w_down=ffn_w_down.astype(BF16),
    )
    b = x_prompt.shape[0]
    hg0 = jnp.zeros((state_hgrn.shape[0], b) + state_hgrn.shape[2:], F32)
    conv0 = jnp.zeros((state_conv.shape[0], b) + state_conv.shape[2:], F32)
    y_p, hg_p, conv_p, k_p, v_p, ki_p = _trunk(x_prompt, hg0, conv0, None, p)
    y_s, hg_s, conv_s, k_s, v_s, ki_s = _trunk(x_sample, state_hgrn, state_conv,
                                               (cache_k[0], cache_v[0], cache_kidx[0]), p)
    return (y_p, y_s, k_p, v_p, ki_p, hg_p, conv_p, k_s, v_s, ki_s, hg_s, conv_s)
```

```python
import functools

import jax
import jax.numpy as jnp
from jax import lax
from jax.experimental import pallas as pl
from jax.experimental.pallas import tpu as pltpu

F32 = jnp.float32
BF16 = jnp.bfloat16
I32 = jnp.int32
I16 = jnp.int16

CHUNK = 64
EPS = 1e-6
HG_HEADS = 8
HG_KDIM = 128
N_HEADS = 16
HEAD_DIM = 64
N_KV_HEADS = 4
IDX_HEADS = 8
IDX_DIM = 64
TOPK_MAX = 256
ROPE_THETA = 500000.0
CONV_W = 3

LANES = 128
VMEM_LIMIT = 56 * 1024 * 1024
EXP_CLAMP = 80.0
LOG2_E = 1.4426950408889634
ONES_ROWS = 16
KEY_NEG_INF = -2139095041
INT_MIN = -2147483648


def _cparams(n_axes, flags=None):
    return pltpu.CompilerParams(dimension_semantics=("arbitrary",) * n_axes,
                                vmem_limit_bytes=VMEM_LIMIT, flags=flags)


def _const_spec(shape):
    nd = len(shape)
    return pl.BlockSpec(shape, lambda *_: (0,) * nd, pipeline_mode=pl.Buffered(1))


def _rms(x, g):
    return x * lax.rsqrt(jnp.mean(x * x, axis=-1, keepdims=True) + EPS) * g


def _silu(x):
    return x * jax.nn.sigmoid(x)


def _dot(a, b):
    return jnp.dot(a, b, preferred_element_type=F32)


def _dot_nt(a, b):
    return lax.dot_general(a, b, (((1,), (1,)), ((), ())), preferred_element_type=F32)


def _dot_tn(a, b):
    return lax.dot_general(a, b, (((0,), (0,)), ((), ())), preferred_element_type=F32)


def _hgrn_kernel(x_ref, s0_ref, g_ref, win_ref, lb_ref, hn_ref, wout_ref, o_ref, s_ref,
                 q_s, k_s, g_s, v_s, gate_s, on_s, st_s, o_s, *, bb, tt, blk, sub):
    j = pl.program_id(1)
    rows = bb * tt
    d = x_ref.shape[-1]
    fdim = HG_HEADS * HG_KDIM
    n_sub = blk // sub

    @pl.when(j == 0)
    def _():
        for bi in range(bb):
            for hh in range(HG_HEADS):
                st_s[bi * HG_HEADS + hh] = s0_ref[bi, hh].T

    x = x_ref[...].reshape(rows, d)
    h = _rms(x, g_ref[...]).astype(BF16)
    q_s[...] = _silu(_dot(h, win_ref[:, 0:fdim]))
    lb = lb_ref[...]
    fg = lb + (1.0 - lb) * jax.nn.sigmoid(_dot(h, win_ref[:, fdim:2 * fdim]))
    k_s[...] = 1.0 - fg
    g_s[...] = jnp.log(fg)
    v_s[...] = _dot(h, win_ref[:, 2 * fdim:2 * fdim + d])
    gate_s[...] = _silu(_dot(h, win_ref[:, 2 * fdim + d:]))

    r_i = lax.broadcasted_iota(I32, (blk, blk), 0)
    c_i = lax.broadcasted_iota(I32, (blk, blk), 1)
    causal = c_i <= r_i
    tril = jnp.where(causal, 1.0, 0.0).astype(BF16)
    hn = hn_ref[...]
    blocks_per_stream = tt // blk

    def block_body(idx, carry):
        r0 = pl.multiple_of(idx * blk, blk)
        bi = idx // blocks_per_stream
        lg = g_s[pl.ds(r0, blk), :]
        lg_hi = lg.astype(BF16)
        rem = lg - lg_hi.astype(F32)
        lg_mid = rem.astype(BF16)
        lg_lo = (rem - lg_mid.astype(F32)).astype(BF16)
        g_all = _dot(tril, lg_hi) + _dot(tril, lg_mid) + _dot(tril, lg_lo)
        run_zero = lax.shift_right_arithmetic(idx, 31)

        def anchor(x):
            tail = lax.bitcast_convert_type(x[x.shape[0] - 8:, :], I32) & run_zero
            return lax.bitcast_convert_type(tail, F32)[0:1, :]

        heads = [slice(hh * HG_KDIM, (hh + 1) * HG_KDIM) for hh in range(HG_HEADS)]
        qcat, kcat, qfull, kdec_last, v16, decay = [], [], [], [], [], []
        for sl in heads:
            gc = g_all[:, sl]
            q = q_s[pl.ds(r0, blk), sl]
            kk = k_s[pl.ds(r0, blk), sl]
            refs = [jnp.zeros((1, HG_KDIM), F32)] + [gc[s * sub - 1:s * sub, :] for s in range(1, n_sub)]
            q_parts, k_parts = [], []
            for s in range(n_sub):
                lo_r, hi_r = s * sub, (s + 1) * sub
                qg = q[lo_r:hi_r, :] * jnp.exp(gc[lo_r:hi_r, :] - refs[s])
                pieces = [qg]
                if lo_r:
                    pieces.insert(0, jnp.zeros((lo_r, HG_KDIM), F32))
                if blk - hi_r:
                    pieces.append(jnp.zeros((blk - hi_r, HG_KDIM), F32))
                q_parts.append(pieces[0] if len(pieces) == 1 else jnp.concatenate(pieces, axis=0))
                kd_s = kk[0:hi_r, :] * jnp.exp(jnp.minimum(refs[s] - gc[0:hi_r, :], EXP_CLAMP))
                k_parts.append(kd_s if hi_r == blk else
                               jnp.concatenate([kd_s, jnp.zeros((blk - hi_r, HG_KDIM), F32)], axis=0))
            qcat.append(jnp.concatenate(q_parts, axis=1).astype(BF16))
            kcat.append(jnp.concatenate(k_parts, axis=1).astype(BF16))
            qfull.append((q * jnp.exp(gc)).astype(BF16))
            g_last = gc[blk - 1:blk, :]
            kdec_last.append((kk * jnp.exp(g_last - gc)).astype(BF16))
            decay.append(jnp.exp(g_last))
            v16.append(v_s[pl.ds(r0, blk), sl].astype(BF16))
        a_list = [_dot_nt(qcat[hh], kcat[hh]) for hh in range(HG_HEADS)]
        s_inc = [_dot_tn(v16[hh], kdec_last[hh]) for hh in range(HG_HEADS)]
        a_list[0] = a_list[0] + anchor(a_list[-1])
        o_list = []
        for hh in range(HG_HEADS):
            a = jnp.where(causal, a_list[hh], 0.0).astype(BF16)
            st = st_s[bi * HG_HEADS + hh]
            o_list.append(_dot(a, v16[hh]) + _dot_nt(qfull[hh], st.astype(BF16)))
            st_s[bi * HG_HEADS + hh] = st * decay[hh] + s_inc[hh]
        o_list[0] = o_list[0] + anchor(o_list[-1])
        for hh, sl in enumerate(heads):
            on = _rms(o_list[hh], hn) * gate_s[pl.ds(r0, blk), sl]
            on_s[pl.ds(r0, blk), sl] = on.astype(BF16)
        return carry

    sub_sums = jnp.sum(g_s[...].reshape(rows // sub, sub, fdim), axis=1)
    blocked_ok = jnp.min(sub_sums) >= -EXP_CLAMP

    @pl.when(blocked_ok)
    def _():
        lax.fori_loop(0, rows // blk, block_body, 0, unroll=2)

    @pl.when(jnp.logical_not(blocked_ok))
    def _():
        grp = 16
        o_s[...] = jnp.zeros(o_s.shape, F32)
        in_grp = lax.broadcasted_iota(I32, (grp, 1), 0)

        def frame_body(r, carry):
            r0 = pl.multiple_of((r // grp) * grp, grp)
            bi = r // tt
            this = in_grp == (r % grp)
            for hh in range(HG_HEADS):
                sl = slice(hh * HG_KDIM, (hh + 1) * HG_KDIM)

                def only(ref):
                    return jnp.where(this, ref[pl.ds(r0, grp), sl], 0.0)

                forget = jnp.exp(jnp.sum(only(g_s), axis=0, keepdims=True))
                st = (st_s[bi * HG_HEADS + hh] * forget
                      + _dot_tn(only(v_s).astype(BF16), only(k_s).astype(BF16)))
                st_s[bi * HG_HEADS + hh] = st
                o_s[pl.ds(r0, grp), sl] += _dot_nt(only(q_s).astype(BF16), st.astype(BF16))
            return carry

        lax.fori_loop(0, rows, frame_body, 0)
        for hh in range(HG_HEADS):
            sl = slice(hh * HG_KDIM, (hh + 1) * HG_KDIM)
            on_s[:, sl] = (_rms(o_s[:, sl], hn) * gate_s[:, sl]).astype(BF16)

    out = _dot(on_s[...], wout_ref[...]) + x
    o_ref[...] = out.reshape(bb, tt, d)

    @pl.when(j == pl.num_programs(1) - 1)
    def _():
        for bi in range(bb):
            for hh in range(HG_HEADS):
                s_ref[bi, hh] = st_s[bi * HG_HEADS + hh].T


def _hgrn_mixer(x, s0, norm_g, w_in, lb, hn, w_out, *, bb, tt, blk, sub):
    b, t, d = x.shape
    fdim = HG_HEADS * HG_KDIM
    rows = bb * tt
    kern = functools.partial(_hgrn_kernel, bb=bb, tt=tt, blk=blk, sub=sub)
    return pl.pallas_call(
        kern,
        grid=(b // bb, t // tt),
        in_specs=[
            pl.BlockSpec((bb, tt, d), lambda i, j: (i, j, 0)),
            pl.BlockSpec((bb, HG_HEADS, HG_KDIM, d // HG_HEADS), lambda i, j: (i, 0, 0, 0)),
            _const_spec((1, d)),
            _const_spec(w_in.shape),
            _const_spec((1, fdim)),
            _const_spec((1, d // HG_HEADS)),
            _const_spec(w_out.shape),
        ],
        out_specs=[
            pl.BlockSpec((bb, tt, d), lambda i, j: (i, j, 0)),
            pl.BlockSpec((bb, HG_HEADS, HG_KDIM, d // HG_HEADS), lambda i, j: (i, 0, 0, 0)),
        ],
        out_shape=[jax.ShapeDtypeStruct(x.shape, F32), jax.ShapeDtypeStruct(s0.shape, F32)],
        scratch_shapes=[
            pltpu.VMEM((rows, fdim), F32),
            pltpu.VMEM((rows, fdim), F32),
            pltpu.VMEM((rows, fdim), F32),
            pltpu.VMEM((rows, d), F32),
            pltpu.VMEM((rows, d), F32),
            pltpu.VMEM((rows, d), BF16),
            pltpu.VMEM((bb * HG_HEADS, d // HG_HEADS, HG_KDIM), F32),
            pltpu.VMEM((rows, d), F32),
        ],
        compiler_params=_cparams(2),
        name="hgrn_mixer",
    )(x, s0, norm_g.reshape(1, d), w_in, lb.reshape(1, fdim), hn.reshape(1, -1), w_out)


CONV_HEAD = 8


def _ffn_kernel(x_ref, cs_ref, g_ref, win_ref, cw_ref, cb_ref, wdn_ref, fg_ref, o_ref, ns_ref,
                a_s, *, bb, tt, final_norm):
    j = pl.program_id(1)
    rows = bb * tt
    d = x_ref.shape[-1]
    dff = cw_ref.shape[-1]
    hist = CONV_W - 1

    @pl.when(j == 0)
    def _():
        a_s[:, CONV_HEAD - hist:CONV_HEAD, :] = cs_ref[...]

    x = x_ref[...].reshape(rows, d)
    h = _rms(x, g_ref[...]).astype(BF16)
    a = _dot(h, win_ref[:, 0:dff])
    u = _dot(h, win_ref[:, dff:])
    a_s[:, CONV_HEAD:CONV_HEAD + tt, :] = a.reshape(bb, tt, dff)
    c = jnp.broadcast_to(cb_ref[...].reshape(1, 1, dff), (bb, tt, dff))
    for w in range(CONV_W):
        lo = CONV_HEAD - hist + w
        c = c + a_s[:, lo:lo + tt, :] * cw_ref[w:w + 1, :].reshape(1, 1, dff)
    new_state = a_s[:, CONV_HEAD + tt - hist:CONV_HEAD + tt, :]
    a_s[:, CONV_HEAD - hist:CONV_HEAD, :] = new_state
    ns_ref[...] = new_state
    act = (_silu(c).reshape(rows, dff) * u).astype(BF16)
    y = _dot(act, wdn_ref[...]) + x
    if final_norm:
        y = _rms(y, fg_ref[...])
    o_ref[...] = y.reshape(bb, tt, d)


def _conv_ffn(x, conv_state, norm_g, w_in, conv_w, conv_b, w_down, final_g, *, bb, tt, final_norm):
    b, t, d = x.shape
    dff = conv_w.shape[-1]
    kern = functools.partial(_ffn_kernel, bb=bb, tt=tt, final_norm=final_norm)
    return pl.pallas_call(
        kern,
        grid=(b // bb, t // tt),
        in_specs=[
            pl.BlockSpec((bb, tt, d), lambda i, j: (i, j, 0)),
            pl.BlockSpec((bb, CONV_W - 1, dff), lambda i, j: (i, 0, 0)),
            _const_spec((1, d)),
            _const_spec(w_in.shape),
            _const_spec(conv_w.shape),
            _const_spec((1, dff)),
            _const_spec(w_down.shape),
            _const_spec((1, d)),
        ],
        out_specs=[
            pl.BlockSpec((bb, tt, d), lambda i, j: (i, j, 0)),
            pl.BlockSpec((bb, CONV_W - 1, dff), lambda i, j: (i, 0, 0)),
        ],
        out_shape=[jax.ShapeDtypeStruct(x.shape, F32), jax.ShapeDtypeStruct(conv_state.shape, F32)],
        scratch_shapes=[pltpu.VMEM((bb, CONV_HEAD + tt, dff), F32)],
        compiler_params=_cparams(2),
        name="conv_ffn",
    )(x, conv_state, norm_g.reshape(1, d), w_in, conv_w, conv_b.reshape(1, dff), w_down,
      final_g.reshape(1, d))


def _rope_tables(pos):
    rot = HEAD_DIM // 4
    half = rot // 2
    inv_freq = ROPE_THETA ** (-jnp.arange(half, dtype=F32) / half)
    ang = pos.astype(F32)[:, None] * inv_freq[None, :]
    cos, sin = jnp.cos(ang), jnp.sin(ang)
    t = pos.shape[0]
    pad = HEAD_DIM - rot
    one = jnp.ones((t, pad), F32)
    zero_h = jnp.zeros((t, half), F32)
    zero_p = jnp.zeros((t, pad), F32)
    c64 = jnp.concatenate([cos, cos, one], axis=1)
    s1_64 = jnp.concatenate([zero_h, sin, zero_p], axis=1)
    s2_64 = jnp.concatenate([-sin, zero_h, zero_p], axis=1)
    ident = jnp.ones((t, HEAD_DIM), F32)
    zero64 = jnp.zeros((t, HEAD_DIM), F32)
    return jnp.stack([
        jnp.concatenate([c64, c64], axis=1), jnp.concatenate([s1_64, s1_64], axis=1),
        jnp.concatenate([s2_64, s2_64], axis=1),
        jnp.concatenate([c64, ident], axis=1), jnp.concatenate([s1_64, zero64], axis=1),
        jnp.concatenate([s2_64, zero64], axis=1)])


def _dsa_proj_kernel(x_ref, g_ref, w_ref, tab_ref, lnw_ref, lnb_ref,
                     q_ref, k_ref, v_ref, qi_ref, kw_ref, *, bb, tt):
    rows = bb * tt
    d = x_ref.shape[-1]
    half = HEAD_DIM // 8
    o1 = N_HEADS * HEAD_DIM
    o2 = o1 + N_KV_HEADS * HEAD_DIM
    o3 = o2 + N_KV_HEADS * HEAD_DIM
    o4 = o3 + IDX_HEADS * IDX_DIM

    def tab(i):
        t = tab_ref[i]
        return jnp.broadcast_to(t[None], (bb, tt, LANES)).reshape(rows, LANES)

    def rope(y, base):
        cos, s1, s2 = tab(base), tab(base + 1), tab(base + 2)
        tiles = []
        for m in range(y.shape[1] // LANES):
            yt = y[:, m * LANES:(m + 1) * LANES]
            tiles.append(yt * cos + pltpu.roll(yt, half, 1) * s1 + pltpu.roll(yt, LANES - half, 1) * s2)
        return tiles[0] if len(tiles) == 1 else jnp.concatenate(tiles, axis=1)

    x = x_ref[...].reshape(rows, d)
    h = _rms(x, g_ref[...]).astype(BF16)
    q = rope(_dot(h, w_ref[:, 0:o1]), 0) * (HEAD_DIM ** -0.5 * LOG2_E)
    q_ref[...] = q.astype(BF16).reshape(bb, tt, o1)
    k_ref[...] = rope(_dot(h, w_ref[:, o1:o2]), 0).reshape(bb, tt, o2 - o1)
    v_ref[...] = _dot(h, w_ref[:, o2:o3]).reshape(bb, tt, o3 - o2)
    qi_ref[...] = rope(_dot(h, w_ref[:, o3:o4]), 0).astype(BF16).reshape(bb, tt, o4 - o3)
    t = _dot(h, w_ref[:, o4:o4 + LANES])
    lane = lax.broadcasted_iota(I32, (1, LANES), 1)
    is_ki = lane < IDX_DIM
    mu = jnp.sum(jnp.where(is_ki, t, 0.0), axis=-1, keepdims=True) / IDX_DIM
    cen = jnp.where(is_ki, t - mu, 0.0)
    var = jnp.sum(cen * cen, axis=-1, keepdims=True) / IDX_DIM
    ki = rope(cen * lax.rsqrt(var + EPS) * lnw_ref[...] + lnb_ref[...], 3)
    wi = t * ((IDX_HEADS * IDX_DIM) ** -0.5)
    kw = jnp.where(is_ki, ki, jnp.where(lane < IDX_DIM + IDX_HEADS, wi, 0.0))
    kw_ref[...] = kw.reshape(bb, tt, LANES)


def _dsa_proj(x, norm_g, w_pad, tables, ln_w, ln_b, *, bb, tt):
    b, t, d = x.shape
    n_q = N_HEADS * HEAD_DIM
    n_kv = N_KV_HEADS * HEAD_DIM
    n_qi = IDX_HEADS * IDX_DIM
    pad = jnp.zeros((LANES - IDX_DIM,), F32)
    lnw = jnp.concatenate([ln_w, pad]).reshape(1, LANES)
    lnb = jnp.concatenate([ln_b, pad]).reshape(1, LANES)
    kern = functools.partial(_dsa_proj_kernel, bb=bb, tt=tt)

    def row_spec(n):
        return pl.BlockSpec((bb, tt, n), lambda i, j: (i, j, 0))

    return pl.pallas_call(
        kern,
        grid=(b // bb, t // tt),
        in_specs=[
            row_spec(d),
            _const_spec((1, d)),
            _const_spec(w_pad.shape),
            pl.BlockSpec((6, tt, LANES), lambda i, j: (0, j, 0)),
            _const_spec((1, LANES)),
            _const_spec((1, LANES)),
        ],
        out_specs=[row_spec(n_q), row_spec(n_kv), row_spec(n_kv), row_spec(n_qi), row_spec(LANES)],
        out_shape=[
            jax.ShapeDtypeStruct((b, t, n_q), BF16),
            jax.ShapeDtypeStruct((b, t, n_kv), F32),
            jax.ShapeDtypeStruct((b, t, n_kv), F32),
            jax.ShapeDtypeStruct((b, t, n_qi), BF16),
            jax.ShapeDtypeStruct((b, t, LANES), F32),
        ],
        compiler_params=_cparams(2),
        name="dsa_proj",
    )(x, norm_g.reshape(1, d), w_pad, tables, lnw, lnb)


def _order_key(score):
    score = jnp.where(score == 0.0, 0.0, score)
    bits = lax.bitcast_convert_type(score, I32)
    return bits ^ ((bits >> 31) & 0x7FFFFFFF)


def _fill_padded(dst_e, dst_o, src, r0, nrows):
    lane = lax.broadcasted_iota(I32, (1, LANES), 1)
    lo = lane < HEAD_DIM
    for m in range(src.shape[1] // LANES):
        tile = src[:, m * LANES:(m + 1) * LANES]
        rolled = pltpu.roll(tile, HEAD_DIM, 1)
        a_lo = jnp.where(lo, tile, 0.0).astype(BF16)
        b_hi = jnp.where(lo, 0.0, tile).astype(BF16)
        b_lo = jnp.where(lo, rolled, 0.0).astype(BF16)
        a_hi = jnp.where(lo, 0.0, rolled).astype(BF16)
        c0, c1 = 2 * m * LANES, (2 * m + 1) * LANES
        dst_e[pl.ds(r0, nrows), c0:c0 + LANES] = a_lo
        dst_o[pl.ds(r0, nrows), c0:c0 + LANES] = a_hi
        dst_e[pl.ds(r0, nrows), c1:c1 + LANES] = b_lo
        dst_o[pl.ds(r0, nrows), c1:c1 + LANES] = b_hi


def _attn_core(q_ref, qi_ref, wq_ref, x_ref, wout_ref, o_ref, segs, *, tq, q_pos0, n_sel):
    qrow = lax.broadcasted_iota(I32, (tq, 1), 0)
    q_chunk = (q_pos0 + qrow) // CHUNK
    wq = wq_ref[...].reshape(tq, LANES)
    wi = [wq[:, IDX_DIM + hh:IDX_DIM + hh + 1] for hh in range(IDX_HEADS)]
    qi = qi_ref[...].reshape(tq, IDX_HEADS * IDX_DIM)

    for sg in segs:
        for c0 in range(0, sg["len"], sg["chunk"]):
            cw = min(sg["chunk"], sg["len"] - c0)
            sc = jnp.zeros((tq, cw), F32)
            for m in range(IDX_HEADS // 2):
                qt = qi[:, m * LANES:(m + 1) * LANES]
                de = _dot_nt(qt, sg["kie"][c0:c0 + cw, 0:LANES])
                do = _dot_nt(qt, sg["kio"][c0:c0 + cw, 0:LANES])
                sc = sc + wi[2 * m] * jnp.maximum(de, 0.0) + wi[2 * m + 1] * jnp.maximum(do, 0.0)
            kpos = sg["pos0"] + c0 + lax.broadcasted_iota(I32, (1, cw), 1)
            adm = (kpos // CHUNK) <= q_chunk
            sg["key_s"][:, c0:c0 + cw] = jnp.where(adm, _order_key(sc), KEY_NEG_INF)

    def count(pred):
        tot = jnp.zeros((tq, 1), I32)
        for sg in segs:
            keys = sg["key_s"][:, 0:sg["len"]]
            tot = tot + jnp.sum(pred(keys).astype(I32), axis=-1, keepdims=True)
        return tot

    thr = jnp.where(count(lambda k: k >= 0) >= n_sel, 0, INT_MIN).astype(I32)

    def bit_step(i, thr):
        cand = thr | lax.shift_left(jnp.int32(1), 30 - i)
        return jnp.where(count(lambda k: k >= cand) >= n_sel, cand, thr)

    thr = lax.fori_loop(0, 31, bit_step, thr)
    n_gt = count(lambda k: k > thr)
    n_ge = count(lambda k: k >= thr)
    ties_wanted = n_sel - n_gt

    for sg in segs:
        keys = sg["key_s"][:, 0:sg["len"]]
        sel = (keys >= thr) & (keys > KEY_NEG_INF)
        sg["bias_s"][:, 0:sg["len"]] = jnp.where(sel, 0.0, -jnp.inf)

    has_excess = jnp.max(jnp.where(thr > KEY_NEG_INF, n_ge - n_sel, 0)) > 0

    @pl.when(has_excess)
    def _():
        seen = jnp.zeros((tq, 1), F32)
        wanted = ties_wanted.astype(F32)
        for sg in segs:
            for c0 in range(0, sg["len"], LANES):
                cw = min(LANES, sg["len"] - c0)
                keys = sg["key_s"][:, c0:c0 + cw]
                tie = keys == thr
                r_i = lax.broadcasted_iota(I32, (cw, cw), 0)
                c_i = lax.broadcasted_iota(I32, (cw, cw), 1)
                upper = jnp.where(r_i <= c_i, 1.0, 0.0).astype(BF16)
                tie_f = jnp.where(tie, 1.0, 0.0)
                rank = seen + _dot(tie_f.astype(BF16), upper)
                sel = ((keys > thr) | (tie & (rank <= wanted))) & (keys > KEY_NEG_INF)
                sg["bias_s"][:, c0:c0 + cw] = jnp.where(sel, 0.0, -jnp.inf)
                seen = seen + jnp.sum(tie_f, axis=-1, keepdims=True)

    q = q_ref[...].reshape(tq, N_HEADS * HEAD_DIM)
    tiles = []
    for m in range(N_HEADS // 2):
        g = (2 * m) // (N_HEADS // N_KV_HEADS)
        gs = slice(g * LANES, (g + 1) * LANES)
        qt = q[:, m * LANES:(m + 1) * LANES]
        acc = jnp.zeros((tq, LANES), F32)
        for kname, vname in (("ke", "ve"), ("ko", "vo")):
            logits = [_dot_nt(qt, sg[kname][0:sg["len"], gs]) + sg["bias_s"][:, 0:sg["len"]] for sg in segs]
            mx = functools.reduce(jnp.maximum, [jnp.max(l, axis=-1, keepdims=True) for l in logits])
            ps = [jnp.exp2(l - mx) for l in logits]
            den = functools.reduce(jnp.add, [jnp.sum(p, axis=-1, keepdims=True) for p in ps])
            pv = functools.reduce(jnp.add, [_dot(p.astype(BF16), sg[vname][0:sg["len"], gs])
                                            for p, sg in zip(ps, segs)])
            acc = acc + pv / den
        tiles.append(acc.astype(BF16))
    o = jnp.concatenate(tiles, axis=1)
    d = x_ref.shape[-1]
    o_ref[...] = (_dot(o, wout_ref[...]) + x_ref[...].reshape(tq, d)).reshape(o_ref.shape)


def _attn_prompt_kernel(q_ref, qi_ref, wq_ref, k_ref, v_ref, kw_ref, x_ref, wout_ref, o_ref,
                        ke, ko, vt, kie, kio, qs, key_s, bias_s, ot_s, half_s, *, tq, t_len, kc, n_sel):
    j = pl.program_id(1)
    nct = t_len // kc
    sub_per_chunk = kc // LANES
    lane = lax.broadcasted_iota(I32, (1, LANES), 1)
    lo = lane < HEAD_DIM

    @pl.when(j == 0)
    def _():
        for c in range(nct):
            rows = slice(c * kc, (c + 1) * kc)
            k_c = k_ref[0, rows, :]
            for m in range(N_KV_HEADS // 2):
                tile = k_c[:, m * LANES:(m + 1) * LANES]
                rolled = pltpu.roll(tile, HEAD_DIM, 1)
                ke[(2 * m) * nct + c] = jnp.where(lo, tile, 0.0).astype(BF16)
                ko[(2 * m) * nct + c] = jnp.where(lo, 0.0, rolled).astype(BF16)
                ke[(2 * m + 1) * nct + c] = jnp.where(lo, rolled, 0.0).astype(BF16)
                ko[(2 * m + 1) * nct + c] = jnp.where(lo, 0.0, tile).astype(BF16)
            v_t = v_ref[0, rows, :].T
            for g in range(N_KV_HEADS):
                vt[g * nct + c, 0:HEAD_DIM, :] = v_t[g * HEAD_DIM:(g + 1) * HEAD_DIM, :].astype(BF16)
                vt[g * nct + c, HEAD_DIM:HEAD_DIM + ONES_ROWS, :] = jnp.ones((ONES_ROWS, kc), BF16)
            kw_c = kw_ref[0, rows, :]
            kie[c] = jnp.where(lo, kw_c, 0.0).astype(BF16)
            kio[c] = jnp.where(lo, 0.0, pltpu.roll(kw_c, HEAD_DIM, 1)).astype(BF16)

    for m in range(N_HEADS // 2):
        qs[m] = q_ref[0, :, m * LANES:(m + 1) * LANES]

    n_chunks = ((j + 1) * tq + kc - 1) // kc
    w_t = wq_ref[0].T
    w_rows = [w_t[IDX_DIM + hh:IDX_DIM + hh + 1, :] for hh in range(IDX_HEADS)]
    qi = qi_ref[0]
    q_chunk = (j * tq + lax.broadcasted_iota(I32, (1, tq), 1)) // CHUNK

    def score_body(c, carry):
        kie_c, kio_c = kie[c], kio[c]
        sc = jnp.zeros((kc, tq), F32)
        for m in range(IDX_HEADS // 2):
            qt = qi[:, m * LANES:(m + 1) * LANES]
            sc = (sc + w_rows[2 * m] * jnp.maximum(_dot_nt(kie_c, qt), 0.0)
                  + w_rows[2 * m + 1] * jnp.maximum(_dot_nt(kio_c, qt), 0.0))
        k_chunk = (c * kc + lax.broadcasted_iota(I32, (kc, 1), 0)) // CHUNK
        keys = jnp.where(k_chunk <= q_chunk, _order_key(sc), KEY_NEG_INF)
        key_s[c] = keys
        half_s[c] = lax.shift_right_arithmetic(keys, 16).astype(I16)
        return carry

    lax.fori_loop(0, n_chunks, score_body, 0)

    def search16(need):
        def count_ge(cand):
            def body(c, acc):
                hit = (half_s[c] >= cand).astype(I16)
                for r in range(0, kc, 16):
                    acc = acc + hit[r:r + 16, :]
                return acc
            acc = lax.fori_loop(0, n_chunks, body, jnp.zeros((16, tq), I16))
            return jnp.sum(acc.astype(I32), axis=0, keepdims=True)

        def bit_step(i, t16):
            cand = t16 ^ lax.shift_left(jnp.int32(1), 15 - i)
            cand = lax.shift_right_arithmetic(lax.shift_left(cand, 16), 16)
            return jnp.where(count_ge(cand.astype(I16)) >= need, cand, t16)

        return lax.fori_loop(0, 16, bit_step, jnp.full((1, tq), -32768, I32))

    thr_hi = search16(n_sel)

    def low_body(c, acc):
        keys = key_s[c]
        hi = lax.shift_right_arithmetic(keys, 16)
        low = (keys & 0xFFFF) - 32768
        half_s[c] = jnp.where(hi == thr_hi, low, -32768).astype(I16)
        return acc + jnp.sum((hi > thr_hi).astype(I32).reshape(kc // 8, 8, tq), axis=0)

    n_above = jnp.sum(lax.fori_loop(0, n_chunks, low_body, jnp.zeros((8, tq), I32)), axis=0, keepdims=True)
    thr_lo = search16(n_sel - n_above)
    thr = lax.shift_left(thr_hi, 16) | (thr_lo + 32768)

    def bias_body(c, acc):
        keys = key_s[c]
        ge = keys >= thr
        bias_s[c] = jnp.where(ge & (keys > KEY_NEG_INF), 0.0, -jnp.inf)
        fold = lambda hit: jnp.sum(hit.astype(I32).reshape(kc // 8, 8, tq), axis=0)
        return acc[0] + fold(ge), acc[1] + fold(keys > thr)

    zero8 = jnp.zeros((8, tq), I32)
    n_ge, n_gt = [jnp.sum(a, axis=0, keepdims=True)
                  for a in lax.fori_loop(0, n_chunks, bias_body, (zero8, zero8))]

    has_excess = jnp.max(jnp.where(thr > KEY_NEG_INF, n_ge - n_sel, 0)) > 0

    @pl.when(has_excess)
    def _():
        wanted = (n_sel - n_gt).astype(F32)
        r_i = lax.broadcasted_iota(I32, (LANES, LANES), 0)
        c_i = lax.broadcasted_iota(I32, (LANES, LANES), 1)
        lower = jnp.where(c_i <= r_i, 1.0, 0.0).astype(BF16)

        def tie_body(u, seen):
            c = u // sub_per_chunk
            r0 = pl.multiple_of((u % sub_per_chunk) * LANES, LANES)
            keys = key_s[c, pl.ds(r0, LANES), :]
            tie = keys == thr
            tie_f = jnp.where(tie, 1.0, 0.0)
            rank = seen + _dot(lower, tie_f.astype(BF16))
            sel = ((keys > thr) | (tie & (rank <= wanted))) & (keys > KEY_NEG_INF)
            bias_s[c, pl.ds(r0, LANES), :] = jnp.where(sel, 0.0, -jnp.inf)
            return seen + jnp.sum(tie_f, axis=0, keepdims=True)

        lax.fori_loop(0, n_chunks * sub_per_chunk, tie_body, jnp.zeros((1, tq), F32))

    n_grp = N_HEADS // N_KV_HEADS

    def group_body(g, carry):
        q_tiles = [qs[g * (n_grp // 2) + i] for i in range(n_grp // 2)]

        def chunk_body(c, st):
            bias = bias_s[c]
            v_c = vt[g * nct + c]
            k_e, k_o = ke[g * nct + c], ko[g * nct + c]
            logits = [_dot_nt(k_x, qt) for qt in q_tiles for k_x in (k_e, k_o)]
            tails = [lax.bitcast_convert_type(l[kc - 8:kc, :], I32) for l in logits[1:]]
            anchor = functools.reduce(jnp.bitwise_or, tails) & lax.shift_right_arithmetic(c, 31)
            anchor = lax.bitcast_convert_type(anchor, F32)[0:1, :]
            out = []
            for h in range(n_grp):
                m_run, acc = st[2 * h:2 * h + 2]
                if h == 0:
                    m_run = m_run + anchor
                logit = logits[h] + bias
                m_new = jnp.maximum(m_run, jnp.max(logit, axis=0, keepdims=True))
                m_safe = jnp.where(m_new > -jnp.inf, m_new, 0.0)
                p = jnp.exp2(logit - m_safe)
                alpha = jnp.exp2(m_run - m_safe)
                out += [m_new, alpha * acc + _dot(v_c, p.astype(BF16))]
            return tuple(out)

        init = (jnp.full((1, tq), -jnp.inf, F32), jnp.zeros((HEAD_DIM + ONES_ROWS, tq), F32))
        st = lax.fori_loop(0, n_chunks, chunk_body, init * n_grp)
        for h in range(n_grp):
            r0 = pl.multiple_of(g * (n_grp * HEAD_DIM) + h * HEAD_DIM, HEAD_DIM)
            acc = st[2 * h + 1]
            ot_s[pl.ds(r0, HEAD_DIM), :] = acc[0:HEAD_DIM, :] / acc[HEAD_DIM:HEAD_DIM + 1, :]
        return carry

    lax.fori_loop(0, N_KV_HEADS, group_body, 0)

    o = jnp.concatenate([ot_s[m * LANES:(m + 1) * LANES, :].T for m in range(N_HEADS // 2)], axis=1)
    o_ref[0] = _dot(o.astype(BF16), wout_ref[...]) + x_ref[0]


def _dsa_attn_prompt(q, qi, kw, k, v, x, w_out, *, tq, kc):
    b, t, d = x.shape
    n_sel = min(TOPK_MAX, t // 4)
    n_kv = N_KV_HEADS * HEAD_DIM
    nct = t // kc
    kern = functools.partial(_attn_prompt_kernel, tq=tq, t_len=t, kc=kc, n_sel=n_sel)

    def tile_spec(n):
        return pl.BlockSpec((1, tq, n), lambda i, j: (i, j, 0))

    def full_spec(n):
        return pl.BlockSpec((1, t, n), lambda i, j: (i, 0, 0))

    return pl.pallas_call(
        kern,
        grid=(b, t // tq),
        in_specs=[tile_spec(q.shape[-1]), tile_spec(qi.shape[-1]), tile_spec(LANES),
                  full_spec(n_kv), full_spec(n_kv), full_spec(LANES),
                  tile_spec(d), _const_spec(w_out.shape)],
        out_specs=tile_spec(d),
        out_shape=jax.ShapeDtypeStruct(x.shape, F32),
        scratch_shapes=[
            pltpu.VMEM((N_KV_HEADS * nct, kc, LANES), BF16), pltpu.VMEM((N_KV_HEADS * nct, kc, LANES), BF16),
            pltpu.VMEM((N_KV_HEADS * nct, HEAD_DIM + ONES_ROWS, kc), BF16),
            pltpu.VMEM((nct, kc, LANES), BF16), pltpu.VMEM((nct, kc, LANES), BF16),
            pltpu.VMEM((N_HEADS // 2, tq, LANES), BF16),
            pltpu.VMEM((nct, kc, tq), I32), pltpu.VMEM((nct, kc, tq), F32),
            pltpu.VMEM((N_HEADS * HEAD_DIM, tq), F32),
            pltpu.VMEM((nct, kc, tq), I16),
        ],
        compiler_params=_cparams(2),
        name="dsa_attn_prompt",
    )(q, qi, kw, k, v, kw, x, w_out)


def _attn_sample_kernel(q_ref, qi_ref, kw_ref, k_ref, v_ref, ckt_ref, cvt_ref, ckit_ref, x_ref, wout_ref, o_ref,
                        kt16, vt16, qg_s, kn_s, vn_s, og_s, *, tq, past, n_sel):
    n_grp = N_HEADS // N_KV_HEADS
    lane = lax.broadcasted_iota(I32, (1, LANES), 1)
    lo = lane < HEAD_DIM

    def lower_half(x128, upper):
        return jnp.where(lo, pltpu.roll(x128, HEAD_DIM, 1) if upper else x128, 0.0)

    kt16[...] = ckt_ref[0].astype(BF16)
    vt16[...] = cvt_ref[0].astype(BF16)
    kit16 = ckit_ref[0].astype(BF16)
    kw = kw_ref[0]
    kw16 = kw.astype(BF16)
    k_new, v_new = k_ref[0], v_ref[0]
    q32 = q_ref[0].astype(F32)
    for g in range(N_KV_HEADS):
        tile = slice((g // 2) * LANES, (g // 2 + 1) * LANES)
        kn_s[g] = lower_half(k_new[:, tile], g % 2 == 1).astype(BF16)
        vn_s[g] = lower_half(v_new[:, tile], g % 2 == 1).astype(BF16)
        heads = [n_grp * g + h for h in range(n_grp)]
        qg_s[g] = jnp.concatenate(
            [lower_half(q32[:, (hd // 2) * LANES:(hd // 2 + 1) * LANES], hd % 2 == 1) for hd in heads],
            axis=0).astype(BF16)

    qi32 = qi_ref[0].astype(F32)
    qi_rows = jnp.concatenate(
        [lower_half(qi32[:, (hh // 2) * LANES:(hh // 2 + 1) * LANES], hh % 2 == 1) for hh in range(IDX_HEADS)],
        axis=0).astype(BF16)
    d_c = _dot(qi_rows[:, 0:IDX_DIM], kit16)
    d_n = _dot_nt(qi_rows, kw16)
    sc_c = jnp.zeros((tq, past), F32)
    sc_n = jnp.zeros((tq, tq), F32)
    for hh in range(IDX_HEADS):
        w_col = kw[:, IDX_DIM + hh:IDX_DIM + hh + 1]
        rows = slice(hh * tq, (hh + 1) * tq)
        sc_c = sc_c + w_col * jnp.maximum(d_c[rows, :], 0.0)
        sc_n = sc_n + w_col * jnp.maximum(d_n[rows, :], 0.0)
    q_chunk = (past + lax.broadcasted_iota(I32, (tq, 1), 0)) // CHUNK
    k_chunk = (past + lax.broadcasted_iota(I32, (1, tq), 1)) // CHUNK
    key_c = _order_key(sc_c)
    key_n = jnp.where(k_chunk <= q_chunk, _order_key(sc_n), KEY_NEG_INF)

    def count(pred):
        return (jnp.sum(pred(key_c).astype(I32), axis=-1, keepdims=True)
                + jnp.sum(pred(key_n).astype(I32), axis=-1, keepdims=True))

    def bit_step(i, thr):
        cand = thr ^ lax.shift_left(jnp.int32(1), 31 - i)
        return jnp.where(count(lambda k: k >= cand) >= n_sel, cand, thr)

    thr = lax.fori_loop(0, 32, bit_step, jnp.full((tq, 1), INT_MIN, I32))
    wanted = (n_sel - count(lambda k: k > thr)).astype(F32)

    def select(keys, seen):
        tie = keys == thr
        tie_f = jnp.where(tie, 1.0, 0.0)
        ranks = []
        for c0 in range(0, keys.shape[1], LANES):
            cw = min(LANES, keys.shape[1] - c0)
            r_i = lax.broadcasted_iota(I32, (cw, cw), 0)
            c_i = lax.broadcasted_iota(I32, (cw, cw), 1)
            upper = jnp.where(r_i <= c_i, 1.0, 0.0).astype(BF16)
            part = tie_f[:, c0:c0 + cw]
            ranks.append(seen + _dot(part.astype(BF16), upper))
            seen = seen + jnp.sum(part, axis=-1, keepdims=True)
        rank = ranks[0] if len(ranks) == 1 else jnp.concatenate(ranks, axis=1)
        sel = ((keys > thr) | (tie & (rank <= wanted))) & (keys > KEY_NEG_INF)
        return jnp.where(sel, 0.0, -jnp.inf), seen

    bias_c, seen = select(key_c, jnp.zeros((tq, 1), F32))
    bias_n, _ = select(key_n, seen)
    bias_c = jnp.concatenate([bias_c] * n_grp, axis=0)
    bias_n = jnp.concatenate([bias_n] * n_grp, axis=0)

    def group_body(g, carry):
        r0 = pl.multiple_of(g * HEAD_DIM, HEAD_DIM)
        qg = qg_s[g]
        l_c = _dot(qg[:, 0:HEAD_DIM], kt16[pl.ds(r0, HEAD_DIM), :]) + bias_c
        l_n = _dot_nt(qg, kn_s[g]) + bias_n
        mx = jnp.maximum(jnp.max(l_c, axis=-1, keepdims=True), jnp.max(l_n, axis=-1, keepdims=True))
        p_c = jnp.exp2(l_c - mx)
        p_n = jnp.exp2(l_n - mx)
        den = jnp.sum(p_c, axis=-1, keepdims=True) + jnp.sum(p_n, axis=-1, keepdims=True)
        pv = (_dot_nt(p_c.astype(BF16), vt16[pl.ds(r0, HEAD_DIM), :])
              + _dot(p_n.astype(BF16), vn_s[g])[:, 0:HEAD_DIM])
        og_s[g] = pv / den
        return carry

    lax.fori_loop(0, N_KV_HEADS, group_body, 0)

    tiles = []
    for m in range(N_HEADS // 2):
        pair = []
        for hd in (2 * m, 2 * m + 1):
            h = hd % n_grp
            pair.append(og_s[hd // n_grp, h * tq:(h + 1) * tq, :])
        tiles.append(jnp.concatenate(pair, axis=1))
    o = jnp.concatenate(tiles, axis=1).astype(BF16)
    o_ref[0] = _dot(o, wout_ref[...]) + x_ref[0]


def _dsa_attn_sample(q, qi, kw, k, v, cache_kt, cache_vt, cache_kit, x, w_out):
    b, t, d = x.shape
    past = cache_kt.shape[-1]
    n_sel = min(TOPK_MAX, (past + t) // 4)
    n_kv = N_KV_HEADS * HEAD_DIM
    n_grp = N_HEADS // N_KV_HEADS
    kern = functools.partial(_attn_sample_kernel, tq=t, past=past, n_sel=n_sel)

    def spec(rows, n):
        return pl.BlockSpec((1, rows, n), lambda i: (i, 0, 0))

    return pl.pallas_call(
        kern,
        grid=(b,),
        in_specs=[spec(t, q.shape[-1]), spec(t, qi.shape[-1]), spec(t, LANES), spec(t, n_kv), spec(t, n_kv),
                  spec(n_kv, past), spec(n_kv, past), spec(IDX_DIM, past),
                  spec(t, d), _const_spec(w_out.shape)],
        out_specs=spec(t, d),
        out_shape=jax.ShapeDtypeStruct(x.shape, F32),
        scratch_shapes=[
            pltpu.VMEM((n_kv, past), BF16), pltpu.VMEM((n_kv, past), BF16),
            pltpu.VMEM((N_KV_HEADS, n_grp * t, LANES), BF16),
            pltpu.VMEM((N_KV_HEADS, t, LANES), BF16), pltpu.VMEM((N_KV_HEADS, t, LANES), BF16),
            pltpu.VMEM((N_KV_HEADS, n_grp * t, HEAD_DIM), F32),
        ],
        compiler_params=_cparams(1),
        name="dsa_attn_sample",
    )(q, qi, kw, k, v, cache_kt, cache_vt, cache_kit, x, w_out)


def _tiling(b, t):
    if t >= 256:
        return dict(bb=1, tt=256, blk=128, sub=32)
    bb = max(1, min(b, 128 // t))
    return dict(bb=bb, tt=t, blk=t, sub=t)


def _trunk(x, hg_state, conv_state, cache, p):
    b, t, d = x.shape
    tl = _tiling(b, t)
    bb, tt = tl["bb"], tl["tt"]
    x, s_new = _hgrn_mixer(x, hg_state[0], p["norm_mix"][0], p["hg_w_in"][0], p["lower_bounds"][0],
                           p["hg_norm"][0], p["hg_w_out"][0], **dict(tl, tt=512 if t % 512 == 0 else tt))
    x, cs0 = _conv_ffn(x, conv_state[0], p["norm_ffn"][0], p["ffn_w_in"][0], p["ffn_conv_w"][0],
                       p["ffn_conv_b"][0], p["ffn_w_down"][0], p["norm_final"], bb=bb, tt=tt, final_norm=False)
    past = 0 if cache is None else cache[0].shape[1]
    tables = _rope_tables(past + jnp.arange(t))
    q, k, v, qi, kw = _dsa_proj(x, p["norm_mix"][1], p["ds_w_in"][0], tables, p["ds_kln_w"][0],
                                p["ds_kln_b"][0], bb=bb, tt=tt)
    if cache is None:
        x = _dsa_attn_prompt(q, qi, kw, k, v, x, p["ds_w_out"][0], tq=min(t, 512), kc=min(t, 512))
    else:
        ck, cv, cki = cache
        def keys_last(c):
            return jnp.swapaxes(c.reshape(b, past, -1), 1, 2)

        x = _dsa_attn_sample(q, qi, kw, k, v, keys_last(ck), keys_last(cv), keys_last(cki), x, p["ds_w_out"][0])
    x, cs1 = _conv_ffn(x, conv_state[1], p["norm_ffn"][1], p["ffn_w_in"][1], p["ffn_conv_w"][1],
                       p["ffn_conv_b"][1], p["ffn_w_down"][1], p["norm_final"], bb=bb, tt=tt, final_norm=True)
    k = k.reshape(1, b, t, N_KV_HEADS, HEAD_DIM)
    v = v.reshape(1, b, t, N_KV_HEADS, HEAD_DIM)
    ki = kw[:, :, :IDX_DIM].reshape(1, b, t, IDX_DIM)
    return x, s_new[None], jnp.stack([cs0, cs1]), k, v, ki


def kernel(x_prompt, x_sample, cache_k, cache_v, cache_kidx, state_hgrn, state_conv, norm_mix, norm_ffn, norm_final, hg_w_in, hg_lb, hg_norm, hg_w_out, ds_w_in, ds_kln_w, ds_kln_b, ds_w_out, ffn_w_in, ffn_conv_w, ffn_conv_b, ffn_w_down):
    dsa_in = ds_w_in.shape[-1]
    dsa_pad = (-dsa_in) % LANES
    p = dict(
        norm_mix=norm_mix, norm_ffn=norm_ffn, norm_final=norm_final,
        hg_w_in=hg_w_in.astype(BF16), hg_norm=hg_norm, hg_w_out=hg_w_out.astype(BF16),
        lower_bounds=jnp.cumsum(jax.nn.softmax(hg_lb.astype(F32), axis=0), axis=0),
        ds_w_in=jnp.pad(ds_w_in, ((0, 0), (0, 0), (0, dsa_pad))).astype(BF16),
        ds_kln_w=ds_kln_w, ds_kln_b=ds_kln_b, ds_w_out=ds_w_out.astype(BF16),
        ffn_w_in=ffn_w_in.astype(BF16), ffn_conv_w=ffn_conv_w, ffn_conv_b=ffn_conv_b,
        ffn_w_down=ffn_w_down.astype(BF16),
    )
    b = x_prompt.shape[0]
    hg0 = jnp.zeros((state_hgrn.shape[0], b) + state_hgrn.shape[2:], F32)
    conv0 = jnp.zeros((state_conv.shape[0], b) + state_conv.shape[2:], F32)
    y_p, hg_p, conv_p, k_p, v_p, ki_p = _trunk(x_prompt, hg0, conv0, None, p)
    y_s, hg_s, conv_s, k_s, v_s, ki_s = _trunk(x_sample, state_hgrn, state_conv,
                                               (cache_k[0], cache_v[0], cache_kidx[0]), p)
    return (y_p, y_s, k_p, v_p, ki_p, hg_p, conv_p, k_s, v_s, ki_s, hg_s, conv_s)
```

```python
import functools

import jax
import jax.numpy as jnp
from jax import lax
from jax.experimental import pallas as pl
from jax.experimental.pallas import tpu as pltpu

F32 = jnp.float32
BF16 = jnp.bfloat16
I32 = jnp.int32
I16 = jnp.int16

CHUNK = 64
EPS = 1e-6
HG_HEADS = 8
HG_KDIM = 128
N_HEADS = 16
HEAD_DIM = 64
N_KV_HEADS = 4
IDX_HEADS = 8
IDX_DIM = 64
TOPK_MAX = 256
ROPE_THETA = 500000.0
CONV_W = 3

LANES = 128
VMEM_LIMIT = 56 * 1024 * 1024
EXP_CLAMP = 80.0
LOG2_E = 1.4426950408889634
ONES_ROWS = 16
KEY_NEG_INF = -2139095041
INT_MIN = -2147483648


def _cparams(n_axes, flags=None):
    return pltpu.CompilerParams(dimension_semantics=("arbitrary",) * n_axes,
                                vmem_limit_bytes=VMEM_LIMIT, flags=flags)


def _const_spec(shape):
    nd = len(shape)
    return pl.BlockSpec(shape, lambda *_: (0,) * nd, pipeline_mode=pl.Buffered(1))


def _rms(x, g):
    return x * lax.rsqrt(jnp.mean(x * x, axis=-1, keepdims=True) + EPS) * g


def _silu(x):
    return x * jax.nn.sigmoid(x)


def _dot(a, b):
    return jnp.dot(a, b, preferred_element_type=F32)


def _dot_nt(a, b):
    return lax.dot_general(a, b, (((1,), (1,)), ((), ())), preferred_element_type=F32)


def _dot_tn(a, b):
    return lax.dot_general(a, b, (((0,), (0,)), ((), ())), preferred_element_type=F32)


def _hgrn_kernel(x_ref, s0_ref, g_ref, win_ref, lb_ref, hn_ref, wout_ref, o_ref, s_ref,
                 q_s, k_s, g_s, v_s, gate_s, on_s, st_s, o_s, *, bb, tt, blk, sub):
    j = pl.program_id(1)
    rows = bb * tt
    d = x_ref.shape[-1]
    fdim = HG_HEADS * HG_KDIM
    n_sub = blk // sub

    @pl.when(j == 0)
    def _():
        for bi in range(bb):
            for hh in range(HG_HEADS):
                st_s[bi * HG_HEADS + hh] = s0_ref[bi, hh].T

    x = x_ref[...].reshape(rows, d)
    h = _rms(x, g_ref[...]).astype(BF16)
    lb = lb_ref[...]
    fg = lb + (1.0 - lb) * jax.nn.sigmoid(_dot(h, win_ref[:, fdim:2 * fdim]))
    k_s[...] = 1.0 - fg
    g_s[...] = jnp.log(fg)
    q_s[...] = _silu(_dot(h, win_ref[:, 0:fdim]))
    gate_s[...] = _silu(_dot(h, win_ref[:, 2 * fdim + d:]))
    v_s[...] = _dot(h, win_ref[:, 2 * fdim:2 * fdim + d])

    r_i = lax.broadcasted_iota(I32, (blk, blk), 0)
    c_i = lax.broadcasted_iota(I32, (blk, blk), 1)
    causal = c_i <= r_i
    tril = jnp.where(causal, 1.0, 0.0).astype(BF16)
    hn = hn_ref[...]
    blocks_per_stream = tt // blk

    def block_body(idx, carry):
        r0 = pl.multiple_of(idx * blk, blk)
        bi = idx // blocks_per_stream
        lg = g_s[pl.ds(r0, blk), :]
        lg_hi = lg.astype(BF16)
        rem = lg - lg_hi.astype(F32)
        lg_mid = rem.astype(BF16)
        lg_lo = (rem - lg_mid.astype(F32)).astype(BF16)
        g_all = _dot(tril, lg_hi) + _dot(tril, lg_mid) + _dot(tril, lg_lo)
        run_zero = lax.shift_right_arithmetic(idx, 31)

        def anchor(x):
            tail = lax.bitcast_convert_type(x[x.shape[0] - 8:, :], I32) & run_zero
            return lax.bitcast_convert_type(tail, F32)[0:1, :]

        heads = [slice(hh * HG_KDIM, (hh + 1) * HG_KDIM) for hh in range(HG_HEADS)]
        qcat, kcat, qfull, kdec_last, v16, decay = [], [], [], [], [], []
        for sl in heads:
            gc = g_all[:, sl]
            q = q_s[pl.ds(r0, blk), sl]
            kk = k_s[pl.ds(r0, blk), sl]
            refs = [jnp.zeros((1, HG_KDIM), F32)] + [gc[s * sub - 1:s * sub, :] for s in range(1, n_sub)]
            q_parts, k_parts = [], []
            for s in range(n_sub):
                lo_r, hi_r = s * sub, (s + 1) * sub
                qg = q[lo_r:hi_r, :] * jnp.exp(gc[lo_r:hi_r, :] - refs[s])
                pieces = [qg]
                if lo_r:
                    pieces.insert(0, jnp.zeros((lo_r, HG_KDIM), F32))
                if blk - hi_r:
                    pieces.append(jnp.zeros((blk - hi_r, HG_KDIM), F32))
                q_parts.append(pieces[0] if len(pieces) == 1 else jnp.concatenate(pieces, axis=0))
                kd_s = kk[0:hi_r, :] * jnp.exp(jnp.minimum(refs[s] - gc[0:hi_r, :], EXP_CLAMP))
                k_parts.append(kd_s if hi_r == blk else
                               jnp.concatenate([kd_s, jnp.zeros((blk - hi_r, HG_KDIM), F32)], axis=0))
            qcat.append(jnp.concatenate(q_parts, axis=1).astype(BF16))
            kcat.append(jnp.concatenate(k_parts, axis=1).astype(BF16))
            qfull.append((q * jnp.exp(gc)).astype(BF16))
            g_last = gc[blk - 1:blk, :]
            kdec_last.append((kk * jnp.exp(g_last - gc)).astype(BF16))
            decay.append(jnp.exp(g_last))
            v16.append(v_s[pl.ds(r0, blk), sl].astype(BF16))
        a_list = [_dot_nt(qcat[hh], kcat[hh]) for hh in range(HG_HEADS)]
        s_inc = [_dot_tn(v16[hh], kdec_last[hh]) for hh in range(HG_HEADS)]
        a_list[0] = a_list[0] + anchor(a_list[-1])
        o_list = []
        for hh in range(HG_HEADS):
            a = jnp.where(causal, a_list[hh], 0.0).astype(BF16)
            st = st_s[bi * HG_HEADS + hh]
            o_list.append(_dot(a, v16[hh]) + _dot_nt(qfull[hh], st.astype(BF16)))
            st_s[bi * HG_HEADS + hh] = st * decay[hh] + s_inc[hh]
        o_list[0] = o_list[0] + anchor(o_list[-1])
        for hh, sl in enumerate(heads):
            on = _rms(o_list[hh], hn) * gate_s[pl.ds(r0, blk), sl]
            on_s[pl.ds(r0, blk), sl] = on.astype(BF16)
        return carry

    sub_sums = jnp.sum(g_s[...].reshape(rows // sub, sub, fdim), axis=1)
    blocked_ok = jnp.min(sub_sums) >= -EXP_CLAMP

    @pl.when(blocked_ok)
    def _():
        lax.fori_loop(0, rows // blk, block_body, 0, unroll=4)

    @pl.when(jnp.logical_not(blocked_ok))
    def _():
        grp = 16
        o_s[...] = jnp.zeros(o_s.shape, F32)
        in_grp = lax.broadcasted_iota(I32, (grp, 1), 0)

        def frame_body(r, carry):
            r0 = pl.multiple_of((r // grp) * grp, grp)
            bi = r // tt
            this = in_grp == (r % grp)
            for hh in range(HG_HEADS):
                sl = slice(hh * HG_KDIM, (hh + 1) * HG_KDIM)

                def only(ref):
                    return jnp.where(this, ref[pl.ds(r0, grp), sl], 0.0)

                forget = jnp.exp(jnp.sum(only(g_s), axis=0, keepdims=True))
                st = (st_s[bi * HG_HEADS + hh] * forget
                      + _dot_tn(only(v_s).astype(BF16), only(k_s).astype(BF16)))
                st_s[bi * HG_HEADS + hh] = st
                o_s[pl.ds(r0, grp), sl] += _dot_nt(only(q_s).astype(BF16), st.astype(BF16))
            return carry

        lax.fori_loop(0, rows, frame_body, 0)
        for hh in range(HG_HEADS):
            sl = slice(hh * HG_KDIM, (hh + 1) * HG_KDIM)
            on_s[:, sl] = (_rms(o_s[:, sl], hn) * gate_s[:, sl]).astype(BF16)

    out = _dot(on_s[...], wout_ref[...]) + x
    o_ref[...] = out.reshape(bb, tt, d)

    @pl.when(j == pl.num_programs(1) - 1)
    def _():
        for bi in range(bb):
            for hh in range(HG_HEADS):
                s_ref[bi, hh] = st_s[bi * HG_HEADS + hh].T


def _hgrn_mixer(x, s0, norm_g, w_in, lb, hn, w_out, *, bb, tt, blk, sub):
    b, t, d = x.shape
    fdim = HG_HEADS * HG_KDIM
    rows = bb * tt
    kern = functools.partial(_hgrn_kernel, bb=bb, tt=tt, blk=blk, sub=sub)
    return pl.pallas_call(
        kern,
        grid=(b // bb, t // tt),
        in_specs=[
            pl.BlockSpec((bb, tt, d), lambda i, j: (i, j, 0)),
            pl.BlockSpec((bb, HG_HEADS, HG_KDIM, d // HG_HEADS), lambda i, j: (i, 0, 0, 0)),
            _const_spec((1, d)),
            _const_spec(w_in.shape),
            _const_spec((1, fdim)),
            _const_spec((1, d // HG_HEADS)),
            _const_spec(w_out.shape),
        ],
        out_specs=[
            pl.BlockSpec((bb, tt, d), lambda i, j: (i, j, 0)),
            pl.BlockSpec((bb, HG_HEADS, HG_KDIM, d // HG_HEADS), lambda i, j: (i, 0, 0, 0)),
        ],
        out_shape=[jax.ShapeDtypeStruct(x.shape, F32), jax.ShapeDtypeStruct(s0.shape, F32)],
        scratch_shapes=[
            pltpu.VMEM((rows, fdim), F32),
            pltpu.VMEM((rows, fdim), F32),
            pltpu.VMEM((rows, fdim), F32),
            pltpu.VMEM((rows, d), F32),
            pltpu.VMEM((rows, d), F32),
            pltpu.VMEM((rows, d), BF16),
            pltpu.VMEM((bb * HG_HEADS, d // HG_HEADS, HG_KDIM), F32),
            pltpu.VMEM((rows, d), F32),
        ],
        compiler_params=_cparams(2),
        name="hgrn_mixer",
    )(x, s0, norm_g.reshape(1, d), w_in, lb.reshape(1, fdim), hn.reshape(1, -1), w_out)


CONV_HEAD = 8


def _ffn_kernel(x_ref, cs_ref, g_ref, win_ref, cw_ref, cb_ref, wdn_ref, fg_ref, o_ref, ns_ref,
                a_s, *, bb, tt, final_norm):
    j = pl.program_id(1)
    rows = bb * tt
    d = x_ref.shape[-1]
    dff = cw_ref.shape[-1]
    hist = CONV_W - 1

    @pl.when(j == 0)
    def _():
        a_s[:, CONV_HEAD - hist:CONV_HEAD, :] = cs_ref[...]

    x = x_ref[...].reshape(rows, d)
    h = _rms(x, g_ref[...]).astype(BF16)
    a = _dot(h, win_ref[:, 0:dff])
    u = _dot(h, win_ref[:, dff:])
    a_s[:, CONV_HEAD:CONV_HEAD + tt, :] = a.reshape(bb, tt, dff)
    c = jnp.broadcast_to(cb_ref[...].reshape(1, 1, dff), (bb, tt, dff))
    for w in range(CONV_W):
        lo = CONV_HEAD - hist + w
        c = c + a_s[:, lo:lo + tt, :] * cw_ref[w:w + 1, :].reshape(1, 1, dff)
    new_state = a_s[:, CONV_HEAD + tt - hist:CONV_HEAD + tt, :]
    a_s[:, CONV_HEAD - hist:CONV_HEAD, :] = new_state
    ns_ref[...] = new_state
    act = (_silu(c).reshape(rows, dff) * u).astype(BF16)
    y = _dot(act, wdn_ref[...]) + x
    if final_norm:
        y = _rms(y, fg_ref[...])
    o_ref[...] = y.reshape(bb, tt, d)


def _conv_ffn(x, conv_state, norm_g, w_in, conv_w, conv_b, w_down, final_g, *, bb, tt, final_norm):
    b, t, d = x.shape
    dff = conv_w.shape[-1]
    kern = functools.partial(_ffn_kernel, bb=bb, tt=tt, final_norm=final_norm)
    return pl.pallas_call(
        kern,
        grid=(b // bb, t // tt),
        in_specs=[
            pl.BlockSpec((bb, tt, d), lambda i, j: (i, j, 0)),
            pl.BlockSpec((bb, CONV_W - 1, dff), lambda i, j: (i, 0, 0)),
            _const_spec((1, d)),
            _const_spec(w_in.shape),
            _const_spec(conv_w.shape),
            _const_spec((1, dff)),
            _const_spec(w_down.shape),
            _const_spec((1, d)),
        ],
        out_specs=[
            pl.BlockSpec((bb, tt, d), lambda i, j: (i, j, 0)),
            pl.BlockSpec((bb, CONV_W - 1, dff), lambda i, j: (i, 0, 0)),
        ],
        out_shape=[jax.ShapeDtypeStruct(x.shape, F32), jax.ShapeDtypeStruct(conv_state.shape, F32)],
        scratch_shapes=[pltpu.VMEM((bb, CONV_HEAD + tt, dff), F32)],
        compiler_params=_cparams(2),
        name="conv_ffn",
    )(x, conv_state, norm_g.reshape(1, d), w_in, conv_w, conv_b.reshape(1, dff), w_down,
      final_g.reshape(1, d))


def _rope_tables(pos):
    rot = HEAD_DIM // 4
    half = rot // 2
    inv_freq = ROPE_THETA ** (-jnp.arange(half, dtype=F32) / half)
    ang = pos.astype(F32)[:, None] * inv_freq[None, :]
    cos, sin = jnp.cos(ang), jnp.sin(ang)
    t = pos.shape[0]
    pad = HEAD_DIM - rot
    one = jnp.ones((t, pad), F32)
    zero_h = jnp.zeros((t, half), F32)
    zero_p = jnp.zeros((t, pad), F32)
    c64 = jnp.concatenate([cos, cos, one], axis=1)
    s1_64 = jnp.concatenate([zero_h, sin, zero_p], axis=1)
    s2_64 = jnp.concatenate([-sin, zero_h, zero_p], axis=1)
    ident = jnp.ones((t, HEAD_DIM), F32)
    zero64 = jnp.zeros((t, HEAD_DIM), F32)
    return jnp.stack([
        jnp.concatenate([c64, c64], axis=1), jnp.concatenate([s1_64, s1_64], axis=1),
        jnp.concatenate([s2_64, s2_64], axis=1),
        jnp.concatenate([c64, ident], axis=1), jnp.concatenate([s1_64, zero64], axis=1),
        jnp.concatenate([s2_64, zero64], axis=1)])


def _dsa_proj_kernel(x_ref, g_ref, w_ref, tab_ref, lnw_ref, lnb_ref,
                     q_ref, k_ref, v_ref, qi_ref, kw_ref, *, bb, tt):
    rows = bb * tt
    d = x_ref.shape[-1]
    half = HEAD_DIM // 8
    o1 = N_HEADS * HEAD_DIM
    o2 = o1 + N_KV_HEADS * HEAD_DIM
    o3 = o2 + N_KV_HEADS * HEAD_DIM
    o4 = o3 + IDX_HEADS * IDX_DIM

    def tab(i):
        t = tab_ref[i]
        return jnp.broadcast_to(t[None], (bb, tt, LANES)).reshape(rows, LANES)

    def rope(y, base):
        cos, s1, s2 = tab(base), tab(base + 1), tab(base + 2)
        tiles = []
        for m in range(y.shape[1] // LANES):
            yt = y[:, m * LANES:(m + 1) * LANES]
            tiles.append(yt * cos + pltpu.roll(yt, half, 1) * s1 + pltpu.roll(yt, LANES - half, 1) * s2)
        return tiles[0] if len(tiles) == 1 else jnp.concatenate(tiles, axis=1)

    x = x_ref[...].reshape(rows, d)
    h = _rms(x, g_ref[...]).astype(BF16)
    t = _dot(h, w_ref[:, o4:o4 + LANES])
    lane = lax.broadcasted_iota(I32, (1, LANES), 1)
    is_ki = lane < IDX_DIM
    mu = jnp.sum(jnp.where(is_ki, t, 0.0), axis=-1, keepdims=True) / IDX_DIM
    cen = jnp.where(is_ki, t - mu, 0.0)
    var = jnp.sum(cen * cen, axis=-1, keepdims=True) / IDX_DIM
    ki = rope(cen * lax.rsqrt(var + EPS) * lnw_ref[...] + lnb_ref[...], 3)
    wi = t * ((IDX_HEADS * IDX_DIM) ** -0.5)
    kw = jnp.where(is_ki, ki, jnp.where(lane < IDX_DIM + IDX_HEADS, wi, 0.0))
    kw_ref[...] = kw.reshape(bb, tt, LANES)
    qi_ref[...] = rope(_dot(h, w_ref[:, o3:o4]), 0).astype(BF16).reshape(bb, tt, o4 - o3)
    k_ref[...] = rope(_dot(h, w_ref[:, o1:o2]), 0).reshape(bb, tt, o2 - o1)
    q = rope(_dot(h, w_ref[:, 0:o1]), 0) * (HEAD_DIM ** -0.5 * LOG2_E)
    q_ref[...] = q.astype(BF16).reshape(bb, tt, o1)
    v_ref[...] = _dot(h, w_ref[:, o2:o3]).reshape(bb, tt, o3 - o2)


def _dsa_proj(x, norm_g, w_pad, tables, ln_w, ln_b, *, bb, tt):
    b, t, d = x.shape
    n_q = N_HEADS * HEAD_DIM
    n_kv = N_KV_HEADS * HEAD_DIM
    n_qi = IDX_HEADS * IDX_DIM
    pad = jnp.zeros((LANES - IDX_DIM,), F32)
    lnw = jnp.concatenate([ln_w, pad]).reshape(1, LANES)
    lnb = jnp.concatenate([ln_b, pad]).reshape(1, LANES)
    kern = functools.partial(_dsa_proj_kernel, bb=bb, tt=tt)

    def row_spec(n):
        return pl.BlockSpec((bb, tt, n), lambda i, j: (i, j, 0))

    return pl.pallas_call(
        kern,
        grid=(b // bb, t // tt),
        in_specs=[
            row_spec(d),
            _const_spec((1, d)),
            _const_spec(w_pad.shape),
            pl.BlockSpec((6, tt, LANES), lambda i, j: (0, j, 0)),
            _const_spec((1, LANES)),
            _const_spec((1, LANES)),
        ],
        out_specs=[row_spec(n_q), row_spec(n_kv), row_spec(n_kv), row_spec(n_qi), row_spec(LANES)],
        out_shape=[
            jax.ShapeDtypeStruct((b, t, n_q), BF16),
            jax.ShapeDtypeStruct((b, t, n_kv), F32),
            jax.ShapeDtypeStruct((b, t, n_kv), F32),
            jax.ShapeDtypeStruct((b, t, n_qi), BF16),
            jax.ShapeDtypeStruct((b, t, LANES), F32),
        ],
        compiler_params=_cparams(2),
        name="dsa_proj",
    )(x, norm_g.reshape(1, d), w_pad, tables, lnw, lnb)


def _order_key(score):
    score = jnp.where(score == 0.0, 0.0, score)
    bits = lax.bitcast_convert_type(score, I32)
    return bits ^ ((bits >> 31) & 0x7FFFFFFF)


def _attn_prompt_kernel(q_ref, qi_ref, wq_ref, k_ref, v_ref, kw_ref, x_ref, wout_ref, o_ref,
                        ke, ko, vt, kie, kio, qs, key_s, bias_s, ot_s, half_s, *, tq, t_len, kc, n_sel):
    j = pl.program_id(1)
    nct = t_len // kc
    sub_per_chunk = kc // LANES
    lane = lax.broadcasted_iota(I32, (1, LANES), 1)
    lo = lane < HEAD_DIM

    @pl.when(j == 0)
    def _():
        for c in range(nct):
            rows = slice(c * kc, (c + 1) * kc)
            k_c = k_ref[0, rows, :]
            for m in range(N_KV_HEADS // 2):
                tile = k_c[:, m * LANES:(m + 1) * LANES]
                rolled = pltpu.roll(tile, HEAD_DIM, 1)
                ke[(2 * m) * nct + c] = jnp.where(lo, tile, 0.0).astype(BF16)
                ko[(2 * m) * nct + c] = jnp.where(lo, 0.0, rolled).astype(BF16)
                ke[(2 * m + 1) * nct + c] = jnp.where(lo, rolled, 0.0).astype(BF16)
                ko[(2 * m + 1) * nct + c] = jnp.where(lo, 0.0, tile).astype(BF16)
            v_t = v_ref[0, rows, :].T
            for g in range(N_KV_HEADS):
                vt[g * nct + c, 0:HEAD_DIM, :] = v_t[g * HEAD_DIM:(g + 1) * HEAD_DIM, :].astype(BF16)
                vt[g * nct + c, HEAD_DIM:HEAD_DIM + ONES_ROWS, :] = jnp.ones((ONES_ROWS, kc), BF16)
            kw_c = kw_ref[0, rows, :]
            kie[c] = jnp.where(lo, kw_c, 0.0).astype(BF16)
            kio[c] = jnp.where(lo, 0.0, pltpu.roll(kw_c, HEAD_DIM, 1)).astype(BF16)

    for m in range(N_HEADS // 2):
        qs[m] = q_ref[0, :, m * LANES:(m + 1) * LANES]

    n_chunks = ((j + 1) * tq + kc - 1) // kc
    w_t = wq_ref[0].T
    w_rows = [w_t[IDX_DIM + hh:IDX_DIM + hh + 1, :] for hh in range(IDX_HEADS)]
    qi = qi_ref[0]
    q_chunk = (j * tq + lax.broadcasted_iota(I32, (1, tq), 1)) // CHUNK

    def score_body(c, carry):
        kie_c, kio_c = kie[c], kio[c]
        sc = jnp.zeros((kc, tq), F32)
        for m in range(IDX_HEADS // 2):
            qt = qi[:, m * LANES:(m + 1) * LANES]
            sc = (sc + w_rows[2 * m] * jnp.maximum(_dot_nt(kie_c, qt), 0.0)
                  + w_rows[2 * m + 1] * jnp.maximum(_dot_nt(kio_c, qt), 0.0))
        k_chunk = (c * kc + lax.broadcasted_iota(I32, (kc, 1), 0)) // CHUNK
        keys = jnp.where(k_chunk <= q_chunk, _order_key(sc), KEY_NEG_INF)
        key_s[c] = keys
        half_s[c] = lax.shift_right_arithmetic(keys, 16).astype(I16)
        return carry

    lax.fori_loop(0, n_chunks, score_body, 0)

    def search16(need):
        def count_ge(cand):
            def body(c, acc):
                hit = (half_s[c] >= cand).astype(I16)
                for r in range(0, kc, 16):
                    acc = acc + hit[r:r + 16, :]
                return acc
            acc = lax.fori_loop(0, n_chunks, body, jnp.zeros((16, tq), I16))
            return jnp.sum(acc.astype(I32), axis=0, keepdims=True)

        def bit_step(i, t16):
            cand = t16 ^ lax.shift_left(jnp.int32(1), 15 - i)
            cand = lax.shift_right_arithmetic(lax.shift_left(cand, 16), 16)
            return jnp.where(count_ge(cand.astype(I16)) >= need, cand, t16)

        return lax.fori_loop(0, 16, bit_step, jnp.full((1, tq), -32768, I32))

    thr_hi = search16(n_sel)

    def low_body(c, acc):
        keys = key_s[c]
        hi = lax.shift_right_arithmetic(keys, 16)
        low = (keys & 0xFFFF) - 32768
        half_s[c] = jnp.where(hi == thr_hi, low, -32768).astype(I16)
        return acc + jnp.sum((hi > thr_hi).astype(I32).reshape(kc // 8, 8, tq), axis=0)

    n_above = jnp.sum(lax.fori_loop(0, n_chunks, low_body, jnp.zeros((8, tq), I32)), axis=0, keepdims=True)
    thr_lo = search16(n_sel - n_above)
    thr = lax.shift_left(thr_hi, 16) | (thr_lo + 32768)

    def bias_body(c, acc):
        keys = key_s[c]
        ge = keys >= thr
        bias_s[c] = jnp.where(ge & (keys > KEY_NEG_INF), 0.0, -jnp.inf)
        fold = lambda hit: jnp.sum(hit.astype(I32).reshape(kc // 8, 8, tq), axis=0)
        return acc[0] + fold(ge), acc[1] + fold(keys > thr)

    zero8 = jnp.zeros((8, tq), I32)
    n_ge, n_gt = [jnp.sum(a, axis=0, keepdims=True)
                  for a in lax.fori_loop(0, n_chunks, bias_body, (zero8, zero8))]

    has_excess = jnp.max(jnp.where(thr > KEY_NEG_INF, n_ge - n_sel, 0)) > 0

    @pl.when(has_excess)
    def _():
        wanted = (n_sel - n_gt).astype(F32)
        r_i = lax.broadcasted_iota(I32, (LANES, LANES), 0)
        c_i = lax.broadcasted_iota(I32, (LANES, LANES), 1)
        lower = jnp.where(c_i <= r_i, 1.0, 0.0).astype(BF16)

        def tie_body(u, seen):
            c = u // sub_per_chunk
            r0 = pl.multiple_of((u % sub_per_chunk) * LANES, LANES)
            keys = key_s[c, pl.ds(r0, LANES), :]
            tie = keys == thr
            tie_f = jnp.where(tie, 1.0, 0.0)
            rank = seen + _dot(lower, tie_f.astype(BF16))
            sel = ((keys > thr) | (tie & (rank <= wanted))) & (keys > KEY_NEG_INF)
            bias_s[c, pl.ds(r0, LANES), :] = jnp.where(sel, 0.0, -jnp.inf)
            return seen + jnp.sum(tie_f, axis=0, keepdims=True)

        lax.fori_loop(0, n_chunks * sub_per_chunk, tie_body, jnp.zeros((1, tq), F32))

    n_grp = N_HEADS // N_KV_HEADS

    def group_body(g, carry):
        q_tiles = [qs[g * (n_grp // 2) + i] for i in range(n_grp // 2)]

        def chunk_body(c, st):
            bias = bias_s[c]
            v_c = vt[g * nct + c]
            k_e, k_o = ke[g * nct + c], ko[g * nct + c]
            logits = [_dot_nt(k_x, qt) for qt in q_tiles for k_x in (k_e, k_o)]
            tails = [lax.bitcast_convert_type(l[kc - 8:kc, :], I32) for l in logits[1:]]
            anchor = functools.reduce(jnp.bitwise_or, tails) & lax.shift_right_arithmetic(c, 31)
            anchor = lax.bitcast_convert_type(anchor, F32)[0:1, :]
            out = []
            for h in range(n_grp):
                m_run, acc = st[2 * h:2 * h + 2]
                if h == 0:
                    m_run = m_run + anchor
                logit = logits[h] + bias
                m_new = jnp.maximum(m_run, jnp.max(logit, axis=0, keepdims=True))
                m_safe = jnp.where(m_new > -jnp.inf, m_new, 0.0)
                p = jnp.exp2(logit - m_safe)
                alpha = jnp.exp2(m_run - m_safe)
                out += [m_new, alpha * acc + _dot(v_c, p.astype(BF16))]
            return tuple(out)

        init = (jnp.full((1, tq), -jnp.inf, F32), jnp.zeros((HEAD_DIM + ONES_ROWS, tq), F32))
        st = lax.fori_loop(0, n_chunks, chunk_body, init * n_grp)
        for h in range(n_grp):
            r0 = pl.multiple_of(g * (n_grp * HEAD_DIM) + h * HEAD_DIM, HEAD_DIM)
            acc = st[2 * h + 1]
            ot_s[pl.ds(r0, HEAD_DIM), :] = acc[0:HEAD_DIM, :] / acc[HEAD_DIM:HEAD_DIM + 1, :]
        return carry

    lax.fori_loop(0, N_KV_HEADS, group_body, 0)

    o = jnp.concatenate([ot_s[m * LANES:(m + 1) * LANES, :].T for m in range(N_HEADS // 2)], axis=1)
    o_ref[0] = _dot(o.astype(BF16), wout_ref[...]) + x_ref[0]


def _dsa_attn_prompt(q, qi, kw, k, v, x, w_out, *, tq, kc):
    b, t, d = x.shape
    n_sel = min(TOPK_MAX, t // 4)
    n_kv = N_KV_HEADS * HEAD_DIM
    nct = t // kc
    kern = functools.partial(_attn_prompt_kernel, tq=tq, t_len=t, kc=kc, n_sel=n_sel)

    def tile_spec(n):
        return pl.BlockSpec((1, tq, n), lambda i, j: (i, j, 0))

    def full_spec(n):
        return pl.BlockSpec((1, t, n), lambda i, j: (i, 0, 0))

    return pl.pallas_call(
        kern,
        grid=(b, t // tq),
        in_specs=[tile_spec(q.shape[-1]), tile_spec(qi.shape[-1]), tile_spec(LANES),
                  full_spec(n_kv), full_spec(n_kv), full_spec(LANES),
                  tile_spec(d), _const_spec(w_out.shape)],
        out_specs=tile_spec(d),
        out_shape=jax.ShapeDtypeStruct(x.shape, F32),
        scratch_shapes=[
            pltpu.VMEM((N_KV_HEADS * nct, kc, LANES), BF16), pltpu.VMEM((N_KV_HEADS * nct, kc, LANES), BF16),
            pltpu.VMEM((N_KV_HEADS * nct, HEAD_DIM + ONES_ROWS, kc), BF16),
            pltpu.VMEM((nct, kc, LANES), BF16), pltpu.VMEM((nct, kc, LANES), BF16),
            pltpu.VMEM((N_HEADS // 2, tq, LANES), BF16),
            pltpu.VMEM((nct, kc, tq), I32), pltpu.VMEM((nct, kc, tq), F32),
            pltpu.VMEM((N_HEADS * HEAD_DIM, tq), F32),
            pltpu.VMEM((nct, kc, tq), I16),
        ],
        compiler_params=_cparams(2),
        name="dsa_attn_prompt",
    )(q, qi, kw, k, v, kw, x, w_out)


def _attn_sample_kernel(q_ref, qi_ref, kw_ref, k_ref, v_ref, ckt_ref, cvt_ref, ckit_ref, x_ref, wout_ref, o_ref,
                        kt16, vt16, qg_s, kn_s, vn_s, og_s, *, tq, past, n_sel):
    n_grp = N_HEADS // N_KV_HEADS
    lane = lax.broadcasted_iota(I32, (1, LANES), 1)
    lo = lane < HEAD_DIM

    def lower_half(x128, upper):
        return jnp.where(lo, pltpu.roll(x128, HEAD_DIM, 1) if upper else x128, 0.0)

    kt16[...] = ckt_ref[0].astype(BF16)
    vt16[...] = cvt_ref[0].astype(BF16)
    kit16 = ckit_ref[0].astype(BF16)
    kw = kw_ref[0]
    kw16 = kw.astype(BF16)
    k_new, v_new = k_ref[0], v_ref[0]
    q32 = q_ref[0].astype(F32)
    for g in range(N_KV_HEADS):
        tile = slice((g // 2) * LANES, (g // 2 + 1) * LANES)
        kn_s[g] = lower_half(k_new[:, tile], g % 2 == 1).astype(BF16)
        vn_s[g] = lower_half(v_new[:, tile], g % 2 == 1).astype(BF16)
        heads = [n_grp * g + h for h in range(n_grp)]
        qg_s[g] = jnp.concatenate(
            [lower_half(q32[:, (hd // 2) * LANES:(hd // 2 + 1) * LANES], hd % 2 == 1) for hd in heads],
            axis=0).astype(BF16)

    qi32 = qi_ref[0].astype(F32)
    qi_rows = jnp.concatenate(
        [lower_half(qi32[:, (hh // 2) * LANES:(hh // 2 + 1) * LANES], hh % 2 == 1) for hh in range(IDX_HEADS)],
        axis=0).astype(BF16)
    d_c = _dot(qi_rows[:, 0:IDX_DIM], kit16)
    d_n = _dot_nt(qi_rows, kw16)
    sc_c = jnp.zeros((tq, past), F32)
    sc_n = jnp.zeros((tq, tq), F32)
    for hh in range(IDX_HEADS):
        w_col = kw[:, IDX_DIM + hh:IDX_DIM + hh + 1]
        rows = slice(hh * tq, (hh + 1) * tq)
        sc_c = sc_c + w_col * jnp.maximum(d_c[rows, :], 0.0)
        sc_n = sc_n + w_col * jnp.maximum(d_n[rows, :], 0.0)
    q_chunk = (past + lax.broadcasted_iota(I32, (tq, 1), 0)) // CHUNK
    k_chunk = (past + lax.broadcasted_iota(I32, (1, tq), 1)) // CHUNK
    key_c = _order_key(sc_c)
    key_n = jnp.where(k_chunk <= q_chunk, _order_key(sc_n), KEY_NEG_INF)

    def count(pred):
        return (jnp.sum(pred(key_c).astype(I32), axis=-1, keepdims=True)
                + jnp.sum(pred(key_n).astype(I32), axis=-1, keepdims=True))

    def bit_step(i, thr):
        cand = thr ^ lax.shift_left(jnp.int32(1), 31 - i)
        return jnp.where(count(lambda k: k >= cand) >= n_sel, cand, thr)

    thr = lax.fori_loop(0, 32, bit_step, jnp.full((tq, 1), INT_MIN, I32))
    wanted = (n_sel - count(lambda k: k > thr)).astype(F32)

    def select(keys, seen):
        tie = keys == thr
        tie_f = jnp.where(tie, 1.0, 0.0)
        ranks = []
        for c0 in range(0, keys.shape[1], LANES):
            cw = min(LANES, keys.shape[1] - c0)
            r_i = lax.broadcasted_iota(I32, (cw, cw), 0)
            c_i = lax.broadcasted_iota(I32, (cw, cw), 1)
            upper = jnp.where(r_i <= c_i, 1.0, 0.0).astype(BF16)
            part = tie_f[:, c0:c0 + cw]
            ranks.append(seen + _dot(part.astype(BF16), upper))
            seen = seen + jnp.sum(part, axis=-1, keepdims=True)
        rank = ranks[0] if len(ranks) == 1 else jnp.concatenate(ranks, axis=1)
        sel = ((keys > thr) | (tie & (rank <= wanted))) & (keys > KEY_NEG_INF)
        return jnp.where(sel, 0.0, -jnp.inf), seen

    bias_c, seen = select(key_c, jnp.zeros((tq, 1), F32))
    bias_n, _ = select(key_n, seen)
    bias_c = jnp.concatenate([bias_c] * n_grp, axis=0)
    bias_n = jnp.concatenate([bias_n] * n_grp, axis=0)

    groups = range(N_KV_HEADS)
    rows = [slice(g * HEAD_DIM, (g + 1) * HEAD_DIM) for g in groups]
    l_c = [_dot(qg_s[g][:, 0:HEAD_DIM], kt16[rows[g], :]) for g in groups]
    l_n = [_dot_nt(qg_s[g], kn_s[g]) for g in groups]
    tail = lax.bitcast_convert_type(l_c[-1][:, past - LANES:past], I32) & lax.shift_right_arithmetic(pl.program_id(0), 31)
    l_n[0] = l_n[0] + lax.bitcast_convert_type(tail, F32)[:, 0:tq]
    for g in groups:
        lc, ln = l_c[g] + bias_c, l_n[g] + bias_n
        mx = jnp.maximum(jnp.max(lc, axis=-1, keepdims=True), jnp.max(ln, axis=-1, keepdims=True))
        p_c = jnp.exp2(lc - mx)
        p_n = jnp.exp2(ln - mx)
        den = jnp.sum(p_c, axis=-1, keepdims=True) + jnp.sum(p_n, axis=-1, keepdims=True)
        pv = (_dot_nt(p_c.astype(BF16), vt16[rows[g], :])
              + _dot(p_n.astype(BF16), vn_s[g])[:, 0:HEAD_DIM])
        og_s[g] = pv / den

    tiles = []
    for m in range(N_HEADS // 2):
        pair = []
        for hd in (2 * m, 2 * m + 1):
            h = hd % n_grp
            pair.append(og_s[hd // n_grp, h * tq:(h + 1) * tq, :])
        tiles.append(jnp.concatenate(pair, axis=1))
    o = jnp.concatenate(tiles, axis=1).astype(BF16)
    o_ref[0] = _dot(o, wout_ref[...]) + x_ref[0]


def _dsa_attn_sample(q, qi, kw, k, v, cache_kt, cache_vt, cache_kit, x, w_out):
    b, t, d = x.shape
    past = cache_kt.shape[-1]
    n_sel = min(TOPK_MAX, (past + t) // 4)
    n_kv = N_KV_HEADS * HEAD_DIM
    n_grp = N_HEADS // N_KV_HEADS
    kern = functools.partial(_attn_sample_kernel, tq=t, past=past, n_sel=n_sel)

    def spec(rows, n):
        return pl.BlockSpec((1, rows, n), lambda i: (i, 0, 0))

    return pl.pallas_call(
        kern,
        grid=(b,),
        in_specs=[spec(t, q.shape[-1]), spec(t, qi.shape[-1]), spec(t, LANES), spec(t, n_kv), spec(t, n_kv),
                  spec(n_kv, past), spec(n_kv, past), spec(IDX_DIM, past),
                  spec(t, d), _const_spec(w_out.shape)],
        out_specs=spec(t, d),
        out_shape=jax.ShapeDtypeStruct(x.shape, F32),
        scratch_shapes=[
            pltpu.VMEM((n_kv, past), BF16), pltpu.VMEM((n_kv, past), BF16),
            pltpu.VMEM((N_KV_HEADS, n_grp * t, LANES), BF16),
            pltpu.VMEM((N_KV_HEADS, t, LANES), BF16), pltpu.VMEM((N_KV_HEADS, t, LANES), BF16),
            pltpu.VMEM((N_KV_HEADS, n_grp * t, HEAD_DIM), F32),
        ],
        compiler_params=_cparams(1),
        name="dsa_attn_sample",
    )(q, qi, kw, k, v, cache_kt, cache_vt, cache_kit, x, w_out)


def _tiling(b, t):
    if t >= 256:
        return dict(bb=1, tt=256, blk=128, sub=32)
    bb = max(1, min(b, 128 // t))
    return dict(bb=bb, tt=t, blk=t, sub=t)


def _trunk(x, hg_state, conv_state, cache, p):
    b, t, d = x.shape
    tl = _tiling(b, t)
    bb, tt = tl["bb"], tl["tt"]
    x, s_new = _hgrn_mixer(x, hg_state[0], p["norm_mix"][0], p["hg_w_in"][0], p["lower_bounds"][0],
                           p["hg_norm"][0], p["hg_w_out"][0], **dict(tl, tt=512 if t % 512 == 0 else tt))
    x, cs0 = _conv_ffn(x, conv_state[0], p["norm_ffn"][0], p["ffn_w_in"][0], p["ffn_conv_w"][0],
                       p["ffn_conv_b"][0], p["ffn_w_down"][0], p["norm_final"], bb=bb, tt=tt, final_norm=False)
    past = 0 if cache is None else cache[0].shape[1]
    tables = _rope_tables(past + jnp.arange(t))
    q, k, v, qi, kw = _dsa_proj(x, p["norm_mix"][1], p["ds_w_in"][0], tables, p["ds_kln_w"][0],
                                p["ds_kln_b"][0], bb=bb, tt=tt)
    if cache is None:
        x = _dsa_attn_prompt(q, qi, kw, k, v, x, p["ds_w_out"][0], tq=min(t, 512), kc=min(t, 512))
    else:
        ck, cv, cki = cache
        def keys_last(c):
            return jnp.swapaxes(c.reshape(b, past, -1), 1, 2)

        x = _dsa_attn_sample(q, qi, kw, k, v, keys_last(ck), keys_last(cv), keys_last(cki), x, p["ds_w_out"][0])
    x, cs1 = _conv_ffn(x, conv_state[1], p["norm_ffn"][1], p["ffn_w_in"][1], p["ffn_conv_w"][1],
                       p["ffn_conv_b"][1], p["ffn_w_down"][1], p["norm_final"], bb=bb, tt=tt, final_norm=True)
    k = k.reshape(1, b, t, N_KV_HEADS, HEAD_DIM)
    v = v.reshape(1, b, t, N_KV_HEADS, HEAD_DIM)
    ki = kw[:, :, :IDX_DIM].reshape(1, b, t, IDX_DIM)
    return x, s_new[None], jnp.stack([cs0, cs1]), k, v, ki


def kernel(x_prompt, x_sample, cache_k, cache_v, cache_kidx, state_hgrn, state_conv, norm_mix, norm_ffn, norm_final, hg_w_in, hg_lb, hg_norm, hg_w_out, ds_w_in, ds_kln_w, ds_kln_b, ds_w_out, ffn_w_in, ffn_conv_w, ffn_conv_b, ffn_w_down):
    dsa_in = ds_w_in.shape[-1]
    dsa_pad = (-dsa_in) % LANES
    p = dict(
        norm_mix=norm_mix, norm_ffn=norm_ffn, norm_final=norm_final,
        hg_w_in=hg_w_in.astype(BF16), hg_norm=hg_norm, hg_w_out=hg_w_out.astype(BF16),
        lower_bounds=jnp.cumsum(jax.nn.softmax(hg_lb.astype(F32), axis=0), axis=0),
        ds_w_in=jnp.pad(ds_w_in, ((0, 0), (0, 0), (0, dsa_pad))).astype(BF16),
        ds_kln_w=ds_kln_w, ds_kln_b=ds_kln_b, ds_w_out=ds_w_out.astype(BF16),
        ffn_w_in=ffn_w_in.astype(BF16), ffn_conv_w=ffn_conv_w, ffn_conv_b=ffn_conv_b,
        ffn_w_down=ffn_w_down.astype(BF16),
    )
    b = x_prompt.shape[0]
    hg0 = jnp.zeros((state_hgrn.shape[0], b) + state_hgrn.shape[2:], F32)
    conv0 = jnp.zeros((state_conv.shape[0], b) + state_conv.shape[2:], F32)
    y_p, hg_p, conv_p, k_p, v_p, ki_p = _trunk(x_prompt, hg0, conv0, None, p)
    y_s, hg_s, conv_s, k_s, v_s, ki_s = _trunk(x_sample, state_hgrn, state_conv,
                                               (cache_k[0], cache_v[0], cache_kidx[0]), p)
    return (y_p, y_s, k_p, v_p, ki_p, hg_p, conv_p, k_s, v_s, ki_s, hg_s, conv_s)
```

```python
import functools

import jax
import jax.numpy as jnp
from jax import lax
from jax.experimental import pallas as pl
from jax.experimental.pallas import tpu as pltpu

F32 = jnp.float32
BF16 = jnp.bfloat16
I32 = jnp.int32
I16 = jnp.int16

CHUNK = 64
EPS = 1e-6
HG_HEADS = 8
HG_KDIM = 128
N_HEADS = 16
HEAD_DIM = 64
N_KV_HEADS = 4
IDX_HEADS = 8
IDX_DIM = 64
TOPK_MAX = 256
ROPE_THETA = 500000.0
CONV_W = 3

LANES = 128
VMEM_LIMIT = 56 * 1024 * 1024
EXP_CLAMP = 80.0
LOG2_E = 1.4426950408889634
ONES_ROWS = 16
BOUND_SLACK = 1.02
MIN_DENOMINATOR = 2.0 ** -60
KEY_NEG_INF = -2139095041
INT_MIN = -2147483648


def _cparams(n_axes, flags=None):
    return pltpu.CompilerParams(dimension_semantics=("arbitrary",) * n_axes,
                                vmem_limit_bytes=VMEM_LIMIT, flags=flags)


def _const_spec(shape):
    nd = len(shape)
    return pl.BlockSpec(shape, lambda *_: (0,) * nd, pipeline_mode=pl.Buffered(1))


def _rms(x, g):
    return x * lax.rsqrt(jnp.mean(x * x, axis=-1, keepdims=True) + EPS) * g


def _silu(x):
    return x * jax.nn.sigmoid(x)


def _dot(a, b):
    return jnp.dot(a, b, preferred_element_type=F32)


def _dot_nt(a, b):
    return lax.dot_general(a, b, (((1,), (1,)), ((), ())), preferred_element_type=F32)


def _dot_tn(a, b):
    return lax.dot_general(a, b, (((0,), (0,)), ((), ())), preferred_element_type=F32)


def _hgrn_kernel(x_ref, s0_ref, g_ref, win_ref, lb_ref, hn_ref, wout_ref, o_ref, s_ref,
                 q_s, k_s, g_s, v_s, gate_s, on_s, st_s, o_s, *, bb, tt, blk, sub):
    j = pl.program_id(1)
    rows = bb * tt
    d = x_ref.shape[-1]
    fdim = HG_HEADS * HG_KDIM
    n_sub = blk // sub

    @pl.when(j == 0)
    def _():
        for bi in range(bb):
            for hh in range(HG_HEADS):
                st_s[bi * HG_HEADS + hh] = s0_ref[bi, hh].T

    x = x_ref[...].reshape(rows, d)
    h = _rms(x, g_ref[...]).astype(BF16)
    lb = lb_ref[...]
    fg = lb + (1.0 - lb) * jax.nn.sigmoid(_dot(h, win_ref[:, fdim:2 * fdim]))
    k_s[...] = 1.0 - fg
    g_s[...] = jnp.log(fg)
    q_s[...] = _silu(_dot(h, win_ref[:, 0:fdim]))
    gate_s[...] = _silu(_dot(h, win_ref[:, 2 * fdim + d:]))
    v_s[...] = _dot(h, win_ref[:, 2 * fdim:2 * fdim + d])

    r_i = lax.broadcasted_iota(I32, (blk, blk), 0)
    c_i = lax.broadcasted_iota(I32, (blk, blk), 1)
    causal = c_i <= r_i
    tril = jnp.where(causal, 1.0, 0.0).astype(BF16)
    hn = hn_ref[...]
    blocks_per_stream = tt // blk

    def block_body(idx, carry):
        r0 = pl.multiple_of(idx * blk, blk)
        bi = idx // blocks_per_stream
        lg = g_s[pl.ds(r0, blk), :]
        lg_hi = lg.astype(BF16)
        rem = lg - lg_hi.astype(F32)
        lg_mid = rem.astype(BF16)
        lg_lo = (rem - lg_mid.astype(F32)).astype(BF16)
        g_all = _dot(tril, lg_hi) + _dot(tril, lg_mid) + _dot(tril, lg_lo)
        run_zero = lax.shift_right_arithmetic(idx, 31)

        def anchor(x):
            tail = lax.bitcast_convert_type(x[x.shape[0] - 8:, :], I32) & run_zero
            return lax.bitcast_convert_type(tail, F32)[0:1, :]

        heads = [slice(hh * HG_KDIM, (hh + 1) * HG_KDIM) for hh in range(HG_HEADS)]
        qcat, kcat, qfull, kdec_last, v16, decay = [], [], [], [], [], []
        for sl in heads:
            gc = g_all[:, sl]
            q = q_s[pl.ds(r0, blk), sl]
            kk = k_s[pl.ds(r0, blk), sl]
            refs = [jnp.zeros((1, HG_KDIM), F32)] + [gc[s * sub - 1:s * sub, :] for s in range(1, n_sub)]
            q_parts, k_parts = [], []
            for s in range(n_sub):
                lo_r, hi_r = s * sub, (s + 1) * sub
                qg = q[lo_r:hi_r, :] * jnp.exp(gc[lo_r:hi_r, :] - refs[s])
                pieces = [qg]
                if lo_r:
                    pieces.insert(0, jnp.zeros((lo_r, HG_KDIM), F32))
                if blk - hi_r:
                    pieces.append(jnp.zeros((blk - hi_r, HG_KDIM), F32))
                q_parts.append(pieces[0] if len(pieces) == 1 else jnp.concatenate(pieces, axis=0))
                kd_s = kk[0:hi_r, :] * jnp.exp(jnp.minimum(refs[s] - gc[0:hi_r, :], EXP_CLAMP))
                k_parts.append(kd_s if hi_r == blk else
                               jnp.concatenate([kd_s, jnp.zeros((blk - hi_r, HG_KDIM), F32)], axis=0))
            qcat.append(jnp.concatenate(q_parts, axis=1).astype(BF16))
            kcat.append(jnp.concatenate(k_parts, axis=1).astype(BF16))
            qfull.append((q * jnp.exp(gc)).astype(BF16))
            g_last = gc[blk - 1:blk, :]
            kdec_last.append((kk * jnp.exp(g_last - gc)).astype(BF16))
            decay.append(jnp.exp(g_last))
            v16.append(v_s[pl.ds(r0, blk), sl].astype(BF16))
        a_list = [_dot_nt(qcat[hh], kcat[hh]) for hh in range(HG_HEADS)]
        s_inc = [_dot_tn(v16[hh], kdec_last[hh]) for hh in range(HG_HEADS)]
        a_list[0] = a_list[0] + anchor(a_list[-1])
        o_list = []
        for hh in range(HG_HEADS):
            a = jnp.where(causal, a_list[hh], 0.0).astype(BF16)
            st = st_s[bi * HG_HEADS + hh]
            o_list.append(_dot(a, v16[hh]) + _dot_nt(qfull[hh], st.astype(BF16)))
            st_s[bi * HG_HEADS + hh] = st * decay[hh] + s_inc[hh]
        o_list[0] = o_list[0] + anchor(o_list[-1])
        for hh, sl in enumerate(heads):
            on = _rms(o_list[hh], hn) * gate_s[pl.ds(r0, blk), sl]
            on_s[pl.ds(r0, blk), sl] = on.astype(BF16)
        return carry

    sub_sums = jnp.sum(g_s[...].reshape(rows // sub, sub, fdim), axis=1)
    blocked_ok = jnp.min(sub_sums) >= -EXP_CLAMP

    @pl.when(blocked_ok)
    def _():
        lax.fori_loop(0, rows // blk, block_body, 0, unroll=4)

    @pl.when(jnp.logical_not(blocked_ok))
    def _():
        grp = 16
        o_s[...] = jnp.zeros(o_s.shape, F32)
        in_grp = lax.broadcasted_iota(I32, (grp, 1), 0)

        def frame_body(r, carry):
            r0 = pl.multiple_of((r // grp) * grp, grp)
            bi = r // tt
            this = in_grp == (r % grp)
            for hh in range(HG_HEADS):
                sl = slice(hh * HG_KDIM, (hh + 1) * HG_KDIM)

                def only(ref):
                    return jnp.where(this, ref[pl.ds(r0, grp), sl], 0.0)

                forget = jnp.exp(jnp.sum(only(g_s), axis=0, keepdims=True))
                st = (st_s[bi * HG_HEADS + hh] * forget
                      + _dot_tn(only(v_s).astype(BF16), only(k_s).astype(BF16)))
                st_s[bi * HG_HEADS + hh] = st
                o_s[pl.ds(r0, grp), sl] += _dot_nt(only(q_s).astype(BF16), st.astype(BF16))
            return carry

        lax.fori_loop(0, rows, frame_body, 0)
        for hh in range(HG_HEADS):
            sl = slice(hh * HG_KDIM, (hh + 1) * HG_KDIM)
            on_s[:, sl] = (_rms(o_s[:, sl], hn) * gate_s[:, sl]).astype(BF16)

    out = _dot(on_s[...], wout_ref[...]) + x
    o_ref[...] = out.reshape(bb, tt, d)

    @pl.when(j == pl.num_programs(1) - 1)
    def _():
        for bi in range(bb):
            for hh in range(HG_HEADS):
                s_ref[bi, hh] = st_s[bi * HG_HEADS + hh].T


def _hgrn_mixer(x, s0, norm_g, w_in, lb, hn, w_out, *, bb, tt, blk, sub):
    b, t, d = x.shape
    fdim = HG_HEADS * HG_KDIM
    rows = bb * tt
    kern = functools.partial(_hgrn_kernel, bb=bb, tt=tt, blk=blk, sub=sub)
    return pl.pallas_call(
        kern,
        grid=(b // bb, t // tt),
        in_specs=[
            pl.BlockSpec((bb, tt, d), lambda i, j: (i, j, 0)),
            pl.BlockSpec((bb, HG_HEADS, HG_KDIM, d // HG_HEADS), lambda i, j: (i, 0, 0, 0)),
            _const_spec((1, d)),
            _const_spec(w_in.shape),
            _const_spec((1, fdim)),
            _const_spec((1, d // HG_HEADS)),
            _const_spec(w_out.shape),
        ],
        out_specs=[
            pl.BlockSpec((bb, tt, d), lambda i, j: (i, j, 0)),
            pl.BlockSpec((bb, HG_HEADS, HG_KDIM, d // HG_HEADS), lambda i, j: (i, 0, 0, 0)),
        ],
        out_shape=[jax.ShapeDtypeStruct(x.shape, F32), jax.ShapeDtypeStruct(s0.shape, F32)],
        scratch_shapes=[
            pltpu.VMEM((rows, fdim), F32),
            pltpu.VMEM((rows, fdim), F32),
            pltpu.VMEM((rows, fdim), F32),
            pltpu.VMEM((rows, d), F32),
            pltpu.VMEM((rows, d), F32),
            pltpu.VMEM((rows, d), BF16),
            pltpu.VMEM((bb * HG_HEADS, d // HG_HEADS, HG_KDIM), F32),
            pltpu.VMEM((rows, d), F32),
        ],
        compiler_params=_cparams(2),
        name="hgrn_mixer",
    )(x, s0, norm_g.reshape(1, d), w_in, lb.reshape(1, fdim), hn.reshape(1, -1), w_out)


CONV_HEAD = 8


def _ffn_kernel(x_ref, cs_ref, g_ref, win_ref, cw_ref, cb_ref, wdn_ref, fg_ref, o_ref, ns_ref,
                a_s, *, bb, tt, final_norm):
    j = pl.program_id(1)
    rows = bb * tt
    d = x_ref.shape[-1]
    dff = cw_ref.shape[-1]
    hist = CONV_W - 1

    @pl.when(j == 0)
    def _():
        a_s[:, CONV_HEAD - hist:CONV_HEAD, :] = cs_ref[...]

    x = x_ref[...].reshape(rows, d)
    h = _rms(x, g_ref[...]).astype(BF16)
    a = _dot(h, win_ref[:, 0:dff])
    u = _dot(h, win_ref[:, dff:])
    a_s[:, CONV_HEAD:CONV_HEAD + tt, :] = a.reshape(bb, tt, dff)
    c = jnp.broadcast_to(cb_ref[...].reshape(1, 1, dff), (bb, tt, dff))
    for w in range(CONV_W):
        lo = CONV_HEAD - hist + w
        c = c + a_s[:, lo:lo + tt, :] * cw_ref[w:w + 1, :].reshape(1, 1, dff)
    new_state = a_s[:, CONV_HEAD + tt - hist:CONV_HEAD + tt, :]
    a_s[:, CONV_HEAD - hist:CONV_HEAD, :] = new_state
    ns_ref[...] = new_state
    act = (_silu(c).reshape(rows, dff) * u).astype(BF16)
    y = _dot(act, wdn_ref[...]) + x
    if final_norm:
        y = _rms(y, fg_ref[...])
    o_ref[...] = y.reshape(bb, tt, d)


def _conv_ffn(x, conv_state, norm_g, w_in, conv_w, conv_b, w_down, final_g, *, bb, tt, final_norm):
    b, t, d = x.shape
    dff = conv_w.shape[-1]
    kern = functools.partial(_ffn_kernel, bb=bb, tt=tt, final_norm=final_norm)
    return pl.pallas_call(
        kern,
        grid=(b // bb, t // tt),
        in_specs=[
            pl.BlockSpec((bb, tt, d), lambda i, j: (i, j, 0)),
            pl.BlockSpec((bb, CONV_W - 1, dff), lambda i, j: (i, 0, 0)),
            _const_spec((1, d)),
            _const_spec(w_in.shape),
            _const_spec(conv_w.shape),
            _const_spec((1, dff)),
            _const_spec(w_down.shape),
            _const_spec((1, d)),
        ],
        out_specs=[
            pl.BlockSpec((bb, tt, d), lambda i, j: (i, j, 0)),
            pl.BlockSpec((bb, CONV_W - 1, dff), lambda i, j: (i, 0, 0)),
        ],
        out_shape=[jax.ShapeDtypeStruct(x.shape, F32), jax.ShapeDtypeStruct(conv_state.shape, F32)],
        scratch_shapes=[pltpu.VMEM((bb, CONV_HEAD + tt, dff), F32)],
        compiler_params=_cparams(2),
        name="conv_ffn",
    )(x, conv_state, norm_g.reshape(1, d), w_in, conv_w, conv_b.reshape(1, dff), w_down,
      final_g.reshape(1, d))


def _rope_tables(pos):
    rot = HEAD_DIM // 4
    half = rot // 2
    inv_freq = ROPE_THETA ** (-jnp.arange(half, dtype=F32) / half)
    ang = pos.astype(F32)[:, None] * inv_freq[None, :]
    cos, sin = jnp.cos(ang), jnp.sin(ang)
    t = pos.shape[0]
    pad = HEAD_DIM - rot
    one = jnp.ones((t, pad), F32)
    zero_h = jnp.zeros((t, half), F32)
    zero_p = jnp.zeros((t, pad), F32)
    c64 = jnp.concatenate([cos, cos, one], axis=1)
    s1_64 = jnp.concatenate([zero_h, sin, zero_p], axis=1)
    s2_64 = jnp.concatenate([-sin, zero_h, zero_p], axis=1)
    ident = jnp.ones((t, HEAD_DIM), F32)
    zero64 = jnp.zeros((t, HEAD_DIM), F32)
    return jnp.stack([
        jnp.concatenate([c64, c64], axis=1), jnp.concatenate([s1_64, s1_64], axis=1),
        jnp.concatenate([s2_64, s2_64], axis=1),
        jnp.concatenate([c64, ident], axis=1), jnp.concatenate([s1_64, zero64], axis=1),
        jnp.concatenate([s2_64, zero64], axis=1)])


def _dsa_proj_kernel(x_ref, g_ref, w_ref, tab_ref, lnw_ref, lnb_ref,
                     q_ref, k_ref, v_ref, qi_ref, kw_ref, *, bb, tt):
    rows = bb * tt
    d = x_ref.shape[-1]
    half = HEAD_DIM // 8
    o1 = N_HEADS * HEAD_DIM
    o2 = o1 + N_KV_HEADS * HEAD_DIM
    o3 = o2 + N_KV_HEADS * HEAD_DIM
    o4 = o3 + IDX_HEADS * IDX_DIM

    def tab(i):
        t = tab_ref[i]
        return jnp.broadcast_to(t[None], (bb, tt, LANES)).reshape(rows, LANES)

    def rope(y, base):
        cos, s1, s2 = tab(base), tab(base + 1), tab(base + 2)
        tiles = []
        for m in range(y.shape[1] // LANES):
            yt = y[:, m * LANES:(m + 1) * LANES]
            tiles.append(yt * cos + pltpu.roll(yt, half, 1) * s1 + pltpu.roll(yt, LANES - half, 1) * s2)
        return tiles[0] if len(tiles) == 1 else jnp.concatenate(tiles, axis=1)

    x = x_ref[...].reshape(rows, d)
    h = _rms(x, g_ref[...]).astype(BF16)
    t = _dot(h, w_ref[:, o4:o4 + LANES])
    lane = lax.broadcasted_iota(I32, (1, LANES), 1)
    is_ki = lane < IDX_DIM
    mu = jnp.sum(jnp.where(is_ki, t, 0.0), axis=-1, keepdims=True) / IDX_DIM
    cen = jnp.where(is_ki, t - mu, 0.0)
    var = jnp.sum(cen * cen, axis=-1, keepdims=True) / IDX_DIM
    ki = rope(cen * lax.rsqrt(var + EPS) * lnw_ref[...] + lnb_ref[...], 3)
    wi = t * ((IDX_HEADS * IDX_DIM) ** -0.5)
    kw = jnp.where(is_ki, ki, jnp.where(lane < IDX_DIM + IDX_HEADS, wi, 0.0))
    kw_ref[...] = kw.reshape(bb, tt, LANES)
    qi_ref[...] = rope(_dot(h, w_ref[:, o3:o4]), 0).astype(BF16).reshape(bb, tt, o4 - o3)
    k_ref[...] = rope(_dot(h, w_ref[:, o1:o2]), 0).reshape(bb, tt, o2 - o1)
    q = rope(_dot(h, w_ref[:, 0:o1]), 0) * (HEAD_DIM ** -0.5 * LOG2_E)
    q_ref[...] = q.astype(BF16).reshape(bb, tt, o1)
    v_ref[...] = _dot(h, w_ref[:, o2:o3]).reshape(bb, tt, o3 - o2)


def _dsa_proj(x, norm_g, w_pad, tables, ln_w, ln_b, *, bb, tt):
    b, t, d = x.shape
    n_q = N_HEADS * HEAD_DIM
    n_kv = N_KV_HEADS * HEAD_DIM
    n_qi = IDX_HEADS * IDX_DIM
    pad = jnp.zeros((LANES - IDX_DIM,), F32)
    lnw = jnp.concatenate([ln_w, pad]).reshape(1, LANES)
    lnb = jnp.concatenate([ln_b, pad]).reshape(1, LANES)
    kern = functools.partial(_dsa_proj_kernel, bb=bb, tt=tt)

    def row_spec(n):
        return pl.BlockSpec((bb, tt, n), lambda i, j: (i, j, 0))

    return pl.pallas_call(
        kern,
        grid=(b // bb, t // tt),
        in_specs=[
            row_spec(d),
            _const_spec((1, d)),
            _const_spec(w_pad.shape),
            pl.BlockSpec((6, tt, LANES), lambda i, j: (0, j, 0)),
            _const_spec((1, LANES)),
            _const_spec((1, LANES)),
        ],
        out_specs=[row_spec(n_q), row_spec(n_kv), row_spec(n_kv), row_spec(n_qi), row_spec(LANES)],
        out_shape=[
            jax.ShapeDtypeStruct((b, t, n_q), BF16),
            jax.ShapeDtypeStruct((b, t, n_kv), F32),
            jax.ShapeDtypeStruct((b, t, n_kv), F32),
            jax.ShapeDtypeStruct((b, t, n_qi), BF16),
            jax.ShapeDtypeStruct((b, t, LANES), F32),
        ],
        compiler_params=_cparams(2),
        name="dsa_proj",
    )(x, norm_g.reshape(1, d), w_pad, tables, lnw, lnb)


def _order_key(score):
    score = jnp.where(score == 0.0, 0.0, score)
    bits = lax.bitcast_convert_type(score, I32)
    return bits ^ ((bits >> 31) & 0x7FFFFFFF)


def _attn_prompt_kernel(q_ref, qi_ref, wq_ref, k_ref, v_ref, kw_ref, x_ref, wout_ref, o_ref,
                        ke, ko, vt, kie, kio, qs, key_s, bias_s, ot_s, half_s, kmax_s, *, tq, t_len, kc, n_sel):
    j = pl.program_id(1)
    nct = t_len // kc
    sub_per_chunk = kc // LANES
    lane = lax.broadcasted_iota(I32, (1, LANES), 1)
    lo = lane < HEAD_DIM

    unit_hi = jnp.where(lane == HEAD_DIM, 1.0, 0.0)
    unit_lo = jnp.where(lane == 0, 1.0, 0.0)

    r_i = lax.broadcasted_iota(I32, (LANES, LANES), 0)
    c_i = lax.broadcasted_iota(I32, (LANES, LANES), 1)
    half_sums = jnp.where(((r_i < HEAD_DIM) & (c_i == HEAD_DIM)) | ((r_i >= HEAD_DIM) & (c_i == 0)),
                          1.0, 0.0).astype(BF16)

    @pl.when(j == 0)
    def _():
        k_sq_max = [jnp.zeros((1, LANES), F32) for _ in range(N_KV_HEADS // 2)]
        for c in range(nct):
            rows = slice(c * kc, (c + 1) * kc)
            k_c = k_ref[0, rows, :].astype(BF16).astype(F32)
            for m in range(N_KV_HEADS // 2):
                tile = k_c[:, m * LANES:(m + 1) * LANES]
                rolled = pltpu.roll(tile, HEAD_DIM, 1)
                ke[(2 * m) * nct + c] = jnp.where(lo, tile, unit_hi).astype(BF16)
                ko[(2 * m) * nct + c] = jnp.where(lo, unit_lo, rolled).astype(BF16)
                ke[(2 * m + 1) * nct + c] = jnp.where(lo, rolled, unit_hi).astype(BF16)
                ko[(2 * m + 1) * nct + c] = jnp.where(lo, unit_lo, tile).astype(BF16)
                norm2 = _dot((tile * tile).astype(BF16), half_sums)
                k_sq_max[m] = jnp.maximum(k_sq_max[m], jnp.max(norm2, axis=0, keepdims=True))
            v_t = v_ref[0, rows, :].T
            for g in range(N_KV_HEADS):
                vt[g * nct + c, 0:HEAD_DIM, :] = v_t[g * HEAD_DIM:(g + 1) * HEAD_DIM, :].astype(BF16)
                vt[g * nct + c, HEAD_DIM:HEAD_DIM + ONES_ROWS, :] = jnp.ones((ONES_ROWS, kc), BF16)
            kw_c = kw_ref[0, rows, :]
            kie[c] = jnp.where(lo, kw_c, 0.0).astype(BF16)
            kio[c] = jnp.where(lo, 0.0, pltpu.roll(kw_c, HEAD_DIM, 1)).astype(BF16)
        for g in range(N_KV_HEADS):
            at = HEAD_DIM if g % 2 == 0 else 0
            k_sq = jnp.max(jnp.where(lane == at, k_sq_max[g // 2], 0.0), axis=-1, keepdims=True)
            kmax_s[g] = jnp.broadcast_to(jnp.sqrt(k_sq), (1, LANES))

    for m in range(N_HEADS // 2):
        tile = q_ref[0, :, m * LANES:(m + 1) * LANES].astype(F32)
        k_norm = kmax_s[(2 * m) // (N_HEADS // N_KV_HEADS)]
        bound = jnp.sqrt(_dot((tile * tile).astype(BF16), half_sums)) * (k_norm * -BOUND_SLACK)
        qs[2 * m] = jnp.where(lo, tile, jnp.where(lane == HEAD_DIM, bound, 0.0)).astype(BF16)
        qs[2 * m + 1] = jnp.where(lo, jnp.where(lane == 0, bound, 0.0), tile).astype(BF16)

    n_chunks = ((j + 1) * tq + kc - 1) // kc
    w_t = wq_ref[0].T
    w_rows = [w_t[IDX_DIM + hh:IDX_DIM + hh + 1, :] for hh in range(IDX_HEADS)]
    qi = qi_ref[0]
    q_chunk = (j * tq + lax.broadcasted_iota(I32, (1, tq), 1)) // CHUNK

    def score_body(c, carry):
        kie_c, kio_c = kie[c], kio[c]
        sc = jnp.zeros((kc, tq), F32)
        for m in range(IDX_HEADS // 2):
            qt = qi[:, m * LANES:(m + 1) * LANES]
            sc = (sc + w_rows[2 * m] * jnp.maximum(_dot_nt(kie_c, qt), 0.0)
                  + w_rows[2 * m + 1] * jnp.maximum(_dot_nt(kio_c, qt), 0.0))
        k_chunk = (c * kc + lax.broadcasted_iota(I32, (kc, 1), 0)) // CHUNK
        keys = jnp.where(k_chunk <= q_chunk, _order_key(sc), KEY_NEG_INF)
        key_s[c] = keys
        half_s[c] = lax.shift_right_arithmetic(keys, 16).astype(I16)
        return carry

    lax.fori_loop(0, n_chunks, score_body, 0)

    def search16(need):
        def count_ge(cand):
            def body(c, acc):
                hit = (half_s[c] >= cand).astype(I16)
                for r in range(0, kc, 16):
                    acc = acc + hit[r:r + 16, :]
                return acc
            acc = lax.fori_loop(0, n_chunks, body, jnp.zeros((16, tq), I16))
            return jnp.sum(acc.astype(I32), axis=0, keepdims=True)

        def bit_step(i, t16):
            cand = t16 ^ lax.shift_left(jnp.int32(1), 15 - i)
            cand = lax.shift_right_arithmetic(lax.shift_left(cand, 16), 16)
            return jnp.where(count_ge(cand.astype(I16)) >= need, cand, t16)

        return lax.fori_loop(0, 16, bit_step, jnp.full((1, tq), -32768, I32))

    thr_hi = search16(n_sel)

    def low_body(c, acc):
        keys = key_s[c]
        hi = lax.shift_right_arithmetic(keys, 16)
        low = (keys & 0xFFFF) - 32768
        half_s[c] = jnp.where(hi == thr_hi, low, -32768).astype(I16)
        return acc + jnp.sum((hi > thr_hi).astype(I32).reshape(kc // 8, 8, tq), axis=0)

    n_above = jnp.sum(lax.fori_loop(0, n_chunks, low_body, jnp.zeros((8, tq), I32)), axis=0, keepdims=True)
    thr_lo = search16(n_sel - n_above)
    thr = lax.shift_left(thr_hi, 16) | (thr_lo + 32768)

    def bias_body(c, acc):
        keys = key_s[c]
        ge = keys >= thr
        bias_s[c] = jnp.where(ge & (keys > KEY_NEG_INF), 0.0, -jnp.inf)
        fold = lambda hit: jnp.sum(hit.astype(I32).reshape(kc // 8, 8, tq), axis=0)
        return acc[0] + fold(ge), acc[1] + fold(keys > thr)

    zero8 = jnp.zeros((8, tq), I32)
    n_ge, n_gt = [jnp.sum(a, axis=0, keepdims=True)
                  for a in lax.fori_loop(0, n_chunks, bias_body, (zero8, zero8))]

    has_excess = jnp.max(jnp.where(thr > KEY_NEG_INF, n_ge - n_sel, 0)) > 0

    @pl.when(has_excess)
    def _():
        wanted = (n_sel - n_gt).astype(F32)
        r_i = lax.broadcasted_iota(I32, (LANES, LANES), 0)
        c_i = lax.broadcasted_iota(I32, (LANES, LANES), 1)
        lower = jnp.where(c_i <= r_i, 1.0, 0.0).astype(BF16)

        def tie_body(u, seen):
            c = u // sub_per_chunk
            r0 = pl.multiple_of((u % sub_per_chunk) * LANES, LANES)
            keys = key_s[c, pl.ds(r0, LANES), :]
            tie = keys == thr
            tie_f = jnp.where(tie, 1.0, 0.0)
            rank = seen + _dot(lower, tie_f.astype(BF16))
            sel = ((keys > thr) | (tie & (rank <= wanted))) & (keys > KEY_NEG_INF)
            bias_s[c, pl.ds(r0, LANES), :] = jnp.where(sel, 0.0, -jnp.inf)
            return seen + jnp.sum(tie_f, axis=0, keepdims=True)

        lax.fori_loop(0, n_chunks * sub_per_chunk, tie_body, jnp.zeros((1, tq), F32))

    n_grp = N_HEADS // N_KV_HEADS
    acc0 = jnp.zeros((HEAD_DIM + ONES_ROWS, tq), F32)

    def attend(running_max):
        def group_body(g, den_min):
            q_tiles = [qs[g * n_grp + h] for h in range(n_grp)]

            def chunk_body(c, st):
                bias = bias_s[c]
                v_c = vt[g * nct + c]
                k_tiles = (ke[g * nct + c], ko[g * nct + c])
                logits = [_dot_nt(k_tiles[h % 2], q_tiles[h]) for h in range(n_grp)]
                tails = [lax.bitcast_convert_type(l[kc - 8:kc, :], I32) for l in logits[1:]]
                anchor = functools.reduce(jnp.bitwise_or, tails) & lax.shift_right_arithmetic(c, 31)
                bias0 = bias + lax.bitcast_convert_type(anchor, F32)[0:1, :]
                out = []
                for h in range(n_grp):
                    logit = logits[h] + (bias0 if h == 0 else bias)
                    if running_max:
                        m_run, acc = st[2 * h:2 * h + 2]
                        m_new = jnp.maximum(m_run, jnp.max(logit, axis=0, keepdims=True))
                        m_safe = jnp.where(m_new > -jnp.inf, m_new, 0.0)
                        p = jnp.exp2(logit - m_safe)
                        out += [m_new, jnp.exp2(m_run - m_safe) * acc + _dot(v_c, p.astype(BF16))]
                    else:
                        out.append(st[h] + _dot(v_c, jnp.exp2(logit).astype(BF16)))
                return tuple(out)

            if running_max:
                st = lax.fori_loop(0, n_chunks, chunk_body, (jnp.full((1, tq), -jnp.inf, F32), acc0) * n_grp)
                accs = st[1::2]
            else:
                accs = lax.fori_loop(0, n_chunks, chunk_body, (acc0,) * n_grp)
            for h in range(n_grp):
                r0 = pl.multiple_of(g * (n_grp * HEAD_DIM) + h * HEAD_DIM, HEAD_DIM)
                den = accs[h][HEAD_DIM:HEAD_DIM + 1, :]
                ot_s[pl.ds(r0, HEAD_DIM), :] = accs[h][0:HEAD_DIM, :] / den
                den_min = jnp.minimum(den_min, den)
            return den_min

        return lax.fori_loop(0, N_KV_HEADS, group_body, jnp.full((1, tq), jnp.inf, F32))

    den_min = attend(running_max=False)

    @pl.when(jnp.logical_not(jnp.min(den_min) >= MIN_DENOMINATOR))
    def _():
        attend(running_max=True)

    o = jnp.concatenate([ot_s[m * LANES:(m + 1) * LANES, :].T for m in range(N_HEADS // 2)], axis=1)
    o_ref[0] = _dot(o.astype(BF16), wout_ref[...]) + x_ref[0]


def _dsa_attn_prompt(q, qi, kw, k, v, x, w_out, *, tq, kc):
    b, t, d = x.shape
    n_sel = min(TOPK_MAX, t // 4)
    n_kv = N_KV_HEADS * HEAD_DIM
    nct = t // kc
    kern = functools.partial(_attn_prompt_kernel, tq=tq, t_len=t, kc=kc, n_sel=n_sel)

    def tile_spec(n):
        return pl.BlockSpec((1, tq, n), lambda i, j: (i, j, 0))

    def full_spec(n):
        return pl.BlockSpec((1, t, n), lambda i, j: (i, 0, 0), pipeline_mode=pl.Buffered(1))

    return pl.pallas_call(
        kern,
        grid=(b, t // tq),
        in_specs=[tile_spec(q.shape[-1]), tile_spec(qi.shape[-1]), tile_spec(LANES),
                  full_spec(n_kv), full_spec(n_kv), full_spec(LANES),
                  tile_spec(d), _const_spec(w_out.shape)],
        out_specs=tile_spec(d),
        out_shape=jax.ShapeDtypeStruct(x.shape, F32),
        scratch_shapes=[
            pltpu.VMEM((N_KV_HEADS * nct, kc, LANES), BF16), pltpu.VMEM((N_KV_HEADS * nct, kc, LANES), BF16),
            pltpu.VMEM((N_KV_HEADS * nct, HEAD_DIM + ONES_ROWS, kc), BF16),
            pltpu.VMEM((nct, kc, LANES), BF16), pltpu.VMEM((nct, kc, LANES), BF16),
            pltpu.VMEM((N_HEADS, tq, LANES), BF16),
            pltpu.VMEM((nct, kc, tq), I32), pltpu.VMEM((nct, kc, tq), F32),
            pltpu.VMEM((N_HEADS * HEAD_DIM, tq), F32),
            pltpu.VMEM((nct, kc, tq), I16),
            pltpu.VMEM((N_KV_HEADS, 1, LANES), F32),
        ],
        compiler_params=_cparams(2),
        name="dsa_attn_prompt",
    )(q, qi, kw, k, v, kw, x, w_out)


def _attn_sample_kernel(q_ref, qi_ref, kw_ref, k_ref, v_ref, ckt_ref, cvt_ref, ckit_ref, x_ref, wout_ref, o_ref,
                        kt16, vt16, qg_s, kn_s, vn_s, og_s, *, tq, past, n_sel):
    n_grp = N_HEADS // N_KV_HEADS
    lane = lax.broadcasted_iota(I32, (1, LANES), 1)
    lo = lane < HEAD_DIM

    def lower_half(x128, upper):
        return jnp.where(lo, pltpu.roll(x128, HEAD_DIM, 1) if upper else x128, 0.0)

    kt16[...] = ckt_ref[0].astype(BF16)
    vt16[...] = cvt_ref[0].astype(BF16)
    kit16 = ckit_ref[0].astype(BF16)
    kw = kw_ref[0]
    kw16 = kw.astype(BF16)
    k_new, v_new = k_ref[0], v_ref[0]
    q32 = q_ref[0].astype(F32)
    for g in range(N_KV_HEADS):
        tile = slice((g // 2) * LANES, (g // 2 + 1) * LANES)
        kn_s[g] = lower_half(k_new[:, tile], g % 2 == 1).astype(BF16)
        vn_s[g] = lower_half(v_new[:, tile], g % 2 == 1).astype(BF16)
        heads = [n_grp * g + h for h in range(n_grp)]
        qg_s[g] = jnp.concatenate(
            [lower_half(q32[:, (hd // 2) * LANES:(hd // 2 + 1) * LANES], hd % 2 == 1) for hd in heads],
            axis=0).astype(BF16)

    qi32 = qi_ref[0].astype(F32)
    qi_rows = jnp.concatenate(
        [lower_half(qi32[:, (hh // 2) * LANES:(hh // 2 + 1) * LANES], hh % 2 == 1) for hh in range(IDX_HEADS)],
        axis=0).astype(BF16)
    d_c = _dot(qi_rows[:, 0:IDX_DIM], kit16)
    d_n = _dot_nt(qi_rows, kw16)
    sc_c = jnp.zeros((tq, past), F32)
    sc_n = jnp.zeros((tq, tq), F32)
    for hh in range(IDX_HEADS):
        w_col = kw[:, IDX_DIM + hh:IDX_DIM + hh + 1]
        rows = slice(hh * tq, (hh + 1) * tq)
        sc_c = sc_c + w_col * jnp.maximum(d_c[rows, :], 0.0)
        sc_n = sc_n + w_col * jnp.maximum(d_n[rows, :], 0.0)
    q_chunk = (past + lax.broadcasted_iota(I32, (tq, 1), 0)) // CHUNK
    k_chunk = (past + lax.broadcasted_iota(I32, (1, tq), 1)) // CHUNK
    key_c = _order_key(sc_c)
    key_n = jnp.where(k_chunk <= q_chunk, _order_key(sc_n), KEY_NEG_INF)

    def count(pred):
        return (jnp.sum(pred(key_c).astype(I32), axis=-1, keepdims=True)
                + jnp.sum(pred(key_n).astype(I32), axis=-1, keepdims=True))

    def bit_step(i, thr):
        cand = thr ^ lax.shift_left(jnp.int32(1), 31 - i)
        return jnp.where(count(lambda k: k >= cand) >= n_sel, cand, thr)

    thr = lax.fori_loop(0, 32, bit_step, jnp.full((tq, 1), INT_MIN, I32))
    wanted = (n_sel - count(lambda k: k > thr)).astype(F32)

    def select(keys, seen):
        tie = keys == thr
        tie_f = jnp.where(tie, 1.0, 0.0)
        ranks = []
        for c0 in range(0, keys.shape[1], LANES):
            cw = min(LANES, keys.shape[1] - c0)
            r_i = lax.broadcasted_iota(I32, (cw, cw), 0)
            c_i = lax.broadcasted_iota(I32, (cw, cw), 1)
            upper = jnp.where(r_i <= c_i, 1.0, 0.0).astype(BF16)
            part = tie_f[:, c0:c0 + cw]
            ranks.append(seen + _dot(part.astype(BF16), upper))
            seen = seen + jnp.sum(part, axis=-1, keepdims=True)
        rank = ranks[0] if len(ranks) == 1 else jnp.concatenate(ranks, axis=1)
        sel = ((keys > thr) | (tie & (rank <= wanted))) & (keys > KEY_NEG_INF)
        return jnp.where(sel, 0.0, -jnp.inf), seen

    bias_c, seen = select(key_c, jnp.zeros((tq, 1), F32))
    bias_n, _ = select(key_n, seen)
    bias_c = jnp.concatenate([bias_c] * n_grp, axis=0)
    bias_n = jnp.concatenate([bias_n] * n_grp, axis=0)

    groups = range(N_KV_HEADS)
    rows = [slice(g * HEAD_DIM, (g + 1) * HEAD_DIM) for g in groups]
    l_c = [_dot(qg_s[g][:, 0:HEAD_DIM], kt16[rows[g], :]) for g in groups]
    l_n = [_dot_nt(qg_s[g], kn_s[g]) for g in groups]
    tail = lax.bitcast_convert_type(l_c[-1][:, past - LANES:past], I32) & lax.shift_right_arithmetic(pl.program_id(0), 31)
    l_n[0] = l_n[0] + lax.bitcast_convert_type(tail, F32)[:, 0:tq]
    for g in groups:
        lc, ln = l_c[g] + bias_c, l_n[g] + bias_n
        mx = jnp.maximum(jnp.max(lc, axis=-1, keepdims=True), jnp.max(ln, axis=-1, keepdims=True))
        p_c = jnp.exp2(lc - mx)
        p_n = jnp.exp2(ln - mx)
        den = jnp.sum(p_c, axis=-1, keepdims=True) + jnp.sum(p_n, axis=-1, keepdims=True)
        pv = (_dot_nt(p_c.astype(BF16), vt16[rows[g], :])
              + _dot(p_n.astype(BF16), vn_s[g])[:, 0:HEAD_DIM])
        og_s[g] = pv / den

    tiles = []
    for m in range(N_HEADS // 2):
        pair = []
        for hd in (2 * m, 2 * m + 1):
            h = hd % n_grp
            pair.append(og_s[hd // n_grp, h * tq:(h + 1) * tq, :])
        tiles.append(jnp.concatenate(pair, axis=1))
    o = jnp.concatenate(tiles, axis=1).astype(BF16)
    o_ref[0] = _dot(o, wout_ref[...]) + x_ref[0]


def _dsa_attn_sample(q, qi, kw, k, v, cache_kt, cache_vt, cache_kit, x, w_out):
    b, t, d = x.shape
    past = cache_kt.shape[-1]
    n_sel = min(TOPK_MAX, (past + t) // 4)
    n_kv = N_KV_HEADS * HEAD_DIM
    n_grp = N_HEADS // N_KV_HEADS
    kern = functools.partial(_attn_sample_kernel, tq=t, past=past, n_sel=n_sel)

    def spec(rows, n):
        return pl.BlockSpec((1, rows, n), lambda i: (i, 0, 0))

    return pl.pallas_call(
        kern,
        grid=(b,),
        in_specs=[spec(t, q.shape[-1]), spec(t, qi.shape[-1]), spec(t, LANES), spec(t, n_kv), spec(t, n_kv),
                  spec(n_kv, past), spec(n_kv, past), spec(IDX_DIM, past),
                  spec(t, d), _const_spec(w_out.shape)],
        out_specs=spec(t, d),
        out_shape=jax.ShapeDtypeStruct(x.shape, F32),
        scratch_shapes=[
            pltpu.VMEM((n_kv, past), BF16), pltpu.VMEM((n_kv, past), BF16),
            pltpu.VMEM((N_KV_HEADS, n_grp * t, LANES), BF16),
            pltpu.VMEM((N_KV_HEADS, t, LANES), BF16), pltpu.VMEM((N_KV_HEADS, t, LANES), BF16),
            pltpu.VMEM((N_KV_HEADS, n_grp * t, HEAD_DIM), F32),
        ],
        compiler_params=_cparams(1),
        name="dsa_attn_sample",
    )(q, qi, kw, k, v, cache_kt, cache_vt, cache_kit, x, w_out)


def _tiling(b, t):
    if t >= 256:
        return dict(bb=1, tt=256, blk=128, sub=32)
    bb = max(1, min(b, 128 // t))
    return dict(bb=bb, tt=t, blk=t, sub=t)


def _trunk(x, hg_state, conv_state, cache, p):
    b, t, d = x.shape
    tl = _tiling(b, t)
    bb, tt = tl["bb"], tl["tt"]
    x, s_new = _hgrn_mixer(x, hg_state[0], p["norm_mix"][0], p["hg_w_in"][0], p["lower_bounds"][0],
                           p["hg_norm"][0], p["hg_w_out"][0], **dict(tl, tt=512 if t % 512 == 0 else tt))
    x, cs0 = _conv_ffn(x, conv_state[0], p["norm_ffn"][0], p["ffn_w_in"][0], p["ffn_conv_w"][0],
                       p["ffn_conv_b"][0], p["ffn_w_down"][0], p["norm_final"], bb=bb, tt=tt, final_norm=False)
    past = 0 if cache is None else cache[0].shape[1]
    tables = _rope_tables(past + jnp.arange(t))
    q, k, v, qi, kw = _dsa_proj(x, p["norm_mix"][1], p["ds_w_in"][0], tables, p["ds_kln_w"][0],
                                p["ds_kln_b"][0], bb=bb, tt=tt)
    if cache is None:
        x = _dsa_attn_prompt(q, qi, kw, k, v, x, p["ds_w_out"][0], tq=min(t, 512), kc=min(t, 512))
    else:
        ck, cv, cki = cache
        def keys_last(c):
            return jnp.swapaxes(c.reshape(b, past, -1), 1, 2)

        x = _dsa_attn_sample(q, qi, kw, k, v, keys_last(ck), keys_last(cv), keys_last(cki), x, p["ds_w_out"][0])
    x, cs1 = _conv_ffn(x, conv_state[1], p["norm_ffn"][1], p["ffn_w_in"][1], p["ffn_conv_w"][1],
                       p["ffn_conv_b"][1], p["ffn_w_down"][1], p["norm_final"], bb=bb, tt=tt, final_norm=True)
    k = k.reshape(1, b, t, N_KV_HEADS, HEAD_DIM)
    v = v.reshape(1, b, t, N_KV_HEADS, HEAD_DIM)
    ki = kw[:, :, :IDX_DIM].reshape(1, b, t, IDX_DIM)
    return x, s_new[None], jnp.stack([cs0, cs1]), k, v, ki


def kernel(x_prompt, x_sample, cache_k, cache_v, cache_kidx, state_hgrn, state_conv, norm_mix, norm_ffn, norm_final, hg_w_in, hg_lb, hg_norm, hg_w_out, ds_w_in, ds_kln_w, ds_kln_b, ds_w_out, ffn_w_in, ffn_conv_w, ffn_conv_b, ffn_w_down):
    dsa_in = ds_w_in.shape[-1]
    dsa_pad = (-dsa_in) % LANES
    p = dict(
        norm_mix=norm_mix, norm_ffn=norm_ffn, norm_final=norm_final,
        hg_w_in=hg_w_in.astype(BF16), hg_norm=hg_norm, hg_w_out=hg_w_out.astype(BF16),
        lower_bounds=jnp.cumsum(jax.nn.softmax(hg_lb.astype(F32), axis=0), axis=0),
        ds_w_in=jnp.pad(ds_w_in, ((0, 0), (0, 0), (0, dsa_pad))).astype(BF16),
        ds_kln_w=ds_kln_w, ds_kln_b=ds_kln_b, ds_w_out=ds_w_out.astype(BF16),
        ffn_w_in=ffn_w_in.astype(BF16), ffn_conv_w=ffn_conv_w, ffn_conv_b=ffn_conv_b,
        ffn_w_down=ffn_w_down.astype(BF16),
    )
    b = x_prompt.shape[0]
    hg0 = jnp.zeros((state_hgrn.shape[0], b) + state_hgrn.shape[2:], F32)
    conv0 = jnp.zeros((state_conv.shape[0], b) + state_conv.shape[2:], F32)
    y_p, hg_p, conv_p, k_p, v_p, ki_p = _trunk(x_prompt, hg0, conv0, None, p)
    y_s, hg_s, conv_s, k_s, v_s, ki_s = _trunk(x_sample, state_hgrn, state_conv,
                                               (cache_k[0], cache_v[0], cache_kidx[0]), p)
    return (y_p, y_s, k_p, v_p, ki_p, hg_p, conv_p, k_s, v_s, ki_s, hg_s, conv_s)
```

```python
import functools

import jax
import jax.numpy as jnp
from jax import lax
from jax.experimental import pallas as pl
from jax.experimental.pallas import tpu as pltpu

F32 = jnp.float32
BF16 = jnp.bfloat16
I32 = jnp.int32
I16 = jnp.int16

CHUNK = 64
EPS = 1e-6
HG_HEADS = 8
HG_KDIM = 128
N_HEADS = 16
HEAD_DIM = 64
N_KV_HEADS = 4
IDX_HEADS = 8
IDX_DIM = 64
TOPK_MAX = 256
ROPE_THETA = 500000.0
CONV_W = 3

LANES = 128
VMEM_LIMIT = 56 * 1024 * 1024
EXP_CLAMP = 80.0
LOG2_E = 1.4426950408889634
ONES_ROWS = 16
BOUND_SLACK = 1.02
MIN_DENOMINATOR = 2.0 ** -60
KEY_NEG_INF = -2139095041
INT_MIN = -2147483648


def _cparams(n_axes, flags=None):
    return pltpu.CompilerParams(dimension_semantics=("arbitrary",) * n_axes,
                                vmem_limit_bytes=VMEM_LIMIT, flags=flags)


def _const_spec(shape):
    nd = len(shape)
    return pl.BlockSpec(shape, lambda *_: (0,) * nd, pipeline_mode=pl.Buffered(1))


def _rms(x, g):
    return x * lax.rsqrt(jnp.mean(x * x, axis=-1, keepdims=True) + EPS) * g


def _silu(x):
    return x * jax.nn.sigmoid(x)


def _dot(a, b):
    return jnp.dot(a, b, preferred_element_type=F32)


def _dot_nt(a, b):
    return lax.dot_general(a, b, (((1,), (1,)), ((), ())), preferred_element_type=F32)


def _dot_tn(a, b):
    return lax.dot_general(a, b, (((0,), (0,)), ((), ())), preferred_element_type=F32)


def _hgrn_kernel(x_ref, s0_ref, g_ref, win_ref, lb_ref, hn_ref, wout_ref, o_ref, s_ref,
                 q_s, k_s, g_s, v_s, gate_s, on_s, st_s, o_s, *, bb, tt, blk, sub):
    j = pl.program_id(1)
    rows = bb * tt
    d = x_ref.shape[-1]
    fdim = HG_HEADS * HG_KDIM
    n_sub = blk // sub

    @pl.when(j == 0)
    def _():
        for bi in range(bb):
            for hh in range(HG_HEADS):
                st_s[bi * HG_HEADS + hh] = s0_ref[bi, hh].T

    x = x_ref[...].reshape(rows, d)
    h = _rms(x, g_ref[...]).astype(BF16)
    lb = lb_ref[...]
    fg = lb + (1.0 - lb) * jax.nn.sigmoid(_dot(h, win_ref[:, fdim:2 * fdim]))
    k_s[...] = 1.0 - fg
    g_s[...] = jnp.log(fg)
    q_s[...] = _silu(_dot(h, win_ref[:, 0:fdim]))
    gate_s[...] = _silu(_dot(h, win_ref[:, 2 * fdim + d:]))
    v_s[...] = _dot(h, win_ref[:, 2 * fdim:2 * fdim + d])

    r_i = lax.broadcasted_iota(I32, (blk, blk), 0)
    c_i = lax.broadcasted_iota(I32, (blk, blk), 1)
    causal = c_i <= r_i
    tril = jnp.where(causal, 1.0, 0.0).astype(BF16)
    hn = hn_ref[...]
    blocks_per_stream = tt // blk

    def block_body(idx, carry):
        r0 = idx * blk
        bi = idx // blocks_per_stream
        lg = g_s[pl.ds(r0, blk), :]
        lg_hi = lg.astype(BF16)
        rem = lg - lg_hi.astype(F32)
        lg_mid = rem.astype(BF16)
        lg_lo = (rem - lg_mid.astype(F32)).astype(BF16)
        g_all = _dot(tril, lg_hi) + _dot(tril, lg_mid) + _dot(tril, lg_lo)
        run_zero = lax.shift_right_arithmetic(j, 31)

        def anchor(x):
            tail = lax.bitcast_convert_type(x[x.shape[0] - 8:, :], I32) & run_zero
            return lax.bitcast_convert_type(tail, F32)[0:1, :]

        heads = [slice(hh * HG_KDIM, (hh + 1) * HG_KDIM) for hh in range(HG_HEADS)]
        qcat, kcat, qfull, kdec_last, v16, decay = [], [], [], [], [], []
        for sl in heads:
            gc = g_all[:, sl]
            q = q_s[pl.ds(r0, blk), sl]
            kk = k_s[pl.ds(r0, blk), sl]
            refs = [jnp.zeros((1, HG_KDIM), F32)] + [gc[s * sub - 1:s * sub, :] for s in range(1, n_sub)]
            q_parts, k_parts = [], []
            for s in range(n_sub):
                lo_r, hi_r = s * sub, (s + 1) * sub
                qg = q[lo_r:hi_r, :] * jnp.exp(gc[lo_r:hi_r, :] - refs[s])
                pieces = [qg]
                if lo_r:
                    pieces.insert(0, jnp.zeros((lo_r, HG_KDIM), F32))
                if blk - hi_r:
                    pieces.append(jnp.zeros((blk - hi_r, HG_KDIM), F32))
                q_parts.append(pieces[0] if len(pieces) == 1 else jnp.concatenate(pieces, axis=0))
                kd_s = kk[0:hi_r, :] * jnp.exp(jnp.minimum(refs[s] - gc[0:hi_r, :], EXP_CLAMP))
                k_parts.append(kd_s if hi_r == blk else
                               jnp.concatenate([kd_s, jnp.zeros((blk - hi_r, HG_KDIM), F32)], axis=0))
            qcat.append(jnp.concatenate(q_parts, axis=1).astype(BF16))
            kcat.append(jnp.concatenate(k_parts, axis=1).astype(BF16))
            qfull.append((q * jnp.exp(gc)).astype(BF16))
            g_last = gc[blk - 1:blk, :]
            kdec_last.append((kk * jnp.exp(g_last - gc)).astype(BF16))
            decay.append(jnp.exp(g_last))
            v16.append(v_s[pl.ds(r0, blk), sl].astype(BF16))
        a_list = [_dot_nt(qcat[hh], kcat[hh]) for hh in range(HG_HEADS)]
        s_inc = [_dot_tn(v16[hh], kdec_last[hh]) for hh in range(HG_HEADS)]
        a_list[0] = a_list[0] + anchor(a_list[-1])
        o_list = []
        for hh in range(HG_HEADS):
            a = jnp.where(causal, a_list[hh], 0.0).astype(BF16)
            st = st_s[bi * HG_HEADS + hh]
            o_list.append(_dot(a, v16[hh]) + _dot_nt(qfull[hh], st.astype(BF16)))
            st_s[bi * HG_HEADS + hh] = st * decay[hh] + s_inc[hh]
        o_list[0] = o_list[0] + anchor(o_list[-1])
        for hh, sl in enumerate(heads):
            on = _rms(o_list[hh], hn) * gate_s[pl.ds(r0, blk), sl]
            on_s[pl.ds(r0, blk), sl] = on.astype(BF16)
        return carry

    sub_sums = jnp.sum(g_s[...].reshape(rows // sub, sub, fdim), axis=1)
    blocked_ok = jnp.min(sub_sums) >= -EXP_CLAMP

    def project_out(r0, r1):
        out = _dot(on_s[r0:r1, :], wout_ref[...]) + x[r0:r1, :]
        if bb == 1:
            o_ref[0, r0:r1, :] = out
        else:
            o_ref[r0 // tt:r1 // tt] = out.reshape((r1 - r0) // tt, tt, d)

    @pl.when(blocked_ok)
    def _():
        n_blocks = rows // blk
        half = n_blocks // 2 if n_blocks % 2 == 0 else n_blocks
        for b0 in range(0, n_blocks, half):
            for idx in range(b0, b0 + half):
                block_body(idx, 0)
            project_out(b0 * blk, (b0 + half) * blk)

    @pl.when(jnp.logical_not(blocked_ok))
    def _():
        grp = 16
        o_s[...] = jnp.zeros(o_s.shape, F32)
        in_grp = lax.broadcasted_iota(I32, (grp, 1), 0)

        def frame_body(r, carry):
            r0 = pl.multiple_of((r // grp) * grp, grp)
            bi = r // tt
            this = in_grp == (r % grp)
            for hh in range(HG_HEADS):
                sl = slice(hh * HG_KDIM, (hh + 1) * HG_KDIM)

                def only(ref):
                    return jnp.where(this, ref[pl.ds(r0, grp), sl], 0.0)

                forget = jnp.exp(jnp.sum(only(g_s), axis=0, keepdims=True))
                st = (st_s[bi * HG_HEADS + hh] * forget
                      + _dot_tn(only(v_s).astype(BF16), only(k_s).astype(BF16)))
                st_s[bi * HG_HEADS + hh] = st
                o_s[pl.ds(r0, grp), sl] += _dot_nt(only(q_s).astype(BF16), st.astype(BF16))
            return carry

        lax.fori_loop(0, rows, frame_body, 0)
        for hh in range(HG_HEADS):
            sl = slice(hh * HG_KDIM, (hh + 1) * HG_KDIM)
            on_s[:, sl] = (_rms(o_s[:, sl], hn) * gate_s[:, sl]).astype(BF16)
        project_out(0, rows)

    @pl.when(j == pl.num_programs(1) - 1)
    def _():
        for bi in range(bb):
            for hh in range(HG_HEADS):
                s_ref[bi, hh] = st_s[bi * HG_HEADS + hh].T


def _hgrn_mixer(x, s0, norm_g, w_in, lb, hn, w_out, *, bb, tt, blk, sub):
    b, t, d = x.shape
    fdim = HG_HEADS * HG_KDIM
    rows = bb * tt
    kern = functools.partial(_hgrn_kernel, bb=bb, tt=tt, blk=blk, sub=sub)
    return pl.pallas_call(
        kern,
        grid=(b // bb, t // tt),
        in_specs=[
            pl.BlockSpec((bb, tt, d), lambda i, j: (i, j, 0)),
            pl.BlockSpec((bb, HG_HEADS, HG_KDIM, d // HG_HEADS), lambda i, j: (i, 0, 0, 0)),
            _const_spec((1, d)),
            _const_spec(w_in.shape),
            _const_spec((1, fdim)),
            _const_spec((1, d // HG_HEADS)),
            _const_spec(w_out.shape),
        ],
        out_specs=[
            pl.BlockSpec((bb, tt, d), lambda i, j: (i, j, 0)),
            pl.BlockSpec((bb, HG_HEADS, HG_KDIM, d // HG_HEADS), lambda i, j: (i, 0, 0, 0)),
        ],
        out_shape=[jax.ShapeDtypeStruct(x.shape, F32), jax.ShapeDtypeStruct(s0.shape, F32)],
        scratch_shapes=[
            pltpu.VMEM((rows, fdim), F32),
            pltpu.VMEM((rows, fdim), F32),
            pltpu.VMEM((rows, fdim), F32),
            pltpu.VMEM((rows, d), F32),
            pltpu.VMEM((rows, d), F32),
            pltpu.VMEM((rows, d), BF16),
            pltpu.VMEM((bb * HG_HEADS, d // HG_HEADS, HG_KDIM), F32),
            pltpu.VMEM((rows, d), F32),
        ],
        compiler_params=_cparams(2),
        name="hgrn_mixer",
    )(x, s0, norm_g.reshape(1, d), w_in, lb.reshape(1, fdim), hn.reshape(1, -1), w_out)


CONV_HEAD = 8


def _ffn_kernel(x_ref, cs_ref, g_ref, win_ref, cw_ref, cb_ref, wdn_ref, fg_ref, o_ref, ns_ref,
                a_s, *, bb, tt, final_norm):
    j = pl.program_id(1)
    rows = bb * tt
    d = x_ref.shape[-1]
    dff = cw_ref.shape[-1]
    hist = CONV_W - 1

    @pl.when(j == 0)
    def _():
        a_s[:, CONV_HEAD - hist:CONV_HEAD, :] = cs_ref[...]

    x = x_ref[...].reshape(rows, d)
    h = _rms(x, g_ref[...]).astype(BF16)
    a = _dot(h, win_ref[:, 0:dff])
    u = _dot(h, win_ref[:, dff:])
    a_s[:, CONV_HEAD:CONV_HEAD + tt, :] = a.reshape(bb, tt, dff)
    c = jnp.broadcast_to(cb_ref[...].reshape(1, 1, dff), (bb, tt, dff))
    for w in range(CONV_W):
        lo = CONV_HEAD - hist + w
        c = c + a_s[:, lo:lo + tt, :] * cw_ref[w:w + 1, :].reshape(1, 1, dff)
    new_state = a_s[:, CONV_HEAD + tt - hist:CONV_HEAD + tt, :]
    a_s[:, CONV_HEAD - hist:CONV_HEAD, :] = new_state
    ns_ref[...] = new_state
    act = (_silu(c).reshape(rows, dff) * u).astype(BF16)
    y = _dot(act, wdn_ref[...]) + x
    if final_norm:
        y = _rms(y, fg_ref[...])
    o_ref[...] = y.reshape(bb, tt, d)


def _conv_ffn(x, conv_state, norm_g, w_in, conv_w, conv_b, w_down, final_g, *, bb, tt, final_norm):
    b, t, d = x.shape
    dff = conv_w.shape[-1]
    kern = functools.partial(_ffn_kernel, bb=bb, tt=tt, final_norm=final_norm)
    return pl.pallas_call(
        kern,
        grid=(b // bb, t // tt),
        in_specs=[
            pl.BlockSpec((bb, tt, d), lambda i, j: (i, j, 0)),
            pl.BlockSpec((bb, CONV_W - 1, dff), lambda i, j: (i, 0, 0)),
            _const_spec((1, d)),
            _const_spec(w_in.shape),
            _const_spec(conv_w.shape),
            _const_spec((1, dff)),
            _const_spec(w_down.shape),
            _const_spec((1, d)),
        ],
        out_specs=[
            pl.BlockSpec((bb, tt, d), lambda i, j: (i, j, 0)),
            pl.BlockSpec((bb, CONV_W - 1, dff), lambda i, j: (i, 0, 0)),
        ],
        out_shape=[jax.ShapeDtypeStruct(x.shape, F32), jax.ShapeDtypeStruct(conv_state.shape, F32)],
        scratch_shapes=[pltpu.VMEM((bb, CONV_HEAD + tt, dff), F32)],
        compiler_params=_cparams(2),
        name="conv_ffn",
    )(x, conv_state, norm_g.reshape(1, d), w_in, conv_w, conv_b.reshape(1, dff), w_down,
      final_g.reshape(1, d))


def _rope_tables(pos):
    rot = HEAD_DIM // 4
    half = rot // 2
    inv_freq = ROPE_THETA ** (-jnp.arange(half, dtype=F32) / half)
    ang = pos.astype(F32)[:, None] * inv_freq[None, :]
    cos, sin = jnp.cos(ang), jnp.sin(ang)
    t = pos.shape[0]
    pad = HEAD_DIM - rot
    one = jnp.ones((t, pad), F32)
    zero_h = jnp.zeros((t, half), F32)
    zero_p = jnp.zeros((t, pad), F32)
    c64 = jnp.concatenate([cos, cos, one], axis=1)
    s1_64 = jnp.concatenate([zero_h, sin, zero_p], axis=1)
    s2_64 = jnp.concatenate([-sin, zero_h, zero_p], axis=1)
    ident = jnp.ones((t, HEAD_DIM), F32)
    zero64 = jnp.zeros((t, HEAD_DIM), F32)
    return jnp.stack([
        jnp.concatenate([c64, c64], axis=1), jnp.concatenate([s1_64, s1_64], axis=1),
        jnp.concatenate([s2_64, s2_64], axis=1),
        jnp.concatenate([c64, ident], axis=1), jnp.concatenate([s1_64, zero64], axis=1),
        jnp.concatenate([s2_64, zero64], axis=1)])


def _dsa_proj_kernel(x_ref, g_ref, w_ref, tab_ref, lnw_ref, lnb_ref,
                     q_ref, k_ref, v_ref, qi_ref, kw_ref, *, bb, tt):
    rows = bb * tt
    d = x_ref.shape[-1]
    half = HEAD_DIM // 8
    o1 = N_HEADS * HEAD_DIM
    o2 = o1 + N_KV_HEADS * HEAD_DIM
    o3 = o2 + N_KV_HEADS * HEAD_DIM
    o4 = o3 + IDX_HEADS * IDX_DIM

    def tab(i):
        t = tab_ref[i]
        return jnp.broadcast_to(t[None], (bb, tt, LANES)).reshape(rows, LANES)

    def rope(y, base):
        cos, s1, s2 = tab(base), tab(base + 1), tab(base + 2)
        tiles = []
        for m in range(y.shape[1] // LANES):
            yt = y[:, m * LANES:(m + 1) * LANES]
            tiles.append(yt * cos + pltpu.roll(yt, half, 1) * s1 + pltpu.roll(yt, LANES - half, 1) * s2)
        return tiles[0] if len(tiles) == 1 else jnp.concatenate(tiles, axis=1)

    x = x_ref[...].reshape(rows, d)
    h = _rms(x, g_ref[...]).astype(BF16)
    t = _dot(h, w_ref[:, o4:o4 + LANES])
    lane = lax.broadcasted_iota(I32, (1, LANES), 1)
    is_ki = lane < IDX_DIM
    mu = jnp.sum(jnp.where(is_ki, t, 0.0), axis=-1, keepdims=True) / IDX_DIM
    cen = jnp.where(is_ki, t - mu, 0.0)
    var = jnp.sum(cen * cen, axis=-1, keepdims=True) / IDX_DIM
    ki = rope(cen * lax.rsqrt(var + EPS) * lnw_ref[...] + lnb_ref[...], 3)
    wi = t * ((IDX_HEADS * IDX_DIM) ** -0.5)
    kw = jnp.where(is_ki, ki, jnp.where(lane < IDX_DIM + IDX_HEADS, wi, 0.0))
    kw_ref[...] = kw.reshape(bb, tt, LANES)
    qi_ref[...] = rope(_dot(h, w_ref[:, o3:o4]), 0).astype(BF16).reshape(bb, tt, o4 - o3)
    k_ref[...] = rope(_dot(h, w_ref[:, o1:o2]), 0).reshape(bb, tt, o2 - o1)
    q = rope(_dot(h, w_ref[:, 0:o1]), 0) * (HEAD_DIM ** -0.5 * LOG2_E)
    q_ref[...] = q.astype(BF16).reshape(bb, tt, o1)
    v_ref[...] = _dot(h, w_ref[:, o2:o3]).reshape(bb, tt, o3 - o2)


def _dsa_proj(x, norm_g, w_pad, tables, ln_w, ln_b, *, bb, tt):
    b, t, d = x.shape
    n_q = N_HEADS * HEAD_DIM
    n_kv = N_KV_HEADS * HEAD_DIM
    n_qi = IDX_HEADS * IDX_DIM
    pad = jnp.zeros((LANES - IDX_DIM,), F32)
    lnw = jnp.concatenate([ln_w, pad]).reshape(1, LANES)
    lnb = jnp.concatenate([ln_b, pad]).reshape(1, LANES)
    kern = functools.partial(_dsa_proj_kernel, bb=bb, tt=tt)

    def row_spec(n):
        return pl.BlockSpec((bb, tt, n), lambda i, j: (i, j, 0))

    return pl.pallas_call(
        kern,
        grid=(b // bb, t // tt),
        in_specs=[
            row_spec(d),
            _const_spec((1, d)),
            _const_spec(w_pad.shape),
            pl.BlockSpec((6, tt, LANES), lambda i, j: (0, j, 0)),
            _const_spec((1, LANES)),
            _const_spec((1, LANES)),
        ],
        out_specs=[row_spec(n_q), row_spec(n_kv), row_spec(n_kv), row_spec(n_qi), row_spec(LANES)],
        out_shape=[
            jax.ShapeDtypeStruct((b, t, n_q), BF16),
            jax.ShapeDtypeStruct((b, t, n_kv), F32),
            jax.ShapeDtypeStruct((b, t, n_kv), F32),
            jax.ShapeDtypeStruct((b, t, n_qi), BF16),
            jax.ShapeDtypeStruct((b, t, LANES), F32),
        ],
        compiler_params=_cparams(2),
        name="dsa_proj",
    )(x, norm_g.reshape(1, d), w_pad, tables, lnw, lnb)


def _order_key(score):
    score = jnp.where(score == 0.0, 0.0, score)
    bits = lax.bitcast_convert_type(score, I32)
    return bits ^ ((bits >> 31) & 0x7FFFFFFF)


def _attn_prompt_kernel(q_ref, qi_ref, wq_ref, k_ref, v_ref, kw_ref, x_ref, wout_ref, o_ref,
                        ke, ko, vt, kie, kio, qs, key_s, bias_s, ot_s, half_s, kmax_s, *, tq, t_len, kc, n_sel):
    j = pl.program_id(1)
    nct = t_len // kc
    sub_per_chunk = kc // LANES
    lane = lax.broadcasted_iota(I32, (1, LANES), 1)
    lo = lane < HEAD_DIM

    unit_hi = jnp.where(lane == HEAD_DIM, 1.0, 0.0)
    unit_lo = jnp.where(lane == 0, 1.0, 0.0)

    r_i = lax.broadcasted_iota(I32, (LANES, LANES), 0)
    c_i = lax.broadcasted_iota(I32, (LANES, LANES), 1)
    half_sums = jnp.where(((r_i < HEAD_DIM) & (c_i == HEAD_DIM)) | ((r_i >= HEAD_DIM) & (c_i == 0)),
                          1.0, 0.0).astype(BF16)

    @pl.when(j == 0)
    def _():
        k_sq_max = [jnp.zeros((1, LANES), F32) for _ in range(N_KV_HEADS // 2)]
        for c in range(nct):
            rows = slice(c * kc, (c + 1) * kc)
            k_c = k_ref[0, rows, :].astype(BF16).astype(F32)
            for m in range(N_KV_HEADS // 2):
                tile = k_c[:, m * LANES:(m + 1) * LANES]
                rolled = pltpu.roll(tile, HEAD_DIM, 1)
                ke[(2 * m) * nct + c] = jnp.where(lo, tile, unit_hi).astype(BF16)
                ko[(2 * m) * nct + c] = jnp.where(lo, unit_lo, rolled).astype(BF16)
                ke[(2 * m + 1) * nct + c] = jnp.where(lo, rolled, unit_hi).astype(BF16)
                ko[(2 * m + 1) * nct + c] = jnp.where(lo, unit_lo, tile).astype(BF16)
                norm2 = _dot((tile * tile).astype(BF16), half_sums)
                k_sq_max[m] = jnp.maximum(k_sq_max[m], jnp.max(norm2, axis=0, keepdims=True))
            v_t = v_ref[0, rows, :].T
            for g in range(N_KV_HEADS):
                vt[g * nct + c, 0:HEAD_DIM, :] = v_t[g * HEAD_DIM:(g + 1) * HEAD_DIM, :].astype(BF16)
                vt[g * nct + c, HEAD_DIM:HEAD_DIM + ONES_ROWS, :] = jnp.ones((ONES_ROWS, kc), BF16)
            kw_c = kw_ref[0, rows, :]
            kie[c] = jnp.where(lo, kw_c, 0.0).astype(BF16)
            kio[c] = jnp.where(lo, 0.0, pltpu.roll(kw_c, HEAD_DIM, 1)).astype(BF16)
        for g in range(N_KV_HEADS):
            at = HEAD_DIM if g % 2 == 0 else 0
            k_sq = jnp.max(jnp.where(lane == at, k_sq_max[g // 2], 0.0), axis=-1, keepdims=True)
            kmax_s[g] = jnp.broadcast_to(jnp.sqrt(k_sq), (1, LANES))

    for m in range(N_HEADS // 2):
        tile = q_ref[0, :, m * LANES:(m + 1) * LANES].astype(F32)
        k_norm = kmax_s[(2 * m) // (N_HEADS // N_KV_HEADS)]
        bound = jnp.sqrt(_dot((tile * tile).astype(BF16), half_sums)) * (k_norm * -BOUND_SLACK)
        qs[2 * m] = jnp.where(lo, tile, jnp.where(lane == HEAD_DIM, bound, 0.0)).astype(BF16)
        qs[2 * m + 1] = jnp.where(lo, jnp.where(lane == 0, bound, 0.0), tile).astype(BF16)

    n_chunks = ((j + 1) * tq + kc - 1) // kc
    w_t = wq_ref[0].T
    w_rows = [w_t[IDX_DIM + hh:IDX_DIM + hh + 1, :] for hh in range(IDX_HEADS)]
    qi = qi_ref[0]
    q_chunk = (j * tq + lax.broadcasted_iota(I32, (1, tq), 1)) // CHUNK

    def score_body(c, carry):
        kie_c, kio_c = kie[c], kio[c]
        sc = jnp.zeros((kc, tq), F32)
        for m in range(IDX_HEADS // 2):
            qt = qi[:, m * LANES:(m + 1) * LANES]
            sc = (sc + w_rows[2 * m] * jnp.maximum(_dot_nt(kie_c, qt), 0.0)
                  + w_rows[2 * m + 1] * jnp.maximum(_dot_nt(kio_c, qt), 0.0))
        k_chunk = (c * kc + lax.broadcasted_iota(I32, (kc, 1), 0)) // CHUNK
        keys = jnp.where(k_chunk <= q_chunk, _order_key(sc), KEY_NEG_INF)
        key_s[c] = keys
        half_s[c] = lax.shift_right_arithmetic(keys, 16).astype(I16)
        return carry

    lax.fori_loop(0, n_chunks, score_body, 0)

    def search16(need):
        def count_ge(cand):
            def body(c, acc):
                hit = (half_s[c] >= cand).astype(I16)
                for r in range(0, kc, 16):
                    acc = acc + hit[r:r + 16, :]
                return acc
            acc = lax.fori_loop(0, n_chunks, body, jnp.zeros((16, tq), I16))
            return jnp.sum(acc.astype(I32), axis=0, keepdims=True)

        def bit_step(i, t16):
            cand = t16 ^ lax.shift_left(jnp.int32(1), 15 - i)
            cand = lax.shift_right_arithmetic(lax.shift_left(cand, 16), 16)
            return jnp.where(count_ge(cand.astype(I16)) >= need, cand, t16)

        return lax.fori_loop(0, 16, bit_step, jnp.full((1, tq), -32768, I32))

    thr_hi = search16(n_sel)

    def low_body(c, acc):
        keys = key_s[c]
        hi = lax.shift_right_arithmetic(keys, 16)
        low = (keys & 0xFFFF) - 32768
        half_s[c] = jnp.where(hi == thr_hi, low, -32768).astype(I16)
        return acc + jnp.sum((hi > thr_hi).astype(I32).reshape(kc // 8, 8, tq), axis=0)

    n_above = jnp.sum(lax.fori_loop(0, n_chunks, low_body, jnp.zeros((8, tq), I32)), axis=0, keepdims=True)
    thr_lo = search16(n_sel - n_above)
    thr = lax.shift_left(thr_hi, 16) | (thr_lo + 32768)

    def bias_body(c, acc):
        keys = key_s[c]
        ge = keys >= thr
        bias_s[c] = jnp.where(ge & (keys > KEY_NEG_INF), 0.0, -jnp.inf)
        fold = lambda hit: jnp.sum(hit.astype(I32).reshape(kc // 8, 8, tq), axis=0)
        return acc[0] + fold(ge), acc[1] + fold(keys > thr)

    zero8 = jnp.zeros((8, tq), I32)
    n_ge, n_gt = [jnp.sum(a, axis=0, keepdims=True)
                  for a in lax.fori_loop(0, n_chunks, bias_body, (zero8, zero8))]

    has_excess = jnp.max(jnp.where(thr > KEY_NEG_INF, n_ge - n_sel, 0)) > 0

    @pl.when(has_excess)
    def _():
        wanted = (n_sel - n_gt).astype(F32)
        r_i = lax.broadcasted_iota(I32, (LANES, LANES), 0)
        c_i = lax.broadcasted_iota(I32, (LANES, LANES), 1)
        lower = jnp.where(c_i <= r_i, 1.0, 0.0).astype(BF16)

        def tie_body(u, seen):
            c = u // sub_per_chunk
            r0 = pl.multiple_of((u % sub_per_chunk) * LANES, LANES)
            keys = key_s[c, pl.ds(r0, LANES), :]
            tie = keys == thr
            tie_f = jnp.where(tie, 1.0, 0.0)
            rank = seen + _dot(lower, tie_f.astype(BF16))
            sel = ((keys > thr) | (tie & (rank <= wanted))) & (keys > KEY_NEG_INF)
            bias_s[c, pl.ds(r0, LANES), :] = jnp.where(sel, 0.0, -jnp.inf)
            return seen + jnp.sum(tie_f, axis=0, keepdims=True)

        lax.fori_loop(0, n_chunks * sub_per_chunk, tie_body, jnp.zeros((1, tq), F32))

    n_grp = N_HEADS // N_KV_HEADS
    acc0 = jnp.zeros((HEAD_DIM + ONES_ROWS, tq), F32)

    def attend(running_max):
        def group_body(g, den_min):
            q_tiles = [qs[g * n_grp + h] for h in range(n_grp)]

            def chunk_body(c, st):
                bias = bias_s[c]
                v_c = vt[g * nct + c]
                k_tiles = (ke[g * nct + c], ko[g * nct + c])
                logits = [_dot_nt(k_tiles[h % 2], q_tiles[h]) for h in range(n_grp)]
                tails = [lax.bitcast_convert_type(l[kc - 8:kc, :], I32) for l in logits[1:]]
                anchor = functools.reduce(jnp.bitwise_or, tails) & lax.shift_right_arithmetic(c, 31)
                bias0 = bias + lax.bitcast_convert_type(anchor, F32)[0:1, :]
                out = []
                for h in range(n_grp):
                    logit = logits[h] + (bias0 if h == 0 else bias)
                    if running_max:
                        m_run, acc = st[2 * h:2 * h + 2]
                        m_new = jnp.maximum(m_run, jnp.max(logit, axis=0, keepdims=True))
                        m_safe = jnp.where(m_new > -jnp.inf, m_new, 0.0)
                        p = jnp.exp2(logit - m_safe)
                        out += [m_new, jnp.exp2(m_run - m_safe) * acc + _dot(v_c, p.astype(BF16))]
                    else:
                        out.append(st[h] + _dot(v_c, jnp.exp2(logit).astype(BF16)))
                return tuple(out)

            if running_max:
                st = lax.fori_loop(0, n_chunks, chunk_body, (jnp.full((1, tq), -jnp.inf, F32), acc0) * n_grp)
                accs = st[1::2]
            else:
                accs = lax.fori_loop(0, n_chunks, chunk_body, (acc0,) * n_grp)
            for h in range(n_grp):
                r0 = pl.multiple_of(g * (n_grp * HEAD_DIM) + h * HEAD_DIM, HEAD_DIM)
                den = accs[h][HEAD_DIM:HEAD_DIM + 1, :]
                ot_s[pl.ds(r0, HEAD_DIM), :] = accs[h][0:HEAD_DIM, :] / den
                den_min = jnp.minimum(den_min, den)
            return den_min

        return lax.fori_loop(0, N_KV_HEADS, group_body, jnp.full((1, tq), jnp.inf, F32))

    den_min = attend(running_max=False)

    @pl.when(jnp.logical_not(jnp.min(den_min) >= MIN_DENOMINATOR))
    def _():
        attend(running_max=True)

    o = jnp.concatenate([ot_s[m * LANES:(m + 1) * LANES, :].T for m in range(N_HEADS // 2)], axis=1)
    o_ref[0] = _dot(o.astype(BF16), wout_ref[...]) + x_ref[0]


def _dsa_attn_prompt(q, qi, kw, k, v, x, w_out, *, tq, kc):
    b, t, d = x.shape
    n_sel = min(TOPK_MAX, t // 4)
    n_kv = N_KV_HEADS * HEAD_DIM
    nct = t // kc
    kern = functools.partial(_attn_prompt_kernel, tq=tq, t_len=t, kc=kc, n_sel=n_sel)

    def tile_spec(n):
        return pl.BlockSpec((1, tq, n), lambda i, j: (i, j, 0))

    def full_spec(n):
        return pl.BlockSpec((1, t, n), lambda i, j: (i, 0, 0), pipeline_mode=pl.Buffered(1))

    return pl.pallas_call(
        kern,
        grid=(b, t // tq),
        in_specs=[tile_spec(q.shape[-1]), tile_spec(qi.shape[-1]), tile_spec(LANES),
                  full_spec(n_kv), full_spec(n_kv), full_spec(LANES),
                  tile_spec(d), _const_spec(w_out.shape)],
        out_specs=tile_spec(d),
        out_shape=jax.ShapeDtypeStruct(x.shape, F32),
        scratch_shapes=[
            pltpu.VMEM((N_KV_HEADS * nct, kc, LANES), BF16), pltpu.VMEM((N_KV_HEADS * nct, kc, LANES), BF16),
            pltpu.VMEM((N_KV_HEADS * nct, HEAD_DIM + ONES_ROWS, kc), BF16),
            pltpu.VMEM((nct, kc, LANES), BF16), pltpu.VMEM((nct, kc, LANES), BF16),
            pltpu.VMEM((N_HEADS, tq, LANES), BF16),
            pltpu.VMEM((nct, kc, tq), I32), pltpu.VMEM((nct, kc, tq), F32),
            pltpu.VMEM((N_HEADS * HEAD_DIM, tq), F32),
            pltpu.VMEM((nct, kc, tq), I16),
            pltpu.VMEM((N_KV_HEADS, 1, LANES), F32),
        ],
        compiler_params=_cparams(2),
        name="dsa_attn_prompt",
    )(q, qi, kw, k, v, kw, x, w_out)


def _attn_sample_kernel(q_ref, qi_ref, kw_ref, k_ref, v_ref, ckt_ref, cvt_ref, ckit_ref, x_ref, wout_ref, o_ref,
                        kt16, vt16, qg_s, kn_s, vn_s, og_s, *, tq, past, n_sel):
    n_grp = N_HEADS // N_KV_HEADS
    lane = lax.broadcasted_iota(I32, (1, LANES), 1)
    lo = lane < HEAD_DIM

    def lower_half(x128, upper):
        return jnp.where(lo, pltpu.roll(x128, HEAD_DIM, 1) if upper else x128, 0.0)

    kt16[...] = ckt_ref[0].astype(BF16)
    vt16[...] = cvt_ref[0].astype(BF16)
    kit16 = ckit_ref[0].astype(BF16)
    kw = kw_ref[0]
    kw16 = kw.astype(BF16)
    k_new, v_new = k_ref[0], v_ref[0]
    q32 = q_ref[0].astype(F32)
    for g in range(N_KV_HEADS):
        tile = slice((g // 2) * LANES, (g // 2 + 1) * LANES)
        kn_s[g] = lower_half(k_new[:, tile], g % 2 == 1).astype(BF16)
        vn_s[g] = lower_half(v_new[:, tile], g % 2 == 1).astype(BF16)
        heads = [n_grp * g + h for h in range(n_grp)]
        qg_s[g] = jnp.concatenate(
            [lower_half(q32[:, (hd // 2) * LANES:(hd // 2 + 1) * LANES], hd % 2 == 1) for hd in heads],
            axis=0).astype(BF16)

    qi32 = qi_ref[0].astype(F32)
    qi_rows = jnp.concatenate(
        [lower_half(qi32[:, (hh // 2) * LANES:(hh // 2 + 1) * LANES], hh % 2 == 1) for hh in range(IDX_HEADS)],
        axis=0).astype(BF16)
    d_c = _dot(qi_rows[:, 0:IDX_DIM], kit16)
    d_n = _dot_nt(qi_rows, kw16)
    sc_c = jnp.zeros((tq, past), F32)
    sc_n = jnp.zeros((tq, tq), F32)
    for hh in range(IDX_HEADS):
        w_col = kw[:, IDX_DIM + hh:IDX_DIM + hh + 1]
        rows = slice(hh * tq, (hh + 1) * tq)
        sc_c = sc_c + w_col * jnp.maximum(d_c[rows, :], 0.0)
        sc_n = sc_n + w_col * jnp.maximum(d_n[rows, :], 0.0)
    q_chunk = (past + lax.broadcasted_iota(I32, (tq, 1), 0)) // CHUNK
    k_chunk = (past + lax.broadcasted_iota(I32, (1, tq), 1)) // CHUNK
    key_c = _order_key(sc_c)
    key_n = jnp.where(k_chunk <= q_chunk, _order_key(sc_n), KEY_NEG_INF)

    def count(pred):
        return (jnp.sum(pred(key_c).astype(I32), axis=-1, keepdims=True)
                + jnp.sum(pred(key_n).astype(I32), axis=-1, keepdims=True))

    def bit_step(i, thr):
        cand = thr ^ lax.shift_left(jnp.int32(1), 31 - i)
        return jnp.where(count(lambda k: k >= cand) >= n_sel, cand, thr)

    thr = lax.fori_loop(0, 32, bit_step, jnp.full((tq, 1), INT_MIN, I32))
    wanted = (n_sel - count(lambda k: k > thr)).astype(F32)

    def select(keys, seen):
        tie = keys == thr
        tie_f = jnp.where(tie, 1.0, 0.0)
        ranks = []
        for c0 in range(0, keys.shape[1], LANES):
            cw = min(LANES, keys.shape[1] - c0)
            r_i = lax.broadcasted_iota(I32, (cw, cw), 0)
            c_i = lax.broadcasted_iota(I32, (cw, cw), 1)
            upper = jnp.where(r_i <= c_i, 1.0, 0.0).astype(BF16)
            part = tie_f[:, c0:c0 + cw]
            ranks.append(seen + _dot(part.astype(BF16), upper))
            seen = seen + jnp.sum(part, axis=-1, keepdims=True)
        rank = ranks[0] if len(ranks) == 1 else jnp.concatenate(ranks, axis=1)
        sel = ((keys > thr) | (tie & (rank <= wanted))) & (keys > KEY_NEG_INF)
        return jnp.where(sel, 0.0, -jnp.inf), seen

    bias_c, seen = select(key_c, jnp.zeros((tq, 1), F32))
    bias_n, _ = select(key_n, seen)
    bias_c = jnp.concatenate([bias_c] * n_grp, axis=0)
    bias_n = jnp.concatenate([bias_n] * n_grp, axis=0)

    groups = range(N_KV_HEADS)
    rows = [slice(g * HEAD_DIM, (g + 1) * HEAD_DIM) for g in groups]
    l_c = [_dot(qg_s[g][:, 0:HEAD_DIM], kt16[rows[g], :]) for g in groups]
    l_n = [_dot_nt(qg_s[g], kn_s[g]) for g in groups]
    tail = lax.bitcast_convert_type(l_c[-1][:, past - LANES:past], I32) & lax.shift_right_arithmetic(pl.program_id(0), 31)
    l_n[0] = l_n[0] + lax.bitcast_convert_type(tail, F32)[:, 0:tq]
    for g in groups:
        lc, ln = l_c[g] + bias_c, l_n[g] + bias_n
        mx = jnp.maximum(jnp.max(lc, axis=-1, keepdims=True), jnp.max(ln, axis=-1, keepdims=True))
        p_c = jnp.exp2(lc - mx)
        p_n = jnp.exp2(ln - mx)
        den = jnp.sum(p_c, axis=-1, keepdims=True) + jnp.sum(p_n, axis=-1, keepdims=True)
        pv = (_dot_nt(p_c.astype(BF16), vt16[rows[g], :])
              + _dot(p_n.astype(BF16), vn_s[g])[:, 0:HEAD_DIM])
        og_s[g] = pv / den

    tiles = []
    for m in range(N_HEADS // 2):
        pair = []
        for hd in (2 * m, 2 * m + 1):
            h = hd % n_grp
            pair.append(og_s[hd // n_grp, h * tq:(h + 1) * tq, :])
        tiles.append(jnp.concatenate(pair, axis=1))
    o = jnp.concatenate(tiles, axis=1).astype(BF16)
    o_ref[0] = _dot(o, wout_ref[...]) + x_ref[0]


def _dsa_attn_sample(q, qi, kw, k, v, cache_kt, cache_vt, cache_kit, x, w_out):
    b, t, d = x.shape
    past = cache_kt.shape[-1]
    n_sel = min(TOPK_MAX, (past + t) // 4)
    n_kv = N_KV_HEADS * HEAD_DIM
    n_grp = N_HEADS // N_KV_HEADS
    kern = functools.partial(_attn_sample_kernel, tq=t, past=past, n_sel=n_sel)

    def spec(rows, n):
        return pl.BlockSpec((1, rows, n), lambda i: (i, 0, 0))

    return pl.pallas_call(
        kern,
        grid=(b,),
        in_specs=[spec(t, q.shape[-1]), spec(t, qi.shape[-1]), spec(t, LANES), spec(t, n_kv), spec(t, n_kv),
                  spec(n_kv, past), spec(n_kv, past), spec(IDX_DIM, past),
                  spec(t, d), _const_spec(w_out.shape)],
        out_specs=spec(t, d),
        out_shape=jax.ShapeDtypeStruct(x.shape, F32),
        scratch_shapes=[
            pltpu.VMEM((n_kv, past), BF16), pltpu.VMEM((n_kv, past), BF16),
            pltpu.VMEM((N_KV_HEADS, n_grp * t, LANES), BF16),
            pltpu.VMEM((N_KV_HEADS, t, LANES), BF16), pltpu.VMEM((N_KV_HEADS, t, LANES), BF16),
            pltpu.VMEM((N_KV_HEADS, n_grp * t, HEAD_DIM), F32),
        ],
        compiler_params=_cparams(1),
        name="dsa_attn_sample",
    )(q, qi, kw, k, v, cache_kt, cache_vt, cache_kit, x, w_out)


def _tiling(b, t):
    if t >= 256:
        return dict(bb=1, tt=256, blk=128, sub=32)
    bb = max(1, min(b, 128 // t))
    return dict(bb=bb, tt=t, blk=t, sub=t)


def _trunk(x, hg_state, conv_state, cache, p):
    b, t, d = x.shape
    tl = _tiling(b, t)
    bb, tt = tl["bb"], tl["tt"]
    x, s_new = _hgrn_mixer(x, hg_state[0], p["norm_mix"][0], p["hg_w_in"][0], p["lower_bounds"][0],
                           p["hg_norm"][0], p["hg_w_out"][0], **dict(tl, tt=512 if t % 512 == 0 else tt))
    x, cs0 = _conv_ffn(x, conv_state[0], p["norm_ffn"][0], p["ffn_w_in"][0], p["ffn_conv_w"][0],
                       p["ffn_conv_b"][0], p["ffn_w_down"][0], p["norm_final"], bb=bb, tt=tt, final_norm=False)
    past = 0 if cache is None else cache[0].shape[1]
    tables = _rope_tables(past + jnp.arange(t))
    q, k, v, qi, kw = _dsa_proj(x, p["norm_mix"][1], p["ds_w_in"][0], tables, p["ds_kln_w"][0],
                                p["ds_kln_b"][0], bb=bb, tt=tt)
    if cache is None:
        x = _dsa_attn_prompt(q, qi, kw, k, v, x, p["ds_w_out"][0], tq=min(t, 512), kc=min(t, 512))
    else:
        ck, cv, cki = cache
        def keys_last(c):
            return jnp.swapaxes(c.reshape(b, past, -1), 1, 2)

        x = _dsa_attn_sample(q, qi, kw, k, v, keys_last(ck), keys_last(cv), keys_last(cki), x, p["ds_w_out"][0])
    x, cs1 = _conv_ffn(x, conv_state[1], p["norm_ffn"][1], p["ffn_w_in"][1], p["ffn_conv_w"][1],
                       p["ffn_conv_b"][1], p["ffn_w_down"][1], p["norm_final"], bb=bb, tt=tt, final_norm=True)
    k = k.reshape(1, b, t, N_KV_HEADS, HEAD_DIM)
    v = v.reshape(1, b, t, N_KV_HEADS, HEAD_DIM)
    ki = kw[:, :, :IDX_DIM].reshape(1, b, t, IDX_DIM)
    return x, s_new[None], jnp.stack([cs0, cs1]), k, v, ki


def kernel(x_prompt, x_sample, cache_k, cache_v, cache_kidx, state_hgrn, state_conv, norm_mix, norm_ffn, norm_final, hg_w_in, hg_lb, hg_norm, hg_w_out, ds_w_in, ds_kln_w, ds_kln_b, ds_w_out, ffn_w_in, ffn_conv_w, ffn_conv_b, ffn_w_down):
    dsa_in = ds_w_in.shape[-1]
    dsa_pad = (-dsa_in) % LANES
    p = dict(
        norm_mix=norm_mix, norm_ffn=norm_ffn, norm_final=norm_final,
        hg_w_in=hg_w_in.astype(BF16), hg_norm=hg_norm, hg_w_out=hg_w_out.astype(BF16),
        lower_bounds=jnp.cumsum(jax.nn.softmax(hg_lb.astype(F32), axis=0), axis=0),
        ds_w_in=jnp.pad(ds_w_in, ((0, 0), (0, 0), (0, dsa_pad))).astype(BF16),
        ds_kln_w=ds_kln_w, ds_kln_b=ds_kln_b, ds_w_out=ds_w_out.astype(BF16),
        ffn_w_in=ffn_w_in.astype(BF16), ffn_conv_w=ffn_conv_w, ffn_conv_b=ffn_conv_b,
        ffn_w_down=ffn_w_down.astype(BF16),
    )
    b = x_prompt.shape[0]
    hg0 = jnp.zeros((state_hgrn.shape[0], b) + state_hgrn.shape[2:], F32)
    conv0 = jnp.zeros((state_conv.shape[0], b) + state_conv.shape[2:], F32)
    y_p, hg_p, conv_p, k_p, v_p, ki_p = _trunk(x_prompt, hg0, conv0, None, p)
    y_s, hg_s, conv_s, k_s, v_s, ki_s = _trunk(x_sample, state_hgrn, state_conv,
                                               (cache_k[0], cache_v[0], cache_kidx[0]), p)
    return (y_p, y_s, k_p, v_p, ki_p, hg_p, conv_p, k_s, v_s, ki_s, hg_s, conv_s)
```

```python
import functools

import jax
import jax.numpy as jnp
from jax import lax
from jax.experimental import pallas as pl
from jax.experimental.pallas import tpu as pltpu

F32 = jnp.float32
BF16 = jnp.bfloat16
I32 = jnp.int32
I16 = jnp.int16

CHUNK = 64
EPS = 1e-6
HG_HEADS = 8
HG_KDIM = 128
N_HEADS = 16
HEAD_DIM = 64
N_KV_HEADS = 4
IDX_HEADS = 8
IDX_DIM = 64
TOPK_MAX = 256
ROPE_THETA = 500000.0
CONV_W = 3

LANES = 128
VMEM_LIMIT = 56 * 1024 * 1024
EXP_CLAMP = 80.0
LOG2_E = 1.4426950408889634
ONES_ROWS = 16
BOUND_SLACK = 1.02
MIN_DENOMINATOR = 2.0 ** -60
KEY_NEG_INF = -2139095041
INT_MIN = -2147483648


def _cparams(n_axes, flags=None):
    return pltpu.CompilerParams(dimension_semantics=("arbitrary",) * n_axes,
                                vmem_limit_bytes=VMEM_LIMIT, flags=flags)


def _const_spec(shape):
    nd = len(shape)
    return pl.BlockSpec(shape, lambda *_: (0,) * nd, pipeline_mode=pl.Buffered(1))


def _rms(x, g):
    return x * lax.rsqrt(jnp.mean(x * x, axis=-1, keepdims=True) + EPS) * g


def _silu(x):
    return x * jax.nn.sigmoid(x)


def _dot(a, b):
    return jnp.dot(a, b, preferred_element_type=F32)


def _dot_nt(a, b):
    return lax.dot_general(a, b, (((1,), (1,)), ((), ())), preferred_element_type=F32)


def _dot_tn(a, b):
    return lax.dot_general(a, b, (((0,), (0,)), ((), ())), preferred_element_type=F32)


def _hgrn_kernel(x_ref, s0_ref, g_ref, win_ref, lb_ref, hn_ref, wout_ref, o_ref, s_ref,
                 q_s, k_s, g_s, v_s, gate_s, on_s, st_s, o_s, *, bb, tt, blk, sub):
    j = pl.program_id(1)
    rows = bb * tt
    d = x_ref.shape[-1]
    fdim = HG_HEADS * HG_KDIM
    n_sub = blk // sub

    @pl.when(j == 0)
    def _():
        for bi in range(bb):
            for hh in range(HG_HEADS):
                st_s[bi * HG_HEADS + hh] = s0_ref[bi, hh].T

    x = x_ref[...].reshape(rows, d)
    h = _rms(x, g_ref[...]).astype(BF16)
    lb = lb_ref[...]
    fg = lb + (1.0 - lb) * jax.nn.sigmoid(_dot(h, win_ref[:, fdim:2 * fdim]))
    k_s[...] = 1.0 - fg
    g_s[...] = jnp.log(fg)
    q_s[...] = _silu(_dot(h, win_ref[:, 0:fdim]))
    gate_s[...] = _silu(_dot(h, win_ref[:, 2 * fdim + d:]))
    v_s[...] = _dot(h, win_ref[:, 2 * fdim:2 * fdim + d])

    r_i = lax.broadcasted_iota(I32, (blk, blk), 0)
    c_i = lax.broadcasted_iota(I32, (blk, blk), 1)
    causal = c_i <= r_i
    tril = jnp.where(causal, 1.0, 0.0).astype(BF16)
    hn = hn_ref[...]
    blocks_per_stream = tt // blk

    def block_body(idx, carry):
        r0 = pl.multiple_of(idx * blk, blk)
        bi = idx // blocks_per_stream
        lg = g_s[pl.ds(r0, blk), :]
        lg_hi = lg.astype(BF16)
        rem = lg - lg_hi.astype(F32)
        lg_mid = rem.astype(BF16)
        lg_lo = (rem - lg_mid.astype(F32)).astype(BF16)
        g_all = _dot(tril, lg_hi) + _dot(tril, lg_mid) + _dot(tril, lg_lo)
        run_zero = lax.shift_right_arithmetic(idx, 31)

        def anchor(x):
            tail = lax.bitcast_convert_type(x[x.shape[0] - 8:, :], I32) & run_zero
            return lax.bitcast_convert_type(tail, F32)[0:1, :]

        heads = [slice(hh * HG_KDIM, (hh + 1) * HG_KDIM) for hh in range(HG_HEADS)]
        qcat, kcat, qfull, kdec_last, v16, decay = [], [], [], [], [], []
        for sl in heads:
            gc = g_all[:, sl]
            q = q_s[pl.ds(r0, blk), sl]
            kk = k_s[pl.ds(r0, blk), sl]
            refs = [jnp.zeros((1, HG_KDIM), F32)] + [gc[s * sub - 1:s * sub, :] for s in range(1, n_sub)]
            q_parts, k_parts = [], []
            for s in range(n_sub):
                lo_r, hi_r = s * sub, (s + 1) * sub
                qg = q[lo_r:hi_r, :] * jnp.exp(gc[lo_r:hi_r, :] - refs[s])
                pieces = [qg]
                if lo_r:
                    pieces.insert(0, jnp.zeros((lo_r, HG_KDIM), F32))
                if blk - hi_r:
                    pieces.append(jnp.zeros((blk - hi_r, HG_KDIM), F32))
                q_parts.append(pieces[0] if len(pieces) == 1 else jnp.concatenate(pieces, axis=0))
                kd_s = kk[0:hi_r, :] * jnp.exp(jnp.minimum(refs[s] - gc[0:hi_r, :], EXP_CLAMP))
                k_parts.append(kd_s if hi_r == blk else
                               jnp.concatenate([kd_s, jnp.zeros((blk - hi_r, HG_KDIM), F32)], axis=0))
            qcat.append(jnp.concatenate(q_parts, axis=1).astype(BF16))
            kcat.append(jnp.concatenate(k_parts, axis=1).astype(BF16))
            qfull.append((q * jnp.exp(gc)).astype(BF16))
            g_last = gc[blk - 1:blk, :]
            kdec_last.append((kk * jnp.exp(g_last - gc)).astype(BF16))
            decay.append(jnp.exp(g_last))
            v16.append(v_s[pl.ds(r0, blk), sl].astype(BF16))
        a_list = [_dot_nt(qcat[hh], kcat[hh]) for hh in range(HG_HEADS)]
        s_inc = [_dot_tn(v16[hh], kdec_last[hh]) for hh in range(HG_HEADS)]
        a_list[0] = a_list[0] + anchor(a_list[-1])
        o_list = []
        for hh in range(HG_HEADS):
            a = jnp.where(causal, a_list[hh], 0.0).astype(BF16)
            st = st_s[bi * HG_HEADS + hh]
            o_list.append(_dot(a, v16[hh]) + _dot_nt(qfull[hh], st.astype(BF16)))
            st_s[bi * HG_HEADS + hh] = st * decay[hh] + s_inc[hh]
        o_list[0] = o_list[0] + anchor(o_list[-1])
        for hh, sl in enumerate(heads):
            on = _rms(o_list[hh], hn) * gate_s[pl.ds(r0, blk), sl]
            on_s[pl.ds(r0, blk), sl] = on.astype(BF16)
        return carry

    sub_sums = jnp.sum(g_s[...].reshape(rows // sub, sub, fdim), axis=1)
    blocked_ok = jnp.min(sub_sums) >= -EXP_CLAMP

    @pl.when(blocked_ok)
    def _():
        lax.fori_loop(0, rows // blk, block_body, 0, unroll=4)

    @pl.when(jnp.logical_not(blocked_ok))
    def _():
        grp = 16
        o_s[...] = jnp.zeros(o_s.shape, F32)
        in_grp = lax.broadcasted_iota(I32, (grp, 1), 0)

        def frame_body(r, carry):
            r0 = pl.multiple_of((r // grp) * grp, grp)
            bi = r // tt
            this = in_grp == (r % grp)
            for hh in range(HG_HEADS):
                sl = slice(hh * HG_KDIM, (hh + 1) * HG_KDIM)

                def only(ref):
                    return jnp.where(this, ref[pl.ds(r0, grp), sl], 0.0)

                forget = jnp.exp(jnp.sum(only(g_s), axis=0, keepdims=True))
                st = (st_s[bi * HG_HEADS + hh] * forget
                      + _dot_tn(only(v_s).astype(BF16), only(k_s).astype(BF16)))
                st_s[bi * HG_HEADS + hh] = st
                o_s[pl.ds(r0, grp), sl] += _dot_nt(only(q_s).astype(BF16), st.astype(BF16))
            return carry

        lax.fori_loop(0, rows, frame_body, 0)
        for hh in range(HG_HEADS):
            sl = slice(hh * HG_KDIM, (hh + 1) * HG_KDIM)
            on_s[:, sl] = (_rms(o_s[:, sl], hn) * gate_s[:, sl]).astype(BF16)

    out = _dot(on_s[...], wout_ref[...]) + x
    o_ref[...] = out.reshape(bb, tt, d)

    @pl.when(j == pl.num_programs(1) - 1)
    def _():
        for bi in range(bb):
            for hh in range(HG_HEADS):
                s_ref[bi, hh] = st_s[bi * HG_HEADS + hh].T


def _hgrn_mixer(x, s0, norm_g, w_in, lb, hn, w_out, *, bb, tt, blk, sub):
    b, t, d = x.shape
    fdim = HG_HEADS * HG_KDIM
    rows = bb * tt
    kern = functools.partial(_hgrn_kernel, bb=bb, tt=tt, blk=blk, sub=sub)
    return pl.pallas_call(
        kern,
        grid=(b // bb, t // tt),
        in_specs=[
            pl.BlockSpec((bb, tt, d), lambda i, j: (i, j, 0)),
            pl.BlockSpec((bb, HG_HEADS, HG_KDIM, d // HG_HEADS), lambda i, j: (i, 0, 0, 0)),
            _const_spec((1, d)),
            _const_spec(w_in.shape),
            _const_spec((1, fdim)),
            _const_spec((1, d // HG_HEADS)),
            _const_spec(w_out.shape),
        ],
        out_specs=[
            pl.BlockSpec((bb, tt, d), lambda i, j: (i, j, 0)),
            pl.BlockSpec((bb, HG_HEADS, HG_KDIM, d // HG_HEADS), lambda i, j: (i, 0, 0, 0)),
        ],
        out_shape=[jax.ShapeDtypeStruct(x.shape, F32), jax.ShapeDtypeStruct(s0.shape, F32)],
        scratch_shapes=[
            pltpu.VMEM((rows, fdim), F32),
            pltpu.VMEM((rows, fdim), F32),
            pltpu.VMEM((rows, fdim), F32),
            pltpu.VMEM((rows, d), F32),
            pltpu.VMEM((rows, d), F32),
            pltpu.VMEM((rows, d), BF16),
            pltpu.VMEM((bb * HG_HEADS, d // HG_HEADS, HG_KDIM), F32),
            pltpu.VMEM((rows, d), F32),
        ],
        compiler_params=_cparams(2),
        name="hgrn_mixer",
    )(x, s0, norm_g.reshape(1, d), w_in, lb.reshape(1, fdim), hn.reshape(1, -1), w_out)


CONV_HEAD = 8


def _ffn_kernel(x_ref, cs_ref, g_ref, win_ref, cw_ref, cb_ref, wdn_ref, fg_ref, o_ref, ns_ref,
                a_s, *, bb, tt, final_norm):
    j = pl.program_id(1)
    rows = bb * tt
    d = x_ref.shape[-1]
    dff = cw_ref.shape[-1]
    hist = CONV_W - 1

    @pl.when(j == 0)
    def _():
        a_s[:, CONV_HEAD - hist:CONV_HEAD, :] = cs_ref[...]

    x = x_ref[...].reshape(rows, d)
    h = _rms(x, g_ref[...]).astype(BF16)
    a = _dot(h, win_ref[:, 0:dff])
    u = _dot(h, win_ref[:, dff:])
    a_s[:, CONV_HEAD:CONV_HEAD + tt, :] = a.reshape(bb, tt, dff)
    c = jnp.broadcast_to(cb_ref[...].reshape(1, 1, dff), (bb, tt, dff))
    for w in range(CONV_W):
        lo = CONV_HEAD - hist + w
        c = c + a_s[:, lo:lo + tt, :] * cw_ref[w:w + 1, :].reshape(1, 1, dff)
    new_state = a_s[:, CONV_HEAD + tt - hist:CONV_HEAD + tt, :]
    a_s[:, CONV_HEAD - hist:CONV_HEAD, :] = new_state
    ns_ref[...] = new_state
    act = (_silu(c).reshape(rows, dff) * u).astype(BF16)
    y = _dot(act, wdn_ref[...]) + x
    if final_norm:
        y = _rms(y, fg_ref[...])
    o_ref[...] = y.reshape(bb, tt, d)


def _conv_ffn(x, conv_state, norm_g, w_in, conv_w, conv_b, w_down, final_g, *, bb, tt, final_norm):
    b, t, d = x.shape
    dff = conv_w.shape[-1]
    kern = functools.partial(_ffn_kernel, bb=bb, tt=tt, final_norm=final_norm)
    return pl.pallas_call(
        kern,
        grid=(b // bb, t // tt),
        in_specs=[
            pl.BlockSpec((bb, tt, d), lambda i, j: (i, j, 0)),
            pl.BlockSpec((bb, CONV_W - 1, dff), lambda i, j: (i, 0, 0)),
            _const_spec((1, d)),
            _const_spec(w_in.shape),
            _const_spec(conv_w.shape),
            _const_spec((1, dff)),
            _const_spec(w_down.shape),
            _const_spec((1, d)),
        ],
        out_specs=[
            pl.BlockSpec((bb, tt, d), lambda i, j: (i, j, 0)),
            pl.BlockSpec((bb, CONV_W - 1, dff), lambda i, j: (i, 0, 0)),
        ],
        out_shape=[jax.ShapeDtypeStruct(x.shape, F32), jax.ShapeDtypeStruct(conv_state.shape, F32)],
        scratch_shapes=[pltpu.VMEM((bb, CONV_HEAD + tt, dff), F32)],
        compiler_params=_cparams(2),
        name="conv_ffn",
    )(x, conv_state, norm_g.reshape(1, d), w_in, conv_w, conv_b.reshape(1, dff), w_down,
      final_g.reshape(1, d))


def _rope_tables(pos):
    rot = HEAD_DIM // 4
    half = rot // 2
    inv_freq = ROPE_THETA ** (-jnp.arange(half, dtype=F32) / half)
    ang = pos.astype(F32)[:, None] * inv_freq[None, :]
    cos, sin = jnp.cos(ang), jnp.sin(ang)
    t = pos.shape[0]
    pad = HEAD_DIM - rot
    one = jnp.ones((t, pad), F32)
    zero_h = jnp.zeros((t, half), F32)
    zero_p = jnp.zeros((t, pad), F32)
    c64 = jnp.concatenate([cos, cos, one], axis=1)
    s1_64 = jnp.concatenate([zero_h, sin, zero_p], axis=1)
    s2_64 = jnp.concatenate([-sin, zero_h, zero_p], axis=1)
    ident = jnp.ones((t, HEAD_DIM), F32)
    zero64 = jnp.zeros((t, HEAD_DIM), F32)
    return jnp.stack([
        jnp.concatenate([c64, c64], axis=1), jnp.concatenate([s1_64, s1_64], axis=1),
        jnp.concatenate([s2_64, s2_64], axis=1),
        jnp.concatenate([c64, ident], axis=1), jnp.concatenate([s1_64, zero64], axis=1),
        jnp.concatenate([s2_64, zero64], axis=1)])


def _dsa_proj_kernel(x_ref, g_ref, w_ref, tab_ref, lnw_ref, lnb_ref,
                     q_ref, k_ref, v_ref, qi_ref, kw_ref, *, bb, tt):
    rows = bb * tt
    d = x_ref.shape[-1]
    half = HEAD_DIM // 8
    o1 = N_HEADS * HEAD_DIM
    o2 = o1 + N_KV_HEADS * HEAD_DIM
    o3 = o2 + N_KV_HEADS * HEAD_DIM
    o4 = o3 + IDX_HEADS * IDX_DIM

    def tab(i):
        t = tab_ref[i]
        return jnp.broadcast_to(t[None], (bb, tt, LANES)).reshape(rows, LANES)

    def rope(y, base):
        cos, s1, s2 = tab(base), tab(base + 1), tab(base + 2)
        tiles = []
        for m in range(y.shape[1] // LANES):
            yt = y[:, m * LANES:(m + 1) * LANES]
            tiles.append(yt * cos + pltpu.roll(yt, half, 1) * s1 + pltpu.roll(yt, LANES - half, 1) * s2)
        return tiles[0] if len(tiles) == 1 else jnp.concatenate(tiles, axis=1)

    x = x_ref[...].reshape(rows, d)
    h = _rms(x, g_ref[...]).astype(BF16)
    t = _dot(h, w_ref[:, o4:o4 + LANES])
    lane = lax.broadcasted_iota(I32, (1, LANES), 1)
    is_ki = lane < IDX_DIM
    mu = jnp.sum(jnp.where(is_ki, t, 0.0), axis=-1, keepdims=True) / IDX_DIM
    cen = jnp.where(is_ki, t - mu, 0.0)
    var = jnp.sum(cen * cen, axis=-1, keepdims=True) / IDX_DIM
    ki = rope(cen * lax.rsqrt(var + EPS) * lnw_ref[...] + lnb_ref[...], 3)
    wi = t * ((IDX_HEADS * IDX_DIM) ** -0.5)
    kw = jnp.where(is_ki, ki, jnp.where(lane < IDX_DIM + IDX_HEADS, wi, 0.0))
    kw_ref[...] = kw.reshape(bb, tt, LANES)
    qi_ref[...] = rope(_dot(h, w_ref[:, o3:o4]), 0).astype(BF16).reshape(bb, tt, o4 - o3)
    k_ref[...] = rope(_dot(h, w_ref[:, o1:o2]), 0).reshape(bb, tt, o2 - o1)
    q = rope(_dot(h, w_ref[:, 0:o1]), 0) * (HEAD_DIM ** -0.5 * LOG2_E)
    q_ref[...] = q.astype(BF16).reshape(bb, tt, o1)
    v_ref[...] = _dot(h, w_ref[:, o2:o3]).reshape(bb, tt, o3 - o2)


def _dsa_proj(x, norm_g, w_pad, tables, ln_w, ln_b, *, bb, tt):
    b, t, d = x.shape
    n_q = N_HEADS * HEAD_DIM
    n_kv = N_KV_HEADS * HEAD_DIM
    n_qi = IDX_HEADS * IDX_DIM
    pad = jnp.zeros((LANES - IDX_DIM,), F32)
    lnw = jnp.concatenate([ln_w, pad]).reshape(1, LANES)
    lnb = jnp.concatenate([ln_b, pad]).reshape(1, LANES)
    kern = functools.partial(_dsa_proj_kernel, bb=bb, tt=tt)

    def row_spec(n):
        return pl.BlockSpec((bb, tt, n), lambda i, j: (i, j, 0))

    return pl.pallas_call(
        kern,
        grid=(b // bb, t // tt),
        in_specs=[
            row_spec(d),
            _const_spec((1, d)),
            _const_spec(w_pad.shape),
            pl.BlockSpec((6, tt, LANES), lambda i, j: (0, j, 0)),
            _const_spec((1, LANES)),
            _const_spec((1, LANES)),
        ],
        out_specs=[row_spec(n_q), row_spec(n_kv), row_spec(n_kv), row_spec(n_qi), row_spec(LANES)],
        out_shape=[
            jax.ShapeDtypeStruct((b, t, n_q), BF16),
            jax.ShapeDtypeStruct((b, t, n_kv), F32),
            jax.ShapeDtypeStruct((b, t, n_kv), F32),
            jax.ShapeDtypeStruct((b, t, n_qi), BF16),
            jax.ShapeDtypeStruct((b, t, LANES), F32),
        ],
        compiler_params=_cparams(2),
        name="dsa_proj",
    )(x, norm_g.reshape(1, d), w_pad, tables, lnw, lnb)


def _order_key(score):
    score = jnp.where(score == 0.0, 0.0, score)
    bits = lax.bitcast_convert_type(score, I32)
    return bits ^ ((bits >> 31) & 0x7FFFFFFF)


def _attn_prompt_kernel(q_ref, qi_ref, wq_ref, k_ref, v_ref, kw_ref, x_ref, wout_ref, o_ref,
                        ke, ko, vt, kie, kio, qs, key_s, bias_s, ot_s, half_s, kmax_s, thr_s, *, tq, t_len, kc, n_sel):
    j = pl.program_id(1)
    nct = t_len // kc
    sub_per_chunk = kc // LANES
    lane = lax.broadcasted_iota(I32, (1, LANES), 1)
    lo = lane < HEAD_DIM

    unit_hi = jnp.where(lane == HEAD_DIM, 1.0, 0.0)
    unit_lo = jnp.where(lane == 0, 1.0, 0.0)

    r_i = lax.broadcasted_iota(I32, (LANES, LANES), 0)
    c_i = lax.broadcasted_iota(I32, (LANES, LANES), 1)
    half_sums = jnp.where(((r_i < HEAD_DIM) & (c_i == HEAD_DIM)) | ((r_i >= HEAD_DIM) & (c_i == 0)),
                          1.0, 0.0).astype(BF16)

    @pl.when(j == 0)
    def _():
        k_sq_max = [jnp.zeros((1, LANES), F32) for _ in range(N_KV_HEADS // 2)]
        for c in range(nct):
            rows = slice(c * kc, (c + 1) * kc)
            k_c = k_ref[0, rows, :].astype(BF16).astype(F32)
            for m in range(N_KV_HEADS // 2):
                tile = k_c[:, m * LANES:(m + 1) * LANES]
                rolled = pltpu.roll(tile, HEAD_DIM, 1)
                ke[(2 * m) * nct + c] = jnp.where(lo, tile, unit_hi).astype(BF16)
                ko[(2 * m) * nct + c] = jnp.where(lo, unit_lo, rolled).astype(BF16)
                ke[(2 * m + 1) * nct + c] = jnp.where(lo, rolled, unit_hi).astype(BF16)
                ko[(2 * m + 1) * nct + c] = jnp.where(lo, unit_lo, tile).astype(BF16)
                norm2 = _dot((tile * tile).astype(BF16), half_sums)
                k_sq_max[m] = jnp.maximum(k_sq_max[m], jnp.max(norm2, axis=0, keepdims=True))
            v_t = v_ref[0, rows, :].T
            for g in range(N_KV_HEADS):
                vt[g * nct + c, 0:HEAD_DIM, :] = v_t[g * HEAD_DIM:(g + 1) * HEAD_DIM, :].astype(BF16)
                vt[g * nct + c, HEAD_DIM:HEAD_DIM + ONES_ROWS, :] = jnp.ones((ONES_ROWS, kc), BF16)
            kw_c = kw_ref[0, rows, :]
            kie[c] = jnp.where(lo, kw_c, 0.0).astype(BF16)
            kio[c] = jnp.where(lo, 0.0, pltpu.roll(kw_c, HEAD_DIM, 1)).astype(BF16)
        for g in range(N_KV_HEADS):
            at = HEAD_DIM if g % 2 == 0 else 0
            k_sq = jnp.max(jnp.where(lane == at, k_sq_max[g // 2], 0.0), axis=-1, keepdims=True)
            kmax_s[g] = jnp.broadcast_to(jnp.sqrt(k_sq), (1, LANES))

    for m in range(N_HEADS // 2):
        tile = q_ref[0, :, m * LANES:(m + 1) * LANES].astype(F32)
        k_norm = kmax_s[(2 * m) // (N_HEADS // N_KV_HEADS)]
        bound = jnp.sqrt(_dot((tile * tile).astype(BF16), half_sums)) * (k_norm * -BOUND_SLACK)
        qs[2 * m] = jnp.where(lo, tile, jnp.where(lane == HEAD_DIM, bound, 0.0)).astype(BF16)
        qs[2 * m + 1] = jnp.where(lo, jnp.where(lane == 0, bound, 0.0), tile).astype(BF16)

    n_chunks = ((j + 1) * tq + kc - 1) // kc
    w_t = wq_ref[0].T
    w_rows = [w_t[IDX_DIM + hh:IDX_DIM + hh + 1, :] for hh in range(IDX_HEADS)]
    qi = qi_ref[0]
    q_chunk = (j * tq + lax.broadcasted_iota(I32, (1, tq), 1)) // CHUNK

    def score_body(c, carry):
        kie_c, kio_c = kie[c], kio[c]
        sc = jnp.zeros((kc, tq), F32)
        for m in range(IDX_HEADS // 2):
            qt = qi[:, m * LANES:(m + 1) * LANES]
            sc = (sc + w_rows[2 * m] * jnp.maximum(_dot_nt(kie_c, qt), 0.0)
                  + w_rows[2 * m + 1] * jnp.maximum(_dot_nt(kio_c, qt), 0.0))
        k_chunk = (c * kc + lax.broadcasted_iota(I32, (kc, 1), 0)) // CHUNK
        keys = jnp.where(k_chunk <= q_chunk, _order_key(sc), KEY_NEG_INF)
        key_s[c] = keys
        half_s[c] = lax.shift_right_arithmetic(keys, 16).astype(I16)
        return carry

    lax.fori_loop(0, n_chunks, score_body, 0)

    def search16(need, n_static):
        def count_ge(cand):
            acc = jnp.zeros((16, tq), I16)
            for c in range(n_static):
                hit = (half_s[c] >= cand).astype(I16)
                for r in range(0, kc, 16):
                    acc = acc + hit[r:r + 16, :]
            return jnp.sum(acc.astype(I32), axis=0, keepdims=True)

        def bit_step(i, t16):
            cand = t16 ^ lax.shift_left(jnp.int32(1), 15 - i)
            cand = lax.shift_right_arithmetic(lax.shift_left(cand, 16), 16)
            return jnp.where(count_ge(cand.astype(I16)) >= need, cand, t16)

        return lax.fori_loop(0, 16, bit_step, jnp.full((1, tq), -32768, I32))

    def find_threshold(n_static):
        thr_hi = search16(n_sel, n_static)
        n_above = jnp.zeros((8, tq), I32)
        for c in range(n_static):
            keys = key_s[c]
            hi = lax.shift_right_arithmetic(keys, 16)
            low = (keys & 0xFFFF) - 32768
            half_s[c] = jnp.where(hi == thr_hi, low, -32768).astype(I16)
            n_above = n_above + jnp.sum((hi > thr_hi).astype(I32).reshape(kc // 8, 8, tq), axis=0)
        thr_lo = search16(n_sel - jnp.sum(n_above, axis=0, keepdims=True), n_static)
        thr_s[...] = jnp.broadcast_to(lax.shift_left(thr_hi, 16) | (thr_lo + 32768), thr_s.shape)

    for n_static in range(1, nct + 1):
        pl.when(n_chunks == n_static)(functools.partial(find_threshold, n_static))
    thr = thr_s[0:1, :]

    def bias_body(c, acc):
        keys = key_s[c]
        ge = keys >= thr
        bias_s[c] = jnp.where(ge & (keys > KEY_NEG_INF), 0.0, -jnp.inf)
        fold = lambda hit: jnp.sum(hit.astype(I32).reshape(kc // 8, 8, tq), axis=0)
        return acc[0] + fold(ge), acc[1] + fold(keys > thr)

    zero8 = jnp.zeros((8, tq), I32)
    n_ge, n_gt = [jnp.sum(a, axis=0, keepdims=True)
                  for a in lax.fori_loop(0, n_chunks, bias_body, (zero8, zero8))]

    has_excess = jnp.max(jnp.where(thr > KEY_NEG_INF, n_ge - n_sel, 0)) > 0

    @pl.when(has_excess)
    def _():
        wanted = (n_sel - n_gt).astype(F32)
        r_i = lax.broadcasted_iota(I32, (LANES, LANES), 0)
        c_i = lax.broadcasted_iota(I32, (LANES, LANES), 1)
        lower = jnp.where(c_i <= r_i, 1.0, 0.0).astype(BF16)

        def tie_body(u, seen):
            c = u // sub_per_chunk
            r0 = pl.multiple_of((u % sub_per_chunk) * LANES, LANES)
            keys = key_s[c, pl.ds(r0, LANES), :]
            tie = keys == thr
            tie_f = jnp.where(tie, 1.0, 0.0)
            rank = seen + _dot(lower, tie_f.astype(BF16))
            sel = ((keys > thr) | (tie & (rank <= wanted))) & (keys > KEY_NEG_INF)
            bias_s[c, pl.ds(r0, LANES), :] = jnp.where(sel, 0.0, -jnp.inf)
            return seen + jnp.sum(tie_f, axis=0, keepdims=True)

        lax.fori_loop(0, n_chunks * sub_per_chunk, tie_body, jnp.zeros((1, tq), F32))

    n_grp = N_HEADS // N_KV_HEADS
    acc0 = jnp.zeros((HEAD_DIM + ONES_ROWS, tq), F32)

    def attend(running_max):
        def group_body(g, den_min):
            q_tiles = [qs[g * n_grp + h] for h in range(n_grp)]

            def chunk_body(c, st):
                bias = bias_s[c]
                v_c = vt[g * nct + c]
                k_tiles = (ke[g * nct + c], ko[g * nct + c])
                logits = [_dot_nt(k_tiles[h % 2], q_tiles[h]) for h in range(n_grp)]
                tails = [lax.bitcast_convert_type(l[kc - 8:kc, :], I32) for l in logits[1:]]
                anchor = functools.reduce(jnp.bitwise_or, tails) & lax.shift_right_arithmetic(c, 31)
                bias0 = bias + lax.bitcast_convert_type(anchor, F32)[0:1, :]
                out = []
                for h in range(n_grp):
                    logit = logits[h] + (bias0 if h == 0 else bias)
                    if running_max:
                        m_run, acc = st[2 * h:2 * h + 2]
                        m_new = jnp.maximum(m_run, jnp.max(logit, axis=0, keepdims=True))
                        m_safe = jnp.where(m_new > -jnp.inf, m_new, 0.0)
                        p = jnp.exp2(logit - m_safe)
                        out += [m_new, jnp.exp2(m_run - m_safe) * acc + _dot(v_c, p.astype(BF16))]
                    else:
                        out.append(st[h] + _dot(v_c, jnp.exp2(logit).astype(BF16)))
                return tuple(out)

            if running_max:
                st = lax.fori_loop(0, n_chunks, chunk_body, (jnp.full((1, tq), -jnp.inf, F32), acc0) * n_grp)
                accs = st[1::2]
            else:
                accs = lax.fori_loop(0, n_chunks, chunk_body, (acc0,) * n_grp)
            for h in range(n_grp):
                r0 = pl.multiple_of(g * (n_grp * HEAD_DIM) + h * HEAD_DIM, HEAD_DIM)
                den = accs[h][HEAD_DIM:HEAD_DIM + 1, :]
                ot_s[pl.ds(r0, HEAD_DIM), :] = accs[h][0:HEAD_DIM, :] / den
                den_min = jnp.minimum(den_min, den)
            return den_min

        return lax.fori_loop(0, N_KV_HEADS, group_body, jnp.full((1, tq), jnp.inf, F32))

    den_min = attend(running_max=False)

    @pl.when(jnp.logical_not(jnp.min(den_min) >= MIN_DENOMINATOR))
    def _():
        attend(running_max=True)

    o = jnp.concatenate([ot_s[m * LANES:(m + 1) * LANES, :].T for m in range(N_HEADS // 2)], axis=1)
    o_ref[0] = _dot(o.astype(BF16), wout_ref[...]) + x_ref[0]


def _dsa_attn_prompt(q, qi, kw, k, v, x, w_out, *, tq, kc):
    b, t, d = x.shape
    n_sel = min(TOPK_MAX, t // 4)
    n_kv = N_KV_HEADS * HEAD_DIM
    nct = t // kc
    kern = functools.partial(_attn_prompt_kernel, tq=tq, t_len=t, kc=kc, n_sel=n_sel)

    def tile_spec(n):
        return pl.BlockSpec((1, tq, n), lambda i, j: (i, j, 0))

    def full_spec(n):
        return pl.BlockSpec((1, t, n), lambda i, j: (i, 0, 0))

    return pl.pallas_call(
        kern,
        grid=(b, t // tq),
        in_specs=[tile_spec(q.shape[-1]), tile_spec(qi.shape[-1]), tile_spec(LANES),
                  full_spec(n_kv), full_spec(n_kv), full_spec(LANES),
                  tile_spec(d), _const_spec(w_out.shape)],
        out_specs=tile_spec(d),
        out_shape=jax.ShapeDtypeStruct(x.shape, F32),
        scratch_shapes=[
            pltpu.VMEM((N_KV_HEADS * nct, kc, LANES), BF16), pltpu.VMEM((N_KV_HEADS * nct, kc, LANES), BF16),
            pltpu.VMEM((N_KV_HEADS * nct, HEAD_DIM + ONES_ROWS, kc), BF16),
            pltpu.VMEM((nct, kc, LANES), BF16), pltpu.VMEM((nct, kc, LANES), BF16),
            pltpu.VMEM((N_HEADS, tq, LANES), BF16),
            pltpu.VMEM((nct, kc, tq), I32), pltpu.VMEM((nct, kc, tq), F32),
            pltpu.VMEM((N_HEADS * HEAD_DIM, tq), F32),
            pltpu.VMEM((nct, kc, tq), I16),
            pltpu.VMEM((N_KV_HEADS, 1, LANES), F32),
            pltpu.VMEM((8, tq), I32),
        ],
        compiler_params=_cparams(2),
        name="dsa_attn_prompt",
    )(q, qi, kw, k, v, kw, x, w_out)


def _attn_sample_kernel(q_ref, qi_ref, kw_ref, k_ref, v_ref, ckt_ref, cvt_ref, ckit_ref, x_ref, wout_ref, o_ref,
                        kt16, vt16, qg_s, kn_s, vn_s, og_s, *, tq, past, n_sel):
    n_grp = N_HEADS // N_KV_HEADS
    lane = lax.broadcasted_iota(I32, (1, LANES), 1)
    lo = lane < HEAD_DIM

    def lower_half(x128, upper):
        return jnp.where(lo, pltpu.roll(x128, HEAD_DIM, 1) if upper else x128, 0.0)

    kt16[...] = ckt_ref[0].astype(BF16)
    vt16[...] = cvt_ref[0].astype(BF16)
    kit16 = ckit_ref[0].astype(BF16)
    kw = kw_ref[0]
    kw16 = kw.astype(BF16)
    k_new, v_new = k_ref[0], v_ref[0]
    q32 = q_ref[0].astype(F32)
    for g in range(N_KV_HEADS):
        tile = slice((g // 2) * LANES, (g // 2 + 1) * LANES)
        kn_s[g] = lower_half(k_new[:, tile], g % 2 == 1).astype(BF16)
        vn_s[g] = lower_half(v_new[:, tile], g % 2 == 1).astype(BF16)
        heads = [n_grp * g + h for h in range(n_grp)]
        qg_s[g] = jnp.concatenate(
            [lower_half(q32[:, (hd // 2) * LANES:(hd // 2 + 1) * LANES], hd % 2 == 1) for hd in heads],
            axis=0).astype(BF16)

    qi32 = qi_ref[0].astype(F32)
    qi_rows = jnp.concatenate(
        [lower_half(qi32[:, (hh // 2) * LANES:(hh // 2 + 1) * LANES], hh % 2 == 1) for hh in range(IDX_HEADS)],
        axis=0).astype(BF16)
    d_c = _dot(qi_rows[:, 0:IDX_DIM], kit16)
    d_n = _dot_nt(qi_rows, kw16)
    sc_c = jnp.zeros((tq, past), F32)
    sc_n = jnp.zeros((tq, tq), F32)
    for hh in range(IDX_HEADS):
        w_col = kw[:, IDX_DIM + hh:IDX_DIM + hh + 1]
        rows = slice(hh * tq, (hh + 1) * tq)
        sc_c = sc_c + w_col * jnp.maximum(d_c[rows, :], 0.0)
        sc_n = sc_n + w_col * jnp.maximum(d_n[rows, :], 0.0)
    q_chunk = (past + lax.broadcasted_iota(I32, (tq, 1), 0)) // CHUNK
    k_chunk = (past + lax.broadcasted_iota(I32, (1, tq), 1)) // CHUNK
    key_c = _order_key(sc_c)
    key_n = jnp.where(k_chunk <= q_chunk, _order_key(sc_n), KEY_NEG_INF)

    def count(pred):
        return (jnp.sum(pred(key_c).astype(I32), axis=-1, keepdims=True)
                + jnp.sum(pred(key_n).astype(I32), axis=-1, keepdims=True))

    def bit_step(i, thr):
        cand = thr ^ lax.shift_left(jnp.int32(1), 31 - i)
        return jnp.where(count(lambda k: k >= cand) >= n_sel, cand, thr)

    thr = lax.fori_loop(0, 32, bit_step, jnp.full((tq, 1), INT_MIN, I32))
    wanted = (n_sel - count(lambda k: k > thr)).astype(F32)

    def select(keys, seen):
        tie = keys == thr
        tie_f = jnp.where(tie, 1.0, 0.0)
        ranks = []
        for c0 in range(0, keys.shape[1], LANES):
            cw = min(LANES, keys.shape[1] - c0)
            r_i = lax.broadcasted_iota(I32, (cw, cw), 0)
            c_i = lax.broadcasted_iota(I32, (cw, cw), 1)
            upper = jnp.where(r_i <= c_i, 1.0, 0.0).astype(BF16)
            part = tie_f[:, c0:c0 + cw]
            ranks.append(seen + _dot(part.astype(BF16), upper))
            seen = seen + jnp.sum(part, axis=-1, keepdims=True)
        rank = ranks[0] if len(ranks) == 1 else jnp.concatenate(ranks, axis=1)
        sel = ((keys > thr) | (tie & (rank <= wanted))) & (keys > KEY_NEG_INF)
        return jnp.where(sel, 0.0, -jnp.inf), seen

    bias_c, seen = select(key_c, jnp.zeros((tq, 1), F32))
    bias_n, _ = select(key_n, seen)
    bias_c = jnp.concatenate([bias_c] * n_grp, axis=0)
    bias_n = jnp.concatenate([bias_n] * n_grp, axis=0)

    groups = range(N_KV_HEADS)
    rows = [slice(g * HEAD_DIM, (g + 1) * HEAD_DIM) for g in groups]
    l_c = [_dot(qg_s[g][:, 0:HEAD_DIM], kt16[rows[g], :]) for g in groups]
    l_n = [_dot_nt(qg_s[g], kn_s[g]) for g in groups]
    tail = lax.bitcast_convert_type(l_c[-1][:, past - LANES:past], I32) & lax.shift_right_arithmetic(pl.program_id(0), 31)
    l_n[0] = l_n[0] + lax.bitcast_convert_type(tail, F32)[:, 0:tq]
    for g in groups:
        lc, ln = l_c[g] + bias_c, l_n[g] + bias_n
        mx = jnp.maximum(jnp.max(lc, axis=-1, keepdims=True), jnp.max(ln, axis=-1, keepdims=True))
        p_c = jnp.exp2(lc - mx)
        p_n = jnp.exp2(ln - mx)
        den = jnp.sum(p_c, axis=-1, keepdims=True) + jnp.sum(p_n, axis=-1, keepdims=True)
        pv = (_dot_nt(p_c.astype(BF16), vt16[rows[g], :])
              + _dot(p_n.astype(BF16), vn_s[g])[:, 0:HEAD_DIM])
        og_s[g] = pv / den

    tiles = []
    for m in range(N_HEADS // 2):
        pair = []
        for hd in (2 * m, 2 * m + 1):
            h = hd % n_grp
            pair.append(og_s[hd // n_grp, h * tq:(h + 1) * tq, :])
        tiles.append(jnp.concatenate(pair, axis=1))
    o = jnp.concatenate(tiles, axis=1).astype(BF16)
    o_ref[0] = _dot(o, wout_ref[...]) + x_ref[0]


def _dsa_attn_sample(q, qi, kw, k, v, cache_kt, cache_vt, cache_kit, x, w_out):
    b, t, d = x.shape
    past = cache_kt.shape[-1]
    n_sel = min(TOPK_MAX, (past + t) // 4)
    n_kv = N_KV_HEADS * HEAD_DIM
    n_grp = N_HEADS // N_KV_HEADS
    kern = functools.partial(_attn_sample_kernel, tq=t, past=past, n_sel=n_sel)

    def spec(rows, n):
        return pl.BlockSpec((1, rows, n), lambda i: (i, 0, 0))

    return pl.pallas_call(
        kern,
        grid=(b,),
        in_specs=[spec(t, q.shape[-1]), spec(t, qi.shape[-1]), spec(t, LANES), spec(t, n_kv), spec(t, n_kv),
                  spec(n_kv, past), spec(n_kv, past), spec(IDX_DIM, past),
                  spec(t, d), _const_spec(w_out.shape)],
        out_specs=spec(t, d),
        out_shape=jax.ShapeDtypeStruct(x.shape, F32),
        scratch_shapes=[
            pltpu.VMEM((n_kv, past), BF16), pltpu.VMEM((n_kv, past), BF16),
            pltpu.VMEM((N_KV_HEADS, n_grp * t, LANES), BF16),
            pltpu.VMEM((N_KV_HEADS, t, LANES), BF16), pltpu.VMEM((N_KV_HEADS, t, LANES), BF16),
            pltpu.VMEM((N_KV_HEADS, n_grp * t, HEAD_DIM), F32),
        ],
        compiler_params=_cparams(1),
        name="dsa_attn_sample",
    )(q, qi, kw, k, v, cache_kt, cache_vt, cache_kit, x, w_out)


def _tiling(b, t):
    if t >= 256:
        return dict(bb=1, tt=256, blk=128, sub=32)
    bb = max(1, min(b, 128 // t))
    return dict(bb=bb, tt=t, blk=t, sub=t)


def _trunk(x, hg_state, conv_state, cache, p):
    b, t, d = x.shape
    tl = _tiling(b, t)
    bb, tt = tl["bb"], tl["tt"]
    x, s_new = _hgrn_mixer(x, hg_state[0], p["norm_mix"][0], p["hg_w_in"][0], p["lower_bounds"][0],
                           p["hg_norm"][0], p["hg_w_out"][0], **dict(tl, tt=512 if t % 512 == 0 else tt))
    x, cs0 = _conv_ffn(x, conv_state[0], p["norm_ffn"][0], p["ffn_w_in"][0], p["ffn_conv_w"][0],
                       p["ffn_conv_b"][0], p["ffn_w_down"][0], p["norm_final"], bb=bb, tt=tt, final_norm=False)
    past = 0 if cache is None else cache[0].shape[1]
    tables = _rope_tables(past + jnp.arange(t))
    q, k, v, qi, kw = _dsa_proj(x, p["norm_mix"][1], p["ds_w_in"][0], tables, p["ds_kln_w"][0],
                                p["ds_kln_b"][0], bb=bb, tt=tt)
    if cache is None:
        x = _dsa_attn_prompt(q, qi, kw, k, v, x, p["ds_w_out"][0], tq=min(t, 512), kc=min(t, 512))
    else:
        ck, cv, cki = cache
        def keys_last(c):
            return jnp.swapaxes(c.reshape(b, past, -1), 1, 2)

        x = _dsa_attn_sample(q, qi, kw, k, v, keys_last(ck), keys_last(cv), keys_last(cki), x, p["ds_w_out"][0])
    x, cs1 = _conv_ffn(x, conv_state[1], p["norm_ffn"][1], p["ffn_w_in"][1], p["ffn_conv_w"][1],
                       p["ffn_conv_b"][1], p["ffn_w_down"][1], p["norm_final"], bb=bb, tt=tt, final_norm=True)
    k = k.reshape(1, b, t, N_KV_HEADS, HEAD_DIM)
    v = v.reshape(1, b, t, N_KV_HEADS, HEAD_DIM)
    ki = kw[:, :, :IDX_DIM].reshape(1, b, t, IDX_DIM)
    return x, s_new[None], jnp.stack([cs0, cs1]), k, v, ki


def kernel(x_prompt, x_sample, cache_k, cache_v, cache_kidx, state_hgrn, state_conv, norm_mix, norm_ffn, norm_final, hg_w_in, hg_lb, hg_norm, hg_w_out, ds_w_in, ds_kln_w, ds_kln_b, ds_w_out, ffn_w_in, ffn_conv_w, ffn_conv_b, ffn_w_down):
    dsa_in = ds_w_in.shape[-1]
    dsa_pad = (-dsa_in) % LANES
    p = dict(
        norm_mix=norm_mix, norm_ffn=norm_ffn, norm_final=norm_final,
        hg_w_in=hg_w_in.astype(BF16), hg_norm=hg_norm, hg_w_out=hg_w_out.astype(BF16),
        lower_bounds=jnp.cumsum(jax.nn.softmax(hg_lb.astype(F32), axis=0), axis=0),
        ds_w_in=jnp.pad(ds_w_in, ((0, 0), (0, 0), (0, dsa_pad))).astype(BF16),
        ds_kln_w=ds_kln_w, ds_kln_b=ds_kln_b, ds_w_out=ds_w_out.astype(BF16),
        ffn_w_in=ffn_w_in.astype(BF16), ffn_conv_w=ffn_conv_w, ffn_conv_b=ffn_conv_b,
        ffn_w_down=ffn_w_down.astype(BF16),
    )
    b = x_prompt.shape[0]
    hg0 = jnp.zeros((state_hgrn.shape[0], b) + state_hgrn.shape[2:], F32)
    conv0 = jnp.zeros((state_conv.shape[0], b) + state_conv.shape[2:], F32)
    y_p, hg_p, conv_p, k_p, v_p, ki_p = _trunk(x_prompt, hg0, conv0, None, p)
    y_s, hg_s, conv_s, k_s, v_s, ki_s = _trunk(x_sample, state_hgrn, state_conv,
                                               (cache_k[0], cache_v[0], cache_kidx[0]), p)
    return (y_p, y_s, k_p, v_p, ki_p, hg_p, conv_p, k_s, v_s, ki_s, hg_s, conv_s)
```

```python
import functools

import jax
import jax.numpy as jnp
from jax import lax
from jax.experimental import pallas as pl
from jax.experimental.pallas import tpu as pltpu

F32 = jnp.float32
BF16 = jnp.bfloat16
I32 = jnp.int32
I16 = jnp.int16

CHUNK = 64
EPS = 1e-6
HG_HEADS = 8
HG_KDIM = 128
N_HEADS = 16
HEAD_DIM = 64
N_KV_HEADS = 4
IDX_HEADS = 8
IDX_DIM = 64
TOPK_MAX = 256
ROPE_THETA = 500000.0
CONV_W = 3

LANES = 128
VMEM_LIMIT = 56 * 1024 * 1024
EXP_CLAMP = 80.0
LOG2_E = 1.4426950408889634
ONES_ROWS = 16
BOUND_SLACK = 1.02
MIN_DENOMINATOR = 2.0 ** -60
KEY_NEG_INF = -2139095041
INT_MIN = -2147483648


def _cparams(n_axes, flags=None):
    return pltpu.CompilerParams(dimension_semantics=("arbitrary",) * n_axes,
                                vmem_limit_bytes=VMEM_LIMIT, flags=flags)


def _const_spec(shape):
    nd = len(shape)
    return pl.BlockSpec(shape, lambda *_: (0,) * nd, pipeline_mode=pl.Buffered(1))


def _rms(x, g):
    return x * lax.rsqrt(jnp.mean(x * x, axis=-1, keepdims=True) + EPS) * g


def _silu(x):
    return x * jax.nn.sigmoid(x)


def _dot(a, b):
    return jnp.dot(a, b, preferred_element_type=F32)


def _dot_nt(a, b):
    return lax.dot_general(a, b, (((1,), (1,)), ((), ())), preferred_element_type=F32)


def _dot_tn(a, b):
    return lax.dot_general(a, b, (((0,), (0,)), ((), ())), preferred_element_type=F32)


def _hgrn_kernel(x_ref, s0_ref, g_ref, win_ref, lb_ref, hn_ref, wout_ref, o_ref, s_ref,
                 q_s, k_s, g_s, v_s, gate_s, on_s, st_s, o_s, *, bb, tt, blk, sub):
    j = pl.program_id(1)
    rows = bb * tt
    d = x_ref.shape[-1]
    fdim = HG_HEADS * HG_KDIM
    n_sub = blk // sub

    @pl.when(j == 0)
    def _():
        for bi in range(bb):
            for hh in range(HG_HEADS):
                st_s[bi * HG_HEADS + hh] = s0_ref[bi, hh].T

    x = x_ref[...].reshape(rows, d)
    h = _rms(x, g_ref[...]).astype(BF16)
    lb = lb_ref[...]
    fg = lb + (1.0 - lb) * jax.nn.sigmoid(_dot(h, win_ref[:, fdim:2 * fdim]))
    k_s[...] = 1.0 - fg
    g_s[...] = jnp.log(fg)
    q_s[...] = _silu(_dot(h, win_ref[:, 0:fdim]))
    gate_s[...] = _silu(_dot(h, win_ref[:, 2 * fdim + d:]))
    v_s[...] = _dot(h, win_ref[:, 2 * fdim:2 * fdim + d])

    r_i = lax.broadcasted_iota(I32, (blk, blk), 0)
    c_i = lax.broadcasted_iota(I32, (blk, blk), 1)
    causal = c_i <= r_i
    tril = jnp.where(causal, 1.0, 0.0).astype(BF16)
    hn = hn_ref[...]
    blocks_per_stream = tt // blk

    def block_body(idx, carry):
        r0 = pl.multiple_of(idx * blk, blk)
        bi = idx // blocks_per_stream
        lg = g_s[pl.ds(r0, blk), :]
        lg_hi = lg.astype(BF16)
        rem = lg - lg_hi.astype(F32)
        lg_mid = rem.astype(BF16)
        lg_lo = (rem - lg_mid.astype(F32)).astype(BF16)
        g_all = _dot(tril, lg_hi) + _dot(tril, lg_mid) + _dot(tril, lg_lo)
        run_zero = lax.shift_right_arithmetic(idx, 31)

        def anchor(x):
            tail = lax.bitcast_convert_type(x[x.shape[0] - 8:, :], I32) & run_zero
            return lax.bitcast_convert_type(tail, F32)[0:1, :]

        heads = [slice(hh * HG_KDIM, (hh + 1) * HG_KDIM) for hh in range(HG_HEADS)]
        qcat, kcat, qfull, kdec_last, v16, decay = [], [], [], [], [], []
        for sl in heads:
            gc = g_all[:, sl]
            q = q_s[pl.ds(r0, blk), sl]
            kk = k_s[pl.ds(r0, blk), sl]
            refs = [jnp.zeros((1, HG_KDIM), F32)] + [gc[s * sub - 1:s * sub, :] for s in range(1, n_sub)]
            q_parts, k_parts = [], []
            for s in range(n_sub):
                lo_r, hi_r = s * sub, (s + 1) * sub
                qg = q[lo_r:hi_r, :] * jnp.exp(gc[lo_r:hi_r, :] - refs[s])
                pieces = [qg]
                if lo_r:
                    pieces.insert(0, jnp.zeros((lo_r, HG_KDIM), F32))
                if blk - hi_r:
                    pieces.append(jnp.zeros((blk - hi_r, HG_KDIM), F32))
                q_parts.append(pieces[0] if len(pieces) == 1 else jnp.concatenate(pieces, axis=0))
                kd_s = kk[0:hi_r, :] * jnp.exp(jnp.minimum(refs[s] - gc[0:hi_r, :], EXP_CLAMP))
                k_parts.append(kd_s if hi_r == blk else
                               jnp.concatenate([kd_s, jnp.zeros((blk - hi_r, HG_KDIM), F32)], axis=0))
            qcat.append(jnp.concatenate(q_parts, axis=1).astype(BF16))
            kcat.append(jnp.concatenate(k_parts, axis=1).astype(BF16))
            qfull.append((q * jnp.exp(gc)).astype(BF16))
            g_last = gc[blk - 1:blk, :]
            kdec_last.append((kk * jnp.exp(g_last - gc)).astype(BF16))
            decay.append(jnp.exp(g_last))
            v16.append(v_s[pl.ds(r0, blk), sl].astype(BF16))
        a_list = [_dot_nt(qcat[hh], kcat[hh]) for hh in range(HG_HEADS)]
        s_inc = [_dot_tn(v16[hh], kdec_last[hh]) for hh in range(HG_HEADS)]
        a_list[0] = a_list[0] + anchor(a_list[-1])
        o_list = []
        for hh in range(HG_HEADS):
            a = jnp.where(causal, a_list[hh], 0.0).astype(BF16)
            st = st_s[bi * HG_HEADS + hh]
            o_list.append(_dot(a, v16[hh]) + _dot_nt(qfull[hh], st.astype(BF16)))
            st_s[bi * HG_HEADS + hh] = st * decay[hh] + s_inc[hh]
        o_list[0] = o_list[0] + anchor(o_list[-1])
        for hh, sl in enumerate(heads):
            on = _rms(o_list[hh], hn) * gate_s[pl.ds(r0, blk), sl]
            on_s[pl.ds(r0, blk), sl] = on.astype(BF16)
        return carry

    sub_sums = jnp.sum(g_s[...].reshape(rows // sub, sub, fdim), axis=1)
    blocked_ok = jnp.min(sub_sums) >= -EXP_CLAMP

    @pl.when(blocked_ok)
    def _():
        lax.fori_loop(0, rows // blk, block_body, 0, unroll=4)

    @pl.when(jnp.logical_not(blocked_ok))
    def _():
        grp = 16
        o_s[...] = jnp.zeros(o_s.shape, F32)
        in_grp = lax.broadcasted_iota(I32, (grp, 1), 0)

        def frame_body(r, carry):
            r0 = pl.multiple_of((r // grp) * grp, grp)
            bi = r // tt
            this = in_grp == (r % grp)
            for hh in range(HG_HEADS):
                sl = slice(hh * HG_KDIM, (hh + 1) * HG_KDIM)

                def only(ref):
                    return jnp.where(this, ref[pl.ds(r0, grp), sl], 0.0)

                forget = jnp.exp(jnp.sum(only(g_s), axis=0, keepdims=True))
                st = (st_s[bi * HG_HEADS + hh] * forget
                      + _dot_tn(only(v_s).astype(BF16), only(k_s).astype(BF16)))
                st_s[bi * HG_HEADS + hh] = st
                o_s[pl.ds(r0, grp), sl] += _dot_nt(only(q_s).astype(BF16), st.astype(BF16))
            return carry

        lax.fori_loop(0, rows, frame_body, 0)
        for hh in range(HG_HEADS):
            sl = slice(hh * HG_KDIM, (hh + 1) * HG_KDIM)
            on_s[:, sl] = (_rms(o_s[:, sl], hn) * gate_s[:, sl]).astype(BF16)

    out = _dot(on_s[...], wout_ref[...]) + x
    o_ref[...] = out.reshape(bb, tt, d)

    @pl.when(j == pl.num_programs(1) - 1)
    def _():
        for bi in range(bb):
            for hh in range(HG_HEADS):
                s_ref[bi, hh] = st_s[bi * HG_HEADS + hh].T


def _hgrn_mixer(x, s0, norm_g, w_in, lb, hn, w_out, *, bb, tt, blk, sub):
    b, t, d = x.shape
    assert b % bb == 0 and t % tt == 0 and tt % blk == 0 and blk % sub == 0 and sub % 16 == 0, (b, t, bb, tt, blk, sub)
    assert bb == 1 or tt == t, "several streams per block only when a block holds whole streams"
    fdim = HG_HEADS * HG_KDIM
    rows = bb * tt
    kern = functools.partial(_hgrn_kernel, bb=bb, tt=tt, blk=blk, sub=sub)
    return pl.pallas_call(
        kern,
        grid=(b // bb, t // tt),
        in_specs=[
            pl.BlockSpec((bb, tt, d), lambda i, j: (i, j, 0)),
            pl.BlockSpec((bb, HG_HEADS, HG_KDIM, d // HG_HEADS), lambda i, j: (i, 0, 0, 0)),
            _const_spec((1, d)),
            _const_spec(w_in.shape),
            _const_spec((1, fdim)),
            _const_spec((1, d // HG_HEADS)),
            _const_spec(w_out.shape),
        ],
        out_specs=[
            pl.BlockSpec((bb, tt, d), lambda i, j: (i, j, 0)),
            pl.BlockSpec((bb, HG_HEADS, HG_KDIM, d // HG_HEADS), lambda i, j: (i, 0, 0, 0)),
        ],
        out_shape=[jax.ShapeDtypeStruct(x.shape, F32), jax.ShapeDtypeStruct(s0.shape, F32)],
        scratch_shapes=[
            pltpu.VMEM((rows, fdim), F32),
            pltpu.VMEM((rows, fdim), F32),
            pltpu.VMEM((rows, fdim), F32),
            pltpu.VMEM((rows, d), F32),
            pltpu.VMEM((rows, d), F32),
            pltpu.VMEM((rows, d), BF16),
            pltpu.VMEM((bb * HG_HEADS, d // HG_HEADS, HG_KDIM), F32),
            pltpu.VMEM((rows, d), F32),
        ],
        compiler_params=_cparams(2),
        name="hgrn_mixer",
    )(x, s0, norm_g.reshape(1, d), w_in, lb.reshape(1, fdim), hn.reshape(1, -1), w_out)


CONV_HEAD = 8


def _ffn_kernel(x_ref, cs_ref, g_ref, win_ref, cw_ref, cb_ref, wdn_ref, fg_ref, o_ref, ns_ref,
                a_s, *, bb, tt, final_norm):
    j = pl.program_id(1)
    rows = bb * tt
    d = x_ref.shape[-1]
    dff = cw_ref.shape[-1]
    hist = CONV_W - 1

    @pl.when(j == 0)
    def _():
        a_s[:, CONV_HEAD - hist:CONV_HEAD, :] = cs_ref[...]

    x = x_ref[...].reshape(rows, d)
    h = _rms(x, g_ref[...]).astype(BF16)
    a = _dot(h, win_ref[:, 0:dff])
    u = _dot(h, win_ref[:, dff:])
    a_s[:, CONV_HEAD:CONV_HEAD + tt, :] = a.reshape(bb, tt, dff)
    c = jnp.broadcast_to(cb_ref[...].reshape(1, 1, dff), (bb, tt, dff))
    for w in range(CONV_W):
        lo = CONV_HEAD - hist + w
        c = c + a_s[:, lo:lo + tt, :] * cw_ref[w:w + 1, :].reshape(1, 1, dff)
    new_state = a_s[:, CONV_HEAD + tt - hist:CONV_HEAD + tt, :]
    a_s[:, CONV_HEAD - hist:CONV_HEAD, :] = new_state
    ns_ref[...] = new_state
    act = (_silu(c).reshape(rows, dff) * u).astype(BF16)
    y = _dot(act, wdn_ref[...]) + x
    if final_norm:
        y = _rms(y, fg_ref[...])
    o_ref[...] = y.reshape(bb, tt, d)


def _conv_ffn(x, conv_state, norm_g, w_in, conv_w, conv_b, w_down, final_g, *, bb, tt, final_norm):
    b, t, d = x.shape
    assert b % bb == 0 and t % tt == 0 and tt >= CONV_W - 1 and (bb == 1 or tt == t), (b, t, bb, tt)
    dff = conv_w.shape[-1]
    kern = functools.partial(_ffn_kernel, bb=bb, tt=tt, final_norm=final_norm)
    return pl.pallas_call(
        kern,
        grid=(b // bb, t // tt),
        in_specs=[
            pl.BlockSpec((bb, tt, d), lambda i, j: (i, j, 0)),
            pl.BlockSpec((bb, CONV_W - 1, dff), lambda i, j: (i, 0, 0)),
            _const_spec((1, d)),
            _const_spec(w_in.shape),
            _const_spec(conv_w.shape),
            _const_spec((1, dff)),
            _const_spec(w_down.shape),
            _const_spec((1, d)),
        ],
        out_specs=[
            pl.BlockSpec((bb, tt, d), lambda i, j: (i, j, 0)),
            pl.BlockSpec((bb, CONV_W - 1, dff), lambda i, j: (i, 0, 0)),
        ],
        out_shape=[jax.ShapeDtypeStruct(x.shape, F32), jax.ShapeDtypeStruct(conv_state.shape, F32)],
        scratch_shapes=[pltpu.VMEM((bb, CONV_HEAD + tt, dff), F32)],
        compiler_params=_cparams(2),
        name="conv_ffn",
    )(x, conv_state, norm_g.reshape(1, d), w_in, conv_w, conv_b.reshape(1, dff), w_down,
      final_g.reshape(1, d))


def _rope_tables(pos):
    rot = HEAD_DIM // 4
    half = rot // 2
    inv_freq = ROPE_THETA ** (-jnp.arange(half, dtype=F32) / half)
    ang = pos.astype(F32)[:, None] * inv_freq[None, :]
    cos, sin = jnp.cos(ang), jnp.sin(ang)
    t = pos.shape[0]
    pad = HEAD_DIM - rot
    one = jnp.ones((t, pad), F32)
    zero_h = jnp.zeros((t, half), F32)
    zero_p = jnp.zeros((t, pad), F32)
    c64 = jnp.concatenate([cos, cos, one], axis=1)
    s1_64 = jnp.concatenate([zero_h, sin, zero_p], axis=1)
    s2_64 = jnp.concatenate([-sin, zero_h, zero_p], axis=1)
    ident = jnp.ones((t, HEAD_DIM), F32)
    zero64 = jnp.zeros((t, HEAD_DIM), F32)
    return jnp.stack([
        jnp.concatenate([c64, c64], axis=1), jnp.concatenate([s1_64, s1_64], axis=1),
        jnp.concatenate([s2_64, s2_64], axis=1),
        jnp.concatenate([c64, ident], axis=1), jnp.concatenate([s1_64, zero64], axis=1),
        jnp.concatenate([s2_64, zero64], axis=1)])


def _dsa_proj_kernel(x_ref, g_ref, w_ref, tab_ref, lnw_ref, lnb_ref,
                     q_ref, k_ref, v_ref, qi_ref, kw_ref, *, bb, tt):
    rows = bb * tt
    d = x_ref.shape[-1]
    half = HEAD_DIM // 8
    o1 = N_HEADS * HEAD_DIM
    o2 = o1 + N_KV_HEADS * HEAD_DIM
    o3 = o2 + N_KV_HEADS * HEAD_DIM
    o4 = o3 + IDX_HEADS * IDX_DIM

    def tab(i):
        t = tab_ref[i]
        return jnp.broadcast_to(t[None], (bb, tt, LANES)).reshape(rows, LANES)

    def rope(y, base):
        cos, s1, s2 = tab(base), tab(base + 1), tab(base + 2)
        tiles = []
        for m in range(y.shape[1] // LANES):
            yt = y[:, m * LANES:(m + 1) * LANES]
            tiles.append(yt * cos + pltpu.roll(yt, half, 1) * s1 + pltpu.roll(yt, LANES - half, 1) * s2)
        return tiles[0] if len(tiles) == 1 else jnp.concatenate(tiles, axis=1)

    x = x_ref[...].reshape(rows, d)
    h = _rms(x, g_ref[...]).astype(BF16)
    t = _dot(h, w_ref[:, o4:o4 + LANES])
    lane = lax.broadcasted_iota(I32, (1, LANES), 1)
    is_ki = lane < IDX_DIM
    mu = jnp.sum(jnp.where(is_ki, t, 0.0), axis=-1, keepdims=True) / IDX_DIM
    cen = jnp.where(is_ki, t - mu, 0.0)
    var = jnp.sum(cen * cen, axis=-1, keepdims=True) / IDX_DIM
    ki = rope(cen * lax.rsqrt(var + EPS) * lnw_ref[...] + lnb_ref[...], 3)
    wi = t * ((IDX_HEADS * IDX_DIM) ** -0.5)
    kw = jnp.where(is_ki, ki, jnp.where(lane < IDX_DIM + IDX_HEADS, wi, 0.0))
    kw_ref[...] = kw.reshape(bb, tt, LANES)
    qi_ref[...] = rope(_dot(h, w_ref[:, o3:o4]), 0).astype(BF16).reshape(bb, tt, o4 - o3)
    k_ref[...] = rope(_dot(h, w_ref[:, o1:o2]), 0).reshape(bb, tt, o2 - o1)
    q = rope(_dot(h, w_ref[:, 0:o1]), 0) * (HEAD_DIM ** -0.5 * LOG2_E)
    q_ref[...] = q.astype(BF16).reshape(bb, tt, o1)
    v_ref[...] = _dot(h, w_ref[:, o2:o3]).reshape(bb, tt, o3 - o2)


def _dsa_proj(x, norm_g, w_pad, tables, ln_w, ln_b, *, bb, tt):
    b, t, d = x.shape
    n_q = N_HEADS * HEAD_DIM
    n_kv = N_KV_HEADS * HEAD_DIM
    n_qi = IDX_HEADS * IDX_DIM
    pad = jnp.zeros((LANES - IDX_DIM,), F32)
    lnw = jnp.concatenate([ln_w, pad]).reshape(1, LANES)
    lnb = jnp.concatenate([ln_b, pad]).reshape(1, LANES)
    kern = functools.partial(_dsa_proj_kernel, bb=bb, tt=tt)

    def row_spec(n):
        return pl.BlockSpec((bb, tt, n), lambda i, j: (i, j, 0))

    return pl.pallas_call(
        kern,
        grid=(b // bb, t // tt),
        in_specs=[
            row_spec(d),
            _const_spec((1, d)),
            _const_spec(w_pad.shape),
            pl.BlockSpec((6, tt, LANES), lambda i, j: (0, j, 0)),
            _const_spec((1, LANES)),
            _const_spec((1, LANES)),
        ],
        out_specs=[row_spec(n_q), row_spec(n_kv), row_spec(n_kv), row_spec(n_qi), row_spec(LANES)],
        out_shape=[
            jax.ShapeDtypeStruct((b, t, n_q), BF16),
            jax.ShapeDtypeStruct((b, t, n_kv), F32),
            jax.ShapeDtypeStruct((b, t, n_kv), F32),
            jax.ShapeDtypeStruct((b, t, n_qi), BF16),
            jax.ShapeDtypeStruct((b, t, LANES), F32),
        ],
        compiler_params=_cparams(2),
        name="dsa_proj",
    )(x, norm_g.reshape(1, d), w_pad, tables, lnw, lnb)


def _order_key(score):
    score = jnp.where(score == 0.0, 0.0, score)
    bits = lax.bitcast_convert_type(score, I32)
    return bits ^ ((bits >> 31) & 0x7FFFFFFF)


def _attn_prompt_kernel(q_ref, qi_ref, wq_ref, k_ref, v_ref, kw_ref, x_ref, wout_ref, o_ref,
                        ke, ko, vt, kie, kio, qs, key_s, bias_s, ot_s, half_s, kmax_s, thr_s, *, tq, t_len, kc, n_sel):
    j = pl.program_id(1)
    nct = t_len // kc
    sub_per_chunk = kc // LANES
    lane = lax.broadcasted_iota(I32, (1, LANES), 1)
    lo = lane < HEAD_DIM

    unit_hi = jnp.where(lane == HEAD_DIM, 1.0, 0.0)
    unit_lo = jnp.where(lane == 0, 1.0, 0.0)

    r_i = lax.broadcasted_iota(I32, (LANES, LANES), 0)
    c_i = lax.broadcasted_iota(I32, (LANES, LANES), 1)
    half_sums = jnp.where(((r_i < HEAD_DIM) & (c_i == HEAD_DIM)) | ((r_i >= HEAD_DIM) & (c_i == 0)),
                          1.0, 0.0).astype(BF16)

    @pl.when(j == 0)
    def _():
        k_sq_max = [jnp.zeros((1, LANES), F32) for _ in range(N_KV_HEADS // 2)]
        for c in range(nct):
            rows = slice(c * kc, (c + 1) * kc)
            k_c = k_ref[0, rows, :].astype(BF16).astype(F32)
            for m in range(N_KV_HEADS // 2):
                tile = k_c[:, m * LANES:(m + 1) * LANES]
                rolled = pltpu.roll(tile, HEAD_DIM, 1)
                ke[(2 * m) * nct + c] = jnp.where(lo, tile, unit_hi).astype(BF16)
                ko[(2 * m) * nct + c] = jnp.where(lo, unit_lo, rolled).astype(BF16)
                ke[(2 * m + 1) * nct + c] = jnp.where(lo, rolled, unit_hi).astype(BF16)
                ko[(2 * m + 1) * nct + c] = jnp.where(lo, unit_lo, tile).astype(BF16)
                norm2 = _dot((tile * tile).astype(BF16), half_sums)
                k_sq_max[m] = jnp.maximum(k_sq_max[m], jnp.max(norm2, axis=0, keepdims=True))
            v_t = v_ref[0, rows, :].T
            for g in range(N_KV_HEADS):
                vt[g * nct + c, 0:HEAD_DIM, :] = v_t[g * HEAD_DIM:(g + 1) * HEAD_DIM, :].astype(BF16)
                vt[g * nct + c, HEAD_DIM:HEAD_DIM + ONES_ROWS, :] = jnp.ones((ONES_ROWS, kc), BF16)
            kw_c = kw_ref[0, rows, :]
            kie[c] = jnp.where(lo, kw_c, 0.0).astype(BF16)
            kio[c] = jnp.where(lo, 0.0, pltpu.roll(kw_c, HEAD_DIM, 1)).astype(BF16)
        for g in range(N_KV_HEADS):
            at = HEAD_DIM if g % 2 == 0 else 0
            k_sq = jnp.max(jnp.where(lane == at, k_sq_max[g // 2], 0.0), axis=-1, keepdims=True)
            kmax_s[g] = jnp.broadcast_to(jnp.sqrt(k_sq), (1, LANES))

    for m in range(N_HEADS // 2):
        tile = q_ref[0, :, m * LANES:(m + 1) * LANES].astype(F32)
        k_norm = kmax_s[(2 * m) // (N_HEADS // N_KV_HEADS)]
        bound = jnp.sqrt(_dot((tile * tile).astype(BF16), half_sums)) * (k_norm * -BOUND_SLACK)
        qs[2 * m] = jnp.where(lo, tile, jnp.where(lane == HEAD_DIM, bound, 0.0)).astype(BF16)
        qs[2 * m + 1] = jnp.where(lo, jnp.where(lane == 0, bound, 0.0), tile).astype(BF16)

    n_chunks = ((j + 1) * tq + kc - 1) // kc
    w_t = wq_ref[0].T
    w_rows = [w_t[IDX_DIM + hh:IDX_DIM + hh + 1, :] for hh in range(IDX_HEADS)]
    qi = qi_ref[0]
    q_chunk = (j * tq + lax.broadcasted_iota(I32, (1, tq), 1)) // CHUNK

    def score_body(c, carry):
        kie_c, kio_c = kie[c], kio[c]
        sc = jnp.zeros((kc, tq), F32)
        for m in range(IDX_HEADS // 2):
            qt = qi[:, m * LANES:(m + 1) * LANES]
            sc = (sc + w_rows[2 * m] * jnp.maximum(_dot_nt(kie_c, qt), 0.0)
                  + w_rows[2 * m + 1] * jnp.maximum(_dot_nt(kio_c, qt), 0.0))
        k_chunk = (c * kc + lax.broadcasted_iota(I32, (kc, 1), 0)) // CHUNK
        keys = jnp.where(k_chunk <= q_chunk, _order_key(sc), KEY_NEG_INF)
        key_s[c] = keys
        half_s[c] = lax.shift_right_arithmetic(keys, 16).astype(I16)
        return carry

    lax.fori_loop(0, n_chunks, score_body, 0)

    def search16(need, n_static):
        def count_ge(cand):
            acc = jnp.zeros((16, tq), I16)
            for c in range(n_static):
                hit = (half_s[c] >= cand).astype(I16)
                for r in range(0, kc, 16):
                    acc = acc + hit[r:r + 16, :]
            return jnp.sum(acc.astype(I32), axis=0, keepdims=True)

        def bit_step(i, t16):
            cand = t16 ^ lax.shift_left(jnp.int32(1), 15 - i)
            cand = lax.shift_right_arithmetic(lax.shift_left(cand, 16), 16)
            return jnp.where(count_ge(cand.astype(I16)) >= need, cand, t16)

        return lax.fori_loop(0, 16, bit_step, jnp.full((1, tq), -32768, I32))

    def find_threshold(n_static):
        thr_hi = search16(n_sel, n_static)
        n_above = jnp.zeros((8, tq), I32)
        for c in range(n_static):
            keys = key_s[c]
            hi = lax.shift_right_arithmetic(keys, 16)
            low = (keys & 0xFFFF) - 32768
            half_s[c] = jnp.where(hi == thr_hi, low, -32768).astype(I16)
            n_above = n_above + jnp.sum((hi > thr_hi).astype(I32).reshape(kc // 8, 8, tq), axis=0)
        thr_lo = search16(n_sel - jnp.sum(n_above, axis=0, keepdims=True), n_static)
        thr_s[...] = jnp.broadcast_to(lax.shift_left(thr_hi, 16) | (thr_lo + 32768), thr_s.shape)

    for n_static in range(1, nct + 1):
        pl.when(n_chunks == n_static)(functools.partial(find_threshold, n_static))
    thr = thr_s[0:1, :]

    def bias_body(c, acc):
        keys = key_s[c]
        ge = keys >= thr
        bias_s[c] = jnp.where(ge & (keys > KEY_NEG_INF), 0.0, -jnp.inf)
        fold = lambda hit: jnp.sum(hit.astype(I32).reshape(kc // 8, 8, tq), axis=0)
        return acc[0] + fold(ge), acc[1] + fold(keys > thr)

    zero8 = jnp.zeros((8, tq), I32)
    n_ge, n_gt = [jnp.sum(a, axis=0, keepdims=True)
                  for a in lax.fori_loop(0, n_chunks, bias_body, (zero8, zero8))]

    has_excess = jnp.max(jnp.where(thr > KEY_NEG_INF, n_ge - n_sel, 0)) > 0

    @pl.when(has_excess)
    def _():
        wanted = (n_sel - n_gt).astype(F32)
        r_i = lax.broadcasted_iota(I32, (LANES, LANES), 0)
        c_i = lax.broadcasted_iota(I32, (LANES, LANES), 1)
        lower = jnp.where(c_i <= r_i, 1.0, 0.0).astype(BF16)

        def tie_body(u, seen):
            c = u // sub_per_chunk
            r0 = pl.multiple_of((u % sub_per_chunk) * LANES, LANES)
            keys = key_s[c, pl.ds(r0, LANES), :]
            tie = keys == thr
            tie_f = jnp.where(tie, 1.0, 0.0)
            rank = seen + _dot(lower, tie_f.astype(BF16))
            sel = ((keys > thr) | (tie & (rank <= wanted))) & (keys > KEY_NEG_INF)
            bias_s[c, pl.ds(r0, LANES), :] = jnp.where(sel, 0.0, -jnp.inf)
            return seen + jnp.sum(tie_f, axis=0, keepdims=True)

        lax.fori_loop(0, n_chunks * sub_per_chunk, tie_body, jnp.zeros((1, tq), F32))

    n_grp = N_HEADS // N_KV_HEADS
    acc0 = jnp.zeros((HEAD_DIM + ONES_ROWS, tq), F32)

    def attend(running_max):
        def group_body(g, den_min):
            q_tiles = [qs[g * n_grp + h] for h in range(n_grp)]

            def chunks_body(cs, st):
                biases = [bias_s[c] for c in cs]
                v_cs = [vt[g * nct + c] for c in cs]
                logits = [[_dot_nt((ke, ko)[h % 2][g * nct + c], q_tiles[h]) for h in range(n_grp)] for c in cs]
                tails = [lax.bitcast_convert_type(l[kc - 8:kc, :], I32) for ls in logits for l in ls][1:]
                anchor = functools.reduce(jnp.bitwise_or, tails) & lax.shift_right_arithmetic(cs[0], 31)
                anchor = lax.bitcast_convert_type(anchor, F32)[0:1, :]
                st = list(st)
                for i in range(len(cs)):
                    for h in range(n_grp):
                        logit = logits[i][h] + biases[i]
                        if i == 0 and h == 0:
                            logit = logit + anchor
                        if running_max:
                            m_run, acc = st[2 * h:2 * h + 2]
                            m_new = jnp.maximum(m_run, jnp.max(logit, axis=0, keepdims=True))
                            m_safe = jnp.where(m_new > -jnp.inf, m_new, 0.0)
                            p = jnp.exp2(logit - m_safe)
                            st[2 * h:2 * h + 2] = [m_new, jnp.exp2(m_run - m_safe) * acc + _dot(v_cs[i], p.astype(BF16))]
                        else:
                            st[h] = st[h] + _dot(v_cs[i], jnp.exp2(logit).astype(BF16))
                return tuple(st)

            if running_max:
                st = lax.fori_loop(0, n_chunks, lambda c, st: chunks_body([c], st),
                                   (jnp.full((1, tq), -jnp.inf, F32), acc0) * n_grp)
                accs = st[1::2]
            else:
                accs = lax.fori_loop(0, n_chunks // 2, lambda p, st: chunks_body([2 * p, 2 * p + 1], st),
                                     (acc0,) * n_grp)
                accs = lax.fori_loop(n_chunks - n_chunks % 2, n_chunks, lambda c, st: chunks_body([c], st), accs)
            for h in range(n_grp):
                r0 = pl.multiple_of(g * (n_grp * HEAD_DIM) + h * HEAD_DIM, HEAD_DIM)
                den = accs[h][HEAD_DIM:HEAD_DIM + 1, :]
                ot_s[pl.ds(r0, HEAD_DIM), :] = accs[h][0:HEAD_DIM, :] / den
                den_min = jnp.minimum(den_min, den)
            return den_min

        return lax.fori_loop(0, N_KV_HEADS, group_body, jnp.full((1, tq), jnp.inf, F32))

    den_min = attend(running_max=False)

    @pl.when(jnp.logical_not(jnp.min(den_min) >= MIN_DENOMINATOR))
    def _():
        attend(running_max=True)

    o = jnp.concatenate([ot_s[m * LANES:(m + 1) * LANES, :].T for m in range(N_HEADS // 2)], axis=1)
    o_ref[0] = _dot(o.astype(BF16), wout_ref[...]) + x_ref[0]


def _dsa_attn_prompt(q, qi, kw, k, v, x, w_out, *, tq, kc):
    b, t, d = x.shape
    assert t % tq == 0 and t % kc == 0 and tq % CHUNK == 0 and kc % LANES == 0, (t, tq, kc)
    n_sel = min(TOPK_MAX, t // 4)
    n_kv = N_KV_HEADS * HEAD_DIM
    nct = t // kc
    kern = functools.partial(_attn_prompt_kernel, tq=tq, t_len=t, kc=kc, n_sel=n_sel)

    def tile_spec(n):
        return pl.BlockSpec((1, tq, n), lambda i, j: (i, j, 0))

    def full_spec(n):
        return pl.BlockSpec((1, t, n), lambda i, j: (i, 0, 0))

    return pl.pallas_call(
        kern,
        grid=(b, t // tq),
        in_specs=[tile_spec(q.shape[-1]), tile_spec(qi.shape[-1]), tile_spec(LANES),
                  full_spec(n_kv), full_spec(n_kv), full_spec(LANES),
                  tile_spec(d), _const_spec(w_out.shape)],
        out_specs=tile_spec(d),
        out_shape=jax.ShapeDtypeStruct(x.shape, F32),
        scratch_shapes=[
            pltpu.VMEM((N_KV_HEADS * nct, kc, LANES), BF16), pltpu.VMEM((N_KV_HEADS * nct, kc, LANES), BF16),
            pltpu.VMEM((N_KV_HEADS * nct, HEAD_DIM + ONES_ROWS, kc), BF16),
            pltpu.VMEM((nct, kc, LANES), BF16), pltpu.VMEM((nct, kc, LANES), BF16),
            pltpu.VMEM((N_HEADS, tq, LANES), BF16),
            pltpu.VMEM((nct, kc, tq), I32), pltpu.VMEM((nct, kc, tq), F32),
            pltpu.VMEM((N_HEADS * HEAD_DIM, tq), F32),
            pltpu.VMEM((nct, kc, tq), I16),
            pltpu.VMEM((N_KV_HEADS, 1, LANES), F32),
            pltpu.VMEM((8, tq), I32),
        ],
        compiler_params=_cparams(2),
        name="dsa_attn_prompt",
    )(q, qi, kw, k, v, kw, x, w_out)


def _attn_sample_kernel(q_ref, qi_ref, kw_ref, k_ref, v_ref, ckt_ref, cvt_ref, ckit_ref, x_ref, wout_ref, o_ref,
                        kt16, vt16, qg_s, kn_s, vn_s, og_s, *, tq, past, n_sel):
    n_grp = N_HEADS // N_KV_HEADS
    lane = lax.broadcasted_iota(I32, (1, LANES), 1)
    lo = lane < HEAD_DIM

    def lower_half(x128, upper):
        return jnp.where(lo, pltpu.roll(x128, HEAD_DIM, 1) if upper else x128, 0.0)

    kt16[...] = ckt_ref[0].astype(BF16)
    vt16[...] = cvt_ref[0].astype(BF16)
    kit16 = ckit_ref[0].astype(BF16)
    kw = kw_ref[0]
    kw16 = kw.astype(BF16)
    k_new, v_new = k_ref[0], v_ref[0]
    q32 = q_ref[0].astype(F32)
    for g in range(N_KV_HEADS):
        tile = slice((g // 2) * LANES, (g // 2 + 1) * LANES)
        kn_s[g] = lower_half(k_new[:, tile], g % 2 == 1).astype(BF16)
        vn_s[g] = lower_half(v_new[:, tile], g % 2 == 1).astype(BF16)
        heads = [n_grp * g + h for h in range(n_grp)]
        qg_s[g] = jnp.concatenate(
            [lower_half(q32[:, (hd // 2) * LANES:(hd // 2 + 1) * LANES], hd % 2 == 1) for hd in heads],
            axis=0).astype(BF16)

    qi32 = qi_ref[0].astype(F32)
    qi_rows = jnp.concatenate(
        [lower_half(qi32[:, (hh // 2) * LANES:(hh // 2 + 1) * LANES], hh % 2 == 1) for hh in range(IDX_HEADS)],
        axis=0).astype(BF16)
    d_c = _dot(qi_rows[:, 0:IDX_DIM], kit16)
    d_n = _dot_nt(qi_rows, kw16)
    sc_c = jnp.zeros((tq, past), F32)
    sc_n = jnp.zeros((tq, tq), F32)
    for hh in range(IDX_HEADS):
        w_col = kw[:, IDX_DIM + hh:IDX_DIM + hh + 1]
        rows = slice(hh * tq, (hh + 1) * tq)
        sc_c = sc_c + w_col * jnp.maximum(d_c[rows, :], 0.0)
        sc_n = sc_n + w_col * jnp.maximum(d_n[rows, :], 0.0)
    q_chunk = (past + lax.broadcasted_iota(I32, (tq, 1), 0)) // CHUNK
    k_chunk = (past + lax.broadcasted_iota(I32, (1, tq), 1)) // CHUNK
    key_c = _order_key(sc_c)
    key_n = jnp.where(k_chunk <= q_chunk, _order_key(sc_n), KEY_NEG_INF)

    def count(pred):
        return (jnp.sum(pred(key_c).astype(I32), axis=-1, keepdims=True)
                + jnp.sum(pred(key_n).astype(I32), axis=-1, keepdims=True))

    def bit_step(i, thr):
        cand = thr ^ lax.shift_left(jnp.int32(1), 31 - i)
        return jnp.where(count(lambda k: k >= cand) >= n_sel, cand, thr)

    thr = lax.fori_loop(0, 32, bit_step, jnp.full((tq, 1), INT_MIN, I32))
    wanted = (n_sel - count(lambda k: k > thr)).astype(F32)

    def select(keys, seen):
        tie = keys == thr
        tie_f = jnp.where(tie, 1.0, 0.0)
        ranks = []
        for c0 in range(0, keys.shape[1], LANES):
            cw = min(LANES, keys.shape[1] - c0)
            r_i = lax.broadcasted_iota(I32, (cw, cw), 0)
            c_i = lax.broadcasted_iota(I32, (cw, cw), 1)
            upper = jnp.where(r_i <= c_i, 1.0, 0.0).astype(BF16)
            part = tie_f[:, c0:c0 + cw]
            ranks.append(seen + _dot(part.astype(BF16), upper))
            seen = seen + jnp.sum(part, axis=-1, keepdims=True)
        rank = ranks[0] if len(ranks) == 1 else jnp.concatenate(ranks, axis=1)
        sel = ((keys > thr) | (tie & (rank <= wanted))) & (keys > KEY_NEG_INF)
        return jnp.where(sel, 0.0, -jnp.inf), seen

    bias_c, seen = select(key_c, jnp.zeros((tq, 1), F32))
    bias_n, _ = select(key_n, seen)
    bias_c = jnp.concatenate([bias_c] * n_grp, axis=0)
    bias_n = jnp.concatenate([bias_n] * n_grp, axis=0)

    groups = range(N_KV_HEADS)
    rows = [slice(g * HEAD_DIM, (g + 1) * HEAD_DIM) for g in groups]
    l_c = [_dot(qg_s[g][:, 0:HEAD_DIM], kt16[rows[g], :]) for g in groups]
    l_n = [_dot_nt(qg_s[g], kn_s[g]) for g in groups]
    tail = lax.bitcast_convert_type(l_c[-1][:, past - LANES:past], I32) & lax.shift_right_arithmetic(pl.program_id(0), 31)
    l_n[0] = l_n[0] + lax.bitcast_convert_type(tail, F32)[:, 0:tq]
    for g in groups:
        lc, ln = l_c[g] + bias_c, l_n[g] + bias_n
        mx = jnp.maximum(jnp.max(lc, axis=-1, keepdims=True), jnp.max(ln, axis=-1, keepdims=True))
        p_c = jnp.exp2(lc - mx)
        p_n = jnp.exp2(ln - mx)
        den = jnp.sum(p_c, axis=-1, keepdims=True) + jnp.sum(p_n, axis=-1, keepdims=True)
        pv = (_dot_nt(p_c.astype(BF16), vt16[rows[g], :])
              + _dot(p_n.astype(BF16), vn_s[g])[:, 0:HEAD_DIM])
        og_s[g] = pv / den

    tiles = []
    for m in range(N_HEADS // 2):
        pair = []
        for hd in (2 * m, 2 * m + 1):
            h = hd % n_grp
            pair.append(og_s[hd // n_grp, h * tq:(h + 1) * tq, :])
        tiles.append(jnp.concatenate(pair, axis=1))
    o = jnp.concatenate(tiles, axis=1).astype(BF16)
    o_ref[0] = _dot(o, wout_ref[...]) + x_ref[0]


def _dsa_attn_sample(q, qi, kw, k, v, cache_kt, cache_vt, cache_kit, x, w_out):
    b, t, d = x.shape
    past = cache_kt.shape[-1]
    assert past % LANES == 0 and t % 8 == 0, (past, t)
    n_sel = min(TOPK_MAX, (past + t) // 4)
    n_kv = N_KV_HEADS * HEAD_DIM
    n_grp = N_HEADS // N_KV_HEADS
    kern = functools.partial(_attn_sample_kernel, tq=t, past=past, n_sel=n_sel)

    def spec(rows, n):
        return pl.BlockSpec((1, rows, n), lambda i: (i, 0, 0))

    return pl.pallas_call(
        kern,
        grid=(b,),
        in_specs=[spec(t, q.shape[-1]), spec(t, qi.shape[-1]), spec(t, LANES), spec(t, n_kv), spec(t, n_kv),
                  spec(n_kv, past), spec(n_kv, past), spec(IDX_DIM, past),
                  spec(t, d), _const_spec(w_out.shape)],
        out_specs=spec(t, d),
        out_shape=jax.ShapeDtypeStruct(x.shape, F32),
        scratch_shapes=[
            pltpu.VMEM((n_kv, past), BF16), pltpu.VMEM((n_kv, past), BF16),
            pltpu.VMEM((N_KV_HEADS, n_grp * t, LANES), BF16),
            pltpu.VMEM((N_KV_HEADS, t, LANES), BF16), pltpu.VMEM((N_KV_HEADS, t, LANES), BF16),
            pltpu.VMEM((N_KV_HEADS, n_grp * t, HEAD_DIM), F32),
        ],
        compiler_params=_cparams(1),
        name="dsa_attn_sample",
    )(q, qi, kw, k, v, cache_kt, cache_vt, cache_kit, x, w_out)


def _tiling(b, t):
    if t >= 256:
        return dict(bb=1, tt=256, blk=128, sub=32)
    bb = max(1, min(b, 128 // t))
    return dict(bb=bb, tt=t, blk=t, sub=t)


def _trunk(x, hg_state, conv_state, cache, p):
    b, t, d = x.shape
    tl = _tiling(b, t)
    bb, tt = tl["bb"], tl["tt"]
    x, s_new = _hgrn_mixer(x, hg_state[0], p["norm_mix"][0], p["hg_w_in"][0], p["lower_bounds"][0],
                           p["hg_norm"][0], p["hg_w_out"][0], **dict(tl, tt=512 if t % 512 == 0 else tt))
    x, cs0 = _conv_ffn(x, conv_state[0], p["norm_ffn"][0], p["ffn_w_in"][0], p["ffn_conv_w"][0],
                       p["ffn_conv_b"][0], p["ffn_w_down"][0], p["norm_final"], bb=bb, tt=tt, final_norm=False)
    past = 0 if cache is None else cache[0].shape[1]
    tables = _rope_tables(past + jnp.arange(t))
    q, k, v, qi, kw = _dsa_proj(x, p["norm_mix"][1], p["ds_w_in"][0], tables, p["ds_kln_w"][0],
                                p["ds_kln_b"][0], bb=bb, tt=tt)
    if cache is None:
        x = _dsa_attn_prompt(q, qi, kw, k, v, x, p["ds_w_out"][0], tq=min(t, 512), kc=min(t, 512))
    else:
        ck, cv, cki = cache
        def keys_last(c):
            return jnp.swapaxes(c.reshape(b, past, -1), 1, 2)

        x = _dsa_attn_sample(q, qi, kw, k, v, keys_last(ck), keys_last(cv), keys_last(cki), x, p["ds_w_out"][0])
    x, cs1 = _conv_ffn(x, conv_state[1], p["norm_ffn"][1], p["ffn_w_in"][1], p["ffn_conv_w"][1],
                       p["ffn_conv_b"][1], p["ffn_w_down"][1], p["norm_final"], bb=bb, tt=tt, final_norm=True)
    k = k.reshape(1, b, t, N_KV_HEADS, HEAD_DIM)
    v = v.reshape(1, b, t, N_KV_HEADS, HEAD_DIM)
    ki = kw[:, :, :IDX_DIM].reshape(1, b, t, IDX_DIM)
    return x, s_new[None], jnp.stack([cs0, cs1]), k, v, ki


def kernel(x_prompt, x_sample, cache_k, cache_v, cache_kidx, state_hgrn, state_conv, norm_mix, norm_ffn, norm_final, hg_w_in, hg_lb, hg_norm, hg_w_out, ds_w_in, ds_kln_w, ds_kln_b, ds_w_out, ffn_w_in, ffn_conv_w, ffn_conv_b, ffn_w_down):
    dsa_in = ds_w_in.shape[-1]
    dsa_pad = (-dsa_in) % LANES
    p = dict(
        norm_mix=norm_mix, norm_ffn=norm_ffn, norm_final=norm_final,
        hg_w_in=hg_w_in.astype(BF16), hg_norm=hg_norm, hg_w_out=hg_w_out.astype(BF16),
        lower_bounds=jnp.cumsum(jax.nn.softmax(hg_lb.astype(F32), axis=0), axis=0),
        ds_w_in=jnp.pad(ds_w_in, ((0, 0), (0, 0), (0, dsa_pad))).astype(BF16),
        ds_kln_w=ds_kln_w, ds_kln_b=ds_kln_b, ds_w_out=ds_w_out.astype(BF16),
        ffn_w_in=ffn_w_in.astype(BF16), ffn_conv_w=ffn_conv_w, ffn_conv_b=ffn_conv_b,
        ffn_w_down=ffn_w_down.astype(BF16),
    )
    b = x_prompt.shape[0]
    hg0 = jnp.zeros((state_hgrn.shape[0], b) + state_hgrn.shape[2:], F32)
    conv0 = jnp.zeros((state_conv.shape[0], b) + state_conv.shape[2:], F32)
    y_p, hg_p, conv_p, k_p, v_p, ki_p = _trunk(x_prompt, hg0, conv0, None, p)
    y_s, hg_s, conv_s, k_s, v_s, ki_s = _trunk(x_sample, state_hgrn, state_conv,
                                               (cache_k[0], cache_v[0], cache_kidx[0]), p)
    return (y_p, y_s, k_p, v_p, ki_p, hg_p, conv_p, k_s, v_s, ki_s, hg_s, conv_s)
```

```python
import functools

import jax
import jax.numpy as jnp
from jax import lax
from jax.experimental import pallas as pl
from jax.experimental.pallas import tpu as pltpu

F32 = jnp.float32
BF16 = jnp.bfloat16
I32 = jnp.int32
I16 = jnp.int16

CHUNK = 64
EPS = 1e-6
HG_HEADS = 8
HG_KDIM = 128
N_HEADS = 16
HEAD_DIM = 64
N_KV_HEADS = 4
IDX_HEADS = 8
IDX_DIM = 64
TOPK_MAX = 256
ROPE_THETA = 500000.0
CONV_W = 3

LANES = 128
VMEM_LIMIT = 56 * 1024 * 1024
EXP_CLAMP = 80.0
LOG2_E = 1.4426950408889634
ONES_ROWS = 16
BOUND_SLACK = 1.02
MIN_DENOMINATOR = 2.0 ** -60
KEY_NEG_INF = -2139095041
INT_MIN = -2147483648


def _cparams(n_axes, flags=None):
    return pltpu.CompilerParams(dimension_semantics=("arbitrary",) * n_axes,
                                vmem_limit_bytes=VMEM_LIMIT, flags=flags)


def _const_spec(shape):
    nd = len(shape)
    return pl.BlockSpec(shape, lambda *_: (0,) * nd, pipeline_mode=pl.Buffered(1))


def _rms(x, g):
    return x * lax.rsqrt(jnp.mean(x * x, axis=-1, keepdims=True) + EPS) * g


def _silu(x):
    return x * jax.nn.sigmoid(x)


def _dot(a, b):
    return jnp.dot(a, b, preferred_element_type=F32)


def _dot_nt(a, b):
    return lax.dot_general(a, b, (((1,), (1,)), ((), ())), preferred_element_type=F32)


def _dot_tn(a, b):
    return lax.dot_general(a, b, (((0,), (0,)), ((), ())), preferred_element_type=F32)


def _hgrn_kernel(x_ref, s0_ref, g_ref, win_ref, lb_ref, hn_ref, wout_ref, o_ref, s_ref,
                 q_s, k_s, g_s, v_s, gate_s, on_s, st_s, o_s, *, bb, tt, blk, sub):
    j = pl.program_id(1)
    rows = bb * tt
    d = x_ref.shape[-1]
    fdim = HG_HEADS * HG_KDIM
    n_sub = blk // sub

    @pl.when(j == 0)
    def _():
        for bi in range(bb):
            for hh in range(HG_HEADS):
                st_s[bi * HG_HEADS + hh] = s0_ref[bi, hh].T

    x = x_ref[...].reshape(rows, d)
    h = _rms(x, g_ref[...]).astype(BF16)
    lb = lb_ref[...]
    fg = lb + (1.0 - lb) * jax.nn.sigmoid(_dot(h, win_ref[:, fdim:2 * fdim]))
    k_s[...] = 1.0 - fg
    g_s[...] = jnp.log(fg)
    q_s[...] = _silu(_dot(h, win_ref[:, 0:fdim]))
    gate_s[...] = _silu(_dot(h, win_ref[:, 2 * fdim + d:]))
    v_s[...] = _dot(h, win_ref[:, 2 * fdim:2 * fdim + d])

    r_i = lax.broadcasted_iota(I32, (blk, blk), 0)
    c_i = lax.broadcasted_iota(I32, (blk, blk), 1)
    causal = c_i <= r_i
    tril = jnp.where(causal, 1.0, 0.0).astype(BF16)
    hn = hn_ref[...]
    blocks_per_stream = tt // blk

    def block_body(idx, carry):
        r0 = pl.multiple_of(idx * blk, blk)
        bi = idx // blocks_per_stream
        lg = g_s[pl.ds(r0, blk), :]
        lg_hi = lg.astype(BF16)
        rem = lg - lg_hi.astype(F32)
        lg_mid = rem.astype(BF16)
        lg_lo = (rem - lg_mid.astype(F32)).astype(BF16)
        g_all = _dot(tril, lg_hi) + _dot(tril, lg_mid) + _dot(tril, lg_lo)
        run_zero = lax.shift_right_arithmetic(idx, 31)

        def anchor(x):
            tail = lax.bitcast_convert_type(x[x.shape[0] - 8:, :], I32) & run_zero
            return lax.bitcast_convert_type(tail, F32)[0:1, :]

        heads = [slice(hh * HG_KDIM, (hh + 1) * HG_KDIM) for hh in range(HG_HEADS)]
        qcat, kcat, qfull, kdec_last, v16, decay = [], [], [], [], [], []
        for sl in heads:
            gc = g_all[:, sl]
            q = q_s[pl.ds(r0, blk), sl]
            kk = k_s[pl.ds(r0, blk), sl]
            refs = [jnp.zeros((1, HG_KDIM), F32)] + [gc[s * sub - 1:s * sub, :] for s in range(1, n_sub)]
            q_parts, k_parts = [], []
            for s in range(n_sub):
                lo_r, hi_r = s * sub, (s + 1) * sub
                qg = q[lo_r:hi_r, :] * jnp.exp(gc[lo_r:hi_r, :] - refs[s])
                pieces = [qg]
                if lo_r:
                    pieces.insert(0, jnp.zeros((lo_r, HG_KDIM), F32))
                if blk - hi_r:
                    pieces.append(jnp.zeros((blk - hi_r, HG_KDIM), F32))
                q_parts.append(pieces[0] if len(pieces) == 1 else jnp.concatenate(pieces, axis=0))
                kd_s = kk[0:hi_r, :] * jnp.exp(jnp.minimum(refs[s] - gc[0:hi_r, :], EXP_CLAMP))
                k_parts.append(kd_s if hi_r == blk else
                               jnp.concatenate([kd_s, jnp.zeros((blk - hi_r, HG_KDIM), F32)], axis=0))
            qcat.append(jnp.concatenate(q_parts, axis=1).astype(BF16))
            kcat.append(jnp.concatenate(k_parts, axis=1).astype(BF16))
            qfull.append((q * jnp.exp(gc)).astype(BF16))
            g_last = gc[blk - 1:blk, :]
            kdec_last.append((kk * jnp.exp(g_last - gc)).astype(BF16))
            decay.append(jnp.exp(g_last))
            v16.append(v_s[pl.ds(r0, blk), sl].astype(BF16))
        a_list = [_dot_nt(qcat[hh], kcat[hh]) for hh in range(HG_HEADS)]
        s_inc = [_dot_tn(v16[hh], kdec_last[hh]) for hh in range(HG_HEADS)]
        a_list[0] = a_list[0] + anchor(a_list[-1])
        o_list = []
        for hh in range(HG_HEADS):
            a = jnp.where(causal, a_list[hh], 0.0).astype(BF16)
            st = st_s[bi * HG_HEADS + hh]
            o_list.append(_dot(a, v16[hh]) + _dot_nt(qfull[hh], st.astype(BF16)))
            st_s[bi * HG_HEADS + hh] = st * decay[hh] + s_inc[hh]
        o_list[0] = o_list[0] + anchor(o_list[-1])
        for hh, sl in enumerate(heads):
            on = _rms(o_list[hh], hn) * gate_s[pl.ds(r0, blk), sl]
            on_s[pl.ds(r0, blk), sl] = on.astype(BF16)
        return carry

    sub_sums = jnp.sum(g_s[...].reshape(rows // sub, sub, fdim), axis=1)
    blocked_ok = jnp.min(sub_sums) >= -EXP_CLAMP

    @pl.when(blocked_ok)
    def _():
        lax.fori_loop(0, rows // blk, block_body, 0, unroll=4)

    @pl.when(jnp.logical_not(blocked_ok))
    def _():
        grp = 16
        o_s[...] = jnp.zeros(o_s.shape, F32)
        in_grp = lax.broadcasted_iota(I32, (grp, 1), 0)

        def frame_body(r, carry):
            r0 = pl.multiple_of((r // grp) * grp, grp)
            bi = r // tt
            this = in_grp == (r % grp)
            for hh in range(HG_HEADS):
                sl = slice(hh * HG_KDIM, (hh + 1) * HG_KDIM)

                def only(ref):
                    return jnp.where(this, ref[pl.ds(r0, grp), sl], 0.0)

                forget = jnp.exp(jnp.sum(only(g_s), axis=0, keepdims=True))
                st = (st_s[bi * HG_HEADS + hh] * forget
                      + _dot_tn(only(v_s).astype(BF16), only(k_s).astype(BF16)))
                st_s[bi * HG_HEADS + hh] = st
                o_s[pl.ds(r0, grp), sl] += _dot_nt(only(q_s).astype(BF16), st.astype(BF16))
            return carry

        lax.fori_loop(0, rows, frame_body, 0)
        for hh in range(HG_HEADS):
            sl = slice(hh * HG_KDIM, (hh + 1) * HG_KDIM)
            on_s[:, sl] = (_rms(o_s[:, sl], hn) * gate_s[:, sl]).astype(BF16)

    out = _dot(on_s[...], wout_ref[...]) + x
    o_ref[...] = out.reshape(bb, tt, d)

    @pl.when(j == pl.num_programs(1) - 1)
    def _():
        for bi in range(bb):
            for hh in range(HG_HEADS):
                s_ref[bi, hh] = st_s[bi * HG_HEADS + hh].T


def _hgrn_mixer(x, s0, norm_g, w_in, lb, hn, w_out, *, bb, tt, blk, sub):
    b, t, d = x.shape
    assert b % bb == 0 and t % tt == 0 and tt % blk == 0 and blk % sub == 0 and sub % 16 == 0, (b, t, bb, tt, blk, sub)
    assert bb == 1 or tt == t, "several streams per block only when a block holds whole streams"
    fdim = HG_HEADS * HG_KDIM
    rows = bb * tt
    kern = functools.partial(_hgrn_kernel, bb=bb, tt=tt, blk=blk, sub=sub)
    return pl.pallas_call(
        kern,
        grid=(b // bb, t // tt),
        in_specs=[
            pl.BlockSpec((bb, tt, d), lambda i, j: (i, j, 0)),
            pl.BlockSpec((bb, HG_HEADS, HG_KDIM, d // HG_HEADS), lambda i, j: (i, 0, 0, 0)),
            _const_spec((1, d)),
            _const_spec(w_in.shape),
            _const_spec((1, fdim)),
            _const_spec((1, d // HG_HEADS)),
            _const_spec(w_out.shape),
        ],
        out_specs=[
            pl.BlockSpec((bb, tt, d), lambda i, j: (i, j, 0)),
            pl.BlockSpec((bb, HG_HEADS, HG_KDIM, d // HG_HEADS), lambda i, j: (i, 0, 0, 0)),
        ],
        out_shape=[jax.ShapeDtypeStruct(x.shape, F32), jax.ShapeDtypeStruct(s0.shape, F32)],
        scratch_shapes=[
            pltpu.VMEM((rows, fdim), F32),
            pltpu.VMEM((rows, fdim), F32),
            pltpu.VMEM((rows, fdim), F32),
            pltpu.VMEM((rows, d), F32),
            pltpu.VMEM((rows, d), F32),
            pltpu.VMEM((rows, d), BF16),
            pltpu.VMEM((bb * HG_HEADS, d // HG_HEADS, HG_KDIM), F32),
            pltpu.VMEM((rows, d), F32),
        ],
        compiler_params=_cparams(2),
        name="hgrn_mixer",
    )(x, s0, norm_g.reshape(1, d), w_in, lb.reshape(1, fdim), hn.reshape(1, -1), w_out)


CONV_HEAD = 8


def _ffn_kernel(x_ref, cs_ref, g_ref, win_ref, cw_ref, cb_ref, wdn_ref, fg_ref, o_ref, ns_ref,
                a_s, *, bb, tt, final_norm):
    j = pl.program_id(1)
    rows = bb * tt
    d = x_ref.shape[-1]
    dff = cw_ref.shape[-1]
    hist = CONV_W - 1

    @pl.when(j == 0)
    def _():
        a_s[:, CONV_HEAD - hist:CONV_HEAD, :] = cs_ref[...]

    x = x_ref[...].reshape(rows, d)
    h = _rms(x, g_ref[...]).astype(BF16)
    a = _dot(h, win_ref[:, 0:dff])
    u = _dot(h, win_ref[:, dff:])
    a_s[:, CONV_HEAD:CONV_HEAD + tt, :] = a.reshape(bb, tt, dff)
    c = jnp.broadcast_to(cb_ref[...].reshape(1, 1, dff), (bb, tt, dff))
    for w in range(CONV_W):
        lo = CONV_HEAD - hist + w
        c = c + a_s[:, lo:lo + tt, :] * cw_ref[w:w + 1, :].reshape(1, 1, dff)
    new_state = a_s[:, CONV_HEAD + tt - hist:CONV_HEAD + tt, :]
    a_s[:, CONV_HEAD - hist:CONV_HEAD, :] = new_state
    ns_ref[...] = new_state
    act = (_silu(c).reshape(rows, dff) * u).astype(BF16)
    y = _dot(act, wdn_ref[...]) + x
    if final_norm:
        y = _rms(y, fg_ref[...])
    o_ref[...] = y.reshape(bb, tt, d)


def _conv_ffn(x, conv_state, norm_g, w_in, conv_w, conv_b, w_down, final_g, *, bb, tt, final_norm):
    b, t, d = x.shape
    assert b % bb == 0 and t % tt == 0 and tt >= CONV_W - 1 and (bb == 1 or tt == t), (b, t, bb, tt)
    dff = conv_w.shape[-1]
    kern = functools.partial(_ffn_kernel, bb=bb, tt=tt, final_norm=final_norm)
    return pl.pallas_call(
        kern,
        grid=(b // bb, t // tt),
        in_specs=[
            pl.BlockSpec((bb, tt, d), lambda i, j: (i, j, 0)),
            pl.BlockSpec((bb, CONV_W - 1, dff), lambda i, j: (i, 0, 0)),
            _const_spec((1, d)),
            _const_spec(w_in.shape),
            _const_spec(conv_w.shape),
            _const_spec((1, dff)),
            _const_spec(w_down.shape),
            _const_spec((1, d)),
        ],
        out_specs=[
            pl.BlockSpec((bb, tt, d), lambda i, j: (i, j, 0)),
            pl.BlockSpec((bb, CONV_W - 1, dff), lambda i, j: (i, 0, 0)),
        ],
        out_shape=[jax.ShapeDtypeStruct(x.shape, F32), jax.ShapeDtypeStruct(conv_state.shape, F32)],
        scratch_shapes=[pltpu.VMEM((bb, CONV_HEAD + tt, dff), F32)],
        compiler_params=_cparams(2),
        name="conv_ffn",
    )(x, conv_state, norm_g.reshape(1, d), w_in, conv_w, conv_b.reshape(1, dff), w_down,
      final_g.reshape(1, d))


def _rope_tables(pos):
    rot = HEAD_DIM // 4
    half = rot // 2
    inv_freq = ROPE_THETA ** (-jnp.arange(half, dtype=F32) / half)
    ang = pos.astype(F32)[:, None] * inv_freq[None, :]
    cos, sin = jnp.cos(ang), jnp.sin(ang)
    t = pos.shape[0]
    pad = HEAD_DIM - rot
    one = jnp.ones((t, pad), F32)
    zero_h = jnp.zeros((t, half), F32)
    zero_p = jnp.zeros((t, pad), F32)
    c64 = jnp.concatenate([cos, cos, one], axis=1)
    s1_64 = jnp.concatenate([zero_h, sin, zero_p], axis=1)
    s2_64 = jnp.concatenate([-sin, zero_h, zero_p], axis=1)
    ident = jnp.ones((t, HEAD_DIM), F32)
    zero64 = jnp.zeros((t, HEAD_DIM), F32)
    return jnp.stack([
        jnp.concatenate([c64, c64], axis=1), jnp.concatenate([s1_64, s1_64], axis=1),
        jnp.concatenate([s2_64, s2_64], axis=1),
        jnp.concatenate([c64, ident], axis=1), jnp.concatenate([s1_64, zero64], axis=1),
        jnp.concatenate([s2_64, zero64], axis=1)])


def _dsa_proj_kernel(x_ref, g_ref, w_ref, tab_ref, lnw_ref, lnb_ref,
                     q_ref, k_ref, v_ref, qi_ref, kw_ref, *, bb, tt):
    rows = bb * tt
    d = x_ref.shape[-1]
    half = HEAD_DIM // 8
    o1 = N_HEADS * HEAD_DIM
    o2 = o1 + N_KV_HEADS * HEAD_DIM
    o3 = o2 + N_KV_HEADS * HEAD_DIM
    o4 = o3 + IDX_HEADS * IDX_DIM

    def tab(i):
        t = tab_ref[i]
        return jnp.broadcast_to(t[None], (bb, tt, LANES)).reshape(rows, LANES)

    def rope(y, base):
        cos, s1, s2 = tab(base), tab(base + 1), tab(base + 2)
        tiles = []
        for m in range(y.shape[1] // LANES):
            yt = y[:, m * LANES:(m + 1) * LANES]
            tiles.append(yt * cos + pltpu.roll(yt, half, 1) * s1 + pltpu.roll(yt, LANES - half, 1) * s2)
        return tiles[0] if len(tiles) == 1 else jnp.concatenate(tiles, axis=1)

    x = x_ref[...].reshape(rows, d)
    h = _rms(x, g_ref[...]).astype(BF16)
    t = _dot(h, w_ref[:, o4:o4 + LANES])
    lane = lax.broadcasted_iota(I32, (1, LANES), 1)
    is_ki = lane < IDX_DIM
    mu = jnp.sum(jnp.where(is_ki, t, 0.0), axis=-1, keepdims=True) / IDX_DIM
    cen = jnp.where(is_ki, t - mu, 0.0)
    var = jnp.sum(cen * cen, axis=-1, keepdims=True) / IDX_DIM
    ki = rope(cen * lax.rsqrt(var + EPS) * lnw_ref[...] + lnb_ref[...], 3)
    wi = t * ((IDX_HEADS * IDX_DIM) ** -0.5)
    kw = jnp.where(is_ki, ki, jnp.where(lane < IDX_DIM + IDX_HEADS, wi, 0.0))
    kw_ref[...] = kw.reshape(bb, tt, LANES)
    qi_ref[...] = rope(_dot(h, w_ref[:, o3:o4]), 0).astype(BF16).reshape(bb, tt, o4 - o3)
    k_ref[...] = rope(_dot(h, w_ref[:, o1:o2]), 0).reshape(bb, tt, o2 - o1)
    q = rope(_dot(h, w_ref[:, 0:o1]), 0) * (HEAD_DIM ** -0.5 * LOG2_E)
    q_ref[...] = q.astype(BF16).reshape(bb, tt, o1)
    v_ref[...] = _dot(h, w_ref[:, o2:o3]).reshape(bb, tt, o3 - o2)


def _dsa_proj(x, norm_g, w_pad, tables, ln_w, ln_b, *, bb, tt):
    b, t, d = x.shape
    n_q = N_HEADS * HEAD_DIM
    n_kv = N_KV_HEADS * HEAD_DIM
    n_qi = IDX_HEADS * IDX_DIM
    pad = jnp.zeros((LANES - IDX_DIM,), F32)
    lnw = jnp.concatenate([ln_w, pad]).reshape(1, LANES)
    lnb = jnp.concatenate([ln_b, pad]).reshape(1, LANES)
    kern = functools.partial(_dsa_proj_kernel, bb=bb, tt=tt)

    def row_spec(n):
        return pl.BlockSpec((bb, tt, n), lambda i, j: (i, j, 0))

    return pl.pallas_call(
        kern,
        grid=(b // bb, t // tt),
        in_specs=[
            row_spec(d),
            _const_spec((1, d)),
            _const_spec(w_pad.shape),
            pl.BlockSpec((6, tt, LANES), lambda i, j: (0, j, 0)),
            _const_spec((1, LANES)),
            _const_spec((1, LANES)),
        ],
        out_specs=[row_spec(n_q), row_spec(n_kv), row_spec(n_kv), row_spec(n_qi), row_spec(LANES)],
        out_shape=[
            jax.ShapeDtypeStruct((b, t, n_q), BF16),
            jax.ShapeDtypeStruct((b, t, n_kv), F32),
            jax.ShapeDtypeStruct((b, t, n_kv), F32),
            jax.ShapeDtypeStruct((b, t, n_qi), BF16),
            jax.ShapeDtypeStruct((b, t, LANES), F32),
        ],
        compiler_params=_cparams(2),
        name="dsa_proj",
    )(x, norm_g.reshape(1, d), w_pad, tables, lnw, lnb)


def _order_key(score):
    score = jnp.where(score == 0.0, 0.0, score)
    bits = lax.bitcast_convert_type(score, I32)
    return bits ^ ((bits >> 31) & 0x7FFFFFFF)


def _attn_prompt_kernel(q_ref, qi_ref, wq_ref, k_ref, v_ref, kw_ref, x_ref, wout_ref, o_ref,
                        ke, ko, vt, kie, kio, qs, key_s, bias_s, ot_s, half_s, kmax_s, thr_s, *, tq, t_len, kc, n_sel):
    j = pl.program_id(1)
    nct = t_len // kc
    sub_per_chunk = kc // LANES
    lane = lax.broadcasted_iota(I32, (1, LANES), 1)
    lo = lane < HEAD_DIM

    unit_hi = jnp.where(lane == HEAD_DIM, 1.0, 0.0)
    unit_lo = jnp.where(lane == 0, 1.0, 0.0)

    r_i = lax.broadcasted_iota(I32, (LANES, LANES), 0)
    c_i = lax.broadcasted_iota(I32, (LANES, LANES), 1)
    half_sums = jnp.where(((r_i < HEAD_DIM) & (c_i == HEAD_DIM)) | ((r_i >= HEAD_DIM) & (c_i == 0)),
                          1.0, 0.0).astype(BF16)

    @pl.when(j == 0)
    def _():
        k_sq_max = [jnp.zeros((1, LANES), F32) for _ in range(N_KV_HEADS // 2)]
        for c in range(nct):
            rows = slice(c * kc, (c + 1) * kc)
            k_c = k_ref[0, rows, :].astype(BF16).astype(F32)
            for m in range(N_KV_HEADS // 2):
                tile = k_c[:, m * LANES:(m + 1) * LANES]
                rolled = pltpu.roll(tile, HEAD_DIM, 1)
                ke[(2 * m) * nct + c] = jnp.where(lo, tile, unit_hi).astype(BF16)
                ko[(2 * m) * nct + c] = jnp.where(lo, unit_lo, rolled).astype(BF16)
                ke[(2 * m + 1) * nct + c] = jnp.where(lo, rolled, unit_hi).astype(BF16)
                ko[(2 * m + 1) * nct + c] = jnp.where(lo, unit_lo, tile).astype(BF16)
                norm2 = _dot((tile * tile).astype(BF16), half_sums)
                k_sq_max[m] = jnp.maximum(k_sq_max[m], jnp.max(norm2, axis=0, keepdims=True))
            v_t = v_ref[0, rows, :].T
            for g in range(N_KV_HEADS):
                vt[g * nct + c, 0:HEAD_DIM, :] = v_t[g * HEAD_DIM:(g + 1) * HEAD_DIM, :].astype(BF16)
                vt[g * nct + c, HEAD_DIM:HEAD_DIM + ONES_ROWS, :] = jnp.ones((ONES_ROWS, kc), BF16)
            kw_c = kw_ref[0, rows, :]
            kie[c] = jnp.where(lo, kw_c, 0.0).astype(BF16)
            kio[c] = jnp.where(lo, 0.0, pltpu.roll(kw_c, HEAD_DIM, 1)).astype(BF16)
        for g in range(N_KV_HEADS):
            at = HEAD_DIM if g % 2 == 0 else 0
            k_sq = jnp.max(jnp.where(lane == at, k_sq_max[g // 2], 0.0), axis=-1, keepdims=True)
            kmax_s[g] = jnp.broadcast_to(jnp.sqrt(k_sq), (1, LANES))

    for m in range(N_HEADS // 2):
        tile = q_ref[0, :, m * LANES:(m + 1) * LANES].astype(F32)
        k_norm = kmax_s[(2 * m) // (N_HEADS // N_KV_HEADS)]
        bound = jnp.sqrt(_dot((tile * tile).astype(BF16), half_sums)) * (k_norm * -BOUND_SLACK)
        qs[2 * m] = jnp.where(lo, tile, jnp.where(lane == HEAD_DIM, bound, 0.0)).astype(BF16)
        qs[2 * m + 1] = jnp.where(lo, jnp.where(lane == 0, bound, 0.0), tile).astype(BF16)

    n_chunks = ((j + 1) * tq + kc - 1) // kc
    w_t = wq_ref[0].T
    w_rows = [w_t[IDX_DIM + hh:IDX_DIM + hh + 1, :] for hh in range(IDX_HEADS)]
    qi = qi_ref[0]
    q_chunk = (j * tq + lax.broadcasted_iota(I32, (1, tq), 1)) // CHUNK

    def score_body(c, carry):
        kie_c, kio_c = kie[c], kio[c]
        sc = jnp.zeros((kc, tq), F32)
        for m in range(IDX_HEADS // 2):
            qt = qi[:, m * LANES:(m + 1) * LANES]
            sc = (sc + w_rows[2 * m] * jnp.maximum(_dot_nt(kie_c, qt), 0.0)
                  + w_rows[2 * m + 1] * jnp.maximum(_dot_nt(kio_c, qt), 0.0))
        k_chunk = (c * kc + lax.broadcasted_iota(I32, (kc, 1), 0)) // CHUNK
        keys = jnp.where(k_chunk <= q_chunk, _order_key(sc), KEY_NEG_INF)
        key_s[c] = keys
        half_s[c] = lax.shift_right_arithmetic(keys, 16).astype(I16)
        return carry

    def score_pair(p, carry):
        score_body(2 * p, carry)
        return score_body(2 * p + 1, carry)

    lax.fori_loop(0, n_chunks // 2, score_pair, 0)
    lax.fori_loop(n_chunks - n_chunks % 2, n_chunks, score_body, 0)

    def search16(need, n_static):
        def count_ge(cand):
            acc = jnp.zeros((16, tq), I16)
            for c in range(n_static):
                hit = (half_s[c] >= cand).astype(I16)
                for r in range(0, kc, 16):
                    acc = acc + hit[r:r + 16, :]
            return jnp.sum(acc.astype(I32), axis=0, keepdims=True)

        def bit_step(i, t16):
            cand = t16 ^ lax.shift_left(jnp.int32(1), 15 - i)
            cand = lax.shift_right_arithmetic(lax.shift_left(cand, 16), 16)
            return jnp.where(count_ge(cand.astype(I16)) >= need, cand, t16)

        return lax.fori_loop(0, 16, bit_step, jnp.full((1, tq), -32768, I32))

    def find_threshold(n_static):
        thr_hi = search16(n_sel, n_static)
        n_above = jnp.zeros((8, tq), I32)
        for c in range(n_static):
            keys = key_s[c]
            hi = lax.shift_right_arithmetic(keys, 16)
            low = (keys & 0xFFFF) - 32768
            half_s[c] = jnp.where(hi == thr_hi, low, -32768).astype(I16)
            n_above = n_above + jnp.sum((hi > thr_hi).astype(I32).reshape(kc // 8, 8, tq), axis=0)
        thr_lo = search16(n_sel - jnp.sum(n_above, axis=0, keepdims=True), n_static)
        thr_s[...] = jnp.broadcast_to(lax.shift_left(thr_hi, 16) | (thr_lo + 32768), thr_s.shape)

    for n_static in range(1, nct + 1):
        pl.when(n_chunks == n_static)(functools.partial(find_threshold, n_static))
    thr = thr_s[0:1, :]

    def bias_body(c, acc):
        keys = key_s[c]
        ge = keys >= thr
        bias_s[c] = jnp.where(ge & (keys > KEY_NEG_INF), 0.0, -jnp.inf)
        fold = lambda hit: jnp.sum(hit.astype(I32).reshape(kc // 8, 8, tq), axis=0)
        return acc[0] + fold(ge), acc[1] + fold(keys > thr)

    zero8 = jnp.zeros((8, tq), I32)
    n_ge, n_gt = [jnp.sum(a, axis=0, keepdims=True)
                  for a in lax.fori_loop(0, n_chunks, bias_body, (zero8, zero8))]

    has_excess = jnp.max(jnp.where(thr > KEY_NEG_INF, n_ge - n_sel, 0)) > 0

    @pl.when(has_excess)
    def _():
        wanted = (n_sel - n_gt).astype(F32)
        r_i = lax.broadcasted_iota(I32, (LANES, LANES), 0)
        c_i = lax.broadcasted_iota(I32, (LANES, LANES), 1)
        lower = jnp.where(c_i <= r_i, 1.0, 0.0).astype(BF16)

        def tie_body(u, seen):
            c = u // sub_per_chunk
            r0 = pl.multiple_of((u % sub_per_chunk) * LANES, LANES)
            keys = key_s[c, pl.ds(r0, LANES), :]
            tie = keys == thr
            tie_f = jnp.where(tie, 1.0, 0.0)
            rank = seen + _dot(lower, tie_f.astype(BF16))
            sel = ((keys > thr) | (tie & (rank <= wanted))) & (keys > KEY_NEG_INF)
            bias_s[c, pl.ds(r0, LANES), :] = jnp.where(sel, 0.0, -jnp.inf)
            return seen + jnp.sum(tie_f, axis=0, keepdims=True)

        lax.fori_loop(0, n_chunks * sub_per_chunk, tie_body, jnp.zeros((1, tq), F32))

    n_grp = N_HEADS // N_KV_HEADS
    acc0 = jnp.zeros((HEAD_DIM + ONES_ROWS, tq), F32)

    def attend(running_max):
        def group_body(g, den_min):
            q_tiles = [qs[g * n_grp + h] for h in range(n_grp)]

            def chunks_body(cs, st):
                biases = [bias_s[c] for c in cs]
                v_cs = [vt[g * nct + c] for c in cs]
                logits = [[_dot_nt((ke, ko)[h % 2][g * nct + c], q_tiles[h]) for h in range(n_grp)] for c in cs]
                tails = [lax.bitcast_convert_type(l[kc - 8:kc, :], I32) for ls in logits for l in ls][1:]
                anchor = functools.reduce(jnp.bitwise_or, tails) & lax.shift_right_arithmetic(cs[0], 31)
                anchor = lax.bitcast_convert_type(anchor, F32)[0:1, :]
                st = list(st)
                for i in range(len(cs)):
                    for h in range(n_grp):
                        logit = logits[i][h] + biases[i]
                        if i == 0 and h == 0:
                            logit = logit + anchor
                        if running_max:
                            m_run, acc = st[2 * h:2 * h + 2]
                            m_new = jnp.maximum(m_run, jnp.max(logit, axis=0, keepdims=True))
                            m_safe = jnp.where(m_new > -jnp.inf, m_new, 0.0)
                            p = jnp.exp2(logit - m_safe)
                            st[2 * h:2 * h + 2] = [m_new, jnp.exp2(m_run - m_safe) * acc + _dot(v_cs[i], p.astype(BF16))]
                        else:
                            st[h] = st[h] + _dot(v_cs[i], jnp.exp2(logit).astype(BF16))
                return tuple(st)

            if running_max:
                st = lax.fori_loop(0, n_chunks, lambda c, st: chunks_body([c], st),
                                   (jnp.full((1, tq), -jnp.inf, F32), acc0) * n_grp)
                accs = st[1::2]
            else:
                accs = lax.fori_loop(0, n_chunks // 2, lambda p, st: chunks_body([2 * p, 2 * p + 1], st),
                                     (acc0,) * n_grp)
                accs = lax.fori_loop(n_chunks - n_chunks % 2, n_chunks, lambda c, st: chunks_body([c], st), accs)
            for h in range(n_grp):
                r0 = pl.multiple_of(g * (n_grp * HEAD_DIM) + h * HEAD_DIM, HEAD_DIM)
                den = accs[h][HEAD_DIM:HEAD_DIM + 1, :]
                ot_s[pl.ds(r0, HEAD_DIM), :] = accs[h][0:HEAD_DIM, :] / den
                den_min = jnp.minimum(den_min, den)
            return den_min

        return lax.fori_loop(0, N_KV_HEADS, group_body, jnp.full((1, tq), jnp.inf, F32))

    den_min = attend(running_max=False)

    @pl.when(jnp.logical_not(jnp.min(den_min) >= MIN_DENOMINATOR))
    def _():
        attend(running_max=True)

    o = jnp.concatenate([ot_s[m * LANES:(m + 1) * LANES, :].T for m in range(N_HEADS // 2)], axis=1)
    o_ref[0] = _dot(o.astype(BF16), wout_ref[...]) + x_ref[0]


def _dsa_attn_prompt(q, qi, kw, k, v, x, w_out, *, tq, kc):
    b, t, d = x.shape
    assert t % tq == 0 and t % kc == 0 and tq % CHUNK == 0 and kc % LANES == 0, (t, tq, kc)
    n_sel = min(TOPK_MAX, t // 4)
    n_kv = N_KV_HEADS * HEAD_DIM
    nct = t // kc
    kern = functools.partial(_attn_prompt_kernel, tq=tq, t_len=t, kc=kc, n_sel=n_sel)

    def tile_spec(n):
        return pl.BlockSpec((1, tq, n), lambda i, j: (i, j, 0))

    def full_spec(n):
        return pl.BlockSpec((1, t, n), lambda i, j: (i, 0, 0))

    return pl.pallas_call(
        kern,
        grid=(b, t // tq),
        in_specs=[tile_spec(q.shape[-1]), tile_spec(qi.shape[-1]), tile_spec(LANES),
                  full_spec(n_kv), full_spec(n_kv), full_spec(LANES),
                  tile_spec(d), _const_spec(w_out.shape)],
        out_specs=tile_spec(d),
        out_shape=jax.ShapeDtypeStruct(x.shape, F32),
        scratch_shapes=[
            pltpu.VMEM((N_KV_HEADS * nct, kc, LANES), BF16), pltpu.VMEM((N_KV_HEADS * nct, kc, LANES), BF16),
            pltpu.VMEM((N_KV_HEADS * nct, HEAD_DIM + ONES_ROWS, kc), BF16),
            pltpu.VMEM((nct, kc, LANES), BF16), pltpu.VMEM((nct, kc, LANES), BF16),
            pltpu.VMEM((N_HEADS, tq, LANES), BF16),
            pltpu.VMEM((nct, kc, tq), I32), pltpu.VMEM((nct, kc, tq), F32),
            pltpu.VMEM((N_HEADS * HEAD_DIM, tq), F32),
            pltpu.VMEM((nct, kc, tq), I16),
            pltpu.VMEM((N_KV_HEADS, 1, LANES), F32),
            pltpu.VMEM((8, tq), I32),
        ],
        compiler_params=_cparams(2),
        name="dsa_attn_prompt",
    )(q, qi, kw, k, v, kw, x, w_out)


def _attn_sample_kernel(q_ref, qi_ref, kw_ref, k_ref, v_ref, ckt_ref, cvt_ref, ckit_ref, x_ref, wout_ref, o_ref,
                        kt16, vt16, qg_s, kn_s, vn_s, og_s, *, tq, past, n_sel):
    n_grp = N_HEADS // N_KV_HEADS
    lane = lax.broadcasted_iota(I32, (1, LANES), 1)
    lo = lane < HEAD_DIM

    def lower_half(x128, upper):
        return jnp.where(lo, pltpu.roll(x128, HEAD_DIM, 1) if upper else x128, 0.0)

    kt16[...] = ckt_ref[0].astype(BF16)
    vt16[...] = cvt_ref[0].astype(BF16)
    kit16 = ckit_ref[0].astype(BF16)
    kw = kw_ref[0]
    kw16 = kw.astype(BF16)
    k_new, v_new = k_ref[0], v_ref[0]
    q32 = q_ref[0].astype(F32)
    for g in range(N_KV_HEADS):
        tile = slice((g // 2) * LANES, (g // 2 + 1) * LANES)
        kn_s[g] = lower_half(k_new[:, tile], g % 2 == 1).astype(BF16)
        vn_s[g] = lower_half(v_new[:, tile], g % 2 == 1).astype(BF16)
        heads = [n_grp * g + h for h in range(n_grp)]
        qg_s[g] = jnp.concatenate(
            [lower_half(q32[:, (hd // 2) * LANES:(hd // 2 + 1) * LANES], hd % 2 == 1) for hd in heads],
            axis=0).astype(BF16)

    qi32 = qi_ref[0].astype(F32)
    qi_rows = jnp.concatenate(
        [lower_half(qi32[:, (hh // 2) * LANES:(hh // 2 + 1) * LANES], hh % 2 == 1) for hh in range(IDX_HEADS)],
        axis=0).astype(BF16)
    d_c = _dot(qi_rows[:, 0:IDX_DIM], kit16)
    d_n = _dot_nt(qi_rows, kw16)
    sc_c = jnp.zeros((tq, past), F32)
    sc_n = jnp.zeros((tq, tq), F32)
    for hh in range(IDX_HEADS):
        w_col = kw[:, IDX_DIM + hh:IDX_DIM + hh + 1]
        rows = slice(hh * tq, (hh + 1) * tq)
        sc_c = sc_c + w_col * jnp.maximum(d_c[rows, :], 0.0)
        sc_n = sc_n + w_col * jnp.maximum(d_n[rows, :], 0.0)
    q_chunk = (past + lax.broadcasted_iota(I32, (tq, 1), 0)) // CHUNK
    k_chunk = (past + lax.broadcasted_iota(I32, (1, tq), 1)) // CHUNK
    key_c = _order_key(sc_c)
    key_n = jnp.where(k_chunk <= q_chunk, _order_key(sc_n), KEY_NEG_INF)

    def count(pred):
        return (jnp.sum(pred(key_c).astype(I32), axis=-1, keepdims=True)
                + jnp.sum(pred(key_n).astype(I32), axis=-1, keepdims=True))

    def bit_step(i, thr):
        cand = thr ^ lax.shift_left(jnp.int32(1), 31 - i)
        return jnp.where(count(lambda k: k >= cand) >= n_sel, cand, thr)

    thr = lax.fori_loop(0, 32, bit_step, jnp.full((tq, 1), INT_MIN, I32))
    wanted = (n_sel - count(lambda k: k > thr)).astype(F32)

    def select(keys, seen):
        tie = keys == thr
        tie_f = jnp.where(tie, 1.0, 0.0)
        ranks = []
        for c0 in range(0, keys.shape[1], LANES):
            cw = min(LANES, keys.shape[1] - c0)
            r_i = lax.broadcasted_iota(I32, (cw, cw), 0)
            c_i = lax.broadcasted_iota(I32, (cw, cw), 1)
            upper = jnp.where(r_i <= c_i, 1.0, 0.0).astype(BF16)
            part = tie_f[:, c0:c0 + cw]
            ranks.append(seen + _dot(part.astype(BF16), upper))
            seen = seen + jnp.sum(part, axis=-1, keepdims=True)
        rank = ranks[0] if len(ranks) == 1 else jnp.concatenate(ranks, axis=1)
        sel = ((keys > thr) | (tie & (rank <= wanted))) & (keys > KEY_NEG_INF)
        return jnp.where(sel, 0.0, -jnp.inf), seen

    bias_c, seen = select(key_c, jnp.zeros((tq, 1), F32))
    bias_n, _ = select(key_n, seen)
    bias_c = jnp.concatenate([bias_c] * n_grp, axis=0)
    bias_n = jnp.concatenate([bias_n] * n_grp, axis=0)

    groups = range(N_KV_HEADS)
    rows = [slice(g * HEAD_DIM, (g + 1) * HEAD_DIM) for g in groups]
    l_c = [_dot(qg_s[g][:, 0:HEAD_DIM], kt16[rows[g], :]) for g in groups]
    l_n = [_dot_nt(qg_s[g], kn_s[g]) for g in groups]
    tail = lax.bitcast_convert_type(l_c[-1][:, past - LANES:past], I32) & lax.shift_right_arithmetic(pl.program_id(0), 31)
    l_n[0] = l_n[0] + lax.bitcast_convert_type(tail, F32)[:, 0:tq]
    for g in groups:
        lc, ln = l_c[g] + bias_c, l_n[g] + bias_n
        mx = jnp.maximum(jnp.max(lc, axis=-1, keepdims=True), jnp.max(ln, axis=-1, keepdims=True))
        p_c = jnp.exp2(lc - mx)
        p_n = jnp.exp2(ln - mx)
        den = jnp.sum(p_c, axis=-1, keepdims=True) + jnp.sum(p_n, axis=-1, keepdims=True)
        pv = (_dot_nt(p_c.astype(BF16), vt16[rows[g], :])
              + _dot(p_n.astype(BF16), vn_s[g])[:, 0:HEAD_DIM])
        og_s[g] = pv / den

    tiles = []
    for m in range(N_HEADS // 2):
        pair = []
        for hd in (2 * m, 2 * m + 1):
            h = hd % n_grp
            pair.append(og_s[hd // n_grp, h * tq:(h + 1) * tq, :])
        tiles.append(jnp.concatenate(pair, axis=1))
    o = jnp.concatenate(tiles, axis=1).astype(BF16)
    o_ref[0] = _dot(o, wout_ref[...]) + x_ref[0]


def _dsa_attn_sample(q, qi, kw, k, v, cache_kt, cache_vt, cache_kit, x, w_out):
    b, t, d = x.shape
    past = cache_kt.shape[-1]
    assert past % LANES == 0 and t % 8 == 0, (past, t)
    n_sel = min(TOPK_MAX, (past + t) // 4)
    n_kv = N_KV_HEADS * HEAD_DIM
    n_grp = N_HEADS // N_KV_HEADS
    kern = functools.partial(_attn_sample_kernel, tq=t, past=past, n_sel=n_sel)

    def spec(rows, n):
        return pl.BlockSpec((1, rows, n), lambda i: (i, 0, 0))

    return pl.pallas_call(
        kern,
        grid=(b,),
        in_specs=[spec(t, q.shape[-1]), spec(t, qi.shape[-1]), spec(t, LANES), spec(t, n_kv), spec(t, n_kv),
                  spec(n_kv, past), spec(n_kv, past), spec(IDX_DIM, past),
                  spec(t, d), _const_spec(w_out.shape)],
        out_specs=spec(t, d),
        out_shape=jax.ShapeDtypeStruct(x.shape, F32),
        scratch_shapes=[
            pltpu.VMEM((n_kv, past), BF16), pltpu.VMEM((n_kv, past), BF16),
            pltpu.VMEM((N_KV_HEADS, n_grp * t, LANES), BF16),
            pltpu.VMEM((N_KV_HEADS, t, LANES), BF16), pltpu.VMEM((N_KV_HEADS, t, LANES), BF16),
            pltpu.VMEM((N_KV_HEADS, n_grp * t, HEAD_DIM), F32),
        ],
        compiler_params=_cparams(1),
        name="dsa_attn_sample",
    )(q, qi, kw, k, v, cache_kt, cache_vt, cache_kit, x, w_out)


def _tiling(b, t):
    if t >= 256:
        return dict(bb=1, tt=256, blk=128, sub=32)
    bb = max(1, min(b, 128 // t))
    return dict(bb=bb, tt=t, blk=t, sub=t)


def _trunk(x, hg_state, conv_state, cache, p):
    b, t, d = x.shape
    tl = _tiling(b, t)
    bb, tt = tl["bb"], tl["tt"]
    x, s_new = _hgrn_mixer(x, hg_state[0], p["norm_mix"][0], p["hg_w_in"][0], p["lower_bounds"][0],
                           p["hg_norm"][0], p["hg_w_out"][0], **dict(tl, tt=512 if t % 512 == 0 else tt))
    x, cs0 = _conv_ffn(x, conv_state[0], p["norm_ffn"][0], p["ffn_w_in"][0], p["ffn_conv_w"][0],
                       p["ffn_conv_b"][0], p["ffn_w_down"][0], p["norm_final"], bb=bb, tt=tt, final_norm=False)
    past = 0 if cache is None else cache[0].shape[1]
    tables = _rope_tables(past + jnp.arange(t))
    q, k, v, qi, kw = _dsa_proj(x, p["norm_mix"][1], p["ds_w_in"][0], tables, p["ds_kln_w"][0],
                                p["ds_kln_b"][0], bb=bb, tt=2 * tt if (bb == 1 and t % (2 * tt) == 0) else tt)
    if cache is None:
        x = _dsa_attn_prompt(q, qi, kw, k, v, x, p["ds_w_out"][0], tq=min(t, 512), kc=min(t, 512))
    else:
        ck, cv, cki = cache
        def keys_last(c):
            return jnp.swapaxes(c.reshape(b, past, -1), 1, 2)

        x = _dsa_attn_sample(q, qi, kw, k, v, keys_last(ck), keys_last(cv), keys_last(cki), x, p["ds_w_out"][0])
    x, cs1 = _conv_ffn(x, conv_state[1], p["norm_ffn"][1], p["ffn_w_in"][1], p["ffn_conv_w"][1],
                       p["ffn_conv_b"][1], p["ffn_w_down"][1], p["norm_final"], bb=bb, tt=tt, final_norm=True)
    k = k.reshape(1, b, t, N_KV_HEADS, HEAD_DIM)
    v = v.reshape(1, b, t, N_KV_HEADS, HEAD_DIM)
    ki = kw[:, :, :IDX_DIM].reshape(1, b, t, IDX_DIM)
    return x, s_new[None], jnp.stack([cs0, cs1]), k, v, ki


def kernel(x_prompt, x_sample, cache_k, cache_v, cache_kidx, state_hgrn, state_conv, norm_mix, norm_ffn, norm_final, hg_w_in, hg_lb, hg_norm, hg_w_out, ds_w_in, ds_kln_w, ds_kln_b, ds_w_out, ffn_w_in, ffn_conv_w, ffn_conv_b, ffn_w_down):
    dsa_in = ds_w_in.shape[-1]
    dsa_pad = (-dsa_in) % LANES
    p = dict(
        norm_mix=norm_mix, norm_ffn=norm_ffn, norm_final=norm_final,
        hg_w_in=hg_w_in.astype(BF16), hg_norm=hg_norm, hg_w_out=hg_w_out.astype(BF16),
        lower_bounds=jnp.cumsum(jax.nn.softmax(hg_lb.astype(F32), axis=0), axis=0),
        ds_w_in=jnp.pad(ds_w_in, ((0, 0), (0, 0), (0, dsa_pad))).astype(BF16),
        ds_kln_w=ds_kln_w, ds_kln_b=ds_kln_b, ds_w_out=ds_w_out.astype(BF16),
        ffn_w_in=ffn_w_in.astype(BF16), ffn_conv_w=ffn_conv_w, ffn_conv_b=ffn_conv_b,
        ffn_w_down=ffn_w_down.astype(BF16),
    )
    b = x_prompt.shape[0]
    hg0 = jnp.zeros((state_hgrn.shape[0], b) + state_hgrn.shape[2:], F32)
    conv0 = jnp.zeros((state_conv.shape[0], b) + state_conv.shape[2:], F32)
    y_p, hg_p, conv_p, k_p, v_p, ki_p = _trunk(x_prompt, hg0, conv0, None, p)
    y_s, hg_s, conv_s, k_s, v_s, ki_s = _trunk(x_sample, state_hgrn, state_conv,
                                               (cache_k[0], cache_v[0], cache_kidx[0]), p)
    return (y_p, y_s, k_p, v_p, ki_p, hg_p, conv_p, k_s, v_s, ki_s, hg_s, conv_s)
```

```python
import functools

import jax
import jax.numpy as jnp
from jax import lax
from jax.experimental import pallas as pl
from jax.experimental.pallas import tpu as pltpu

F32 = jnp.float32
BF16 = jnp.bfloat16
I32 = jnp.int32
I16 = jnp.int16

CHUNK = 64
EPS = 1e-6
HG_HEADS = 8
HG_KDIM = 128
N_HEADS = 16
HEAD_DIM = 64
N_KV_HEADS = 4
IDX_HEADS = 8
IDX_DIM = 64
TOPK_MAX = 256
ROPE_THETA = 500000.0
CONV_W = 3

LANES = 128
VMEM_LIMIT = 56 * 1024 * 1024
EXP_CLAMP = 80.0
LOG2_E = 1.4426950408889634
ONES_ROWS = 16
BOUND_SLACK = 1.02
MIN_DENOMINATOR = 2.0 ** -60
KEY_NEG_INF = -2139095041
INT_MIN = -2147483648


def _cparams(n_axes, flags=None):
    return pltpu.CompilerParams(dimension_semantics=("arbitrary",) * n_axes,
                                vmem_limit_bytes=VMEM_LIMIT, flags=flags)


def _const_spec(shape):
    nd = len(shape)
    return pl.BlockSpec(shape, lambda *_: (0,) * nd, pipeline_mode=pl.Buffered(1))


def _rms(x, g):
    return x * lax.rsqrt(jnp.mean(x * x, axis=-1, keepdims=True) + EPS) * g


def _silu(x):
    return x * jax.nn.sigmoid(x)


def _dot(a, b):
    return jnp.dot(a, b, preferred_element_type=F32)


def _dot_nt(a, b):
    return lax.dot_general(a, b, (((1,), (1,)), ((), ())), preferred_element_type=F32)


def _dot_tn(a, b):
    return lax.dot_general(a, b, (((0,), (0,)), ((), ())), preferred_element_type=F32)


def _hgrn_kernel(x_ref, s0_ref, g_ref, win_ref, lb_ref, hn_ref, wout_ref, o_ref, s_ref,
                 q_s, k_s, g_s, v_s, gate_s, on_s, st_s, o_s, *, bb, tt, blk, sub):
    j = pl.program_id(1)
    rows = bb * tt
    d = x_ref.shape[-1]
    fdim = HG_HEADS * HG_KDIM
    n_sub = blk // sub

    @pl.when(j == 0)
    def _():
        for bi in range(bb):
            for hh in range(HG_HEADS):
                st_s[bi * HG_HEADS + hh] = s0_ref[bi, hh].T

    x = x_ref[...].reshape(rows, d)
    h = _rms(x, g_ref[...]).astype(BF16)
    lb = lb_ref[...]
    fg = lb + (1.0 - lb) * jax.nn.sigmoid(_dot(h, win_ref[:, fdim:2 * fdim]))
    k_s[...] = 1.0 - fg
    g_s[...] = jnp.log(fg)
    q_s[...] = _silu(_dot(h, win_ref[:, 0:fdim]))
    gate_s[...] = _silu(_dot(h, win_ref[:, 2 * fdim + d:]))
    v_s[...] = _dot(h, win_ref[:, 2 * fdim:2 * fdim + d])

    r_i = lax.broadcasted_iota(I32, (blk, blk), 0)
    c_i = lax.broadcasted_iota(I32, (blk, blk), 1)
    causal = c_i <= r_i
    tril = jnp.where(causal, 1.0, 0.0).astype(BF16)
    hn = hn_ref[...]
    blocks_per_stream = tt // blk

    def block_body(idx, carry):
        r0 = pl.multiple_of(idx * blk, blk)
        bi = idx // blocks_per_stream
        lg = g_s[pl.ds(r0, blk), :]
        lg_hi = lg.astype(BF16)
        rem = lg - lg_hi.astype(F32)
        lg_mid = rem.astype(BF16)
        lg_lo = (rem - lg_mid.astype(F32)).astype(BF16)
        g_all = _dot(tril, lg_hi) + _dot(tril, lg_mid) + _dot(tril, lg_lo)
        run_zero = lax.shift_right_arithmetic(idx, 31)

        def anchor(x):
            tail = lax.bitcast_convert_type(x[x.shape[0] - 8:, :], I32) & run_zero
            return lax.bitcast_convert_type(tail, F32)[0:1, :]

        heads = [slice(hh * HG_KDIM, (hh + 1) * HG_KDIM) for hh in range(HG_HEADS)]
        qcat, kcat, qfull, kdec_last, v16, decay = [], [], [], [], [], []
        for sl in heads:
            gc = g_all[:, sl]
            q = q_s[pl.ds(r0, blk), sl]
            kk = k_s[pl.ds(r0, blk), sl]
            refs = [jnp.zeros((1, HG_KDIM), F32)] + [gc[s * sub - 1:s * sub, :] for s in range(1, n_sub)]
            q_parts, k_parts = [], []
            for s in range(n_sub):
                lo_r, hi_r = s * sub, (s + 1) * sub
                qg = q[lo_r:hi_r, :] * jnp.exp(gc[lo_r:hi_r, :] - refs[s])
                pieces = [qg]
                if lo_r:
                    pieces.insert(0, jnp.zeros((lo_r, HG_KDIM), F32))
                if blk - hi_r:
                    pieces.append(jnp.zeros((blk - hi_r, HG_KDIM), F32))
                q_parts.append(pieces[0] if len(pieces) == 1 else jnp.concatenate(pieces, axis=0))
                kd_s = kk[0:hi_r, :] * jnp.exp(jnp.minimum(refs[s] - gc[0:hi_r, :], EXP_CLAMP))
                k_parts.append(kd_s if hi_r == blk else
                               jnp.concatenate([kd_s, jnp.zeros((blk - hi_r, HG_KDIM), F32)], axis=0))
            qcat.append(jnp.concatenate(q_parts, axis=1).astype(BF16))
            kcat.append(jnp.concatenate(k_parts, axis=1).astype(BF16))
            qfull.append((q * jnp.exp(gc)).astype(BF16))
            g_last = gc[blk - 1:blk, :]
            kdec_last.append((kk * jnp.exp(g_last - gc)).astype(BF16))
            decay.append(jnp.exp(g_last))
            v16.append(v_s[pl.ds(r0, blk), sl].astype(BF16))
        a_list = [_dot_nt(qcat[hh], kcat[hh]) for hh in range(HG_HEADS)]
        s_inc = [_dot_tn(v16[hh], kdec_last[hh]) for hh in range(HG_HEADS)]
        a_list[0] = a_list[0] + anchor(a_list[-1])
        o_list = []
        for hh in range(HG_HEADS):
            a = jnp.where(causal, a_list[hh], 0.0).astype(BF16)
            st = st_s[bi * HG_HEADS + hh]
            o_list.append(_dot(a, v16[hh]) + _dot_nt(qfull[hh], st.astype(BF16)))
            st_s[bi * HG_HEADS + hh] = st * decay[hh] + s_inc[hh]
        o_list[0] = o_list[0] + anchor(o_list[-1])
        for hh, sl in enumerate(heads):
            on = _rms(o_list[hh], hn) * gate_s[pl.ds(r0, blk), sl]
            on_s[pl.ds(r0, blk), sl] = on.astype(BF16)
        return carry

    sub_sums = jnp.sum(g_s[...].reshape(rows // sub, sub, fdim), axis=1)
    blocked_ok = jnp.min(sub_sums) >= -EXP_CLAMP

    @pl.when(blocked_ok)
    def _():
        lax.fori_loop(0, rows // blk, block_body, 0, unroll=4)

    @pl.when(jnp.logical_not(blocked_ok))
    def _():
        grp = 16
        o_s[...] = jnp.zeros(o_s.shape, F32)
        in_grp = lax.broadcasted_iota(I32, (grp, 1), 0)

        def frame_body(r, carry):
            r0 = pl.multiple_of((r // grp) * grp, grp)
            bi = r // tt
            this = in_grp == (r % grp)
            for hh in range(HG_HEADS):
                sl = slice(hh * HG_KDIM, (hh + 1) * HG_KDIM)

                def only(ref):
                    return jnp.where(this, ref[pl.ds(r0, grp), sl], 0.0)

                forget = jnp.exp(jnp.sum(only(g_s), axis=0, keepdims=True))
                st = (st_s[bi * HG_HEADS + hh] * forget
                      + _dot_tn(only(v_s).astype(BF16), only(k_s).astype(BF16)))
                st_s[bi * HG_HEADS + hh] = st
                o_s[pl.ds(r0, grp), sl] += _dot_nt(only(q_s).astype(BF16), st.astype(BF16))
            return carry

        lax.fori_loop(0, rows, frame_body, 0)
        for hh in range(HG_HEADS):
            sl = slice(hh * HG_KDIM, (hh + 1) * HG_KDIM)
            on_s[:, sl] = (_rms(o_s[:, sl], hn) * gate_s[:, sl]).astype(BF16)

    out = _dot(on_s[...], wout_ref[...]) + x
    o_ref[...] = out.reshape(bb, tt, d)

    @pl.when(j == pl.num_programs(1) - 1)
    def _():
        for bi in range(bb):
            for hh in range(HG_HEADS):
                s_ref[bi, hh] = st_s[bi * HG_HEADS + hh].T


def _hgrn_mixer(x, s0, norm_g, w_in, lb, hn, w_out, *, bb, tt, blk, sub):
    b, t, d = x.shape
    assert b % bb == 0 and t % tt == 0 and tt % blk == 0 and blk % sub == 0 and sub % 16 == 0, (b, t, bb, tt, blk, sub)
    assert bb == 1 or tt == t, "several streams per block only when a block holds whole streams"
    fdim = HG_HEADS * HG_KDIM
    rows = bb * tt
    kern = functools.partial(_hgrn_kernel, bb=bb, tt=tt, blk=blk, sub=sub)
    return pl.pallas_call(
        kern,
        grid=(b // bb, t // tt),
        in_specs=[
            pl.BlockSpec((bb, tt, d), lambda i, j: (i, j, 0)),
            pl.BlockSpec((bb, HG_HEADS, HG_KDIM, d // HG_HEADS), lambda i, j: (i, 0, 0, 0)),
            _const_spec((1, d)),
            _const_spec(w_in.shape),
            _const_spec((1, fdim)),
            _const_spec((1, d // HG_HEADS)),
            _const_spec(w_out.shape),
        ],
        out_specs=[
            pl.BlockSpec((bb, tt, d), lambda i, j: (i, j, 0)),
            pl.BlockSpec((bb, HG_HEADS, HG_KDIM, d // HG_HEADS), lambda i, j: (i, 0, 0, 0)),
        ],
        out_shape=[jax.ShapeDtypeStruct(x.shape, F32), jax.ShapeDtypeStruct(s0.shape, F32)],
        scratch_shapes=[
            pltpu.VMEM((rows, fdim), F32),
            pltpu.VMEM((rows, fdim), F32),
            pltpu.VMEM((rows, fdim), F32),
            pltpu.VMEM((rows, d), F32),
            pltpu.VMEM((rows, d), F32),
            pltpu.VMEM((rows, d), BF16),
            pltpu.VMEM((bb * HG_HEADS, d // HG_HEADS, HG_KDIM), F32),
            pltpu.VMEM((rows, d), F32),
        ],
        compiler_params=_cparams(2),
        name="hgrn_mixer",
    )(x, s0, norm_g.reshape(1, d), w_in, lb.reshape(1, fdim), hn.reshape(1, -1), w_out)


CONV_HEAD = 8


def _ffn_kernel(x_ref, cs_ref, g_ref, win_ref, cw_ref, cb_ref, wdn_ref, fg_ref, o_ref, ns_ref,
                a_s, *, bb, tt, final_norm):
    j = pl.program_id(1)
    rows = bb * tt
    d = x_ref.shape[-1]
    dff = cw_ref.shape[-1]
    hist = CONV_W - 1

    @pl.when(j == 0)
    def _():
        a_s[:, CONV_HEAD - hist:CONV_HEAD, :] = cs_ref[...]

    x = x_ref[...].reshape(rows, d)
    h = _rms(x, g_ref[...]).astype(BF16)
    a = _dot(h, win_ref[:, 0:dff])
    u = _dot(h, win_ref[:, dff:])
    a_s[:, CONV_HEAD:CONV_HEAD + tt, :] = a.reshape(bb, tt, dff)
    c = jnp.broadcast_to(cb_ref[...].reshape(1, 1, dff), (bb, tt, dff))
    for w in range(CONV_W):
        lo = CONV_HEAD - hist + w
        c = c + a_s[:, lo:lo + tt, :] * cw_ref[w:w + 1, :].reshape(1, 1, dff)
    new_state = a_s[:, CONV_HEAD + tt - hist:CONV_HEAD + tt, :]
    a_s[:, CONV_HEAD - hist:CONV_HEAD, :] = new_state
    ns_ref[...] = new_state
    act = (_silu(c).reshape(rows, dff) * u).astype(BF16)
    y = _dot(act, wdn_ref[...]) + x
    if final_norm:
        y = _rms(y, fg_ref[...])
    o_ref[...] = y.reshape(bb, tt, d)


def _conv_ffn(x, conv_state, norm_g, w_in, conv_w, conv_b, w_down, final_g, *, bb, tt, final_norm):
    b, t, d = x.shape
    assert b % bb == 0 and t % tt == 0 and tt >= CONV_W - 1 and (bb == 1 or tt == t), (b, t, bb, tt)
    dff = conv_w.shape[-1]
    kern = functools.partial(_ffn_kernel, bb=bb, tt=tt, final_norm=final_norm)
    return pl.pallas_call(
        kern,
        grid=(b // bb, t // tt),
        in_specs=[
            pl.BlockSpec((bb, tt, d), lambda i, j: (i, j, 0)),
            pl.BlockSpec((bb, CONV_W - 1, dff), lambda i, j: (i, 0, 0)),
            _const_spec((1, d)),
            _const_spec(w_in.shape),
            _const_spec(conv_w.shape),
            _const_spec((1, dff)),
            _const_spec(w_down.shape),
            _const_spec((1, d)),
        ],
        out_specs=[
            pl.BlockSpec((bb, tt, d), lambda i, j: (i, j, 0)),
            pl.BlockSpec((bb, CONV_W - 1, dff), lambda i, j: (i, 0, 0)),
        ],
        out_shape=[jax.ShapeDtypeStruct(x.shape, F32), jax.ShapeDtypeStruct(conv_state.shape, F32)],
        scratch_shapes=[pltpu.VMEM((bb, CONV_HEAD + tt, dff), F32)],
        compiler_params=_cparams(2),
        name="conv_ffn",
    )(x, conv_state, norm_g.reshape(1, d), w_in, conv_w, conv_b.reshape(1, dff), w_down,
      final_g.reshape(1, d))


def _rope_tables(pos):
    rot = HEAD_DIM // 4
    half = rot // 2
    inv_freq = ROPE_THETA ** (-jnp.arange(half, dtype=F32) / half)
    ang = pos.astype(F32)[:, None] * inv_freq[None, :]
    cos, sin = jnp.cos(ang), jnp.sin(ang)
    t = pos.shape[0]
    pad = HEAD_DIM - rot
    one = jnp.ones((t, pad), F32)
    zero_h = jnp.zeros((t, half), F32)
    zero_p = jnp.zeros((t, pad), F32)
    c64 = jnp.concatenate([cos, cos, one], axis=1)
    s1_64 = jnp.concatenate([zero_h, sin, zero_p], axis=1)
    s2_64 = jnp.concatenate([-sin, zero_h, zero_p], axis=1)
    ident = jnp.ones((t, HEAD_DIM), F32)
    zero64 = jnp.zeros((t, HEAD_DIM), F32)
    return jnp.stack([
        jnp.concatenate([c64, c64], axis=1), jnp.concatenate([s1_64, s1_64], axis=1),
        jnp.concatenate([s2_64, s2_64], axis=1),
        jnp.concatenate([c64, ident], axis=1), jnp.concatenate([s1_64, zero64], axis=1),
        jnp.concatenate([s2_64, zero64], axis=1)])


def _dsa_proj_kernel(x_ref, g_ref, w_ref, tab_ref, lnw_ref, lnb_ref,
                     q_ref, k_ref, v_ref, qi_ref, kw_ref, *, bb, tt):
    rows = bb * tt
    d = x_ref.shape[-1]
    half = HEAD_DIM // 8
    o1 = N_HEADS * HEAD_DIM
    o2 = o1 + N_KV_HEADS * HEAD_DIM
    o3 = o2 + N_KV_HEADS * HEAD_DIM
    o4 = o3 + IDX_HEADS * IDX_DIM

    def tab(i):
        t = tab_ref[i]
        return jnp.broadcast_to(t[None], (bb, tt, LANES)).reshape(rows, LANES)

    def rope(y, base):
        cos, s1, s2 = tab(base), tab(base + 1), tab(base + 2)
        tiles = []
        for m in range(y.shape[1] // LANES):
            yt = y[:, m * LANES:(m + 1) * LANES]
            tiles.append(yt * cos + pltpu.roll(yt, half, 1) * s1 + pltpu.roll(yt, LANES - half, 1) * s2)
        return tiles[0] if len(tiles) == 1 else jnp.concatenate(tiles, axis=1)

    x = x_ref[...].reshape(rows, d)
    h = _rms(x, g_ref[...]).astype(BF16)
    t = _dot(h, w_ref[:, o4:o4 + LANES])
    lane = lax.broadcasted_iota(I32, (1, LANES), 1)
    is_ki = lane < IDX_DIM
    mu = jnp.sum(jnp.where(is_ki, t, 0.0), axis=-1, keepdims=True) / IDX_DIM
    cen = jnp.where(is_ki, t - mu, 0.0)
    var = jnp.sum(cen * cen, axis=-1, keepdims=True) / IDX_DIM
    ki = rope(cen * lax.rsqrt(var + EPS) * lnw_ref[...] + lnb_ref[...], 3)
    wi = t * ((IDX_HEADS * IDX_DIM) ** -0.5)
    kw = jnp.where(is_ki, ki, jnp.where(lane < IDX_DIM + IDX_HEADS, wi, 0.0))
    kw_ref[...] = kw.reshape(bb, tt, LANES)
    qi_ref[...] = rope(_dot(h, w_ref[:, o3:o4]), 0).astype(BF16).reshape(bb, tt, o4 - o3)
    k_ref[...] = rope(_dot(h, w_ref[:, o1:o2]), 0).reshape(bb, tt, o2 - o1)
    q = rope(_dot(h, w_ref[:, 0:o1]), 0) * (HEAD_DIM ** -0.5 * LOG2_E)
    q_ref[...] = q.astype(BF16).reshape(bb, tt, o1)
    v_ref[...] = _dot(h, w_ref[:, o2:o3]).reshape(bb, tt, o3 - o2)


def _dsa_proj(x, norm_g, w_pad, tables, ln_w, ln_b, *, bb, tt):
    b, t, d = x.shape
    n_q = N_HEADS * HEAD_DIM
    n_kv = N_KV_HEADS * HEAD_DIM
    n_qi = IDX_HEADS * IDX_DIM
    pad = jnp.zeros((LANES - IDX_DIM,), F32)
    lnw = jnp.concatenate([ln_w, pad]).reshape(1, LANES)
    lnb = jnp.concatenate([ln_b, pad]).reshape(1, LANES)
    kern = functools.partial(_dsa_proj_kernel, bb=bb, tt=tt)

    def row_spec(n):
        return pl.BlockSpec((bb, tt, n), lambda i, j: (i, j, 0))

    return pl.pallas_call(
        kern,
        grid=(b // bb, t // tt),
        in_specs=[
            row_spec(d),
            _const_spec((1, d)),
            _const_spec(w_pad.shape),
            pl.BlockSpec((6, tt, LANES), lambda i, j: (0, j, 0)),
            _const_spec((1, LANES)),
            _const_spec((1, LANES)),
        ],
        out_specs=[row_spec(n_q), row_spec(n_kv), row_spec(n_kv), row_spec(n_qi), row_spec(LANES)],
        out_shape=[
            jax.ShapeDtypeStruct((b, t, n_q), BF16),
            jax.ShapeDtypeStruct((b, t, n_kv), F32),
            jax.ShapeDtypeStruct((b, t, n_kv), F32),
            jax.ShapeDtypeStruct((b, t, n_qi), BF16),
            jax.ShapeDtypeStruct((b, t, LANES), F32),
        ],
        compiler_params=_cparams(2),
        name="dsa_proj",
    )(x, norm_g.reshape(1, d), w_pad, tables, lnw, lnb)


def _order_key(score):
    score = jnp.where(score == 0.0, 0.0, score)
    bits = lax.bitcast_convert_type(score, I32)
    return bits ^ ((bits >> 31) & 0x7FFFFFFF)


def _attn_prompt_kernel(q_ref, qi_ref, wq_ref, k_ref, v_ref, kw_ref, x_ref, wout_ref, o_ref,
                        ke, ko, vt, kie, kio, qs, key_s, bias_s, ot_s, half_s, kmax_s, thr_s, *, tq, t_len, kc, n_sel):
    j = pl.program_id(1)
    nct = t_len // kc
    sub_per_chunk = kc // LANES
    lane = lax.broadcasted_iota(I32, (1, LANES), 1)
    lo = lane < HEAD_DIM

    unit_hi = jnp.where(lane == HEAD_DIM, 1.0, 0.0)
    unit_lo = jnp.where(lane == 0, 1.0, 0.0)

    r_i = lax.broadcasted_iota(I32, (LANES, LANES), 0)
    c_i = lax.broadcasted_iota(I32, (LANES, LANES), 1)
    half_sums = jnp.where(((r_i < HEAD_DIM) & (c_i == HEAD_DIM)) | ((r_i >= HEAD_DIM) & (c_i == 0)),
                          1.0, 0.0).astype(BF16)

    @pl.when(j == 0)
    def _():
        k_sq_max = [jnp.zeros((1, LANES), F32) for _ in range(N_KV_HEADS // 2)]
        for c in range(nct):
            rows = slice(c * kc, (c + 1) * kc)
            k_c = k_ref[0, rows, :].astype(BF16).astype(F32)
            for m in range(N_KV_HEADS // 2):
                tile = k_c[:, m * LANES:(m + 1) * LANES]
                rolled = pltpu.roll(tile, HEAD_DIM, 1)
                ke[(2 * m) * nct + c] = jnp.where(lo, tile, unit_hi).astype(BF16)
                ko[(2 * m) * nct + c] = jnp.where(lo, unit_lo, rolled).astype(BF16)
                ke[(2 * m + 1) * nct + c] = jnp.where(lo, rolled, unit_hi).astype(BF16)
                ko[(2 * m + 1) * nct + c] = jnp.where(lo, unit_lo, tile).astype(BF16)
                norm2 = _dot((tile * tile).astype(BF16), half_sums)
                k_sq_max[m] = jnp.maximum(k_sq_max[m], jnp.max(norm2, axis=0, keepdims=True))
            v_t = v_ref[0, rows, :].T
            for g in range(N_KV_HEADS):
                vt[g * nct + c, 0:HEAD_DIM, :] = v_t[g * HEAD_DIM:(g + 1) * HEAD_DIM, :].astype(BF16)
                vt[g * nct + c, HEAD_DIM:HEAD_DIM + ONES_ROWS, :] = jnp.ones((ONES_ROWS, kc), BF16)
            kw_c = kw_ref[0, rows, :]
            kie[c] = jnp.where(lo, kw_c, 0.0).astype(BF16)
            kio[c] = jnp.where(lo, 0.0, pltpu.roll(kw_c, HEAD_DIM, 1)).astype(BF16)
        for g in range(N_KV_HEADS):
            at = HEAD_DIM if g % 2 == 0 else 0
            k_sq = jnp.max(jnp.where(lane == at, k_sq_max[g // 2], 0.0), axis=-1, keepdims=True)
            kmax_s[g] = jnp.broadcast_to(jnp.sqrt(k_sq), (1, LANES))

    for m in range(N_HEADS // 2):
        tile = q_ref[0, :, m * LANES:(m + 1) * LANES].astype(F32)
        k_norm = kmax_s[(2 * m) // (N_HEADS // N_KV_HEADS)]
        bound = jnp.sqrt(_dot((tile * tile).astype(BF16), half_sums)) * (k_norm * -BOUND_SLACK)
        qs[2 * m] = jnp.where(lo, tile, jnp.where(lane == HEAD_DIM, bound, 0.0)).astype(BF16)
        qs[2 * m + 1] = jnp.where(lo, jnp.where(lane == 0, bound, 0.0), tile).astype(BF16)

    n_chunks = ((j + 1) * tq + kc - 1) // kc
    w_t = wq_ref[0].T
    w_rows = [w_t[IDX_DIM + hh:IDX_DIM + hh + 1, :] for hh in range(IDX_HEADS)]
    qi = qi_ref[0]
    q_chunk = (j * tq + lax.broadcasted_iota(I32, (1, tq), 1)) // CHUNK

    def score_body(c, carry):
        kie_c, kio_c = kie[c], kio[c]
        sc = jnp.zeros((kc, tq), F32)
        for m in range(IDX_HEADS // 2):
            qt = qi[:, m * LANES:(m + 1) * LANES]
            sc = (sc + w_rows[2 * m] * jnp.maximum(_dot_nt(kie_c, qt), 0.0)
                  + w_rows[2 * m + 1] * jnp.maximum(_dot_nt(kio_c, qt), 0.0))
        k_chunk = (c * kc + lax.broadcasted_iota(I32, (kc, 1), 0)) // CHUNK
        keys = jnp.where(k_chunk <= q_chunk, _order_key(sc), KEY_NEG_INF)
        key_s[c] = keys
        half_s[c] = lax.shift_right_arithmetic(keys, 16).astype(I16)
        return carry

    def score_pair(p, carry):
        score_body(2 * p, carry)
        return score_body(2 * p + 1, carry)

    lax.fori_loop(0, n_chunks // 2, score_pair, 0)
    lax.fori_loop(n_chunks - n_chunks % 2, n_chunks, score_body, 0)

    def search16(need, n_static):
        def count_ge(cand):
            acc = jnp.zeros((16, tq), I16)
            for c in range(n_static):
                hit = (half_s[c] >= cand).astype(I16)
                for r in range(0, kc, 16):
                    acc = acc + hit[r:r + 16, :]
            return jnp.sum(acc.astype(I32), axis=0, keepdims=True)

        def bit_step(i, t16):
            cand = t16 ^ lax.shift_left(jnp.int32(1), 15 - i)
            cand = lax.shift_right_arithmetic(lax.shift_left(cand, 16), 16)
            return jnp.where(count_ge(cand.astype(I16)) >= need, cand, t16)

        return lax.fori_loop(0, 16, bit_step, jnp.full((1, tq), -32768, I32))

    def find_threshold(n_static):
        thr_hi = search16(n_sel, n_static)
        n_above = jnp.zeros((8, tq), I32)
        for c in range(n_static):
            keys = key_s[c]
            hi = lax.shift_right_arithmetic(keys, 16)
            low = (keys & 0xFFFF) - 32768
            half_s[c] = jnp.where(hi == thr_hi, low, -32768).astype(I16)
            n_above = n_above + jnp.sum((hi > thr_hi).astype(I32).reshape(kc // 8, 8, tq), axis=0)
        thr_lo = search16(n_sel - jnp.sum(n_above, axis=0, keepdims=True), n_static)
        thr_s[...] = jnp.broadcast_to(lax.shift_left(thr_hi, 16) | (thr_lo + 32768), thr_s.shape)

    for n_static in range(1, nct + 1):
        pl.when(n_chunks == n_static)(functools.partial(find_threshold, n_static))
    thr = thr_s[0:1, :]

    def bias_body(c, acc):
        keys = key_s[c]
        ge = keys >= thr
        bias_s[c] = jnp.where(ge & (keys > KEY_NEG_INF), 0.0, -jnp.inf)
        fold = lambda hit: jnp.sum(hit.astype(I32).reshape(kc // 8, 8, tq), axis=0)
        return acc[0] + fold(ge), acc[1] + fold(keys > thr)

    zero8 = jnp.zeros((8, tq), I32)
    n_ge, n_gt = [jnp.sum(a, axis=0, keepdims=True)
                  for a in lax.fori_loop(0, n_chunks, bias_body, (zero8, zero8))]

    has_excess = jnp.max(jnp.where(thr > KEY_NEG_INF, n_ge - n_sel, 0)) > 0

    @pl.when(has_excess)
    def _():
        wanted = (n_sel - n_gt).astype(F32)
        r_i = lax.broadcasted_iota(I32, (LANES, LANES), 0)
        c_i = lax.broadcasted_iota(I32, (LANES, LANES), 1)
        lower = jnp.where(c_i <= r_i, 1.0, 0.0).astype(BF16)

        def tie_body(u, seen):
            c = u // sub_per_chunk
            r0 = pl.multiple_of((u % sub_per_chunk) * LANES, LANES)
            keys = key_s[c, pl.ds(r0, LANES), :]
            tie = keys == thr
            tie_f = jnp.where(tie, 1.0, 0.0)
            rank = seen + _dot(lower, tie_f.astype(BF16))
            sel = ((keys > thr) | (tie & (rank <= wanted))) & (keys > KEY_NEG_INF)
            bias_s[c, pl.ds(r0, LANES), :] = jnp.where(sel, 0.0, -jnp.inf)
            return seen + jnp.sum(tie_f, axis=0, keepdims=True)

        lax.fori_loop(0, n_chunks * sub_per_chunk, tie_body, jnp.zeros((1, tq), F32))

    n_grp = N_HEADS // N_KV_HEADS
    acc0 = jnp.zeros((HEAD_DIM + ONES_ROWS, tq), F32)

    def attend(running_max):
        def group_body(g, den_min):
            q_tiles = [qs[g * n_grp + h] for h in range(n_grp)]

            def chunks_body(cs, st):
                biases = [bias_s[c] for c in cs]
                v_cs = [vt[g * nct + c] for c in cs]
                logits = [[_dot_nt((ke, ko)[h % 2][g * nct + c], q_tiles[h]) for h in range(n_grp)] for c in cs]
                tails = [lax.bitcast_convert_type(l[kc - 8:kc, :], I32) for ls in logits for l in ls][1:]
                anchor = functools.reduce(jnp.bitwise_or, tails) & lax.shift_right_arithmetic(cs[0], 31)
                anchor = lax.bitcast_convert_type(anchor, F32)[0:1, :]
                st = list(st)
                for i in range(len(cs)):
                    for h in range(n_grp):
                        logit = logits[i][h] + biases[i]
                        if i == 0 and h == 0:
                            logit = logit + anchor
                        if running_max:
                            m_run, acc = st[2 * h:2 * h + 2]
                            m_new = jnp.maximum(m_run, jnp.max(logit, axis=0, keepdims=True))
                            m_safe = jnp.where(m_new > -jnp.inf, m_new, 0.0)
                            p = jnp.exp2(logit - m_safe)
                            st[2 * h:2 * h + 2] = [m_new, jnp.exp2(m_run - m_safe) * acc + _dot(v_cs[i], p.astype(BF16))]
                        else:
                            st[h] = st[h] + _dot(v_cs[i], jnp.exp2(logit).astype(BF16))
                return tuple(st)

            if running_max:
                st = lax.fori_loop(0, n_chunks, lambda c, st: chunks_body([c], st),
                                   (jnp.full((1, tq), -jnp.inf, F32), acc0) * n_grp)
                accs = st[1::2]
            else:
                accs = lax.fori_loop(0, n_chunks // 2, lambda p, st: chunks_body([2 * p, 2 * p + 1], st),
                                     (acc0,) * n_grp)
                accs = lax.fori_loop(n_chunks - n_chunks % 2, n_chunks, lambda c, st: chunks_body([c], st), accs)
            for h in range(n_grp):
                r0 = pl.multiple_of(g * (n_grp * HEAD_DIM) + h * HEAD_DIM, HEAD_DIM)
                den = accs[h][HEAD_DIM:HEAD_DIM + 1, :]
                ot_s[pl.ds(r0, HEAD_DIM), :] = accs[h][0:HEAD_DIM, :] / den
                den_min = jnp.minimum(den_min, den)
            return den_min

        return lax.fori_loop(0, N_KV_HEADS, group_body, jnp.full((1, tq), jnp.inf, F32))

    den_min = attend(running_max=False)

    @pl.when(jnp.logical_not(jnp.min(den_min) >= MIN_DENOMINATOR))
    def _():
        attend(running_max=True)

    o = jnp.concatenate([ot_s[m * LANES:(m + 1) * LANES, :].T for m in range(N_HEADS // 2)], axis=1)
    o_ref[0] = _dot(o.astype(BF16), wout_ref[...]) + x_ref[0]


def _dsa_attn_prompt(q, qi, kw, k, v, x, w_out, *, tq, kc):
    b, t, d = x.shape
    assert t % tq == 0 and t % kc == 0 and tq % CHUNK == 0 and kc % LANES == 0, (t, tq, kc)
    n_sel = min(TOPK_MAX, t // 4)
    n_kv = N_KV_HEADS * HEAD_DIM
    nct = t // kc
    kern = functools.partial(_attn_prompt_kernel, tq=tq, t_len=t, kc=kc, n_sel=n_sel)

    def tile_spec(n):
        return pl.BlockSpec((1, tq, n), lambda i, j: (i, j, 0))

    def full_spec(n):
        return pl.BlockSpec((1, t, n), lambda i, j: (i, 0, 0))

    return pl.pallas_call(
        kern,
        grid=(b, t // tq),
        in_specs=[tile_spec(q.shape[-1]), tile_spec(qi.shape[-1]), tile_spec(LANES),
                  full_spec(n_kv), full_spec(n_kv), full_spec(LANES),
                  tile_spec(d), _const_spec(w_out.shape)],
        out_specs=tile_spec(d),
        out_shape=jax.ShapeDtypeStruct(x.shape, F32),
        scratch_shapes=[
            pltpu.VMEM((N_KV_HEADS * nct, kc, LANES), BF16), pltpu.VMEM((N_KV_HEADS * nct, kc, LANES), BF16),
            pltpu.VMEM((N_KV_HEADS * nct, HEAD_DIM + ONES_ROWS, kc), BF16),
            pltpu.VMEM((nct, kc, LANES), BF16), pltpu.VMEM((nct, kc, LANES), BF16),
            pltpu.VMEM((N_HEADS, tq, LANES), BF16),
            pltpu.VMEM((nct, kc, tq), I32), pltpu.VMEM((nct, kc, tq), F32),
            pltpu.VMEM((N_HEADS * HEAD_DIM, tq), F32),
            pltpu.VMEM((nct, kc, tq), I16),
            pltpu.VMEM((N_KV_HEADS, 1, LANES), F32),
            pltpu.VMEM((8, tq), I32),
        ],
        compiler_params=_cparams(2),
        name="dsa_attn_prompt",
    )(q, qi, kw, k, v, kw, x, w_out)


def _attn_sample_kernel(q_ref, qi_ref, kw_ref, k_ref, v_ref, ckt_ref, cvt_ref, ckit_ref, x_ref, wout_ref, o_ref,
                        kt16, vt16, qg_s, kn_s, vn_s, og_s, *, tq, past, n_sel):
    n_grp = N_HEADS // N_KV_HEADS
    lane = lax.broadcasted_iota(I32, (1, LANES), 1)
    lo = lane < HEAD_DIM

    def lower_half(x128, upper):
        return jnp.where(lo, pltpu.roll(x128, HEAD_DIM, 1) if upper else x128, 0.0)

    kt16[...] = ckt_ref[0].astype(BF16)
    vt16[...] = cvt_ref[0].astype(BF16)
    kit16 = ckit_ref[0].astype(BF16)
    kw = kw_ref[0]
    kw16 = kw.astype(BF16)
    k_new, v_new = k_ref[0], v_ref[0]
    q32 = q_ref[0].astype(F32)
    for g in range(N_KV_HEADS):
        tile = slice((g // 2) * LANES, (g // 2 + 1) * LANES)
        kn_s[g] = lower_half(k_new[:, tile], g % 2 == 1).astype(BF16)
        vn_s[g] = lower_half(v_new[:, tile], g % 2 == 1).astype(BF16)
        heads = [n_grp * g + h for h in range(n_grp)]
        qg_s[g] = jnp.concatenate(
            [lower_half(q32[:, (hd // 2) * LANES:(hd // 2 + 1) * LANES], hd % 2 == 1) for hd in heads],
            axis=0).astype(BF16)

    qi32 = qi_ref[0].astype(F32)
    qi_rows = jnp.concatenate(
        [lower_half(qi32[:, (hh // 2) * LANES:(hh // 2 + 1) * LANES], hh % 2 == 1) for hh in range(IDX_HEADS)],
        axis=0).astype(BF16)
    d_c = _dot(qi_rows[:, 0:IDX_DIM], kit16)
    d_n = _dot_nt(qi_rows, kw16)
    sc_c = jnp.zeros((tq, past), F32)
    sc_n = jnp.zeros((tq, tq), F32)
    for hh in range(IDX_HEADS):
        w_col = kw[:, IDX_DIM + hh:IDX_DIM + hh + 1]
        rows = slice(hh * tq, (hh + 1) * tq)
        sc_c = sc_c + w_col * jnp.maximum(d_c[rows, :], 0.0)
        sc_n = sc_n + w_col * jnp.maximum(d_n[rows, :], 0.0)
    q_chunk = (past + lax.broadcasted_iota(I32, (tq, 1), 0)) // CHUNK
    k_chunk = (past + lax.broadcasted_iota(I32, (1, tq), 1)) // CHUNK
    key_c = _order_key(sc_c)
    key_n = jnp.where(k_chunk <= q_chunk, _order_key(sc_n), KEY_NEG_INF)

    def count(pred):
        return (jnp.sum(pred(key_c).astype(I32), axis=-1, keepdims=True)
                + jnp.sum(pred(key_n).astype(I32), axis=-1, keepdims=True))

    def bit_step(i, thr):
        cand = thr ^ lax.shift_left(jnp.int32(1), 31 - i)
        return jnp.where(count(lambda k: k >= cand) >= n_sel, cand, thr)

    thr = lax.fori_loop(0, 32, bit_step, jnp.full((tq, 1), INT_MIN, I32))
    wanted = (n_sel - count(lambda k: k > thr)).astype(F32)

    def select(keys, seen):
        tie = keys == thr
        tie_f = jnp.where(tie, 1.0, 0.0)
        ranks = []
        for c0 in range(0, keys.shape[1], LANES):
            cw = min(LANES, keys.shape[1] - c0)
            r_i = lax.broadcasted_iota(I32, (cw, cw), 0)
            c_i = lax.broadcasted_iota(I32, (cw, cw), 1)
            upper = jnp.where(r_i <= c_i, 1.0, 0.0).astype(BF16)
            part = tie_f[:, c0:c0 + cw]
            ranks.append(seen + _dot(part.astype(BF16), upper))
            seen = seen + jnp.sum(part, axis=-1, keepdims=True)
        rank = ranks[0] if len(ranks) == 1 else jnp.concatenate(ranks, axis=1)
        sel = ((keys > thr) | (tie & (rank <= wanted))) & (keys > KEY_NEG_INF)
        return jnp.where(sel, 0.0, -jnp.inf), seen

    bias_c, seen = select(key_c, jnp.zeros((tq, 1), F32))
    bias_n, _ = select(key_n, seen)
    bias_c = jnp.concatenate([bias_c] * n_grp, axis=0)
    bias_n = jnp.concatenate([bias_n] * n_grp, axis=0)

    groups = range(N_KV_HEADS)
    rows = [slice(g * HEAD_DIM, (g + 1) * HEAD_DIM) for g in groups]
    l_c = [_dot(qg_s[g][:, 0:HEAD_DIM], kt16[rows[g], :]) for g in groups]
    l_n = [_dot_nt(qg_s[g], kn_s[g]) for g in groups]
    tail = lax.bitcast_convert_type(l_c[-1][:, past - LANES:past], I32) & lax.shift_right_arithmetic(pl.program_id(0), 31)
    l_n[0] = l_n[0] + lax.bitcast_convert_type(tail, F32)[:, 0:tq]
    for g in groups:
        lc, ln = l_c[g] + bias_c, l_n[g] + bias_n
        mx = jnp.maximum(jnp.max(lc, axis=-1, keepdims=True), jnp.max(ln, axis=-1, keepdims=True))
        p_c = jnp.exp2(lc - mx)
        p_n = jnp.exp2(ln - mx)
        den = jnp.sum(p_c, axis=-1, keepdims=True) + jnp.sum(p_n, axis=-1, keepdims=True)
        pv = (_dot_nt(p_c.astype(BF16), vt16[rows[g], :])
              + _dot(p_n.astype(BF16), vn_s[g])[:, 0:HEAD_DIM])
        og_s[g] = pv / den

    tiles = []
    for m in range(N_HEADS // 2):
        pair = []
        for hd in (2 * m, 2 * m + 1):
            h = hd % n_grp
            pair.append(og_s[hd // n_grp, h * tq:(h + 1) * tq, :])
        tiles.append(jnp.concatenate(pair, axis=1))
    o = jnp.concatenate(tiles, axis=1).astype(BF16)
    o_ref[0] = _dot(o, wout_ref[...]) + x_ref[0]


def _dsa_attn_sample(q, qi, kw, k, v, cache_kt, cache_vt, cache_kit, x, w_out):
    b, t, d = x.shape
    past = cache_kt.shape[-1]
    assert past % LANES == 0 and t % 8 == 0, (past, t)
    n_sel = min(TOPK_MAX, (past + t) // 4)
    n_kv = N_KV_HEADS * HEAD_DIM
    n_grp = N_HEADS // N_KV_HEADS
    kern = functools.partial(_attn_sample_kernel, tq=t, past=past, n_sel=n_sel)

    def spec(rows, n):
        return pl.BlockSpec((1, rows, n), lambda i: (i, 0, 0))

    return pl.pallas_call(
        kern,
        grid=(b,),
        in_specs=[spec(t, q.shape[-1]), spec(t, qi.shape[-1]), spec(t, LANES), spec(t, n_kv), spec(t, n_kv),
                  spec(n_kv, past), spec(n_kv, past), spec(IDX_DIM, past),
                  spec(t, d), _const_spec(w_out.shape)],
        out_specs=spec(t, d),
        out_shape=jax.ShapeDtypeStruct(x.shape, F32),
        scratch_shapes=[
            pltpu.VMEM((n_kv, past), BF16), pltpu.VMEM((n_kv, past), BF16),
            pltpu.VMEM((N_KV_HEADS, n_grp * t, LANES), BF16),
            pltpu.VMEM((N_KV_HEADS, t, LANES), BF16), pltpu.VMEM((N_KV_HEADS, t, LANES), BF16),
            pltpu.VMEM((N_KV_HEADS, n_grp * t, HEAD_DIM), F32),
        ],
        compiler_params=_cparams(1),
        name="dsa_attn_sample",
    )(q, qi, kw, k, v, cache_kt, cache_vt, cache_kit, x, w_out)


def _tiling(b, t):
    for tt in (512, 256):
        if t % tt == 0:
            return dict(bb=1, tt=tt, blk=128, sub=32)
    bb = max(1, min(b, 128 // t))
    return dict(bb=bb, tt=t, blk=t, sub=t)


FFN_ROWS = 256


def _trunk(x, hg_state, conv_state, cache, p):
    b, t, d = x.shape
    tl = _tiling(b, t)
    bb, tt = tl["bb"], tl["tt"]
    x, s_new = _hgrn_mixer(x, hg_state[0], p["norm_mix"][0], p["hg_w_in"][0], p["lower_bounds"][0],
                           p["hg_norm"][0], p["hg_w_out"][0], **tl)
    x, cs0 = _conv_ffn(x, conv_state[0], p["norm_ffn"][0], p["ffn_w_in"][0], p["ffn_conv_w"][0],
                       p["ffn_conv_b"][0], p["ffn_w_down"][0], p["norm_final"], bb=bb, tt=min(tt, FFN_ROWS),
                       final_norm=False)
    past = 0 if cache is None else cache[0].shape[1]
    tables = _rope_tables(past + jnp.arange(t))
    q, k, v, qi, kw = _dsa_proj(x, p["norm_mix"][1], p["ds_w_in"][0], tables, p["ds_kln_w"][0],
                                p["ds_kln_b"][0], bb=bb, tt=tt)
    if cache is None:
        x = _dsa_attn_prompt(q, qi, kw, k, v, x, p["ds_w_out"][0], tq=min(t, 512), kc=min(t, 512))
    else:
        ck, cv, cki = cache
        def keys_last(c):
            return jnp.swapaxes(c.reshape(b, past, -1), 1, 2)

        x = _dsa_attn_sample(q, qi, kw, k, v, keys_last(ck), keys_last(cv), keys_last(cki), x, p["ds_w_out"][0])
    x, cs1 = _conv_ffn(x, conv_state[1], p["norm_ffn"][1], p["ffn_w_in"][1], p["ffn_conv_w"][1],
                       p["ffn_conv_b"][1], p["ffn_w_down"][1], p["norm_final"], bb=bb, tt=min(tt, FFN_ROWS),
                       final_norm=True)
    k = k.reshape(1, b, t, N_KV_HEADS, HEAD_DIM)
    v = v.reshape(1, b, t, N_KV_HEADS, HEAD_DIM)
    ki = kw[:, :, :IDX_DIM].reshape(1, b, t, IDX_DIM)
    return x, s_new[None], jnp.stack([cs0, cs1]), k, v, ki


def kernel(x_prompt, x_sample, cache_k, cache_v, cache_kidx, state_hgrn, state_conv, norm_mix, norm_ffn, norm_final, hg_w_in, hg_lb, hg_norm, hg_w_out, ds_w_in, ds_kln_w, ds_kln_b, ds_w_out, ffn_w_in, ffn_conv_w, ffn_conv_b, ffn_w_down):
    dsa_in = ds_w_in.shape[-1]
    dsa_pad = (-dsa_in) % LANES
    p = dict(
        norm_mix=norm_mix, norm_ffn=norm_ffn, norm_final=norm_final,
        hg_w_in=hg_w_in.astype(BF16), hg_norm=hg_norm, hg_w_out=hg_w_out.astype(BF16),
        lower_bounds=jnp.cumsum(jax.nn.softmax(hg_lb.astype(F32), axis=0), axis=0),
        ds_w_in=jnp.pad(ds_w_in, ((0, 0), (0, 0), (0, dsa_pad))).astype(BF16),
        ds_kln_w=ds_kln_w, ds_kln_b=ds_kln_b, ds_w_out=ds_w_out.astype(BF16),
        ffn_w_in=ffn_w_in.astype(BF16), ffn_conv_w=ffn_conv_w, ffn_conv_b=ffn_conv_b,
        ffn_w_down=ffn_w_down.astype(BF16),
    )
    b = x_prompt.shape[0]
    hg0 = jnp.zeros((state_hgrn.shape[0], b) + state_hgrn.shape[2:], F32)
    conv0 = jnp.zeros((state_conv.shape[0], b) + state_conv.shape[2:], F32)
    y_p, hg_p, conv_p, k_p, v_p, ki_p = _trunk(x_prompt, hg0, conv0, None, p)
    y_s, hg_s, conv_s, k_s, v_s, ki_s = _trunk(x_sample, state_hgrn, state_conv,
                                               (cache_k[0], cache_v[0], cache_kidx[0]), p)
    return (y_p, y_s, k_p, v_p, ki_p, hg_p, conv_p, k_s, v_s, ki_s, hg_s, conv_s)
```
